```python
import math
import jax, jax.numpy as jnp
from jax import lax
import numpy as np

D_MODEL = 1024
BATCH = 16
SEQ = 4096
DEPTH = 4

MEM_LEN = 256
CHUNK = 128
Q_BLOCK = 128
A_GROUPS = 4
A_WIDTH = D_MODEL // 2
A_GROUP_DIM = A_WIDTH // A_GROUPS
B_GROUPS = 4
B_WIDTH = D_MODEL // 2
B_GROUP_DIM = B_WIDTH // B_GROUPS
CONV_WIDTH = 31
C_HEADS = 16
C_HEAD_DIM = D_MODEL // C_HEADS
MEM_HEADS = 4
MEM_HEAD_DIM = D_MODEL // MEM_HEADS
D_FF = ((8 * D_MODEL + 3 * 256 - 1) // (3 * 256)) * 256
DEEPNORM_ALPHA = (2.0 * DEPTH) ** 0.25
DEEPNORM_BETA = (8.0 * DEPTH) ** -0.25
LN_EPS = 1e-5
N_EVEN = (DEPTH + 1) // 2
N_ODD = DEPTH // 2

kernel_name = "hybrid_gmlp_conformer_stickbreaking_deepnorm"


def _layer_norm(x, g, b):
    xf = x.astype(jnp.float32)
    mu = jnp.mean(xf, axis=-1, keepdims=True)
    var = jnp.mean(jnp.square(xf - mu), axis=-1, keepdims=True)
    y = (xf - mu) * lax.rsqrt(var + LN_EPS)
    return (y * g.astype(jnp.float32) + b.astype(jnp.float32)).astype(x.dtype)


def _post_norm(x, f, g, b):
    return _layer_norm(DEEPNORM_ALPHA * x + f, g, b)


def _gmlp_chunked(u, v, ln_g, ln_b, w_s, b_s):
    bn, s, _ = u.shape
    u = jax.nn.gelu(u, approximate=False)
    v = jax.nn.gelu(v, approximate=False)
    vg = v.reshape(bn, s // CHUNK, CHUNK, A_GROUPS, A_GROUP_DIM)
    vg = _layer_norm(vg, ln_g.reshape(A_GROUPS, A_GROUP_DIM), ln_b.reshape(A_GROUPS, A_GROUP_DIM))
    causal = jnp.tril(jnp.ones((CHUNK, CHUNK), dtype=bool))
    w = jnp.where(causal[None], w_s, jnp.zeros_like(w_s))
    mixed = jnp.einsum('gts,bcsgd->bctgd', w, vg) + b_s.T[:, :, None]
    return u * mixed.reshape(bn, s, A_WIDTH)


def _conformer_conv(a, gate, conv_w, conv_b, gn_g, gn_b):
    h = a * jax.nn.sigmoid(gate)
    h = lax.conv_general_dilated(
        h, conv_w[:, None, :].astype(h.dtype), window_strides=(1,),
        padding=[(CONV_WIDTH - 1, 0)],
        dimension_numbers=('NWC', 'WIO', 'NWC'),
        feature_group_count=B_WIDTH) + conv_b
    bn, s, _ = h.shape
    hg = _layer_norm(h.reshape(bn, s, B_GROUPS, B_GROUP_DIM),
                     gn_g.reshape(B_GROUPS, B_GROUP_DIM), gn_b.reshape(B_GROUPS, B_GROUP_DIM))
    return jax.nn.silu(hg).reshape(bn, s, B_WIDTH)


def _stick_breaking(q, k, v):
    s_len = q.shape[2]
    scale = C_HEAD_DIM ** -0.5
    outs = []
    for i in range(s_len // Q_BLOCK):
        t0 = i * Q_BLOCK
        kend = t0 + Q_BLOCK
        qb = q[:, :, t0:kend]
        kb = k[:, :, :kend]
        vb = v[:, :, :kend]
        z = jnp.einsum('bhqd,bhkd->bhqk', qb, kb,
                       preferred_element_type=jnp.float32) * scale
        t_pos = t0 + jnp.arange(Q_BLOCK)[:, None]
        s_pos = jnp.arange(kend)[None, :]
        strict = s_pos < t_pos
        log_not = jnp.where(strict, jax.nn.log_sigmoid(-z), 0.0)
        later = lax.cumsum(log_not, axis=log_not.ndim - 1, reverse=True) - log_not
        att = jnp.where(strict, jnp.exp(jax.nn.log_sigmoid(z) + later), 0.0)
        outs.append(jnp.einsum('bhqk,bhkd->bhqd', att.astype(vb.dtype), vb))
    return jnp.concatenate(outs, axis=2)


def _memory_cross_attention(x, mem, wq, wk, wv, wo):
    bn, s, _ = x.shape
    m = mem.shape[1]
    q = (x @ wq).reshape(bn, s, MEM_HEADS, MEM_HEAD_DIM)
    k = (mem @ wk).reshape(bn, m, MEM_HEADS, MEM_HEAD_DIM)
    v = (mem @ wv).reshape(bn, m, MEM_HEADS, MEM_HEAD_DIM)
    sc = jnp.einsum('bqhd,bkhd->bhqk', q, k,
                    preferred_element_type=jnp.float32) * (MEM_HEAD_DIM ** -0.5)
    p = jax.nn.softmax(sc, axis=-1).astype(v.dtype)
    o = jnp.einsum('bhqk,bkhd->bqhd', p, v).reshape(bn, s, D_MODEL)
    return o @ wo


def _swiglu(x, w1, w3, w2):
    return (jax.nn.silu(x @ w1) * (x @ w3)) @ w2


def _fwd_setup_inputs(seed: int = 0) -> dict:
    key = jax.random.key(seed)
    ks = jax.random.split(key, 24)
    f32 = jnp.float32
    nrm = lambda k, shape, sc: jax.random.normal(k, shape, f32) * sc
    d = D_MODEL
    return {
        "x": nrm(ks[0], (BATCH, SEQ, d), 1.0),
        "mem": nrm(ks[1], (BATCH, MEM_LEN, d), 1.0),
        "w_in_ab": nrm(ks[2], (N_EVEN, d, 2 * A_WIDTH + 2 * B_WIDTH), d ** -0.5),
        "gmlp_ln_g": 1.0 + nrm(ks[3], (N_EVEN, A_WIDTH), 0.02),
        "gmlp_ln_b": nrm(ks[4], (N_EVEN, A_WIDTH), 0.02),
        "gmlp_w_s": nrm(ks[5], (N_EVEN, A_GROUPS, CHUNK, CHUNK), CHUNK ** -0.5),
        "gmlp_b_s": 1.0 + nrm(ks[6], (N_EVEN, A_GROUPS, CHUNK), 0.02),
        "conv_w": nrm(ks[7], (N_EVEN, CONV_WIDTH, B_WIDTH), CONV_WIDTH ** -0.5),
        "conv_b": nrm(ks[8], (N_EVEN, B_WIDTH), 0.02),
        "conv_gn_g": 1.0 + nrm(ks[9], (N_EVEN, B_WIDTH), 0.02),
        "conv_gn_b": nrm(ks[10], (N_EVEN, B_WIDTH), 0.02),
        "w_out_ab": nrm(ks[11], (N_EVEN, A_WIDTH + B_WIDTH, d), (A_WIDTH + B_WIDTH) ** -0.5 * DEEPNORM_BETA),
        "w_qkv_c": nrm(ks[12], (N_ODD, d, 3 * d), d ** -0.5),
        "w_out_c": nrm(ks[13], (N_ODD, d, d), d ** -0.5 * DEEPNORM_BETA),
        "mem_wq": nrm(ks[14], (DEPTH, d, d), d ** -0.5),
        "mem_wk": nrm(ks[15], (DEPTH, d, d), d ** -0.5),
        "mem_wv": nrm(ks[16], (DEPTH, d, d), d ** -0.5),
        "mem_wo": nrm(ks[17], (DEPTH, d, d), d ** -0.5 * DEEPNORM_BETA),
        "ffn_w1": nrm(ks[18], (DEPTH, d, D_FF), d ** -0.5),
        "ffn_w3": nrm(ks[19], (DEPTH, d, D_FF), d ** -0.5),
        "ffn_w2": nrm(ks[20], (DEPTH, D_FF, d), D_FF ** -0.5 * DEEPNORM_BETA),
        "ln_g": 1.0 + nrm(ks[21], (DEPTH, 3, d), 0.02),
        "ln_b": nrm(ks[22], (DEPTH, 3, d), 0.02),
    }


def _fwd_reference(x, mem, w_in_ab, gmlp_ln_g, gmlp_ln_b, gmlp_w_s, gmlp_b_s, conv_w, conv_b,
              conv_gn_g, conv_gn_b, w_out_ab, w_qkv_c, w_out_c, mem_wq, mem_wk, mem_wv,
              mem_wo, ffn_w1, ffn_w3, ffn_w2, ln_g, ln_b):
    bn, s, d = x.shape
    for layer in range(DEPTH):
        if layer % 2 == 0:
            e = layer // 2
            h = x @ w_in_ab[e]
            u = h[..., :A_WIDTH]
            v = h[..., A_WIDTH:2 * A_WIDTH]
            a = h[..., 2 * A_WIDTH:2 * A_WIDTH + B_WIDTH]
            gt = h[..., 2 * A_WIDTH + B_WIDTH:]
            ya = _gmlp_chunked(u, v, gmlp_ln_g[e], gmlp_ln_b[e], gmlp_w_s[e], gmlp_b_s[e])
            yb = _conformer_conv(a, gt, conv_w[e], conv_b[e], conv_gn_g[e], conv_gn_b[e])
            mix = jnp.concatenate([ya, yb], axis=-1) @ w_out_ab[e]
        else:
            o = layer // 2
            qkv = (x @ w_qkv_c[o]).reshape(bn, s, 3, C_HEADS, C_HEAD_DIM)
            q = qkv[:, :, 0].transpose(0, 2, 1, 3)
            k = qkv[:, :, 1].transpose(0, 2, 1, 3)
            v = qkv[:, :, 2].transpose(0, 2, 1, 3)
            y = _stick_breaking(q, k, v).transpose(0, 2, 1, 3).reshape(bn, s, d)
            mix = y @ w_out_c[o]
        x = _post_norm(x, mix, ln_g[layer, 0], ln_b[layer, 0])
        cross = _memory_cross_attention(x, mem, mem_wq[layer], mem_wk[layer], mem_wv[layer], mem_wo[layer])
        x = _post_norm(x, cross, ln_g[layer, 1], ln_b[layer, 1])
        x = _post_norm(x, _swiglu(x, ffn_w1[layer], ffn_w3[layer], ffn_w2[layer]), ln_g[layer, 2], ln_b[layer, 2])
    return x


import jax as _jax
import jax.numpy as _jnp

TWIN_FORMAT = 'train_step'
FWD_PARAMS = ['x', 'mem', 'w_in_ab', 'gmlp_ln_g', 'gmlp_ln_b', 'gmlp_w_s', 'gmlp_b_s', 'conv_w', 'conv_b', 'conv_gn_g', 'conv_gn_b', 'w_out_ab', 'w_qkv_c', 'w_out_c', 'mem_wq', 'mem_wk', 'mem_wv', 'mem_wo', 'ffn_w1', 'ffn_w3', 'ffn_w2', 'ln_g', 'ln_b']
TWIN_WEIGHTS = ['w_in_ab', 'gmlp_ln_g', 'gmlp_ln_b', 'gmlp_w_s', 'gmlp_b_s', 'conv_w', 'conv_b', 'conv_gn_g', 'conv_gn_b', 'w_out_ab', 'w_qkv_c', 'w_out_c', 'mem_wq', 'mem_wk', 'mem_wv', 'mem_wo', 'ffn_w1', 'ffn_w3', 'ffn_w2', 'ln_g', 'ln_b']
TWIN_DIFF_INPUT = 'x'
TWIN_INPUTS = ['x', 'mem', 'w_in_ab', 'gmlp_ln_g', 'gmlp_ln_b', 'gmlp_w_s', 'gmlp_b_s', 'conv_w', 'conv_b', 'conv_gn_g', 'conv_gn_b', 'w_out_ab', 'w_qkv_c', 'w_out_c', 'mem_wq', 'mem_wk', 'mem_wv', 'mem_wo', 'ffn_w1', 'ffn_w3', 'ffn_w2', 'ln_g', 'ln_b', 'loss_target', 'm_w_in_ab', 'm_gmlp_ln_g', 'm_gmlp_ln_b', 'm_gmlp_w_s', 'm_gmlp_b_s', 'm_conv_w', 'm_conv_b', 'm_conv_gn_g', 'm_conv_gn_b', 'm_w_out_ab', 'm_w_qkv_c', 'm_w_out_c', 'm_mem_wq', 'm_mem_wk', 'm_mem_wv', 'm_mem_wo', 'm_ffn_w1', 'm_ffn_w3', 'm_ffn_w2', 'm_ln_g', 'm_ln_b', 'v_w_in_ab', 'v_gmlp_ln_g', 'v_gmlp_ln_b', 'v_gmlp_w_s', 'v_gmlp_b_s', 'v_conv_w', 'v_conv_b', 'v_conv_gn_g', 'v_conv_gn_b', 'v_w_out_ab', 'v_w_qkv_c', 'v_w_out_c', 'v_mem_wq', 'v_mem_wk', 'v_mem_wv', 'v_mem_wo', 'v_ffn_w1', 'v_ffn_w3', 'v_ffn_w2', 'v_ln_g', 'v_ln_b']
TWIN_OUTPUTS = ['loss', 'grad_x', 'grad_w_in_ab', 'grad_gmlp_ln_g', 'grad_gmlp_ln_b', 'grad_gmlp_w_s', 'grad_gmlp_b_s', 'grad_conv_w', 'grad_conv_b', 'grad_conv_gn_g', 'grad_conv_gn_b', 'grad_w_out_ab', 'grad_w_qkv_c', 'grad_w_out_c', 'grad_mem_wq', 'grad_mem_wk', 'grad_mem_wv', 'grad_mem_wo', 'grad_ffn_w1', 'grad_ffn_w3', 'grad_ffn_w2', 'grad_ln_g', 'grad_ln_b', 'delta_w_in_ab', 'delta_gmlp_ln_g', 'delta_gmlp_ln_b', 'delta_gmlp_w_s', 'delta_gmlp_b_s', 'delta_conv_w', 'delta_conv_b', 'delta_conv_gn_g', 'delta_conv_gn_b', 'delta_w_out_ab', 'delta_w_qkv_c', 'delta_w_out_c', 'delta_mem_wq', 'delta_mem_wk', 'delta_mem_wv', 'delta_mem_wo', 'delta_ffn_w1', 'delta_ffn_w3', 'delta_ffn_w2', 'delta_ln_g', 'delta_ln_b', 'new_m_w_in_ab', 'new_m_gmlp_ln_g', 'new_m_gmlp_ln_b', 'new_m_gmlp_w_s', 'new_m_gmlp_b_s', 'new_m_conv_w', 'new_m_conv_b', 'new_m_conv_gn_g', 'new_m_conv_gn_b', 'new_m_w_out_ab', 'new_m_w_qkv_c', 'new_m_w_out_c', 'new_m_mem_wq', 'new_m_mem_wk', 'new_m_mem_wv', 'new_m_mem_wo', 'new_m_ffn_w1', 'new_m_ffn_w3', 'new_m_ffn_w2', 'new_m_ln_g', 'new_m_ln_b', 'new_v_w_in_ab', 'new_v_gmlp_ln_g', 'new_v_gmlp_ln_b', 'new_v_gmlp_w_s', 'new_v_gmlp_b_s', 'new_v_conv_w', 'new_v_conv_b', 'new_v_conv_gn_g', 'new_v_conv_gn_b', 'new_v_w_out_ab', 'new_v_w_qkv_c', 'new_v_w_out_c', 'new_v_mem_wq', 'new_v_mem_wk', 'new_v_mem_wv', 'new_v_mem_wo', 'new_v_ffn_w1', 'new_v_ffn_w3', 'new_v_ffn_w2', 'new_v_ln_g', 'new_v_ln_b']
TWIN_LEAF_KINDS = {'loss': 'loss', 'grad_x': 'grad_x', 'grad_w_in_ab': 'grad_w', 'grad_gmlp_ln_g': 'grad_w', 'grad_gmlp_ln_b': 'grad_w', 'grad_gmlp_w_s': 'grad_w', 'grad_gmlp_b_s': 'grad_w', 'grad_conv_w': 'grad_w', 'grad_conv_b': 'grad_w', 'grad_conv_gn_g': 'grad_w', 'grad_conv_gn_b': 'grad_w', 'grad_w_out_ab': 'grad_w', 'grad_w_qkv_c': 'grad_w', 'grad_w_out_c': 'grad_w', 'grad_mem_wq': 'grad_w', 'grad_mem_wk': 'grad_w', 'grad_mem_wv': 'grad_w', 'grad_mem_wo': 'grad_w', 'grad_ffn_w1': 'grad_w', 'grad_ffn_w3': 'grad_w', 'grad_ffn_w2': 'grad_w', 'grad_ln_g': 'grad_w', 'grad_ln_b': 'grad_w', 'delta_w_in_ab': 'delta_w', 'delta_gmlp_ln_g': 'delta_w', 'delta_gmlp_ln_b': 'delta_w', 'delta_gmlp_w_s': 'delta_w', 'delta_gmlp_b_s': 'delta_w', 'delta_conv_w': 'delta_w', 'delta_conv_b': 'delta_w', 'delta_conv_gn_g': 'delta_w', 'delta_conv_gn_b': 'delta_w', 'delta_w_out_ab': 'delta_w', 'delta_w_qkv_c': 'delta_w', 'delta_w_out_c': 'delta_w', 'delta_mem_wq': 'delta_w', 'delta_mem_wk': 'delta_w', 'delta_mem_wv': 'delta_w', 'delta_mem_wo': 'delta_w', 'delta_ffn_w1': 'delta_w', 'delta_ffn_w3': 'delta_w', 'delta_ffn_w2': 'delta_w', 'delta_ln_g': 'delta_w', 'delta_ln_b': 'delta_w', 'new_m_w_in_ab': 'new_m', 'new_m_gmlp_ln_g': 'new_m', 'new_m_gmlp_ln_b': 'new_m', 'new_m_gmlp_w_s': 'new_m', 'new_m_gmlp_b_s': 'new_m', 'new_m_conv_w': 'new_m', 'new_m_conv_b': 'new_m', 'new_m_conv_gn_g': 'new_m', 'new_m_conv_gn_b': 'new_m', 'new_m_w_out_ab': 'new_m', 'new_m_w_qkv_c': 'new_m', 'new_m_w_out_c': 'new_m', 'new_m_mem_wq': 'new_m', 'new_m_mem_wk': 'new_m', 'new_m_mem_wv': 'new_m', 'new_m_mem_wo': 'new_m', 'new_m_ffn_w1': 'new_m', 'new_m_ffn_w3': 'new_m', 'new_m_ffn_w2': 'new_m', 'new_m_ln_g': 'new_m', 'new_m_ln_b': 'new_m', 'new_v_w_in_ab': 'new_v', 'new_v_gmlp_ln_g': 'new_v', 'new_v_gmlp_ln_b': 'new_v', 'new_v_gmlp_w_s': 'new_v', 'new_v_gmlp_b_s': 'new_v', 'new_v_conv_w': 'new_v', 'new_v_conv_b': 'new_v', 'new_v_conv_gn_g': 'new_v', 'new_v_conv_gn_b': 'new_v', 'new_v_w_out_ab': 'new_v', 'new_v_w_qkv_c': 'new_v', 'new_v_w_out_c': 'new_v', 'new_v_mem_wq': 'new_v', 'new_v_mem_wk': 'new_v', 'new_v_mem_wv': 'new_v', 'new_v_mem_wo': 'new_v', 'new_v_ffn_w1': 'new_v', 'new_v_ffn_w3': 'new_v', 'new_v_ffn_w2': 'new_v', 'new_v_ln_g': 'new_v', 'new_v_ln_b': 'new_v'}


def _forward(args):
    return _fwd_reference(*[args[k] for k in FWD_PARAMS])


def _output_shape():
    out = _jax.eval_shape(lambda: _forward(_fwd_setup_inputs(0)))
    return out.shape, out.dtype

N_MICROBATCH = 1
ADAM_LR = 0.001
ADAM_B1 = 0.9
ADAM_B2 = 0.999
ADAM_EPS = 1e-08
ADAM_WD = 0.01
ADAM_STEP = 10
PER_EXAMPLE_BATCH_AXIS = {'x': 0, 'mem': 0, 'loss_target': 0}
SHARED_INPUTS = []
_WEIGHT_DTYPES = {'w_in_ab': _jnp.float32, 'gmlp_ln_g': _jnp.float32, 'gmlp_ln_b': _jnp.float32, 'gmlp_w_s': _jnp.float32, 'gmlp_b_s': _jnp.float32, 'conv_w': _jnp.float32, 'conv_b': _jnp.float32, 'conv_gn_g': _jnp.float32, 'conv_gn_b': _jnp.float32, 'w_out_ab': _jnp.float32, 'w_qkv_c': _jnp.float32, 'w_out_c': _jnp.float32, 'mem_wq': _jnp.float32, 'mem_wk': _jnp.float32, 'mem_wv': _jnp.float32, 'mem_wo': _jnp.float32, 'ffn_w1': _jnp.float32, 'ffn_w3': _jnp.float32, 'ffn_w2': _jnp.float32, 'ln_g': _jnp.float32, 'ln_b': _jnp.float32}
MOMENT_SCALE = {'w_in_ab': 3.839268e-02, 'gmlp_ln_g': 3.253844e-02, 'gmlp_ln_b': 2.895471e-02, 'gmlp_w_s': 2.987073e-02, 'gmlp_b_s': 4.196844e-02, 'conv_w': 4.091580e-02, 'conv_b': 1.320920e-01, 'conv_gn_g': 6.599013e-02, 'conv_gn_b': 8.853349e-02, 'w_out_ab': 1.373873e-01, 'w_qkv_c': 2.916799e-02, 'w_out_c': 1.029260e-01, 'mem_wq': 6.816824e-03, 'mem_wk': 6.835454e-03, 'mem_wv': 8.178561e-03, 'mem_wo': 1.928879e-02, 'ffn_w1': 2.356109e-02, 'ffn_w3': 2.290581e-02, 'ffn_w2': 9.054763e-02, 'ln_g': 1.858510e+01, 'ln_b': 1.874494e+00}


def _to_microbatches(a, axis):
    t = _jnp.moveaxis(a, axis, 0)
    t = t.reshape((N_MICROBATCH, t.shape[0] // N_MICROBATCH) + t.shape[1:])
    return _jnp.moveaxis(t, 1, axis + 1)


def setup_inputs(seed: int = 0) -> dict:
    inp = _fwd_setup_inputs(seed)
    key = _jax.random.fold_in(_jax.random.key(seed), 7919)
    shape, _ = _output_shape()
    out = dict(inp)
    out["loss_target"] = _jax.random.normal(_jax.random.fold_in(key, 0), shape, _jnp.float32)
    for i, name in enumerate(TWIN_WEIGHTS):
        w = inp[name].astype(_jnp.float32)
        if MOMENT_SCALE is None:
            s = _jnp.sqrt(_jnp.mean(_jnp.square(w)) + 1e-30)
        else:
            s = MOMENT_SCALE[name]
        km, kv = _jax.random.split(_jax.random.fold_in(key, i + 1))
        out[name] = w
        out["m_" + name] = s * _jax.random.normal(km, w.shape, _jnp.float32)
        out["v_" + name] = (s * s) * _jax.random.uniform(kv, w.shape, _jnp.float32, 0.5, 1.5)
    if N_MICROBATCH > 1:
        for name, axis in PER_EXAMPLE_BATCH_AXIS.items():
            out[name] = _to_microbatches(out[name], axis)
    return {'x': out['x'], 'mem': out['mem'], 'w_in_ab': out['w_in_ab'], 'gmlp_ln_g': out['gmlp_ln_g'], 'gmlp_ln_b': out['gmlp_ln_b'], 'gmlp_w_s': out['gmlp_w_s'], 'gmlp_b_s': out['gmlp_b_s'], 'conv_w': out['conv_w'], 'conv_b': out['conv_b'], 'conv_gn_g': out['conv_gn_g'], 'conv_gn_b': out['conv_gn_b'], 'w_out_ab': out['w_out_ab'], 'w_qkv_c': out['w_qkv_c'], 'w_out_c': out['w_out_c'], 'mem_wq': out['mem_wq'], 'mem_wk': out['mem_wk'], 'mem_wv': out['mem_wv'], 'mem_wo': out['mem_wo'], 'ffn_w1': out['ffn_w1'], 'ffn_w3': out['ffn_w3'], 'ffn_w2': out['ffn_w2'], 'ln_g': out['ln_g'], 'ln_b': out['ln_b'], 'loss_target': out['loss_target'], 'm_w_in_ab': out['m_w_in_ab'], 'm_gmlp_ln_g': out['m_gmlp_ln_g'], 'm_gmlp_ln_b': out['m_gmlp_ln_b'], 'm_gmlp_w_s': out['m_gmlp_w_s'], 'm_gmlp_b_s': out['m_gmlp_b_s'], 'm_conv_w': out['m_conv_w'], 'm_conv_b': out['m_conv_b'], 'm_conv_gn_g': out['m_conv_gn_g'], 'm_conv_gn_b': out['m_conv_gn_b'], 'm_w_out_ab': out['m_w_out_ab'], 'm_w_qkv_c': out['m_w_qkv_c'], 'm_w_out_c': out['m_w_out_c'], 'm_mem_wq': out['m_mem_wq'], 'm_mem_wk': out['m_mem_wk'], 'm_mem_wv': out['m_mem_wv'], 'm_mem_wo': out['m_mem_wo'], 'm_ffn_w1': out['m_ffn_w1'], 'm_ffn_w3': out['m_ffn_w3'], 'm_ffn_w2': out['m_ffn_w2'], 'm_ln_g': out['m_ln_g'], 'm_ln_b': out['m_ln_b'], 'v_w_in_ab': out['v_w_in_ab'], 'v_gmlp_ln_g': out['v_gmlp_ln_g'], 'v_gmlp_ln_b': out['v_gmlp_ln_b'], 'v_gmlp_w_s': out['v_gmlp_w_s'], 'v_gmlp_b_s': out['v_gmlp_b_s'], 'v_conv_w': out['v_conv_w'], 'v_conv_b': out['v_conv_b'], 'v_conv_gn_g': out['v_conv_gn_g'], 'v_conv_gn_b': out['v_conv_gn_b'], 'v_w_out_ab': out['v_w_out_ab'], 'v_w_qkv_c': out['v_w_qkv_c'], 'v_w_out_c': out['v_w_out_c'], 'v_mem_wq': out['v_mem_wq'], 'v_mem_wk': out['v_mem_wk'], 'v_mem_wv': out['v_mem_wv'], 'v_mem_wo': out['v_mem_wo'], 'v_ffn_w1': out['v_ffn_w1'], 'v_ffn_w3': out['v_ffn_w3'], 'v_ffn_w2': out['v_ffn_w2'], 'v_ln_g': out['v_ln_g'], 'v_ln_b': out['v_ln_b']}


def _loss(weights, diff, rest, loss_target):
    with _jax.named_scope("forward"):
        args = {**rest, TWIN_DIFF_INPUT: diff, **{k: w.astype(_WEIGHT_DTYPES[k]) for k, w in weights.items()}}
        y = _forward(args)
    with _jax.named_scope("loss_head"):
        err = _jnp.square(y.astype(_jnp.float32) - loss_target)
        return 0.5 * _jnp.sum(_jnp.mean(err, axis=-1)) if err.ndim else 0.5 * err


def _adamw(w, g, m, v):
    m = ADAM_B1 * m + (1.0 - ADAM_B1) * g
    v = ADAM_B2 * v + (1.0 - ADAM_B2) * _jnp.square(g)
    m_hat = m / (1.0 - ADAM_B1 ** ADAM_STEP)
    v_hat = v / (1.0 - ADAM_B2 ** ADAM_STEP)
    delta = -ADAM_LR * (m_hat / (_jnp.sqrt(v_hat) + ADAM_EPS) + ADAM_WD * w)
    return delta, m, v


def reference(x, mem, w_in_ab, gmlp_ln_g, gmlp_ln_b, gmlp_w_s, gmlp_b_s, conv_w, conv_b, conv_gn_g, conv_gn_b, w_out_ab, w_qkv_c, w_out_c, mem_wq, mem_wk, mem_wv, mem_wo, ffn_w1, ffn_w3, ffn_w2, ln_g, ln_b, loss_target, m_w_in_ab, m_gmlp_ln_g, m_gmlp_ln_b, m_gmlp_w_s, m_gmlp_b_s, m_conv_w, m_conv_b, m_conv_gn_g, m_conv_gn_b, m_w_out_ab, m_w_qkv_c, m_w_out_c, m_mem_wq, m_mem_wk, m_mem_wv, m_mem_wo, m_ffn_w1, m_ffn_w3, m_ffn_w2, m_ln_g, m_ln_b, v_w_in_ab, v_gmlp_ln_g, v_gmlp_ln_b, v_gmlp_w_s, v_gmlp_b_s, v_conv_w, v_conv_b, v_conv_gn_g, v_conv_gn_b, v_w_out_ab, v_w_qkv_c, v_w_out_c, v_mem_wq, v_mem_wk, v_mem_wv, v_mem_wo, v_ffn_w1, v_ffn_w3, v_ffn_w2, v_ln_g, v_ln_b):
    given = dict(x=x, mem=mem, w_in_ab=w_in_ab, gmlp_ln_g=gmlp_ln_g, gmlp_ln_b=gmlp_ln_b, gmlp_w_s=gmlp_w_s, gmlp_b_s=gmlp_b_s, conv_w=conv_w, conv_b=conv_b, conv_gn_g=conv_gn_g, conv_gn_b=conv_gn_b, w_out_ab=w_out_ab, w_qkv_c=w_qkv_c, w_out_c=w_out_c, mem_wq=mem_wq, mem_wk=mem_wk, mem_wv=mem_wv, mem_wo=mem_wo, ffn_w1=ffn_w1, ffn_w3=ffn_w3, ffn_w2=ffn_w2, ln_g=ln_g, ln_b=ln_b, loss_target=loss_target, m_w_in_ab=m_w_in_ab, m_gmlp_ln_g=m_gmlp_ln_g, m_gmlp_ln_b=m_gmlp_ln_b, m_gmlp_w_s=m_gmlp_w_s, m_gmlp_b_s=m_gmlp_b_s, m_conv_w=m_conv_w, m_conv_b=m_conv_b, m_conv_gn_g=m_conv_gn_g, m_conv_gn_b=m_conv_gn_b, m_w_out_ab=m_w_out_ab, m_w_qkv_c=m_w_qkv_c, m_w_out_c=m_w_out_c, m_mem_wq=m_mem_wq, m_mem_wk=m_mem_wk, m_mem_wv=m_mem_wv, m_mem_wo=m_mem_wo, m_ffn_w1=m_ffn_w1, m_ffn_w3=m_ffn_w3, m_ffn_w2=m_ffn_w2, m_ln_g=m_ln_g, m_ln_b=m_ln_b, v_w_in_ab=v_w_in_ab, v_gmlp_ln_g=v_gmlp_ln_g, v_gmlp_ln_b=v_gmlp_ln_b, v_gmlp_w_s=v_gmlp_w_s, v_gmlp_b_s=v_gmlp_b_s, v_conv_w=v_conv_w, v_conv_b=v_conv_b, v_conv_gn_g=v_conv_gn_g, v_conv_gn_b=v_conv_gn_b, v_w_out_ab=v_w_out_ab, v_w_qkv_c=v_w_qkv_c, v_w_out_c=v_w_out_c, v_mem_wq=v_mem_wq, v_mem_wk=v_mem_wk, v_mem_wv=v_mem_wv, v_mem_wo=v_mem_wo, v_ffn_w1=v_ffn_w1, v_ffn_w3=v_ffn_w3, v_ffn_w2=v_ffn_w2, v_ln_g=v_ln_g, v_ln_b=v_ln_b)
    weights = {n: given[n] for n in TWIN_WEIGHTS}
    shared = {n: given[n] for n in SHARED_INPUTS}
    per_example = {n: given[n] for n in ['x', 'mem']}
    grad_fn = _jax.value_and_grad(_loss, argnums=(0, 1))

    def one_microbatch(ex, loss_target):
        ex = dict(ex)
        diff = ex.pop(TWIN_DIFF_INPUT)
        return grad_fn(weights, diff, {**shared, **ex}, loss_target)

    if N_MICROBATCH == 1:
        loss, (grad_w, grad_x) = one_microbatch(per_example, given["loss_target"])
    else:
        def body(carry, xs):
            loss_sum, grad_sum = carry
            l_k, (gw_k, gx_k) = one_microbatch(xs[0], xs[1])
            with _jax.named_scope("update"):
                return (loss_sum + l_k, _jax.tree.map(_jnp.add, grad_sum, gw_k)), gx_k

        init = (_jnp.zeros((), _jnp.float32), _jax.tree.map(_jnp.zeros_like, weights))
        (loss, grad_w), grad_x = _jax.lax.scan(body, init, (per_example, given["loss_target"]))
    with _jax.named_scope("update"):
        delta_w, new_m, new_v = {}, {}, {}
        for n in TWIN_WEIGHTS:
            delta_w[n], new_m[n], new_v[n] = _adamw(weights[n], grad_w[n], given["m_" + n], given["v_" + n])
    return (loss, grad_x, *[grad_w[n] for n in TWIN_WEIGHTS], *[delta_w[n] for n in TWIN_WEIGHTS],
            *[new_m[n] for n in TWIN_WEIGHTS], *[new_v[n] for n in TWIN_WEIGHTS])
```

```python
import functools
import math

import jax
import jax.numpy as jnp
from jax import lax
from jax.experimental import pallas as pl
from jax.experimental.pallas import tpu as pltpu

F32 = jnp.float32
MM = jnp.bfloat16
ACT = jnp.bfloat16
MESH = pl.DeviceIdType.MESH

DEPTH = 4
CHUNK = 128
CONV_WIDTH = 31
HALO = 32
MEM_HEADS = 4
C_HEAD_DIM = 64
ALPHA = (2.0 * DEPTH) ** 0.25
LN_EPS = 1e-5
ADAM_LR, ADAM_B1, ADAM_B2, ADAM_EPS, ADAM_WD, ADAM_STEP = 0.001, 0.9, 0.999, 1e-08, 0.01, 10

VMEM_CAP_V7X = 64 * 1024 * 1024
VMEM_MAX_REQUEST = 56 * 1024 * 1024
N_CHIPS = 4
N_DEV = 8


def _tile(n, pref):
    if n <= pref:
        return n
    for t in range(pref - pref % 8, 7, -8):
        if n % t == 0:
            return t
    return n


def _nbytes(shape, dtype):
    return math.prod(1 if s is None else s for s in shape) * jnp.dtype(dtype).itemsize


def _vmem_limit(block_bytes, scratch_bytes=0, temp_bytes=0):
    est = 2 * block_bytes + scratch_bytes + temp_bytes
    return int(min(VMEM_MAX_REQUEST, max(16 * 1024 * 1024, est * 5 // 4)))


def _params(sem, block_bytes, scratch_bytes=0, temp_bytes=0):
    return pltpu.CompilerParams(dimension_semantics=sem,
                                vmem_limit_bytes=_vmem_limit(block_bytes, scratch_bytes, temp_bytes))


_DIMS = {"nn": (((1,), (0,)), ((), ())), "nt": (((1,), (1,)), ((), ())), "tn": (((0,), (0,)), ((), ()))}


def _mm(name, mode, grid, a, a_spec, bs, b_specs, outs, out_specs, acc_shape,
        extras=(), extra_specs=(), epilogue=None):
    nb, ne, no = len(bs), len(extras), len(outs)
    nk = grid[2]

    def body(*refs):
        a_ref = refs[0]
        b_refs = refs[1:1 + nb]
        e_refs = refs[1 + nb:1 + nb + ne]
        o_refs = refs[1 + nb + ne:1 + nb + ne + no]
        accs = refs[1 + nb + ne + no:]
        k = pl.program_id(2)

        @pl.when(k == 0)
        def _():
            for acc in accs:
                acc[...] = jnp.zeros_like(acc)

        av = a_ref[...].astype(MM)
        for b_ref, acc in zip(b_refs, accs):
            acc[...] += lax.dot_general(av, b_ref[...].astype(MM), _DIMS[mode], preferred_element_type=F32)

        @pl.when(k == nk - 1)
        def _():
            vals = [acc[...] for acc in accs]
            if epilogue is not None:
                vals = epilogue(vals, [e[...].astype(F32) for e in e_refs])
            for o, v in zip(o_refs, vals):
                o[...] = v.astype(o.dtype)

    blocks = (_nbytes(a_spec.block_shape, a.dtype)
              + sum(_nbytes(s.block_shape, b.dtype) for s, b in zip(b_specs, bs))
              + sum(_nbytes(s.block_shape, e.dtype) for s, e in zip(extra_specs, extras))
              + sum(_nbytes(s.block_shape, o.dtype) for s, o in zip(out_specs, outs)))
    acc_bytes = nb * _nbytes(acc_shape, F32)
    res = pl.pallas_call(
        body, name=name, grid=grid,
        in_specs=[a_spec, *b_specs, *extra_specs], out_specs=list(out_specs), out_shape=list(outs),
        scratch_shapes=[pltpu.VMEM(acc_shape, F32)] * nb,
        compiler_params=_params(("parallel", "parallel", "arbitrary"), blocks, acc_bytes, 4 * acc_bytes),
    )(a, *bs, *extras)
    return res


def _sds(shape, dtype):
    return jax.ShapeDtypeStruct(shape, dtype)


def _dense(name, a, w, out_dtype, extras=(), epilogue=None, n_out=None):
    t, kdim = a.shape
    n = w.shape[1]
    tm, tn, tk = _tile(t, 1024), _tile(n, 512), _tile(kdim, 1024)
    grid = (t // tm, n // tn, kdim // tk)
    return _mm(name, "nn", grid, a, pl.BlockSpec((tm, tk), lambda i, j, k: (i, k)),
               [w], [pl.BlockSpec((tk, tn), lambda i, j, k: (k, j))],
               [_sds((t, n), out_dtype)], [pl.BlockSpec((tm, tn), lambda i, j, k: (i, j))], (tm, tn),
               extras=extras, extra_specs=[pl.BlockSpec((tm, tn), lambda i, j, k: (i, j))] * len(extras),
               epilogue=epilogue)[0]


def _dense_nt(name, a, w, out_dtype, extras=(), epilogue=None):
    t, n = a.shape
    kout = w.shape[0]
    tm, tn, tk = _tile(t, 1024), _tile(kout, 512), _tile(n, 1024)
    grid = (t // tm, kout // tn, n // tk)
    return _mm(name, "nt", grid, a, pl.BlockSpec((tm, tk), lambda i, j, k: (i, k)),
               [w], [pl.BlockSpec((tn, tk), lambda i, j, k: (j, k))],
               [_sds((t, kout), out_dtype)], [pl.BlockSpec((tm, tn), lambda i, j, k: (i, j))], (tm, tn),
               extras=extras, extra_specs=[pl.BlockSpec((tm, tn), lambda i, j, k: (i, j))] * len(extras),
               epilogue=epilogue)[0]


def _dense_tn(name, a, b, out_dtype=MM):
    t, m = a.shape
    n = b.shape[1]
    tm, tn, tk = _tile(m, 512), _tile(n, 512), _tile(t, 1024)
    grid = (m // tm, n // tn, t // tk)
    return _mm(name, "tn", grid, a, pl.BlockSpec((tk, tm), lambda i, j, k: (k, i)),
               [b], [pl.BlockSpec((tk, tn), lambda i, j, k: (k, j))],
               [_sds((m, n), out_dtype)], [pl.BlockSpec((tm, tn), lambda i, j, k: (i, j))], (tm, tn))[0]


_INV_SQRT2 = 0.7071067811865476
_INV_SQRT_2PI = 0.3989422804014327


def _gelu(x):
    return 0.5 * x * (1.0 + lax.erf(x * _INV_SQRT2))


def _gelu_grad(x):
    return 0.5 * (1.0 + lax.erf(x * _INV_SQRT2)) + x * jnp.exp(-0.5 * x * x) * _INV_SQRT_2PI


def _sigmoid(x):
    return 1.0 / (1.0 + jnp.exp(-x))


def _softplus(x):
    return jnp.maximum(x, 0.0) + jnp.log(1.0 + jnp.exp(-jnp.abs(x)))


def _norm_stats(x):
    mu = jnp.mean(x, axis=-1, keepdims=True)
    xc = x - mu
    var = jnp.mean(xc * xc, axis=-1, keepdims=True)
    rstd = lax.rsqrt(var + LN_EPS)
    return xc * rstd, rstd


def _norm_bwd(dy_g, xh, rstd):
    m1 = jnp.mean(dy_g, axis=-1, keepdims=True)
    m2 = jnp.mean(dy_g * xh, axis=-1, keepdims=True)
    return rstd * (dy_g - m1 - xh * m2)


def _rows8(x):
    r, c = x.shape
    return jnp.sum(x.reshape(r // 8, 8, c), axis=0)


def _ln_fwd(name, xh_prev, g_prev, b_prev, f, g, b):
    t, d = f.shape
    tm = _tile(t, 512)

    def body(xp_ref, gp_ref, bp_ref, f_ref, g_ref, b_ref, xh_ref, y_ref, rstd_ref):
        r = ALPHA * (xp_ref[...] * gp_ref[...] + bp_ref[...]) + f_ref[...]
        xh, rstd = _norm_stats(r)
        xh_ref[...] = xh
        y_ref[...] = (xh * g_ref[...] + b_ref[...]).astype(y_ref.dtype)
        rstd_ref[...] = rstd

    row = pl.BlockSpec((tm, d), lambda i: (i, 0))
    vec = pl.BlockSpec((1, d), lambda i: (0, 0))
    return pl.pallas_call(
        body, name=name, grid=(t // tm,),
        in_specs=[row, vec, vec, row, vec, vec],
        out_specs=[row, row, pl.BlockSpec((tm, 1), lambda i: (i, 0))],
        out_shape=[_sds((t, d), F32), _sds((t, d), ACT), _sds((t, 1), F32)],
        compiler_params=_params(("parallel",), 4 * tm * d * 4, 0, 4 * tm * d * 4),
    )(xh_prev, g_prev, b_prev, f, g, b)


def _ln_bwd(name, dy, xh, rstd, g):
    t, d = dy.shape
    tm = _tile(t, 512)
    n = t // tm

    def body(dy_ref, xh_ref, rstd_ref, g_ref, dr_ref, dra_ref, dg_ref, db_ref, dg_acc, db_acc):
        i = pl.program_id(0)

        @pl.when(i == 0)
        def _():
            dg_acc[...] = jnp.zeros_like(dg_acc)
            db_acc[...] = jnp.zeros_like(db_acc)

        dyv = dy_ref[...]
        xhv = xh_ref[...]
        dr = _norm_bwd(dyv * g_ref[...], xhv, rstd_ref[...])
        dr_ref[...] = dr
        dra_ref[...] = dr.astype(dra_ref.dtype)
        dg_acc[...] += _rows8(dyv * xhv)
        db_acc[...] += _rows8(dyv)

        @pl.when(i == n - 1)
        def _():
            dg_ref[...] = jnp.sum(dg_acc[...], axis=0, keepdims=True)
            db_ref[...] = jnp.sum(db_acc[...], axis=0, keepdims=True)

    row = pl.BlockSpec((tm, d), lambda i: (i, 0))
    vec = pl.BlockSpec((1, d), lambda i: (0, 0))
    return pl.pallas_call(
        body, name=name, grid=(n,),
        in_specs=[row, row, pl.BlockSpec((tm, 1), lambda i: (i, 0)), vec],
        out_specs=[row, row, vec, vec],
        out_shape=[_sds((t, d), F32), _sds((t, d), ACT), _sds((1, d), F32), _sds((1, d), F32)],
        scratch_shapes=[pltpu.VMEM((8, d), F32), pltpu.VMEM((8, d), F32)],
        compiler_params=_params(("arbitrary",), 4 * tm * d * 4, 0, 4 * tm * d * 4),
    )(dy, xh, rstd, g)


def _loss_head(xh, g, b, target):
    t, d = xh.shape
    tm = _tile(t, 512)
    n = t // tm

    def body(xh_ref, g_ref, b_ref, tg_ref, dy_ref, loss_ref, acc):
        i = pl.program_id(0)

        @pl.when(i == 0)
        def _():
            acc[...] = jnp.zeros_like(acc)

        err = xh_ref[...] * g_ref[...] + b_ref[...] - tg_ref[...]
        dy_ref[...] = err * (1.0 / d)
        acc[...] += _rows8(err * err)

        @pl.when(i == n - 1)
        def _():
            s = jnp.sum(jnp.sum(acc[...], axis=0, keepdims=True), axis=1, keepdims=True)
            loss_ref[...] = jnp.broadcast_to(s * (0.5 / d), loss_ref.shape)

    row = pl.BlockSpec((tm, d), lambda i: (i, 0))
    vec = pl.BlockSpec((1, d), lambda i: (0, 0))
    dy, loss = pl.pallas_call(
        body, name="loss_head", grid=(n,),
        in_specs=[row, vec, vec, row],
        out_specs=[row, pl.BlockSpec((8, 128), lambda i: (0, 0))],
        out_shape=[_sds((t, d), F32), _sds((8, 128), F32)],
        scratch_shapes=[pltpu.VMEM((8, d), F32)],
        compiler_params=_params(("arbitrary",), 3 * tm * d * 4, 0, 2 * tm * d * 4),
    )(xh, g, b, target)
    return dy, loss[0, 0]


def _causal_w(w):
    r = lax.broadcasted_iota(jnp.int32, w.shape, 0)
    c = lax.broadcasted_iota(jnp.int32, w.shape, 1)
    return jnp.where(r >= c, w, 0.0)


def _gmlp_fwd(name, h, ln_g, ln_b, w_s, b_s_col):
    t = h.shape[0]
    tt = _tile(t, 256)
    wd = 4 * CHUNK

    def body(u_ref, v_ref, g_ref, b_ref, w_ref, bs_ref, ya_ref):
        for gi in range(4):
            ln = slice(gi * CHUNK, (gi + 1) * CHUNK)
            u = _gelu(u_ref[:, ln].astype(F32))
            v = _gelu(v_ref[:, ln].astype(F32))
            xh, _ = _norm_stats(v)
            vg = (xh * g_ref[:, ln] + b_ref[:, ln]).astype(MM)
            w = _causal_w(w_ref[gi]).astype(MM)
            for c in range(tt // CHUNK):
                rs = slice(c * CHUNK, (c + 1) * CHUNK)
                mixed = jnp.dot(w, vg[rs], preferred_element_type=F32) + bs_ref[gi]
                ya_ref[rs, ln] = (u[rs] * mixed).astype(ya_ref.dtype)

    vec = pl.BlockSpec((1, wd), lambda i: (0, 0))
    return pl.pallas_call(
        body, name=name, grid=(t // tt,),
        in_specs=[pl.BlockSpec((tt, wd), lambda i: (i, 0)), pl.BlockSpec((tt, wd), lambda i: (i, 1)), vec, vec,
                  pl.BlockSpec((4, CHUNK, CHUNK), lambda i: (0, 0, 0)), pl.BlockSpec((4, CHUNK, 1), lambda i: (0, 0, 0))],
        out_specs=pl.BlockSpec((tt, wd), lambda i: (i, 0)),
        out_shape=_sds((t, wd), ACT),
        compiler_params=_params(("parallel",), 3 * tt * wd * 4, 0, 8 * tt * CHUNK * 4),
    )(h, h, ln_g, ln_b, w_s, b_s_col)


def _gmlp_bwd(name, h, dyab, ln_g, ln_b, w_s, b_s_col):
    t = h.shape[0]
    tt = _tile(t, 256)
    n = t // tt
    wd = 4 * CHUNK

    def body(u_ref, v_ref, dy_ref, g_ref, b_ref, w_ref, bs_ref, duv_ref, dg_ref, db_ref, dw_ref, dbs_ref,
             dg_acc, db_acc):
        i = pl.program_id(0)

        @pl.when(i == 0)
        def _():
            dg_acc[...] = jnp.zeros_like(dg_acc)
            db_acc[...] = jnp.zeros_like(db_acc)
            dw_ref[...] = jnp.zeros_like(dw_ref)
            dbs_ref[...] = jnp.zeros_like(dbs_ref)

        for gi in range(4):
            ln = slice(gi * CHUNK, (gi + 1) * CHUNK)
            upre = u_ref[:, ln].astype(F32)
            vpre = v_ref[:, ln].astype(F32)
            u = _gelu(upre)
            v = _gelu(vpre)
            xh, rstd = _norm_stats(v)
            gv = g_ref[:, ln]
            vg = (xh * gv + b_ref[:, ln]).astype(MM)
            w = _causal_w(w_ref[gi]).astype(MM)
            dya = dy_ref[:, ln].astype(F32)
            dmixed = dya * u
            dmm = dmixed.astype(MM)
            dvg_parts, mixed_parts = [], []
            dw = jnp.zeros((CHUNK, CHUNK), F32)
            dbs = jnp.zeros((CHUNK, 1), F32)
            for c in range(tt // CHUNK):
                rs = slice(c * CHUNK, (c + 1) * CHUNK)
                mixed_parts.append(jnp.dot(w, vg[rs], preferred_element_type=F32) + bs_ref[gi])
                dw = dw + lax.dot_general(dmm[rs], vg[rs], _DIMS["nt"], preferred_element_type=F32)
                dbs = dbs + jnp.sum(dmixed[rs], axis=1, keepdims=True)
                dvg_parts.append(lax.dot_general(w, dmm[rs], _DIMS["tn"], preferred_element_type=F32))
            mixed = jnp.concatenate(mixed_parts, axis=0)
            dvg = jnp.concatenate(dvg_parts, axis=0)
            dw_ref[gi] += _causal_w(dw)
            dbs_ref[gi] += dbs
            dg_acc[:, ln] += _rows8(dvg * xh)
            db_acc[:, ln] += _rows8(dvg)
            dv = _norm_bwd(dvg * gv, xh, rstd) * _gelu_grad(vpre)
            du = dya * mixed * _gelu_grad(upre)
            duv_ref[:, ln] = du.astype(duv_ref.dtype)
            duv_ref[:, wd + gi * CHUNK: wd + (gi + 1) * CHUNK] = dv.astype(duv_ref.dtype)

        @pl.when(i == n - 1)
        def _():
            dg_ref[...] = jnp.sum(dg_acc[...], axis=0, keepdims=True)
            db_ref[...] = jnp.sum(db_acc[...], axis=0, keepdims=True)

    vec = pl.BlockSpec((1, wd), lambda i: (0, 0))
    wspec = pl.BlockSpec((4, CHUNK, CHUNK), lambda i: (0, 0, 0))
    bspec = pl.BlockSpec((4, CHUNK, 1), lambda i: (0, 0, 0))
    return pl.pallas_call(
        body, name=name, grid=(n,),
        in_specs=[pl.BlockSpec((tt, wd), lambda i: (i, 0)), pl.BlockSpec((tt, wd), lambda i: (i, 1)),
                  pl.BlockSpec((tt, wd), lambda i: (i, 0)), vec, vec, wspec, bspec],
        out_specs=[pl.BlockSpec((tt, 2 * wd), lambda i: (i, 0)), vec, vec, wspec, bspec],
        out_shape=[_sds((t, 2 * wd), ACT), _sds((1, wd), F32), _sds((1, wd), F32),
                   _sds((4, CHUNK, CHUNK), F32), _sds((4, CHUNK, 1), F32)],
        scratch_shapes=[pltpu.VMEM((8, wd), F32), pltpu.VMEM((8, wd), F32)],
        compiler_params=_params(("arbitrary",), 5 * tt * wd * 4, 0, 16 * tt * CHUNK * 4),
    )(h, h, dyab, ln_g, ln_b, w_s, b_s_col)


_ROWS = 256


def _conv_taps(win, cw, lo):
    acc = jnp.zeros((_ROWS, CHUNK), F32)
    for w in range(CONV_WIDTH):
        s = lo(w)
        acc = acc + cw[w:w + 1, :] * win[s:s + _ROWS, :]
    return acc


def _conv_fwd(name, h, n_ex, cw, cb, gg, gb):
    t = h.shape[0]
    s = t // n_ex
    nt = s // _ROWS

    def body(a_ref, gt_ref, cw_ref, cb_ref, gg_ref, gb_ref, yb_ref, hh):
        hh[0:HALO, :] = jnp.zeros((HALO, CHUNK), F32)
        hh[HALO:HALO + s, :] = a_ref[...].astype(F32) * _sigmoid(gt_ref[...].astype(F32))
        cwv = cw_ref[...]

        def tile(i, carry):
            r0 = pl.multiple_of(i * _ROWS, _ROWS)
            win = hh[pl.ds(r0, _ROWS + HALO), :]
            c = _conv_taps(win, cwv, lambda w: w + HALO - (CONV_WIDTH - 1)) + cb_ref[...]
            xh, _ = _norm_stats(c)
            hg = xh * gg_ref[...] + gb_ref[...]
            yb_ref[pl.ds(r0, _ROWS), :] = (hg * _sigmoid(hg)).astype(yb_ref.dtype)
            return carry

        lax.fori_loop(0, nt, tile, 0)

    vec = pl.BlockSpec((1, CHUNK), lambda g, b: (0, g))
    return pl.pallas_call(
        body, name=name, grid=(4, n_ex),
        in_specs=[pl.BlockSpec((s, CHUNK), lambda g, b: (b, 8 + g)), pl.BlockSpec((s, CHUNK), lambda g, b: (b, 12 + g)),
                  pl.BlockSpec((CONV_WIDTH, CHUNK), lambda g, b: (0, g)), vec, vec, vec],
        out_specs=pl.BlockSpec((s, CHUNK), lambda g, b: (b, g)),
        out_shape=_sds((t, 4 * CHUNK), ACT),
        scratch_shapes=[pltpu.VMEM((s + HALO, CHUNK), F32)],
        compiler_params=_params(("parallel", "parallel"), 3 * s * CHUNK * 4, (s + HALO) * CHUNK * 4, 4 * s * CHUNK * 4),
    )(h, h, cw, cb, gg, gb)


def _conv_bwd(name, h, dyab, n_ex, cw, cb, gg, gb):
    t = h.shape[0]
    s = t // n_ex
    nt = s // _ROWS

    def body(a_ref, gt_ref, dy_ref, cw_ref, cb_ref, gg_ref, gb_ref,
             da_ref, dgt_ref, dcw_ref, dcb_ref, dgg_ref, dgb_ref, hh, dcs, acc):
        b = pl.program_id(1)

        @pl.when(b == 0)
        def _():
            dcw_ref[...] = jnp.zeros_like(dcw_ref)
            dcb_ref[...] = jnp.zeros_like(dcb_ref)
            dgg_ref[...] = jnp.zeros_like(dgg_ref)
            dgb_ref[...] = jnp.zeros_like(dgb_ref)

        hh[0:HALO, :] = jnp.zeros((HALO, CHUNK), F32)
        hh[HALO:HALO + s, :] = a_ref[...].astype(F32) * _sigmoid(gt_ref[...].astype(F32))
        dcs[s:s + HALO, :] = jnp.zeros((HALO, CHUNK), F32)
        acc[...] = jnp.zeros_like(acc)
        cwv = cw_ref[...]
        off = HALO - (CONV_WIDTH - 1)

        def tile1(i, carry):
            r0 = pl.multiple_of(i * _ROWS, _ROWS)
            win = hh[pl.ds(r0, _ROWS + HALO), :]
            c = _conv_taps(win, cwv, lambda w: w + off) + cb_ref[...]
            xh, rstd = _norm_stats(c)
            hg = xh * gg_ref[...] + gb_ref[...]
            sg = _sigmoid(hg)
            dhg = dy_ref[pl.ds(r0, _ROWS), :].astype(F32) * (sg * (1.0 + hg * (1.0 - sg)))
            acc[32:40, :] += _rows8(dhg * xh)
            acc[40:48, :] += _rows8(dhg)
            dc = _norm_bwd(dhg * gg_ref[...], xh, rstd)
            dcs[pl.ds(r0, _ROWS), :] = dc
            acc[48:56, :] += _rows8(dc)
            for w in range(CONV_WIDTH):
                acc[w:w + 1, :] += jnp.sum(dc * win[w + off:w + off + _ROWS, :], axis=0, keepdims=True)
            return carry

        lax.fori_loop(0, nt, tile1, 0)

        def tile2(i, carry):
            r0 = pl.multiple_of(i * _ROWS, _ROWS)
            win = dcs[pl.ds(r0, _ROWS + HALO), :]
            dhh = _conv_taps(win, cwv, lambda w: CONV_WIDTH - 1 - w)
            av = a_ref[pl.ds(r0, _ROWS), :].astype(F32)
            sg = _sigmoid(gt_ref[pl.ds(r0, _ROWS), :].astype(F32))
            da_ref[pl.ds(r0, _ROWS), :] = (dhh * sg).astype(da_ref.dtype)
            dgt_ref[pl.ds(r0, _ROWS), :] = (dhh * av * sg * (1.0 - sg)).astype(dgt_ref.dtype)
            return carry

        lax.fori_loop(0, nt, tile2, 0)
        dcw_ref[...] += acc[0:CONV_WIDTH, :]
        dgg_ref[...] += jnp.sum(acc[32:40, :], axis=0, keepdims=True)
        dgb_ref[...] += jnp.sum(acc[40:48, :], axis=0, keepdims=True)
        dcb_ref[...] += jnp.sum(acc[48:56, :], axis=0, keepdims=True)

    vec = pl.BlockSpec((1, CHUNK), lambda g, b: (0, g))
    tap = pl.BlockSpec((CONV_WIDTH, CHUNK), lambda g, b: (0, g))
    seq = pl.BlockSpec((s, CHUNK), lambda g, b: (b, g))
    return pl.pallas_call(
        body, name=name, grid=(4, n_ex),
        in_specs=[pl.BlockSpec((s, CHUNK), lambda g, b: (b, 8 + g)), pl.BlockSpec((s, CHUNK), lambda g, b: (b, 12 + g)),
                  pl.BlockSpec((s, CHUNK), lambda g, b: (b, 4 + g)), tap, vec, vec, vec],
        out_specs=[seq, seq, tap, vec, vec, vec],
        out_shape=[_sds((t, 4 * CHUNK), ACT), _sds((t, 4 * CHUNK), ACT), _sds((CONV_WIDTH, 4 * CHUNK), F32),
                   _sds((1, 4 * CHUNK), F32), _sds((1, 4 * CHUNK), F32), _sds((1, 4 * CHUNK), F32)],
        scratch_shapes=[pltpu.VMEM((s + HALO, CHUNK), F32), pltpu.VMEM((s + HALO, CHUNK), F32),
                        pltpu.VMEM((56, CHUNK), F32)],
        compiler_params=_params(("parallel", "arbitrary"), 5 * s * CHUNK * 4, 2 * (s + HALO) * CHUNK * 4, 4 * s * CHUNK * 4),
    )(h, h, dyab, cw, cb, gg, gb)


_TQ = 256


def _tri(kind):
    r = lax.broadcasted_iota(jnp.int32, (_TQ, _TQ), 0)
    c = lax.broadcasted_iota(jnp.int32, (_TQ, _TQ), 1)
    m = {"gt": r > c, "le": r <= c, "lt": r < c}[kind]
    return jnp.where(m, 1.0, 0.0).astype(jnp.bfloat16)


def _split_dot(x, tri):
    hi = x.astype(jnp.bfloat16)
    lo = (x - hi.astype(F32)).astype(jnp.bfloat16)
    return jnp.dot(hi, tri, preferred_element_type=F32) + jnp.dot(lo, tri, preferred_element_type=F32)


def _strict_mask(i, j):
    r = lax.broadcasted_iota(jnp.int32, (_TQ, _TQ), 0) + i * _TQ
    c = lax.broadcasted_iota(jnp.int32, (_TQ, _TQ), 1) + j * _TQ
    return c < r


def _sb_fwd(name, qkv, n_ex):
    t = qkv.shape[0]
    d = qkv.shape[1] // 3
    npair = d // CHUNK
    s = t // n_ex
    nq = s // _TQ
    scale = C_HEAD_DIM ** -0.5

    def body(q_ref, k_ref, v_ref, o_ref, lt_ref, o_acc, c_acc):
        i = pl.program_id(2)
        lane = lax.broadcasted_iota(jnp.int32, (_TQ, CHUNK), 1)
        q2 = q_ref[...]
        zero = jnp.zeros_like(q2)
        qh = [jnp.where(lane < C_HEAD_DIM, q2, zero), jnp.where(lane < C_HEAD_DIM, zero, q2)]
        tri = _tri("gt")
        o_acc[...] = jnp.zeros_like(o_acc)
        c_acc[...] = jnp.zeros_like(c_acc)

        def step(jj, carry):
            j = i - jj
            r0 = pl.multiple_of(j * _TQ, _TQ)
            kt = k_ref[pl.ds(r0, _TQ), :]
            vt = v_ref[pl.ds(r0, _TQ), :]
            mask = _strict_mask(i, j)
            for hd in range(2):
                z = lax.dot_general(qh[hd], kt, _DIMS["nt"], preferred_element_type=F32) * scale
                sp = _softplus(z)
                ln = jnp.where(mask, -sp, 0.0)
                later = c_acc[hd] + _split_dot(ln, tri)
                att = jnp.where(mask, jnp.exp(z - sp + later), 0.0)
                o_acc[hd] += jnp.dot(att.astype(MM), vt, preferred_element_type=F32)
                c_acc[hd] += jnp.sum(ln, axis=1, keepdims=True)
            return carry

        lax.fori_loop(0, i + 1, step, 0)
        o_ref[...] = jnp.where(lane < C_HEAD_DIM, o_acc[0], o_acc[1])
        lt_ref[:, 0:1] = c_acc[0]
        lt_ref[:, 1:2] = c_acc[1]

    return pl.pallas_call(
        body, name=name, grid=(n_ex, npair, nq),
        in_specs=[pl.BlockSpec((_TQ, CHUNK), lambda b, p, i: (b * nq + i, p)),
                  pl.BlockSpec((s, CHUNK), lambda b, p, i: (b, npair + p)),
                  pl.BlockSpec((s, CHUNK), lambda b, p, i: (b, 2 * npair + p))],
        out_specs=[pl.BlockSpec((_TQ, CHUNK), lambda b, p, i: (b * nq + i, p)),
                   pl.BlockSpec((None, _TQ, 2), lambda b, p, i: (p, b * nq + i, 0))],
        out_shape=[_sds((t, d), F32), _sds((npair, t, 2), F32)],
        scratch_shapes=[pltpu.VMEM((2, _TQ, CHUNK), F32), pltpu.VMEM((2, _TQ, 1), F32)],
        compiler_params=_params(("parallel", "parallel", "arbitrary"), 2 * s * CHUNK * 2 + 4 * _TQ * CHUNK * 4,
                                4 * _TQ * CHUNK * 4, 24 * _TQ * _TQ * 4),
    )(qkv, qkv, qkv)


def _sb_bwd(name, qkv, do, ltot, n_ex):
    t = qkv.shape[0]
    d = qkv.shape[1] // 3
    npair = d // CHUNK
    s = t // n_ex
    nq = s // _TQ
    scale = C_HEAD_DIM ** -0.5

    def body(q_ref, k_ref, v_ref, do_ref, lt_ref, dq_ref, dk_ref, dv_ref, dq_acc, cp_acc, cg_acc):
        i = pl.program_id(2)

        @pl.when(i == 0)
        def _():
            dk_ref[...] = jnp.zeros_like(dk_ref)
            dv_ref[...] = jnp.zeros_like(dv_ref)

        lane = lax.broadcasted_iota(jnp.int32, (_TQ, CHUNK), 1)
        first = lane < C_HEAD_DIM
        q2 = q_ref[...]
        do2 = do_ref[...]
        zero = jnp.zeros_like(q2)
        qh = [jnp.where(first, q2, zero), jnp.where(first, zero, q2)]
        doh = [jnp.where(first, do2, zero), jnp.where(first, zero, do2)]
        lt = [lt_ref[:, 0:1], lt_ref[:, 1:2]]
        tri_le = _tri("le")
        tri_lt = _tri("lt").astype(MM)
        dq_acc[...] = jnp.zeros_like(dq_acc)
        cp_acc[...] = jnp.zeros_like(cp_acc)
        cg_acc[...] = jnp.zeros_like(cg_acc)

        def step(j, carry):
            r0 = pl.multiple_of(j * _TQ, _TQ)
            kt = k_ref[pl.ds(r0, _TQ), :]
            vt = v_ref[pl.ds(r0, _TQ), :]
            mask = _strict_mask(i, j)
            dk_t = jnp.zeros((_TQ, CHUNK), F32)
            dv_t = jnp.zeros((_TQ, CHUNK), F32)
            for hd in range(2):
                sel = first if hd == 0 else jnp.logical_not(first)
                z = lax.dot_general(qh[hd], kt, _DIMS["nt"], preferred_element_type=F32) * scale
                sp = _softplus(z)
                ln = jnp.where(mask, -sp, 0.0)
                ls = z - sp
                later = lt[hd] - cp_acc[hd] - _split_dot(ln, tri_le)
                att = jnp.where(mask, jnp.exp(ls + later), 0.0)
                datt = lax.dot_general(doh[hd], vt, _DIMS["nt"], preferred_element_type=F32)
                g = datt * att
                big_g = cg_acc[hd] + jnp.dot(g.astype(MM), tri_lt, preferred_element_type=F32)
                sig = jnp.exp(ls)
                dz = jnp.where(mask, (g * (1.0 - sig) - big_g * sig) * scale, 0.0).astype(MM)
                dq_acc[hd] += jnp.dot(dz, kt, preferred_element_type=F32)
                dk_t = dk_t + jnp.where(sel, lax.dot_general(dz, q2, _DIMS["tn"], preferred_element_type=F32), 0.0)
                dv_t = dv_t + jnp.where(sel, lax.dot_general(att.astype(MM), do2, _DIMS["tn"],
                                                              preferred_element_type=F32), 0.0)
                cp_acc[hd] += jnp.sum(ln, axis=1, keepdims=True)
                cg_acc[hd] += jnp.sum(g, axis=1, keepdims=True)
            dk_ref[pl.ds(r0, _TQ), :] += dk_t
            dv_ref[pl.ds(r0, _TQ), :] += dv_t
            return carry

        lax.fori_loop(0, i + 1, step, 0)
        dq_ref[...] = jnp.where(first, dq_acc[0], dq_acc[1]).astype(dq_ref.dtype)

    qspec = pl.BlockSpec((_TQ, CHUNK), lambda b, p, i: (b * nq + i, p))
    kv_out = pl.BlockSpec((s, CHUNK), lambda b, p, i: (b, p))
    return pl.pallas_call(
        body, name=name, grid=(n_ex, npair, nq),
        in_specs=[qspec, pl.BlockSpec((s, CHUNK), lambda b, p, i: (b, npair + p)),
                  pl.BlockSpec((s, CHUNK), lambda b, p, i: (b, 2 * npair + p)), qspec,
                  pl.BlockSpec((None, _TQ, 2), lambda b, p, i: (p, b * nq + i, 0))],
        out_specs=[qspec, kv_out, kv_out],
        out_shape=[_sds((t, d), ACT), _sds((t, d), F32), _sds((t, d), F32)],
        scratch_shapes=[pltpu.VMEM((2, _TQ, CHUNK), F32), pltpu.VMEM((2, _TQ, 1), F32), pltpu.VMEM((2, _TQ, 1), F32)],
        compiler_params=_params(("parallel", "parallel", "arbitrary"), 2 * s * CHUNK * 2 + 2 * s * CHUNK * 4,
                                4 * _TQ * CHUNK * 4, 32 * _TQ * _TQ * 4),
    )(qkv, qkv, qkv, do, ltot)


def _xattn_fwd(name, q, kk, vv, n_ex):
    t, d = q.shape
    m = kk.shape[0] // n_ex
    s = t // n_ex
    tq = _tile(s, 512)
    nq = s // tq
    hd_dim = d // MEM_HEADS
    scale = hd_dim ** -0.5

    def body(q_ref, k_ref, v_ref, o_ref):
        for hd in range(MEM_HEADS):
            ln = slice(hd * hd_dim, (hd + 1) * hd_dim)
            sc = lax.dot_general(q_ref[:, ln], k_ref[:, ln], _DIMS["nt"], preferred_element_type=F32) * scale
            p = jnp.exp(sc - jnp.max(sc, axis=-1, keepdims=True))
            p = p / jnp.sum(p, axis=-1, keepdims=True)
            o_ref[:, ln] = jnp.dot(p.astype(MM), v_ref[:, ln], preferred_element_type=F32).astype(o_ref.dtype)

    qspec = pl.BlockSpec((tq, d), lambda b, i: (b * nq + i, 0))
    kspec = pl.BlockSpec((m, d), lambda b, i: (b, 0))
    return pl.pallas_call(
        body, name=name, grid=(n_ex, nq), in_specs=[qspec, kspec, kspec], out_specs=qspec,
        out_shape=_sds((t, d), ACT),
        compiler_params=_params(("parallel", "parallel"), 2 * tq * d * 2 + 2 * m * d * 2, 0, 6 * tq * m * 4),
    )(q, kk, vv)


def _xattn_bwd(name, q, kk, vv, do, n_ex):
    t, d = q.shape
    m = kk.shape[0] // n_ex
    s = t // n_ex
    tq = _tile(s, 512)
    nq = s // tq
    hd_dim = d // MEM_HEADS
    scale = hd_dim ** -0.5

    def body(q_ref, k_ref, v_ref, do_ref, dq_ref, dk_ref, dv_ref):
        i = pl.program_id(1)

        @pl.when(i == 0)
        def _():
            dk_ref[...] = jnp.zeros_like(dk_ref)
            dv_ref[...] = jnp.zeros_like(dv_ref)

        for hd in range(MEM_HEADS):
            ln = slice(hd * hd_dim, (hd + 1) * hd_dim)
            qv, kv, vv_, dov = q_ref[:, ln], k_ref[:, ln], v_ref[:, ln], do_ref[:, ln]
            sc = lax.dot_general(qv, kv, _DIMS["nt"], preferred_element_type=F32) * scale
            p = jnp.exp(sc - jnp.max(sc, axis=-1, keepdims=True))
            p = p / jnp.sum(p, axis=-1, keepdims=True)
            dp = lax.dot_general(dov, vv_, _DIMS["nt"], preferred_element_type=F32)
            ds = (p * (dp - jnp.sum(p * dp, axis=-1, keepdims=True)) * scale).astype(MM)
            dq_ref[:, ln] = jnp.dot(ds, kv, preferred_element_type=F32).astype(dq_ref.dtype)
            dk_ref[:, ln] += lax.dot_general(ds, qv, _DIMS["tn"], preferred_element_type=F32)
            dv_ref[:, ln] += lax.dot_general(p.astype(MM), dov, _DIMS["tn"], preferred_element_type=F32)

    qspec = pl.BlockSpec((tq, d), lambda b, i: (b * nq + i, 0))
    kspec = pl.BlockSpec((m, d), lambda b, i: (b, 0))
    return pl.pallas_call(
        body, name=name, grid=(n_ex, nq), in_specs=[qspec, kspec, kspec, qspec], out_specs=[qspec, kspec, kspec],
        out_shape=[_sds((t, d), ACT), _sds((n_ex * m, d), F32), _sds((n_ex * m, d), F32)],
        compiler_params=_params(("parallel", "arbitrary"), 3 * tq * d * 2 + 2 * m * d * 2 + 2 * m * d * 4, 0,
                                8 * tq * m * 4),
    )(q, kk, vv, do)


def _ffn_up(name, y, w1, w3, layer):
    t, d = y.shape
    f = w1.shape[-1]
    tm = _tile(t, 512)
    wspec = pl.BlockSpec((None, None, d, f), lambda i, j, k: (j, layer, 0, 0))
    hspec = pl.BlockSpec((None, tm, f), lambda i, j, k: (j, i, 0))

    def epi(vals, _):
        h1, h3 = vals
        return [h1, h3, h1 * _sigmoid(h1) * h3]

    return _mm(name, "nn", (t // tm, N_CHIPS, 1), y, pl.BlockSpec((tm, d), lambda i, j, k: (i, 0)),
               [w1, w3], [wspec, wspec], [_sds((N_CHIPS, t, f), ACT)] * 3, [hspec] * 3, (tm, f), epilogue=epi)


def _ffn_down(name, g, w2, layer):
    _, t, f = g.shape
    d = w2.shape[-1]
    tm, tn = _tile(t, 1024), _tile(d, 512)
    return _mm(name, "nn", (t // tm, d // tn, N_CHIPS), g, pl.BlockSpec((None, tm, f), lambda i, j, k: (k, i, 0)),
               [w2], [pl.BlockSpec((None, None, f, tn), lambda i, j, k: (layer, k, 0, j))],
               [_sds((t, d), F32)], [pl.BlockSpec((tm, tn), lambda i, j, k: (i, j))], (tm, tn))[0]


def _ffn_down_bwd(name, dr, w2, h1, h3, layer):
    t, d = dr.shape
    f = w2.shape[-2]
    tm = _tile(t, 512)
    hspec = pl.BlockSpec((None, tm, f), lambda i, j, k: (j, i, 0))

    def epi(vals, ex):
        dg, = vals
        h1v, h3v = ex
        sg = _sigmoid(h1v)
        return [dg * h3v * (sg * (1.0 + h1v * (1.0 - sg))), dg * h1v * sg]

    return _mm(name, "nt", (t // tm, N_CHIPS, 1), dr, pl.BlockSpec((tm, d), lambda i, j, k: (i, 0)),
               [w2], [pl.BlockSpec((None, None, f, d), lambda i, j, k: (layer, j, 0, 0))],
               [_sds((N_CHIPS, t, f), ACT)] * 2, [hspec] * 2, (tm, f),
               extras=[h1, h3], extra_specs=[hspec, hspec], epilogue=epi)


def _ffn_up_bwd(name, dh, w, layer, extras, epilogue):
    _, t, f = dh.shape
    d = w.shape[-2]
    tm, tn = _tile(t, 1024), _tile(d, 512)
    ospec = pl.BlockSpec((tm, tn), lambda i, j, k: (i, j))
    return _mm(name, "nt", (t // tm, d // tn, N_CHIPS), dh, pl.BlockSpec((None, tm, f), lambda i, j, k: (k, i, 0)),
               [w], [pl.BlockSpec((None, None, tn, f), lambda i, j, k: (k, layer, j, 0))],
               [_sds((t, d), F32)], [ospec], (tm, tn),
               extras=extras, extra_specs=[ospec] * len(extras), epilogue=epilogue)[0]


def _ffn_wgrad_up(name, y, dh1, dh3):
    t, d = y.shape
    f = dh1.shape[-1]
    tm, tk = _tile(d, 512), _tile(t, 1024)
    hspec = pl.BlockSpec((None, tk, f), lambda i, j, k: (j, k, 0))
    ospec = pl.BlockSpec((None, tm, f), lambda i, j, k: (j, i, 0))
    return _mm(name, "tn", (d // tm, N_CHIPS, t // tk), y, pl.BlockSpec((tk, tm), lambda i, j, k: (k, i)),
               [dh1, dh3], [hspec, hspec], [_sds((N_CHIPS, d, f), MM)] * 2, [ospec, ospec], (tm, f))


def _ffn_wgrad_down(name, g, dr):
    _, t, f = g.shape
    d = dr.shape[1]
    tn, tk = _tile(d, 512), _tile(t, 1024)
    return _mm(name, "tn", (N_CHIPS, d // tn, t // tk), g, pl.BlockSpec((None, tk, f), lambda i, j, k: (i, k, 0)),
               [dr], [pl.BlockSpec((tk, tn), lambda i, j, k: (k, j))],
               [_sds((N_CHIPS, f, d), MM)], [pl.BlockSpec((None, f, tn), lambda i, j, k: (i, 0, j))], (f, tn))[0]


def _proj_cols(name, y, w, layer, out_dtype):
    t, kdim = y.shape
    wd = w.shape[-1]
    tn = _tile(wd, 512)
    per = wd // tn
    tm = _tile(t, 1024)
    return _mm(name, "nn", (t // tm, N_CHIPS * per, 1), y, pl.BlockSpec((tm, kdim), lambda i, j, k: (i, 0)),
               [w], [pl.BlockSpec((None, None, kdim, tn), lambda i, j, k: (j // per, layer, 0, j % per))],
               [_sds((t, N_CHIPS * wd), out_dtype)], [pl.BlockSpec((tm, tn), lambda i, j, k: (i, j))], (tm, tn))[0]


def _proj_cols_bwd(name, dh, w, layer, extras, epilogue):
    t = dh.shape[0]
    kdim, wd = w.shape[-2], w.shape[-1]
    tm, tn = _tile(t, 1024), _tile(kdim, 512)
    ospec = pl.BlockSpec((tm, tn), lambda i, j, k: (i, j))
    return _mm(name, "nt", (t // tm, kdim // tn, N_CHIPS), dh, pl.BlockSpec((tm, wd), lambda i, j, k: (i, k)),
               [w], [pl.BlockSpec((None, None, tn, wd), lambda i, j, k: (k, layer, j, 0))],
               [_sds((t, kdim), F32)], [ospec], (tm, tn),
               extras=extras, extra_specs=[ospec] * len(extras), epilogue=epilogue)[0]


def _proj_cols_wgrad(name, y, dh):
    t, kdim = y.shape
    wd = dh.shape[1] // N_CHIPS
    tm, tk = _tile(kdim, 512), _tile(t, 1024)
    return _mm(name, "tn", (kdim // tm, N_CHIPS, t // tk), y, pl.BlockSpec((tk, tm), lambda i, j, k: (k, i)),
               [dh], [pl.BlockSpec((tk, wd), lambda i, j, k: (k, j))],
               [_sds((N_CHIPS, kdim, wd), MM)], [pl.BlockSpec((None, tm, wd), lambda i, j, k: (j, i, 0))], (tm, wd))[0]


def _coords():
    return lax.axis_index("x"), lax.axis_index("y"), lax.axis_index("c")


def _chip_peers(x, y):
    return [(1 - x, y), (x, 1 - y), (1 - x, 1 - y)]


def _slot(ref, axis, k):
    return ref.at[k] if axis == 0 else ref.at[:, k]


_ANY = pl.BlockSpec(memory_space=pl.ANY)


def _gather_chips(shards, axes):
    n = len(shards)

    def body(*refs):
        ins, outs = refs[:n], refs[n:2 * n]
        send_sems, recv_sems, loc_sems = refs[2 * n:]
        x, y, c = _coords()
        me = 2 * x + y
        copies = []
        for t in range(n):
            dst = _slot(outs[t], axes[t], me)
            cp = pltpu.make_async_copy(ins[t], dst, loc_sems.at[t])
            cp.start()
            copies.append(cp)
            for j, chip in enumerate(_chip_peers(x, y)):
                cp = pltpu.make_async_remote_copy(src_ref=ins[t], dst_ref=dst, send_sem=send_sems.at[t, j],
                                                  recv_sem=recv_sems.at[t, j], device_id=(*chip, c), device_id_type=MESH)
                cp.start()
                copies.append(cp)
        for cp in copies:
            cp.wait()

    def out_shape(a, ax):
        sh = (N_CHIPS, *a.shape) if ax == 0 else (a.shape[0], N_CHIPS, *a.shape[1:])
        return _sds(sh, a.dtype)

    return pl.pallas_call(
        body, name="gather_weights", in_specs=[_ANY] * n, out_specs=[_ANY] * n,
        out_shape=[out_shape(a, ax) for a, ax in zip(shards, axes)],
        scratch_shapes=[pltpu.SemaphoreType.DMA((n, 3)), pltpu.SemaphoreType.DMA((n, 3)), pltpu.SemaphoreType.DMA((n,))],
    )(*shards)


def _scatter_chips(grads, axes):
    n = len(grads)

    def body(*refs):
        ins, outs = refs[:n], refs[n:2 * n]
        send_sems, recv_sems = refs[2 * n:]
        x, y, c = _coords()
        copies = []
        for t in range(n):
            for j, chip in enumerate(_chip_peers(x, y)):
                cp = pltpu.make_async_remote_copy(src_ref=_slot(ins[t], axes[t], 2 * chip[0] + chip[1]), dst_ref=outs[t].at[j],
                                                  send_sem=send_sems.at[t, j], recv_sem=recv_sems.at[t, j],
                                                  device_id=(*chip, c), device_id_type=MESH)
                cp.start()
                copies.append(cp)
        for cp in copies:
            cp.wait()

    def out_shape(a, ax):
        loc = a.shape[1:] if ax == 0 else (a.shape[0], *a.shape[2:])
        return _sds((3, *loc), a.dtype)

    return pl.pallas_call(
        body, name="scatter_grads", in_specs=[_ANY] * n, out_specs=[_ANY] * n,
        out_shape=[out_shape(a, ax) for a, ax in zip(grads, axes)],
        scratch_shapes=[pltpu.SemaphoreType.DMA((n, 3)), pltpu.SemaphoreType.DMA((n, 3))],
    )(*grads)


def _swap_sibling(arrs):
    n = len(arrs)

    def body(*refs):
        ins, outs = refs[:n], refs[n:2 * n]
        send_sems, recv_sems = refs[2 * n:]
        x, y, c = _coords()
        copies = []
        for t in range(n):
            cp = pltpu.make_async_remote_copy(src_ref=ins[t], dst_ref=outs[t], send_sem=send_sems.at[t],
                                              recv_sem=recv_sems.at[t], device_id=(x, y, 1 - c), device_id_type=MESH)
            cp.start()
            copies.append(cp)
        for cp in copies:
            cp.wait()

    return pl.pallas_call(
        body, name="swap_sibling", in_specs=[_ANY] * n, out_specs=[_ANY] * n,
        out_shape=[_sds(a.shape, a.dtype) for a in arrs],
        scratch_shapes=[pltpu.SemaphoreType.DMA((n,)), pltpu.SemaphoreType.DMA((n,))],
    )(*arrs)


def _gather_all(part):
    def body(in_ref, out_ref, send_sems, recv_sems, loc_sem):
        x, y, c = _coords()
        dst = out_ref.at[4 * x + 2 * y + c]
        copies = [pltpu.make_async_copy(in_ref, dst, loc_sem)]
        for r in range(1, N_DEV):
            fx, fy, fc = (r >> 2) & 1, (r >> 1) & 1, r & 1
            peer = (x ^ fx, y ^ fy, c ^ fc)
            copies.append(pltpu.make_async_remote_copy(src_ref=in_ref, dst_ref=dst, send_sem=send_sems.at[r - 1],
                                                       recv_sem=recv_sems.at[r - 1], device_id=peer, device_id_type=MESH))
        for cp in copies:
            cp.start()
        for cp in copies:
            cp.wait()

    return pl.pallas_call(
        body, name="gather_small_grads", in_specs=[_ANY], out_specs=_ANY,
        out_shape=_sds((N_DEV, *part.shape), part.dtype),
        scratch_shapes=[pltpu.SemaphoreType.DMA((N_DEV - 1,)), pltpu.SemaphoreType.DMA((N_DEV - 1,)), pltpu.SemaphoreType.DMA],
    )(part)


def _sum_chips(grad, recv, axis, me):
    _, ll, rr, cc = recv.shape
    tr = _tile(rr, 512)

    def body(me_ref, g_ref, r0_ref, r1_ref, r2_ref, o_ref):
        o_ref[...] = ((g_ref[...].astype(F32) + r0_ref[...].astype(F32)) + r1_ref[...].astype(F32)) + r2_ref[...].astype(F32)

    if axis == 0:
        gspec = pl.BlockSpec((None, None, tr, cc), lambda l, r, m: (m[0], l, r, 0))
    else:
        gspec = pl.BlockSpec((None, None, tr, cc), lambda l, r, m: (l, m[0], r, 0))
    rspecs = [pl.BlockSpec((None, None, tr, cc), functools.partial(lambda l, r, m, j: (j, l, r, 0), j=j)) for j in range(3)]
    return pl.pallas_call(
        body, name="sum_chip_grads",
        grid_spec=pltpu.PrefetchScalarGridSpec(
            num_scalar_prefetch=1, grid=(ll, rr // tr), in_specs=[gspec, *rspecs],
            out_specs=pl.BlockSpec((None, tr, cc), lambda l, r, m: (l, r, 0))),
        out_shape=_sds((ll, rr, cc), F32),
        compiler_params=_params(("parallel", "parallel"), 4 * tr * cc * 2 + tr * cc * 4, 0, 2 * tr * cc * 4),
    )(me, grad, recv, recv, recv)


def _adamw_math(w, g, m, v):
    m = ADAM_B1 * m + (1.0 - ADAM_B1) * g
    v = ADAM_B2 * v + (1.0 - ADAM_B2) * (g * g)
    m_hat = m / (1.0 - ADAM_B1 ** ADAM_STEP)
    v_hat = v / (1.0 - ADAM_B2 ** ADAM_STEP)
    delta = -ADAM_LR * (m_hat / (jnp.sqrt(v_hat) + ADAM_EPS) + ADAM_WD * w)
    return delta, m, v


def _adamw(name, parts, w, m, v):
    ll, rr, cc = w.shape
    tr = _tile(rr, 256)
    npart = len(parts)

    def body(*refs):
        p_refs = refs[:npart]
        w_ref, m_ref, v_ref, g_ref, d_ref, nm_ref, nv_ref = refs[npart:]
        g = p_refs[0][...]
        for p in p_refs[1:]:
            g = g + p[...]
        d, nm, nv = _adamw_math(w_ref[...], g, m_ref[...], v_ref[...])
        g_ref[...] = g
        d_ref[...] = d
        nm_ref[...] = nm
        nv_ref[...] = nv

    spec = pl.BlockSpec((None, tr, cc), lambda l, r: (l, r, 0))
    out = _sds((ll, rr, cc), F32)
    return pl.pallas_call(
        body, name=name, grid=(ll, rr // tr), in_specs=[spec] * (npart + 3), out_specs=[spec] * 4, out_shape=[out] * 4,
        compiler_params=_params(("parallel", "parallel"), (npart + 7) * tr * cc * 4, 0, 4 * tr * cc * 4),
    )(*parts, w, m, v)


def _sum_devices(allparts):
    _, rr, cc = allparts.shape

    def body(p_ref, o_ref):
        s = p_ref[0]
        for k in range(1, N_DEV):
            s = s + p_ref[k]
        o_ref[...] = s

    return pl.pallas_call(
        body, name="sum_small_grads", grid=(1,), in_specs=[pl.BlockSpec((N_DEV, rr, cc), lambda i: (0, 0, 0))],
        out_specs=pl.BlockSpec((rr, cc), lambda i: (0, 0)), out_shape=_sds((rr, cc), F32),
        compiler_params=_params(("arbitrary",), 9 * rr * cc * 4),
    )(allparts)


def _pack(arrs):
    flat = jnp.concatenate([a.reshape(-1).astype(F32) for a in arrs])
    n = flat.shape[0]
    total = -(-n // 1024) * 1024
    return jnp.pad(flat, (0, total - n)).reshape(total // 128, 128)


def _unpack(block, shapes):
    flat = block.reshape(-1)
    out, off = [], 0
    for sh in shapes:
        n = math.prod(sh)
        out.append(flat[off:off + n].reshape(sh))
        off += n
    return out


_BIG = ["w_in_ab", "w_out_ab", "w_qkv_c", "w_out_c", "mem_wq", "mem_wk", "mem_wv", "mem_wo", "ffn_w1", "ffn_w3", "ffn_w2"]
_BIG_AXIS = {"w_in_ab": 0, "w_out_ab": 1, "w_qkv_c": 0, "w_out_c": 1, "mem_wq": 1, "mem_wk": 1, "mem_wv": 1, "mem_wo": 1,
             "ffn_w1": 0, "ffn_w3": 0, "ffn_w2": 1}
_SMALL_REPL = ["gmlp_ln_g", "gmlp_ln_b", "gmlp_w_s", "gmlp_b_s", "conv_b", "conv_gn_g", "conv_gn_b"]
_SMALL_SHARD = ["conv_w", "ln_g", "ln_b"]
_NAMES = ["w_in_ab", "gmlp_ln_g", "gmlp_ln_b", "gmlp_w_s", "gmlp_b_s", "conv_w", "conv_b", "conv_gn_g", "conv_gn_b",
          "w_out_ab", "w_qkv_c", "w_out_c", "mem_wq", "mem_wk", "mem_wv", "mem_wo", "ffn_w1", "ffn_w3", "ffn_w2",
          "ln_g", "ln_b"]


def _natural(stacked):
    ll, k, rr, cc = stacked.shape
    return stacked.reshape(ll, k * rr, cc)


def _local_step(x, mem, target, wts, small):
    n_ex, s, d = x.shape
    t = n_ex * s
    x2 = x.reshape(t, d)
    mem_a = mem.reshape(-1, d).astype(ACT)
    tgt = target.reshape(t, d)
    one = jnp.ones((1, d), F32)
    zero = jnp.zeros((1, d), F32)
    ln_g, ln_b = small["ln_g"], small["ln_b"]

    def vec(a):
        return a.reshape(1, -1)

    saved = []
    xh, gp, bp = x2, one, zero
    y_act = x2.astype(ACT)
    for l in range(DEPTH):
        sv = {"y0": y_act}
        if l % 2 == 0:
            e = l // 2
            h = _proj_cols(f"in_ab_{l}", y_act, wts["w_in_ab"], e, ACT)
            gl = (vec(small["gmlp_ln_g"][e]), vec(small["gmlp_ln_b"][e]), small["gmlp_w_s"][e],
                  small["gmlp_b_s"][e].reshape(4, CHUNK, 1))
            cl = (small["conv_w"][e], vec(small["conv_b"][e]), vec(small["conv_gn_g"][e]), vec(small["conv_gn_b"][e]))
            ya = _gmlp_fwd(f"gmlp_fwd_{l}", h, *gl)
            yb = _conv_fwd(f"conv_fwd_{l}", h, n_ex, *cl)
            yab = jnp.concatenate([ya, yb], axis=1)
            mix = _dense(f"out_ab_{l}", yab, wts["w_out_ab"][e], F32)
            sv.update(h=h, yab=yab, gl=gl, cl=cl)
        else:
            o = l // 2
            qkv = _proj_cols(f"qkv_{l}", y_act, wts["w_qkv_c"], o, ACT)
            att, ltot = _sb_fwd(f"sb_fwd_{l}", qkv, n_ex)
            att_a = att.astype(ACT)
            mix = _dense(f"out_c_{l}", att_a, wts["w_out_c"][o], F32)
            sv.update(qkv=qkv, att=att_a, ltot=ltot)
        g1, b1 = vec(ln_g[l, 0]), vec(ln_b[l, 0])
        xh1, y1, rstd1 = _ln_fwd(f"ln1_fwd_{l}", xh, gp, bp, mix, g1, b1)
        q = _dense(f"mem_q_{l}", y1, wts["mem_wq"][l], ACT)
        kk = _dense(f"mem_k_{l}", mem_a, wts["mem_wk"][l], ACT)
        vv = _dense(f"mem_v_{l}", mem_a, wts["mem_wv"][l], ACT)
        oc = _xattn_fwd(f"xattn_fwd_{l}", q, kk, vv, n_ex)
        cross = _dense(f"mem_o_{l}", oc, wts["mem_wo"][l], F32)
        g2, b2 = vec(ln_g[l, 1]), vec(ln_b[l, 1])
        xh2, y2, rstd2 = _ln_fwd(f"ln2_fwd_{l}", xh1, g1, b1, cross, g2, b2)
        h1, h3, gact = _ffn_up(f"ffn_up_{l}", y2, wts["ffn_w1"], wts["ffn_w3"], l)
        ffo = _ffn_down(f"ffn_down_{l}", gact, wts["ffn_w2"], l)
        g3, b3 = vec(ln_g[l, 2]), vec(ln_b[l, 2])
        xh3, y3, rstd3 = _ln_fwd(f"ln3_fwd_{l}", xh2, g2, b2, ffo, g3, b3)
        sv.update(xh1=xh1, y1=y1, rstd1=rstd1, g1=g1, q=q, kk=kk, vv=vv, oc=oc, xh2=xh2, y2=y2, rstd2=rstd2, g2=g2,
                  h1=h1, h3=h3, gact=gact, xh3=xh3, rstd3=rstd3, g3=g3)
        saved.append(sv)
        xh, gp, bp, y_act = xh3, g3, b3, y3

    dy, loss = _loss_head(xh, gp, bp, tgt)

    big = {n: [None] * (DEPTH if n.startswith(("mem_", "ffn_")) else DEPTH // 2) for n in _BIG}
    sm = {n: [None] * (DEPTH // 2) for n in _SMALL_REPL + ["conv_w"]}
    d_ln_g = [[None] * 3 for _ in range(DEPTH)]
    d_ln_b = [[None] * 3 for _ in range(DEPTH)]

    def add_res(vals, ex):
        return [vals[0] + ALPHA * ex[0]]

    def add_res2(vals, ex):
        return [vals[0] + ex[0] + ALPHA * ex[1]]

    for l in reversed(range(DEPTH)):
        sv = saved[l]
        dr3, dr3a, d_ln_g[l][2], d_ln_b[l][2] = _ln_bwd(f"ln3_bwd_{l}", dy, sv["xh3"], sv["rstd3"], sv["g3"])
        big["ffn_w2"][l] = _ffn_wgrad_down(f"ffn_w2_grad_{l}", sv["gact"], dr3a)
        dh1, dh3 = _ffn_down_bwd(f"ffn_down_bwd_{l}", dr3a, wts["ffn_w2"], sv["h1"], sv["h3"], l)
        big["ffn_w1"][l], big["ffn_w3"][l] = _ffn_wgrad_up(f"ffn_w13_grad_{l}", sv["y2"], dh1, dh3)
        part = _ffn_up_bwd(f"ffn_up_bwd1_{l}", dh1, wts["ffn_w1"], l, [], None)
        dy = _ffn_up_bwd(f"ffn_up_bwd3_{l}", dh3, wts["ffn_w3"], l, [part, dr3], add_res2)
        dr2, dr2a, d_ln_g[l][1], d_ln_b[l][1] = _ln_bwd(f"ln2_bwd_{l}", dy, sv["xh2"], sv["rstd2"], sv["g2"])
        big["mem_wo"][l] = _dense_tn(f"mem_wo_grad_{l}", sv["oc"], dr2a)
        doc = _dense_nt(f"mem_o_bwd_{l}", dr2a, wts["mem_wo"][l], ACT)
        dq, dkk, dvv = _xattn_bwd(f"xattn_bwd_{l}", sv["q"], sv["kk"], sv["vv"], doc, n_ex)
        big["mem_wq"][l] = _dense_tn(f"mem_wq_grad_{l}", sv["y1"], dq)
        big["mem_wk"][l] = _dense_tn(f"mem_wk_grad_{l}", mem_a, dkk)
        big["mem_wv"][l] = _dense_tn(f"mem_wv_grad_{l}", mem_a, dvv)
        dy = _dense_nt(f"mem_q_bwd_{l}", dq, wts["mem_wq"][l], F32, extras=[dr2], epilogue=add_res)
        dr1, dr1a, d_ln_g[l][0], d_ln_b[l][0] = _ln_bwd(f"ln1_bwd_{l}", dy, sv["xh1"], sv["rstd1"], sv["g1"])
        if l % 2 == 0:
            e = l // 2
            big["w_out_ab"][e] = _dense_tn(f"out_ab_grad_{l}", sv["yab"], dr1a)
            dyab = _dense_nt(f"out_ab_bwd_{l}", dr1a, wts["w_out_ab"][e], ACT)
            duv, dgg, dgb, dws, dbs = _gmlp_bwd(f"gmlp_bwd_{l}", sv["h"], dyab, *sv["gl"])
            da, dgt, dcw, dcb, dng, dnb = _conv_bwd(f"conv_bwd_{l}", sv["h"], dyab, n_ex, *sv["cl"])
            sm["gmlp_ln_g"][e], sm["gmlp_ln_b"][e] = dgg.reshape(-1), dgb.reshape(-1)
            sm["gmlp_w_s"][e], sm["gmlp_b_s"][e] = dws, dbs.reshape(4, CHUNK)
            sm["conv_w"][e], sm["conv_b"][e] = dcw, dcb.reshape(-1)
            sm["conv_gn_g"][e], sm["conv_gn_b"][e] = dng.reshape(-1), dnb.reshape(-1)
            dh = jnp.concatenate([duv, da, dgt], axis=1)
            big["w_in_ab"][e] = _proj_cols_wgrad(f"in_ab_grad_{l}", sv["y0"], dh)
            dy = _proj_cols_bwd(f"in_ab_bwd_{l}", dh, wts["w_in_ab"], e, [dr1], add_res)
        else:
            o = l // 2
            big["w_out_c"][o] = _dense_tn(f"out_c_grad_{l}", sv["att"], dr1a)
            datt = _dense_nt(f"out_c_bwd_{l}", dr1a, wts["w_out_c"][o], ACT)
            dq_, dk_, dv_ = _sb_bwd(f"sb_bwd_{l}", sv["qkv"], datt, sv["ltot"], n_ex)
            dqkv = jnp.concatenate([dq_, dk_.astype(ACT), dv_.astype(ACT)], axis=1)
            big["w_qkv_c"][o] = _proj_cols_wgrad(f"qkv_grad_{l}", sv["y0"], dqkv)
            dy = _proj_cols_bwd(f"qkv_bwd_{l}", dqkv, wts["w_qkv_c"], o, [dr1], add_res)

    grad_x = dy.reshape(n_ex, s, d)
    small_g = {n: jnp.stack(sm[n]) for n in sm}
    small_g["ln_g"] = jnp.stack([jnp.concatenate(r, axis=0) for r in d_ln_g])
    small_g["ln_b"] = jnp.stack([jnp.concatenate(r, axis=0) for r in d_ln_b])
    return loss, grad_x, big, small_g


def _stack_big(name, per_layer):
    if _BIG_AXIS[name] == 0:
        return jnp.stack(per_layer, axis=1)
    st = jnp.stack(per_layer, axis=0)
    if st.ndim == 4:
        return st
    ll, rr, cc = st.shape
    return st.reshape(ll, N_CHIPS, rr // N_CHIPS, cc)


def kernel(x, mem, w_in_ab, gmlp_ln_g, gmlp_ln_b, gmlp_w_s, gmlp_b_s, conv_w, conv_b, conv_gn_g, conv_gn_b, w_out_ab, w_qkv_c, w_out_c, mem_wq, mem_wk, mem_wv, mem_wo, ffn_w1, ffn_w3, ffn_w2, ln_g, ln_b, loss_target, m_w_in_ab, m_gmlp_ln_g, m_gmlp_ln_b, m_gmlp_w_s, m_gmlp_b_s, m_conv_w, m_conv_b, m_conv_gn_g, m_conv_gn_b, m_w_out_ab, m_w_qkv_c, m_w_out_c, m_mem_wq, m_mem_wk, m_mem_wv, m_mem_wo, m_ffn_w1, m_ffn_w3, m_ffn_w2, m_ln_g, m_ln_b, v_w_in_ab, v_gmlp_ln_g, v_gmlp_ln_b, v_gmlp_w_s, v_gmlp_b_s, v_conv_w, v_conv_b, v_conv_gn_g, v_conv_gn_b, v_w_out_ab, v_w_qkv_c, v_w_out_c, v_mem_wq, v_mem_wk, v_mem_wv, v_mem_wo, v_ffn_w1, v_ffn_w3, v_ffn_w2, v_ln_g, v_ln_b):
    w = dict(w_in_ab=w_in_ab, gmlp_ln_g=gmlp_ln_g, gmlp_ln_b=gmlp_ln_b, gmlp_w_s=gmlp_w_s, gmlp_b_s=gmlp_b_s, conv_w=conv_w,
             conv_b=conv_b, conv_gn_g=conv_gn_g, conv_gn_b=conv_gn_b, w_out_ab=w_out_ab, w_qkv_c=w_qkv_c, w_out_c=w_out_c,
             mem_wq=mem_wq, mem_wk=mem_wk, mem_wv=mem_wv, mem_wo=mem_wo, ffn_w1=ffn_w1, ffn_w3=ffn_w3, ffn_w2=ffn_w2,
             ln_g=ln_g, ln_b=ln_b)
    mo = dict(w_in_ab=m_w_in_ab, gmlp_ln_g=m_gmlp_ln_g, gmlp_ln_b=m_gmlp_ln_b, gmlp_w_s=m_gmlp_w_s, gmlp_b_s=m_gmlp_b_s,
              conv_w=m_conv_w, conv_b=m_conv_b, conv_gn_g=m_conv_gn_g, conv_gn_b=m_conv_gn_b, w_out_ab=m_w_out_ab,
              w_qkv_c=m_w_qkv_c, w_out_c=m_w_out_c, mem_wq=m_mem_wq, mem_wk=m_mem_wk, mem_wv=m_mem_wv, mem_wo=m_mem_wo,
              ffn_w1=m_ffn_w1, ffn_w3=m_ffn_w3, ffn_w2=m_ffn_w2, ln_g=m_ln_g, ln_b=m_ln_b)
    vo = dict(w_in_ab=v_w_in_ab, gmlp_ln_g=v_gmlp_ln_g, gmlp_ln_b=v_gmlp_ln_b, gmlp_w_s=v_gmlp_w_s, gmlp_b_s=v_gmlp_b_s,
              conv_w=v_conv_w, conv_b=v_conv_b, conv_gn_g=v_conv_gn_g, conv_gn_b=v_conv_gn_b, w_out_ab=v_w_out_ab,
              w_qkv_c=v_w_qkv_c, w_out_c=v_w_out_c, mem_wq=v_mem_wq, mem_wk=v_mem_wk, mem_wv=v_mem_wv, mem_wo=v_mem_wo,
              ffn_w1=v_ffn_w1, ffn_w3=v_ffn_w3, ffn_w2=v_ffn_w2, ln_g=v_ln_g, ln_b=v_ln_b)
    me = (2 * lax.axis_index("x") + lax.axis_index("y")).astype(jnp.int32).reshape(1)

    shards = [w[n].astype(MM) for n in _BIG] + [w[n] for n in _SMALL_SHARD]
    axes = [_BIG_AXIS[n] for n in _BIG] + [0, 0, 0]
    gathered = _gather_chips(shards, axes)
    wts = {}
    for n, g in zip(_BIG, gathered[:len(_BIG)]):
        wts[n] = g if _BIG_AXIS[n] == 0 else _natural(g)
    wts["ffn_w2"] = gathered[_BIG.index("ffn_w2")]
    small = {n: w[n] for n in _SMALL_REPL}
    cw_g, lg_g, lb_g = gathered[len(_BIG):]
    small["conv_w"] = jnp.moveaxis(cw_g, 0, 2).reshape(cw_g.shape[1], CONV_WIDTH, -1)
    small["ln_g"] = jnp.moveaxis(lg_g, 0, 2).reshape(DEPTH, 3, -1)
    small["ln_b"] = jnp.moveaxis(lb_g, 0, 2).reshape(DEPTH, 3, -1)

    loss, grad_x, big, small_g = _local_step(x, mem, loss_target, wts, small)
    loss = lax.psum(loss, ("x", "y", "c"))

    stacked = [_stack_big(n, big[n]) for n in _BIG]
    baxes = [_BIG_AXIS[n] for n in _BIG]
    recv = _scatter_chips(stacked, baxes)
    sums = []
    for n, st, rc in zip(_BIG, stacked, recv):
        sums.append(_sum_chips(st, rc, _BIG_AXIS[n], me))
    sib = _swap_sibling(sums)

    out = {}
    for n, s_own, s_sib in zip(_BIG, sums, sib):
        out[n] = _adamw(f"adamw_{n}", [s_own, s_sib], w[n], mo[n], vo[n])

    order = _SMALL_REPL + _SMALL_SHARD
    part = _pack([small_g[n] for n in order])
    total = _sum_devices(_gather_all(part))
    full = dict(zip(order, _unpack(total, [small_g[n].shape for n in order])))
    x_i, y_i = lax.axis_index("x"), lax.axis_index("y")
    chip = 2 * x_i + y_i
    loc = {n: full[n] for n in _SMALL_REPL}
    for n in _SMALL_SHARD:
        wd = w[n].shape[-1]
        loc[n] = lax.dynamic_slice_in_dim(full[n], chip * wd, wd, axis=full[n].ndim - 1)
    gp, wp, mp, vp = (_pack([src[n] for n in order]) for src in (loc, w, mo, vo))
    r128 = gp.shape[0]
    res = _adamw("adamw_small", [gp.reshape(1, r128, 128)], wp.reshape(1, r128, 128), mp.reshape(1, r128, 128),
                 vp.reshape(1, r128, 128))
    shapes = [w[n].shape for n in order]
    unp = [_unpack(r.reshape(r128, 128), shapes) for r in res]
    for i, n in enumerate(order):
        out[n] = tuple(u[i] for u in unp)

    grads = [out[n][0] for n in _NAMES]
    deltas = [out[n][1] for n in _NAMES]
    new_m = [out[n][2] for n in _NAMES]
    new_v = [out[n][3] for n in _NAMES]
    return (loss, grad_x, *grads, *deltas, *new_m, *new_v)
```

```python
import functools
import math

import jax
import jax.numpy as jnp
from jax import lax
from jax.experimental import pallas as pl
from jax.experimental.pallas import tpu as pltpu

F32 = jnp.float32
MM = jnp.bfloat16
ACT = jnp.bfloat16
MESH = pl.DeviceIdType.MESH

DEPTH = 4
CHUNK = 128
CONV_WIDTH = 31
HALO = 32
MEM_HEADS = 4
C_HEAD_DIM = 64
ALPHA = (2.0 * DEPTH) ** 0.25
LN_EPS = 1e-5
ADAM_LR, ADAM_B1, ADAM_B2, ADAM_EPS, ADAM_WD, ADAM_STEP = 0.001, 0.9, 0.999, 1e-08, 0.01, 10

VMEM_CAP_V7X = 64 * 1024 * 1024
VMEM_MAX_REQUEST = 56 * 1024 * 1024
N_CHIPS = 4
N_DEV = 8


def _tile(n, pref):
    if n <= pref:
        return n
    for t in range(pref - pref % 8, 7, -8):
        if n % t == 0:
            return t
    return n


def _nbytes(shape, dtype):
    return math.prod(1 if s is None else s for s in shape) * jnp.dtype(dtype).itemsize


def _vmem_limit(block_bytes, scratch_bytes=0, temp_bytes=0):
    est = 2 * block_bytes + scratch_bytes + temp_bytes
    return int(min(VMEM_MAX_REQUEST, max(16 * 1024 * 1024, est * 5 // 4)))


def _params(sem, block_bytes, scratch_bytes=0, temp_bytes=0):
    return pltpu.CompilerParams(dimension_semantics=sem,
                                vmem_limit_bytes=_vmem_limit(block_bytes, scratch_bytes, temp_bytes))


_DIMS = {"nn": (((1,), (0,)), ((), ())), "nt": (((1,), (1,)), ((), ())), "tn": (((0,), (0,)), ((), ()))}


def _mm(name, mode, grid, a, a_spec, bs, b_specs, outs, out_specs, acc_shape,
        extras=(), extra_specs=(), epilogue=None):
    nb, ne, no = len(bs), len(extras), len(outs)
    nk = grid[2]

    def body(*refs):
        a_ref = refs[0]
        b_refs = refs[1:1 + nb]
        e_refs = refs[1 + nb:1 + nb + ne]
        o_refs = refs[1 + nb + ne:1 + nb + ne + no]
        accs = refs[1 + nb + ne + no:]
        k = pl.program_id(2)

        @pl.when(k == 0)
        def _():
            for acc in accs:
                acc[...] = jnp.zeros_like(acc)

        av = a_ref[...].astype(MM)
        for b_ref, acc in zip(b_refs, accs):
            acc[...] += lax.dot_general(av, b_ref[...].astype(MM), _DIMS[mode], preferred_element_type=F32)

        @pl.when(k == nk - 1)
        def _():
            vals = [acc[...] for acc in accs]
            if epilogue is not None:
                vals = epilogue(vals, [e[...].astype(F32) for e in e_refs])
            for o, v in zip(o_refs, vals):
                o[...] = v.astype(o.dtype)

    blocks = (_nbytes(a_spec.block_shape, a.dtype)
              + sum(_nbytes(s.block_shape, b.dtype) for s, b in zip(b_specs, bs))
              + sum(_nbytes(s.block_shape, e.dtype) for s, e in zip(extra_specs, extras))
              + sum(_nbytes(s.block_shape, o.dtype) for s, o in zip(out_specs, outs)))
    acc_bytes = nb * _nbytes(acc_shape, F32)
    res = pl.pallas_call(
        body, name=name, grid=grid,
        in_specs=[a_spec, *b_specs, *extra_specs], out_specs=list(out_specs), out_shape=list(outs),
        scratch_shapes=[pltpu.VMEM(acc_shape, F32)] * nb,
        compiler_params=_params(("parallel", "parallel", "arbitrary"), blocks, acc_bytes, 4 * acc_bytes),
    )(a, *bs, *extras)
    return res


def _sds(shape, dtype):
    return jax.ShapeDtypeStruct(shape, dtype)


def _dense(name, a, w, out_dtype, extras=(), epilogue=None, n_out=None):
    t, kdim = a.shape
    n = w.shape[1]
    tm, tn, tk = _tile(t, 1024), _tile(n, 512), _tile(kdim, 1024)
    grid = (t // tm, n // tn, kdim // tk)
    return _mm(name, "nn", grid, a, pl.BlockSpec((tm, tk), lambda i, j, k: (i, k)),
               [w], [pl.BlockSpec((tk, tn), lambda i, j, k: (k, j))],
               [_sds((t, n), out_dtype)], [pl.BlockSpec((tm, tn), lambda i, j, k: (i, j))], (tm, tn),
               extras=extras, extra_specs=[pl.BlockSpec((tm, tn), lambda i, j, k: (i, j))] * len(extras),
               epilogue=epilogue)[0]


def _dense_nt(name, a, w, out_dtype, extras=(), epilogue=None):
    t, n = a.shape
    kout = w.shape[0]
    tm, tn, tk = _tile(t, 1024), _tile(kout, 512), _tile(n, 1024)
    grid = (t // tm, kout // tn, n // tk)
    return _mm(name, "nt", grid, a, pl.BlockSpec((tm, tk), lambda i, j, k: (i, k)),
               [w], [pl.BlockSpec((tn, tk), lambda i, j, k: (j, k))],
               [_sds((t, kout), out_dtype)], [pl.BlockSpec((tm, tn), lambda i, j, k: (i, j))], (tm, tn),
               extras=extras, extra_specs=[pl.BlockSpec((tm, tn), lambda i, j, k: (i, j))] * len(extras),
               epilogue=epilogue)[0]


def _dense_tn(name, a, b, out_dtype=MM):
    t, m = a.shape
    n = b.shape[1]
    tm, tn, tk = _tile(m, 512), _tile(n, 512), _tile(t, 1024)
    grid = (m // tm, n // tn, t // tk)
    return _mm(name, "tn", grid, a, pl.BlockSpec((tk, tm), lambda i, j, k: (k, i)),
               [b], [pl.BlockSpec((tk, tn), lambda i, j, k: (k, j))],
               [_sds((m, n), out_dtype)], [pl.BlockSpec((tm, tn), lambda i, j, k: (i, j))], (tm, tn))[0]


_INV_SQRT2 = 0.7071067811865476
_INV_SQRT_2PI = 0.3989422804014327


def _gelu(x):
    return 0.5 * x * (1.0 + lax.erf(x * _INV_SQRT2))


def _gelu_grad(x):
    return 0.5 * (1.0 + lax.erf(x * _INV_SQRT2)) + x * jnp.exp(-0.5 * x * x) * _INV_SQRT_2PI


def _sigmoid(x):
    return 1.0 / (1.0 + jnp.exp(-x))


def _norm_stats(x):
    mu = jnp.mean(x, axis=-1, keepdims=True)
    xc = x - mu
    var = jnp.mean(xc * xc, axis=-1, keepdims=True)
    rstd = lax.rsqrt(var + LN_EPS)
    return xc * rstd, rstd


def _norm_bwd(dy_g, xh, rstd):
    m1 = jnp.mean(dy_g, axis=-1, keepdims=True)
    m2 = jnp.mean(dy_g * xh, axis=-1, keepdims=True)
    return rstd * (dy_g - m1 - xh * m2)


def _rows8(x):
    r, c = x.shape
    return jnp.sum(x.reshape(r // 8, 8, c), axis=0)


def _ln_fwd(name, xh_prev, g_prev, b_prev, f, g, b):
    t, d = f.shape
    tm = _tile(t, 512)

    def body(xp_ref, gp_ref, bp_ref, f_ref, g_ref, b_ref, xh_ref, y_ref, rstd_ref):
        r = ALPHA * (xp_ref[...] * gp_ref[...] + bp_ref[...]) + f_ref[...]
        xh, rstd = _norm_stats(r)
        xh_ref[...] = xh
        y_ref[...] = (xh * g_ref[...] + b_ref[...]).astype(y_ref.dtype)
        rstd_ref[...] = rstd

    row = pl.BlockSpec((tm, d), lambda i: (i, 0))
    vec = pl.BlockSpec((1, d), lambda i: (0, 0))
    return pl.pallas_call(
        body, name=name, grid=(t // tm,),
        in_specs=[row, vec, vec, row, vec, vec],
        out_specs=[row, row, pl.BlockSpec((tm, 1), lambda i: (i, 0))],
        out_shape=[_sds((t, d), F32), _sds((t, d), ACT), _sds((t, 1), F32)],
        compiler_params=_params(("parallel",), 4 * tm * d * 4, 0, 4 * tm * d * 4),
    )(xh_prev, g_prev, b_prev, f, g, b)


def _ln_bwd(name, dy, xh, rstd, g):
    t, d = dy.shape
    tm = _tile(t, 512)
    n = t // tm

    def body(dy_ref, xh_ref, rstd_ref, g_ref, dr_ref, dra_ref, dg_ref, db_ref, dg_acc, db_acc):
        i = pl.program_id(0)

        @pl.when(i == 0)
        def _():
            dg_acc[...] = jnp.zeros_like(dg_acc)
            db_acc[...] = jnp.zeros_like(db_acc)

        dyv = dy_ref[...]
        xhv = xh_ref[...]
        dr = _norm_bwd(dyv * g_ref[...], xhv, rstd_ref[...])
        dr_ref[...] = dr
        dra_ref[...] = dr.astype(dra_ref.dtype)
        dg_acc[...] += _rows8(dyv * xhv)
        db_acc[...] += _rows8(dyv)

        @pl.when(i == n - 1)
        def _():
            dg_ref[...] = jnp.sum(dg_acc[...], axis=0, keepdims=True)
            db_ref[...] = jnp.sum(db_acc[...], axis=0, keepdims=True)

    row = pl.BlockSpec((tm, d), lambda i: (i, 0))
    vec = pl.BlockSpec((1, d), lambda i: (0, 0))
    return pl.pallas_call(
        body, name=name, grid=(n,),
        in_specs=[row, row, pl.BlockSpec((tm, 1), lambda i: (i, 0)), vec],
        out_specs=[row, row, vec, vec],
        out_shape=[_sds((t, d), F32), _sds((t, d), ACT), _sds((1, d), F32), _sds((1, d), F32)],
        scratch_shapes=[pltpu.VMEM((8, d), F32), pltpu.VMEM((8, d), F32)],
        compiler_params=_params(("arbitrary",), 4 * tm * d * 4, 0, 4 * tm * d * 4),
    )(dy, xh, rstd, g)


def _loss_head(xh, g, b, target):
    t, d = xh.shape
    tm = _tile(t, 512)
    n = t // tm

    def body(xh_ref, g_ref, b_ref, tg_ref, dy_ref, loss_ref, acc):
        i = pl.program_id(0)

        @pl.when(i == 0)
        def _():
            acc[...] = jnp.zeros_like(acc)

        err = xh_ref[...] * g_ref[...] + b_ref[...] - tg_ref[...]
        dy_ref[...] = err * (1.0 / d)
        acc[...] += _rows8(err * err)

        @pl.when(i == n - 1)
        def _():
            s = jnp.sum(jnp.sum(acc[...], axis=0, keepdims=True), axis=1, keepdims=True)
            loss_ref[...] = jnp.broadcast_to(s * (0.5 / d), loss_ref.shape)

    row = pl.BlockSpec((tm, d), lambda i: (i, 0))
    vec = pl.BlockSpec((1, d), lambda i: (0, 0))
    dy, loss = pl.pallas_call(
        body, name="loss_head", grid=(n,),
        in_specs=[row, vec, vec, row],
        out_specs=[row, pl.BlockSpec((8, 128), lambda i: (0, 0))],
        out_shape=[_sds((t, d), F32), _sds((8, 128), F32)],
        scratch_shapes=[pltpu.VMEM((8, d), F32)],
        compiler_params=_params(("arbitrary",), 3 * tm * d * 4, 0, 2 * tm * d * 4),
    )(xh, g, b, target)
    return dy, loss[0, 0]


def _causal_w(w):
    r = lax.broadcasted_iota(jnp.int32, w.shape, 0)
    c = lax.broadcasted_iota(jnp.int32, w.shape, 1)
    return jnp.where(r >= c, w, 0.0)


def _gmlp_fwd(name, h, ln_g, ln_b, w_s, b_s_col):
    t = h.shape[0]
    tt = _tile(t, 256)
    wd = 4 * CHUNK

    def body(u_ref, v_ref, g_ref, b_ref, w_ref, bs_ref, ya_ref):
        for gi in range(4):
            ln = slice(gi * CHUNK, (gi + 1) * CHUNK)
            u = _gelu(u_ref[:, ln].astype(F32))
            v = _gelu(v_ref[:, ln].astype(F32))
            xh, _ = _norm_stats(v)
            vg = (xh * g_ref[:, ln] + b_ref[:, ln]).astype(MM)
            w = _causal_w(w_ref[gi]).astype(MM)
            for c in range(tt // CHUNK):
                rs = slice(c * CHUNK, (c + 1) * CHUNK)
                mixed = jnp.dot(w, vg[rs], preferred_element_type=F32) + bs_ref[gi]
                ya_ref[rs, ln] = (u[rs] * mixed).astype(ya_ref.dtype)

    vec = pl.BlockSpec((1, wd), lambda i: (0, 0))
    return pl.pallas_call(
        body, name=name, grid=(t // tt,),
        in_specs=[pl.BlockSpec((tt, wd), lambda i: (i, 0)), pl.BlockSpec((tt, wd), lambda i: (i, 1)), vec, vec,
                  pl.BlockSpec((4, CHUNK, CHUNK), lambda i: (0, 0, 0)), pl.BlockSpec((4, CHUNK, 1), lambda i: (0, 0, 0))],
        out_specs=pl.BlockSpec((tt, wd), lambda i: (i, 0)),
        out_shape=_sds((t, wd), ACT),
        compiler_params=_params(("parallel",), 3 * tt * wd * 4, 0, 8 * tt * CHUNK * 4),
    )(h, h, ln_g, ln_b, w_s, b_s_col)


def _gmlp_bwd(name, h, dyab, ln_g, ln_b, w_s, b_s_col):
    t = h.shape[0]
    tt = _tile(t, 256)
    n = t // tt
    wd = 4 * CHUNK

    def body(u_ref, v_ref, dy_ref, g_ref, b_ref, w_ref, bs_ref, duv_ref, dg_ref, db_ref, dw_ref, dbs_ref,
             dg_acc, db_acc):
        i = pl.program_id(0)

        @pl.when(i == 0)
        def _():
            dg_acc[...] = jnp.zeros_like(dg_acc)
            db_acc[...] = jnp.zeros_like(db_acc)
            dw_ref[...] = jnp.zeros_like(dw_ref)
            dbs_ref[...] = jnp.zeros_like(dbs_ref)

        for gi in range(4):
            ln = slice(gi * CHUNK, (gi + 1) * CHUNK)
            upre = u_ref[:, ln].astype(F32)
            vpre = v_ref[:, ln].astype(F32)
            u = _gelu(upre)
            v = _gelu(vpre)
            xh, rstd = _norm_stats(v)
            gv = g_ref[:, ln]
            vg = (xh * gv + b_ref[:, ln]).astype(MM)
            w = _causal_w(w_ref[gi]).astype(MM)
            dya = dy_ref[:, ln].astype(F32)
            dmixed = dya * u
            dmm = dmixed.astype(MM)
            dvg_parts, mixed_parts = [], []
            dw = jnp.zeros((CHUNK, CHUNK), F32)
            dbs = jnp.zeros((CHUNK, 1), F32)
            for c in range(tt // CHUNK):
                rs = slice(c * CHUNK, (c + 1) * CHUNK)
                mixed_parts.append(jnp.dot(w, vg[rs], preferred_element_type=F32) + bs_ref[gi])
                dw = dw + lax.dot_general(dmm[rs], vg[rs], _DIMS["nt"], preferred_element_type=F32)
                dbs = dbs + jnp.sum(dmixed[rs], axis=1, keepdims=True)
                dvg_parts.append(lax.dot_general(w, dmm[rs], _DIMS["tn"], preferred_element_type=F32))
            mixed = jnp.concatenate(mixed_parts, axis=0)
            dvg = jnp.concatenate(dvg_parts, axis=0)
            dw_ref[gi] += _causal_w(dw)
            dbs_ref[gi] += dbs
            dg_acc[:, ln] += _rows8(dvg * xh)
            db_acc[:, ln] += _rows8(dvg)
            dv = _norm_bwd(dvg * gv, xh, rstd) * _gelu_grad(vpre)
            du = dya * mixed * _gelu_grad(upre)
            duv_ref[:, ln] = du.astype(duv_ref.dtype)
            duv_ref[:, wd + gi * CHUNK: wd + (gi + 1) * CHUNK] = dv.astype(duv_ref.dtype)

        @pl.when(i == n - 1)
        def _():
            dg_ref[...] = jnp.sum(dg_acc[...], axis=0, keepdims=True)
            db_ref[...] = jnp.sum(db_acc[...], axis=0, keepdims=True)

    vec = pl.BlockSpec((1, wd), lambda i: (0, 0))
    wspec = pl.BlockSpec((4, CHUNK, CHUNK), lambda i: (0, 0, 0))
    bspec = pl.BlockSpec((4, CHUNK, 1), lambda i: (0, 0, 0))
    return pl.pallas_call(
        body, name=name, grid=(n,),
        in_specs=[pl.BlockSpec((tt, wd), lambda i: (i, 0)), pl.BlockSpec((tt, wd), lambda i: (i, 1)),
                  pl.BlockSpec((tt, wd), lambda i: (i, 0)), vec, vec, wspec, bspec],
        out_specs=[pl.BlockSpec((tt, 2 * wd), lambda i: (i, 0)), vec, vec, wspec, bspec],
        out_shape=[_sds((t, 2 * wd), ACT), _sds((1, wd), F32), _sds((1, wd), F32),
                   _sds((4, CHUNK, CHUNK), F32), _sds((4, CHUNK, 1), F32)],
        scratch_shapes=[pltpu.VMEM((8, wd), F32), pltpu.VMEM((8, wd), F32)],
        compiler_params=_params(("arbitrary",), 5 * tt * wd * 4, 0, 16 * tt * CHUNK * 4),
    )(h, h, dyab, ln_g, ln_b, w_s, b_s_col)


_ROWS = 256


def _conv_taps(win, cw, lo):
    acc = jnp.zeros((_ROWS, CHUNK), F32)
    for w in range(CONV_WIDTH):
        s = lo(w)
        acc = acc + cw[w:w + 1, :] * win[s:s + _ROWS, :]
    return acc


def _conv_fwd(name, h, n_ex, cw, cb, gg, gb):
    t = h.shape[0]
    s = t // n_ex
    nt = s // _ROWS

    def body(a_ref, gt_ref, cw_ref, cb_ref, gg_ref, gb_ref, yb_ref, hh):
        hh[0:HALO, :] = jnp.zeros((HALO, CHUNK), F32)
        hh[HALO:HALO + s, :] = a_ref[...].astype(F32) * _sigmoid(gt_ref[...].astype(F32))
        cwv = cw_ref[...]

        def tile(i, carry):
            r0 = pl.multiple_of(i * _ROWS, _ROWS)
            win = hh[pl.ds(r0, _ROWS + HALO), :]
            c = _conv_taps(win, cwv, lambda w: w + HALO - (CONV_WIDTH - 1)) + cb_ref[...]
            xh, _ = _norm_stats(c)
            hg = xh * gg_ref[...] + gb_ref[...]
            yb_ref[pl.ds(r0, _ROWS), :] = (hg * _sigmoid(hg)).astype(yb_ref.dtype)
            return carry

        lax.fori_loop(0, nt, tile, 0)

    vec = pl.BlockSpec((1, CHUNK), lambda g, b: (0, g))
    return pl.pallas_call(
        body, name=name, grid=(4, n_ex),
        in_specs=[pl.BlockSpec((s, CHUNK), lambda g, b: (b, 8 + g)), pl.BlockSpec((s, CHUNK), lambda g, b: (b, 12 + g)),
                  pl.BlockSpec((CONV_WIDTH, CHUNK), lambda g, b: (0, g)), vec, vec, vec],
        out_specs=pl.BlockSpec((s, CHUNK), lambda g, b: (b, g)),
        out_shape=_sds((t, 4 * CHUNK), ACT),
        scratch_shapes=[pltpu.VMEM((s + HALO, CHUNK), F32)],
        compiler_params=_params(("parallel", "parallel"), 3 * s * CHUNK * 4, (s + HALO) * CHUNK * 4, 4 * s * CHUNK * 4),
    )(h, h, cw, cb, gg, gb)


def _conv_bwd(name, h, dyab, n_ex, cw, cb, gg, gb):
    t = h.shape[0]
    s = t // n_ex
    nt = s // _ROWS

    def body(a_ref, gt_ref, dy_ref, cw_ref, cb_ref, gg_ref, gb_ref,
             da_ref, dgt_ref, dcw_ref, dcb_ref, dgg_ref, dgb_ref, hh, dcs, acc):
        b = pl.program_id(1)

        @pl.when(b == 0)
        def _():
            dcw_ref[...] = jnp.zeros_like(dcw_ref)
            dcb_ref[...] = jnp.zeros_like(dcb_ref)
            dgg_ref[...] = jnp.zeros_like(dgg_ref)
            dgb_ref[...] = jnp.zeros_like(dgb_ref)

        hh[0:HALO, :] = jnp.zeros((HALO, CHUNK), F32)
        hh[HALO:HALO + s, :] = a_ref[...].astype(F32) * _sigmoid(gt_ref[...].astype(F32))
        dcs[s:s + HALO, :] = jnp.zeros((HALO, CHUNK), F32)
        acc[...] = jnp.zeros_like(acc)
        cwv = cw_ref[...]
        off = HALO - (CONV_WIDTH - 1)

        def tile1(i, carry):
            r0 = pl.multiple_of(i * _ROWS, _ROWS)
            win = hh[pl.ds(r0, _ROWS + HALO), :]
            c = _conv_taps(win, cwv, lambda w: w + off) + cb_ref[...]
            xh, rstd = _norm_stats(c)
            hg = xh * gg_ref[...] + gb_ref[...]
            sg = _sigmoid(hg)
            dhg = dy_ref[pl.ds(r0, _ROWS), :].astype(F32) * (sg * (1.0 + hg * (1.0 - sg)))
            acc[32:40, :] += _rows8(dhg * xh)
            acc[40:48, :] += _rows8(dhg)
            dc = _norm_bwd(dhg * gg_ref[...], xh, rstd)
            dcs[pl.ds(r0, _ROWS), :] = dc
            acc[48:56, :] += _rows8(dc)
            for w in range(CONV_WIDTH):
                acc[w:w + 1, :] += jnp.sum(dc * win[w + off:w + off + _ROWS, :], axis=0, keepdims=True)
            return carry

        lax.fori_loop(0, nt, tile1, 0)

        def tile2(i, carry):
            r0 = pl.multiple_of(i * _ROWS, _ROWS)
            win = dcs[pl.ds(r0, _ROWS + HALO), :]
            dhh = _conv_taps(win, cwv, lambda w: CONV_WIDTH - 1 - w)
            av = a_ref[pl.ds(r0, _ROWS), :].astype(F32)
            sg = _sigmoid(gt_ref[pl.ds(r0, _ROWS), :].astype(F32))
            da_ref[pl.ds(r0, _ROWS), :] = (dhh * sg).astype(da_ref.dtype)
            dgt_ref[pl.ds(r0, _ROWS), :] = (dhh * av * sg * (1.0 - sg)).astype(dgt_ref.dtype)
            return carry

        lax.fori_loop(0, nt, tile2, 0)
        dcw_ref[...] += acc[0:CONV_WIDTH, :]
        dgg_ref[...] += jnp.sum(acc[32:40, :], axis=0, keepdims=True)
        dgb_ref[...] += jnp.sum(acc[40:48, :], axis=0, keepdims=True)
        dcb_ref[...] += jnp.sum(acc[48:56, :], axis=0, keepdims=True)

    vec = pl.BlockSpec((1, CHUNK), lambda g, b: (0, g))
    tap = pl.BlockSpec((CONV_WIDTH, CHUNK), lambda g, b: (0, g))
    seq = pl.BlockSpec((s, CHUNK), lambda g, b: (b, g))
    return pl.pallas_call(
        body, name=name, grid=(4, n_ex),
        in_specs=[pl.BlockSpec((s, CHUNK), lambda g, b: (b, 8 + g)), pl.BlockSpec((s, CHUNK), lambda g, b: (b, 12 + g)),
                  pl.BlockSpec((s, CHUNK), lambda g, b: (b, 4 + g)), tap, vec, vec, vec],
        out_specs=[seq, seq, tap, vec, vec, vec],
        out_shape=[_sds((t, 4 * CHUNK), ACT), _sds((t, 4 * CHUNK), ACT), _sds((CONV_WIDTH, 4 * CHUNK), F32),
                   _sds((1, 4 * CHUNK), F32), _sds((1, 4 * CHUNK), F32), _sds((1, 4 * CHUNK), F32)],
        scratch_shapes=[pltpu.VMEM((s + HALO, CHUNK), F32), pltpu.VMEM((s + HALO, CHUNK), F32),
                        pltpu.VMEM((56, CHUNK), F32)],
        compiler_params=_params(("parallel", "arbitrary"), 5 * s * CHUNK * 4, 2 * (s + HALO) * CHUNK * 4, 4 * s * CHUNK * 4),
    )(h, h, dyab, cw, cb, gg, gb)


_TQ = 256
_SB_UNROLL = 2


def _tri(kind):
    r = lax.broadcasted_iota(jnp.int32, (_TQ, _TQ), 0)
    c = lax.broadcasted_iota(jnp.int32, (_TQ, _TQ), 1)
    m = {"gt": r > c, "le": r <= c, "lt": r < c}[kind]
    return jnp.where(m, 1.0, 0.0).astype(jnp.bfloat16)


def _split_dot(x, tri2):
    hi = x.astype(jnp.bfloat16)
    lo = (x - hi.astype(F32)).astype(jnp.bfloat16)
    return jnp.dot(jnp.concatenate([hi, lo], axis=1), tri2, preferred_element_type=F32)


def _neg_abs(x):
    bits = lax.bitcast_convert_type(x, jnp.uint32) | jnp.uint32(0x80000000)
    return lax.bitcast_convert_type(bits, F32)


def _log_not_beta(nz):
    return jnp.minimum(nz, 0.0) - jnp.log(1.0 + jnp.exp(_neg_abs(nz)))


def _sb_fwd(name, qkv, n_ex):
    t = qkv.shape[0]
    d = qkv.shape[1] // 3
    npair = d // CHUNK
    s = t // n_ex
    nq = s // _TQ
    neg_a = -(C_HEAD_DIM ** -0.5)

    def body(q_ref, k_ref, v_ref, o_ref, lt_ref, o_acc, c_acc):
        i = pl.program_id(2)
        first = lax.broadcasted_iota(jnp.int32, (_TQ, CHUNK), 1) < C_HEAD_DIM
        q2 = q_ref[...]
        zero = jnp.zeros_like(q2)
        qh = [jnp.where(first, q2, zero), jnp.where(first, zero, q2)]
        tri2 = jnp.concatenate([_tri("gt")] * 2, axis=0)
        o_acc[...] = jnp.zeros_like(o_acc)
        c_acc[...] = jnp.zeros_like(c_acc)

        def tiles(js, mask):
            work = [(a, hd) for a in range(len(js)) for hd in range(2)]
            rows = [pl.ds(pl.multiple_of(j * _TQ, _TQ), _TQ) for j in js]
            kts = [k_ref[r, :] for r in rows]
            vts = [v_ref[r, :] for r in rows]
            nzs = {w: lax.dot_general(qh[w[1]], kts[w[0]], _DIMS["nt"], preferred_element_type=F32) * neg_a for w in work}
            lns = {w: _log_not_beta(nzs[w]) for w in work}
            if mask is not None:
                lns = {w: jnp.where(mask, lns[w], 0.0) for w in work}
            locs = {w: _split_dot(lns[w], tri2) for w in work}
            laters = {}
            for hd in range(2):
                carry = c_acc[hd]
                for a in range(len(js)):
                    laters[a, hd] = carry + locs[a, hd]
                    carry = laters[a, hd][:, 0:1] + lns[a, hd][:, 0:1]
                c_acc[hd] = carry
            atts = {w: jnp.exp(lns[w] - nzs[w] + laters[w]) for w in work}
            if mask is not None:
                atts = {w: jnp.where(mask, atts[w], 0.0) for w in work}
            for hd in range(2):
                acc = o_acc[hd]
                for a in range(len(js)):
                    acc = acc + jnp.dot(atts[a, hd].astype(MM), vts[a], preferred_element_type=F32)
                o_acc[hd] = acc

        tiles([i], lax.broadcasted_iota(jnp.int32, (_TQ, _TQ), 1) < lax.broadcasted_iota(jnp.int32, (_TQ, _TQ), 0))

        rem = i % _SB_UNROLL

        def single(jj, carry):
            tiles([i - 1 - jj], None)
            return carry

        lax.fori_loop(0, rem, single, 0)

        def step(jj, carry):
            j = i - 1 - rem - _SB_UNROLL * jj
            tiles([j - a for a in range(_SB_UNROLL)], None)
            return carry

        lax.fori_loop(0, i // _SB_UNROLL, step, 0)
        o_ref[...] = jnp.where(first, o_acc[0], o_acc[1])
        lt_ref[:, 0:1] = c_acc[0]
        lt_ref[:, 1:2] = c_acc[1]

    return pl.pallas_call(
        body, name=name, grid=(n_ex, npair, nq),
        in_specs=[pl.BlockSpec((_TQ, CHUNK), lambda b, p, i: (b * nq + i, p)),
                  pl.BlockSpec((s, CHUNK), lambda b, p, i: (b, npair + p)),
                  pl.BlockSpec((s, CHUNK), lambda b, p, i: (b, 2 * npair + p))],
        out_specs=[pl.BlockSpec((_TQ, CHUNK), lambda b, p, i: (b * nq + i, p)),
                   pl.BlockSpec((None, _TQ, 2), lambda b, p, i: (p, b * nq + i, 0))],
        out_shape=[_sds((t, d), F32), _sds((npair, t, 2), F32)],
        scratch_shapes=[pltpu.VMEM((2, _TQ, CHUNK), F32), pltpu.VMEM((2, _TQ, 1), F32)],
        compiler_params=_params(("parallel", "parallel", "arbitrary"), 2 * s * CHUNK * 2 + 4 * _TQ * CHUNK * 4,
                                4 * _TQ * CHUNK * 4, 24 * _TQ * _TQ * 4),
    )(qkv, qkv, qkv)


def _sb_bwd(name, qkv, do, ltot, n_ex):
    t = qkv.shape[0]
    d = qkv.shape[1] // 3
    npair = d // CHUNK
    s = t // n_ex
    nq = s // _TQ
    scale = C_HEAD_DIM ** -0.5
    neg_a = -scale

    def body(q_ref, k_ref, v_ref, do_ref, lt_ref, dq_ref, dk_ref, dv_ref, dq_acc, cp_acc, cg_acc):
        i = pl.program_id(2)

        @pl.when(i == 0)
        def _():
            dk_ref[...] = jnp.zeros_like(dk_ref)
            dv_ref[...] = jnp.zeros_like(dv_ref)

        first = lax.broadcasted_iota(jnp.int32, (_TQ, CHUNK), 1) < C_HEAD_DIM
        q2 = q_ref[...]
        do2 = do_ref[...]
        qs = (q2 * scale).astype(q2.dtype)
        zero = jnp.zeros_like(q2)
        qh = [jnp.where(first, q2, zero), jnp.where(first, zero, q2)]
        doh = [jnp.where(first, do2, zero), jnp.where(first, zero, do2)]
        lt = [lt_ref[:, 0:1], lt_ref[:, 1:2]]
        tri2_le = jnp.concatenate([_tri("le")] * 2, axis=0)
        tri_lt = _tri("lt").astype(MM)
        dq_acc[...] = jnp.zeros_like(dq_acc)
        cp_acc[...] = jnp.zeros_like(cp_acc)
        cg_acc[...] = jnp.zeros_like(cg_acc)

        def tiles(js, mask):
            na = len(js)
            work = [(a, hd) for a in range(na) for hd in range(2)]
            last = slice(_TQ - 1, _TQ)
            rows = [pl.ds(pl.multiple_of(j * _TQ, _TQ), _TQ) for j in js]
            kts = [k_ref[r, :] for r in rows]
            vts = [v_ref[r, :] for r in rows]
            ksc = [(kt * scale).astype(kt.dtype) for kt in kts]
            nzs = {w: lax.dot_general(qh[w[1]], kts[w[0]], _DIMS["nt"], preferred_element_type=F32) * neg_a for w in work}
            datts = {w: lax.dot_general(doh[w[1]], vts[w[0]], _DIMS["nt"], preferred_element_type=F32) for w in work}
            lns = {w: _log_not_beta(nzs[w]) for w in work}
            if mask is not None:
                lns = {w: jnp.where(mask, lns[w], 0.0) for w in work}
            pins = {w: _split_dot(lns[w], tri2_le) for w in work}
            lss = {w: lns[w] - nzs[w] for w in work}
            atts = {}
            for hd in range(2):
                cp = cp_acc[hd]
                for a in range(na):
                    atts[a, hd] = jnp.exp(lss[a, hd] + ((lt[hd] - cp) - pins[a, hd]))
                    cp = cp + pins[a, hd][:, last]
                cp_acc[hd] = cp
            if mask is not None:
                atts = {w: jnp.where(mask, atts[w], 0.0) for w in work}
            gs = {w: datts[w] * atts[w] for w in work}
            locg = {w: jnp.dot(gs[w].astype(MM), tri_lt, preferred_element_type=F32) for w in work}
            dzs = {}
            for hd in range(2):
                carry = cg_acc[hd]
                for a in range(na):
                    big = carry + locg[a, hd]
                    dzs[a, hd] = gs[a, hd] - (gs[a, hd] + big) * jnp.exp(lss[a, hd])
                    carry = big[:, last] + gs[a, hd][:, last]
                cg_acc[hd] = carry
            if mask is not None:
                dzs = {w: jnp.where(mask, dzs[w], 0.0) for w in work}
            dzs = {w: dzs[w].astype(MM) for w in work}
            attm = {w: atts[w].astype(MM) for w in work}
            for hd in range(2):
                acc = dq_acc[hd]
                for a in range(na):
                    acc = acc + jnp.dot(dzs[a, hd], ksc[a], preferred_element_type=F32)
                dq_acc[hd] = acc
            for a in range(na):
                dk0, dk1 = [lax.dot_general(dzs[a, hd], qs, _DIMS["tn"], preferred_element_type=F32) for hd in range(2)]
                dv0, dv1 = [lax.dot_general(attm[a, hd], do2, _DIMS["tn"], preferred_element_type=F32) for hd in range(2)]
                dk_ref[rows[a], :] += jnp.where(first, dk0, dk1)
                dv_ref[rows[a], :] += jnp.where(first, dv0, dv1)

        def step(jj, carry):
            tiles([_SB_UNROLL * jj + a for a in range(_SB_UNROLL)], None)
            return carry

        lax.fori_loop(0, i // _SB_UNROLL, step, 0)

        def single(j, carry):
            tiles([j], None)
            return carry

        lax.fori_loop(i - i % _SB_UNROLL, i, single, 0)
        tiles([i], lax.broadcasted_iota(jnp.int32, (_TQ, _TQ), 1) < lax.broadcasted_iota(jnp.int32, (_TQ, _TQ), 0))
        dq_ref[...] = jnp.where(first, dq_acc[0], dq_acc[1]).astype(dq_ref.dtype)

    qspec = pl.BlockSpec((_TQ, CHUNK), lambda b, p, i: (b * nq + i, p))
    kv_out = pl.BlockSpec((s, CHUNK), lambda b, p, i: (b, p))
    return pl.pallas_call(
        body, name=name, grid=(n_ex, npair, nq),
        in_specs=[qspec, pl.BlockSpec((s, CHUNK), lambda b, p, i: (b, npair + p)),
                  pl.BlockSpec((s, CHUNK), lambda b, p, i: (b, 2 * npair + p)), qspec,
                  pl.BlockSpec((None, _TQ, 2), lambda b, p, i: (p, b * nq + i, 0))],
        out_specs=[qspec, kv_out, kv_out],
        out_shape=[_sds((t, d), ACT), _sds((t, d), F32), _sds((t, d), F32)],
        scratch_shapes=[pltpu.VMEM((2, _TQ, CHUNK), F32), pltpu.VMEM((2, _TQ, 1), F32), pltpu.VMEM((2, _TQ, 1), F32)],
        compiler_params=_params(("parallel", "parallel", "arbitrary"), 2 * s * CHUNK * 2 + 2 * s * CHUNK * 4,
                                4 * _TQ * CHUNK * 4, 32 * _TQ * _TQ * 4),
    )(qkv, qkv, qkv, do, ltot)


def _xattn_fwd(name, q, kk, vv, n_ex):
    t, d = q.shape
    m = kk.shape[0] // n_ex
    s = t // n_ex
    tq = _tile(s, 512)
    nq = s // tq
    hd_dim = d // MEM_HEADS
    scale = hd_dim ** -0.5

    def body(q_ref, k_ref, v_ref, o_ref):
        for hd in range(MEM_HEADS):
            ln = slice(hd * hd_dim, (hd + 1) * hd_dim)
            sc = lax.dot_general(q_ref[:, ln], k_ref[:, ln], _DIMS["nt"], preferred_element_type=F32) * scale
            p = jnp.exp(sc - jnp.max(sc, axis=-1, keepdims=True))
            p = p / jnp.sum(p, axis=-1, keepdims=True)
            o_ref[:, ln] = jnp.dot(p.astype(MM), v_ref[:, ln], preferred_element_type=F32).astype(o_ref.dtype)

    qspec = pl.BlockSpec((tq, d), lambda b, i: (b * nq + i, 0))
    kspec = pl.BlockSpec((m, d), lambda b, i: (b, 0))
    return pl.pallas_call(
        body, name=name, grid=(n_ex, nq), in_specs=[qspec, kspec, kspec], out_specs=qspec,
        out_shape=_sds((t, d), ACT),
        compiler_params=_params(("parallel", "parallel"), 2 * tq * d * 2 + 2 * m * d * 2, 0, 6 * tq * m * 4),
    )(q, kk, vv)


def _xattn_bwd(name, q, kk, vv, do, n_ex):
    t, d = q.shape
    m = kk.shape[0] // n_ex
    s = t // n_ex
    tq = _tile(s, 512)
    nq = s // tq
    hd_dim = d // MEM_HEADS
    scale = hd_dim ** -0.5

    def body(q_ref, k_ref, v_ref, do_ref, dq_ref, dk_ref, dv_ref):
        i = pl.program_id(1)

        @pl.when(i == 0)
        def _():
            dk_ref[...] = jnp.zeros_like(dk_ref)
            dv_ref[...] = jnp.zeros_like(dv_ref)

        for hd in range(MEM_HEADS):
            ln = slice(hd * hd_dim, (hd + 1) * hd_dim)
            qv, kv, vv_, dov = q_ref[:, ln], k_ref[:, ln], v_ref[:, ln], do_ref[:, ln]
            sc = lax.dot_general(qv, kv, _DIMS["nt"], preferred_element_type=F32) * scale
            p = jnp.exp(sc - jnp.max(sc, axis=-1, keepdims=True))
            p = p / jnp.sum(p, axis=-1, keepdims=True)
            dp = lax.dot_general(dov, vv_, _DIMS["nt"], preferred_element_type=F32)
            ds = (p * (dp - jnp.sum(p * dp, axis=-1, keepdims=True)) * scale).astype(MM)
            dq_ref[:, ln] = jnp.dot(ds, kv, preferred_element_type=F32).astype(dq_ref.dtype)
            dk_ref[:, ln] += lax.dot_general(ds, qv, _DIMS["tn"], preferred_element_type=F32)
            dv_ref[:, ln] += lax.dot_general(p.astype(MM), dov, _DIMS["tn"], preferred_element_type=F32)

    qspec = pl.BlockSpec((tq, d), lambda b, i: (b * nq + i, 0))
    kspec = pl.BlockSpec((m, d), lambda b, i: (b, 0))
    return pl.pallas_call(
        body, name=name, grid=(n_ex, nq), in_specs=[qspec, kspec, kspec, qspec], out_specs=[qspec, kspec, kspec],
        out_shape=[_sds((t, d), ACT), _sds((n_ex * m, d), F32), _sds((n_ex * m, d), F32)],
        compiler_params=_params(("parallel", "arbitrary"), 3 * tq * d * 2 + 2 * m * d * 2 + 2 * m * d * 4, 0,
                                8 * tq * m * 4),
    )(q, kk, vv, do)


def _ffn_up(name, y, w1, w3, layer):
    t, d = y.shape
    f = w1.shape[-1]
    tm = _tile(t, 512)
    wspec = pl.BlockSpec((None, None, d, f), lambda i, j, k: (j, layer, 0, 0))
    hspec = pl.BlockSpec((None, tm, f), lambda i, j, k: (j, i, 0))

    def epi(vals, _):
        h1, h3 = vals
        return [h1, h3, h1 * _sigmoid(h1) * h3]

    return _mm(name, "nn", (t // tm, N_CHIPS, 1), y, pl.BlockSpec((tm, d), lambda i, j, k: (i, 0)),
               [w1, w3], [wspec, wspec], [_sds((N_CHIPS, t, f), ACT)] * 3, [hspec] * 3, (tm, f), epilogue=epi)


def _ffn_down(name, g, w2, layer):
    _, t, f = g.shape
    d = w2.shape[-1]
    tm, tn = _tile(t, 1024), _tile(d, 512)
    return _mm(name, "nn", (t // tm, d // tn, N_CHIPS), g, pl.BlockSpec((None, tm, f), lambda i, j, k: (k, i, 0)),
               [w2], [pl.BlockSpec((None, None, f, tn), lambda i, j, k: (layer, k, 0, j))],
               [_sds((t, d), F32)], [pl.BlockSpec((tm, tn), lambda i, j, k: (i, j))], (tm, tn))[0]


def _ffn_down_bwd(name, dr, w2, h1, h3, layer):
    t, d = dr.shape
    f = w2.shape[-2]
    tm = _tile(t, 512)
    hspec = pl.BlockSpec((None, tm, f), lambda i, j, k: (j, i, 0))

    def epi(vals, ex):
        dg, = vals
        h1v, h3v = ex
        sg = _sigmoid(h1v)
        return [dg * h3v * (sg * (1.0 + h1v * (1.0 - sg))), dg * h1v * sg]

    return _mm(name, "nt", (t // tm, N_CHIPS, 1), dr, pl.BlockSpec((tm, d), lambda i, j, k: (i, 0)),
               [w2], [pl.BlockSpec((None, None, f, d), lambda i, j, k: (layer, j, 0, 0))],
               [_sds((N_CHIPS, t, f), ACT)] * 2, [hspec] * 2, (tm, f),
               extras=[h1, h3], extra_specs=[hspec, hspec], epilogue=epi)


def _ffn_up_bwd(name, dh, w, layer, extras, epilogue):
    _, t, f = dh.shape
    d = w.shape[-2]
    tm, tn = _tile(t, 1024), _tile(d, 512)
    ospec = pl.BlockSpec((tm, tn), lambda i, j, k: (i, j))
    return _mm(name, "nt", (t // tm, d // tn, N_CHIPS), dh, pl.BlockSpec((None, tm, f), lambda i, j, k: (k, i, 0)),
               [w], [pl.BlockSpec((None, None, tn, f), lambda i, j, k: (k, layer, j, 0))],
               [_sds((t, d), F32)], [ospec], (tm, tn),
               extras=extras, extra_specs=[ospec] * len(extras), epilogue=epilogue)[0]


def _ffn_wgrad_up(name, y, dh1, dh3):
    t, d = y.shape
    f = dh1.shape[-1]
    tm, tk = _tile(d, 512), _tile(t, 1024)
    hspec = pl.BlockSpec((None, tk, f), lambda i, j, k: (j, k, 0))
    ospec = pl.BlockSpec((None, tm, f), lambda i, j, k: (j, i, 0))
    return _mm(name, "tn", (d // tm, N_CHIPS, t // tk), y, pl.BlockSpec((tk, tm), lambda i, j, k: (k, i)),
               [dh1, dh3], [hspec, hspec], [_sds((N_CHIPS, d, f), MM)] * 2, [ospec, ospec], (tm, f))


def _ffn_wgrad_down(name, g, dr):
    _, t, f = g.shape
    d = dr.shape[1]
    tn, tk = _tile(d, 512), _tile(t, 1024)
    return _mm(name, "tn", (N_CHIPS, d // tn, t // tk), g, pl.BlockSpec((None, tk, f), lambda i, j, k: (i, k, 0)),
               [dr], [pl.BlockSpec((tk, tn), lambda i, j, k: (k, j))],
               [_sds((N_CHIPS, f, d), MM)], [pl.BlockSpec((None, f, tn), lambda i, j, k: (i, 0, j))], (f, tn))[0]


def _proj_cols(name, y, w, layer, out_dtype):
    t, kdim = y.shape
    wd = w.shape[-1]
    tn = _tile(wd, 512)
    per = wd // tn
    tm = _tile(t, 1024)
    return _mm(name, "nn", (t // tm, N_CHIPS * per, 1), y, pl.BlockSpec((tm, kdim), lambda i, j, k: (i, 0)),
               [w], [pl.BlockSpec((None, None, kdim, tn), lambda i, j, k: (j // per, layer, 0, j % per))],
               [_sds((t, N_CHIPS * wd), out_dtype)], [pl.BlockSpec((tm, tn), lambda i, j, k: (i, j))], (tm, tn))[0]


def _proj_cols_bwd(name, dh, w, layer, extras, epilogue):
    t = dh.shape[0]
    kdim, wd = w.shape[-2], w.shape[-1]
    tm, tn = _tile(t, 1024), _tile(kdim, 512)
    ospec = pl.BlockSpec((tm, tn), lambda i, j, k: (i, j))
    return _mm(name, "nt", (t // tm, kdim // tn, N_CHIPS), dh, pl.BlockSpec((tm, wd), lambda i, j, k: (i, k)),
               [w], [pl.BlockSpec((None, None, tn, wd), lambda i, j, k: (k, layer, j, 0))],
               [_sds((t, kdim), F32)], [ospec], (tm, tn),
               extras=extras, extra_specs=[ospec] * len(extras), epilogue=epilogue)[0]


def _proj_cols_wgrad(name, y, dh):
    t, kdim = y.shape
    wd = dh.shape[1] // N_CHIPS
    tm, tk = _tile(kdim, 512), _tile(t, 1024)
    return _mm(name, "tn", (kdim // tm, N_CHIPS, t // tk), y, pl.BlockSpec((tk, tm), lambda i, j, k: (k, i)),
               [dh], [pl.BlockSpec((tk, wd), lambda i, j, k: (k, j))],
               [_sds((N_CHIPS, kdim, wd), MM)], [pl.BlockSpec((None, tm, wd), lambda i, j, k: (j, i, 0))], (tm, wd))[0]


def _coords():
    return lax.axis_index("x"), lax.axis_index("y"), lax.axis_index("c")


def _chip_peers(x, y):
    return [(1 - x, y), (x, 1 - y), (1 - x, 1 - y)]


def _slot(ref, axis, k):
    return ref.at[k] if axis == 0 else ref.at[:, k]


_ANY = pl.BlockSpec(memory_space=pl.ANY)


def _gather_chips(shards, axes):
    n = len(shards)

    def body(*refs):
        ins, outs = refs[:n], refs[n:2 * n]
        send_sems, recv_sems, loc_sems = refs[2 * n:]
        x, y, c = _coords()
        me = 2 * x + y
        copies = []
        for t in range(n):
            dst = _slot(outs[t], axes[t], me)
            cp = pltpu.make_async_copy(ins[t], dst, loc_sems.at[t])
            cp.start()
            copies.append(cp)
            for j, chip in enumerate(_chip_peers(x, y)):
                cp = pltpu.make_async_remote_copy(src_ref=ins[t], dst_ref=dst, send_sem=send_sems.at[t, j],
                                                  recv_sem=recv_sems.at[t, j], device_id=(*chip, c), device_id_type=MESH)
                cp.start()
                copies.append(cp)
        for cp in copies:
            cp.wait()

    def out_shape(a, ax):
        sh = (N_CHIPS, *a.shape) if ax == 0 else (a.shape[0], N_CHIPS, *a.shape[1:])
        return _sds(sh, a.dtype)

    return pl.pallas_call(
        body, name="gather_weights", in_specs=[_ANY] * n, out_specs=[_ANY] * n,
        out_shape=[out_shape(a, ax) for a, ax in zip(shards, axes)],
        scratch_shapes=[pltpu.SemaphoreType.DMA((n, 3)), pltpu.SemaphoreType.DMA((n, 3)), pltpu.SemaphoreType.DMA((n,))],
    )(*shards)


def _scatter_chips(grads, axes):
    n = len(grads)

    def body(*refs):
        ins, outs = refs[:n], refs[n:2 * n]
        send_sems, recv_sems = refs[2 * n:]
        x, y, c = _coords()
        copies = []
        for t in range(n):
            for j, chip in enumerate(_chip_peers(x, y)):
                cp = pltpu.make_async_remote_copy(src_ref=_slot(ins[t], axes[t], 2 * chip[0] + chip[1]), dst_ref=outs[t].at[j],
                                                  send_sem=send_sems.at[t, j], recv_sem=recv_sems.at[t, j],
                                                  device_id=(*chip, c), device_id_type=MESH)
                cp.start()
                copies.append(cp)
        for cp in copies:
            cp.wait()

    def out_shape(a, ax):
        loc = a.shape[1:] if ax == 0 else (a.shape[0], *a.shape[2:])
        return _sds((3, *loc), a.dtype)

    return pl.pallas_call(
        body, name="scatter_grads", in_specs=[_ANY] * n, out_specs=[_ANY] * n,
        out_shape=[out_shape(a, ax) for a, ax in zip(grads, axes)],
        scratch_shapes=[pltpu.SemaphoreType.DMA((n, 3)), pltpu.SemaphoreType.DMA((n, 3))],
    )(*grads)


def _swap_sibling(arrs):
    n = len(arrs)

    def body(*refs):
        ins, outs = refs[:n], refs[n:2 * n]
        send_sems, recv_sems = refs[2 * n:]
        x, y, c = _coords()
        copies = []
        for t in range(n):
            cp = pltpu.make_async_remote_copy(src_ref=ins[t], dst_ref=outs[t], send_sem=send_sems.at[t],
                                              recv_sem=recv_sems.at[t], device_id=(x, y, 1 - c), device_id_type=MESH)
            cp.start()
            copies.append(cp)
        for cp in copies:
            cp.wait()

    return pl.pallas_call(
        body, name="swap_sibling", in_specs=[_ANY] * n, out_specs=[_ANY] * n,
        out_shape=[_sds(a.shape, a.dtype) for a in arrs],
        scratch_shapes=[pltpu.SemaphoreType.DMA((n,)), pltpu.SemaphoreType.DMA((n,))],
    )(*arrs)


def _gather_all(part):
    def body(in_ref, out_ref, send_sems, recv_sems, loc_sem):
        x, y, c = _coords()
        dst = out_ref.at[4 * x + 2 * y + c]
        copies = [pltpu.make_async_copy(in_ref, dst, loc_sem)]
        for r in range(1, N_DEV):
            fx, fy, fc = (r >> 2) & 1, (r >> 1) & 1, r & 1
            peer = (x ^ fx, y ^ fy, c ^ fc)
            copies.append(pltpu.make_async_remote_copy(src_ref=in_ref, dst_ref=dst, send_sem=send_sems.at[r - 1],
                                                       recv_sem=recv_sems.at[r - 1], device_id=peer, device_id_type=MESH))
        for cp in copies:
            cp.start()
        for cp in copies:
            cp.wait()

    return pl.pallas_call(
        body, name="gather_small_grads", in_specs=[_ANY], out_specs=_ANY,
        out_shape=_sds((N_DEV, *part.shape), part.dtype),
        scratch_shapes=[pltpu.SemaphoreType.DMA((N_DEV - 1,)), pltpu.SemaphoreType.DMA((N_DEV - 1,)), pltpu.SemaphoreType.DMA],
    )(part)


def _sum_chips(grad, recv, axis, me):
    _, ll, rr, cc = recv.shape
    tr = _tile(rr, 512)

    def body(me_ref, g_ref, r0_ref, r1_ref, r2_ref, o_ref):
        o_ref[...] = ((g_ref[...].astype(F32) + r0_ref[...].astype(F32)) + r1_ref[...].astype(F32)) + r2_ref[...].astype(F32)

    if axis == 0:
        gspec = pl.BlockSpec((None, None, tr, cc), lambda l, r, m: (m[0], l, r, 0))
    else:
        gspec = pl.BlockSpec((None, None, tr, cc), lambda l, r, m: (l, m[0], r, 0))
    rspecs = [pl.BlockSpec((None, None, tr, cc), functools.partial(lambda l, r, m, j: (j, l, r, 0), j=j)) for j in range(3)]
    return pl.pallas_call(
        body, name="sum_chip_grads",
        grid_spec=pltpu.PrefetchScalarGridSpec(
            num_scalar_prefetch=1, grid=(ll, rr // tr), in_specs=[gspec, *rspecs],
            out_specs=pl.BlockSpec((None, tr, cc), lambda l, r, m: (l, r, 0))),
        out_shape=_sds((ll, rr, cc), F32),
        compiler_params=_params(("parallel", "parallel"), 4 * tr * cc * 2 + tr * cc * 4, 0, 2 * tr * cc * 4),
    )(me, grad, recv, recv, recv)


def _adamw_math(w, g, m, v):
    m = ADAM_B1 * m + (1.0 - ADAM_B1) * g
    v = ADAM_B2 * v + (1.0 - ADAM_B2) * (g * g)
    m_hat = m / (1.0 - ADAM_B1 ** ADAM_STEP)
    v_hat = v / (1.0 - ADAM_B2 ** ADAM_STEP)
    delta = -ADAM_LR * (m_hat / (jnp.sqrt(v_hat) + ADAM_EPS) + ADAM_WD * w)
    return delta, m, v


def _adamw(name, parts, w, m, v):
    ll, rr, cc = w.shape
    tr = _tile(rr, 256)
    npart = len(parts)

    def body(*refs):
        p_refs = refs[:npart]
        w_ref, m_ref, v_ref, g_ref, d_ref, nm_ref, nv_ref = refs[npart:]
        g = p_refs[0][...]
        for p in p_refs[1:]:
            g = g + p[...]
        d, nm, nv = _adamw_math(w_ref[...], g, m_ref[...], v_ref[...])
        g_ref[...] = g
        d_ref[...] = d
        nm_ref[...] = nm
        nv_ref[...] = nv

    spec = pl.BlockSpec((None, tr, cc), lambda l, r: (l, r, 0))
    out = _sds((ll, rr, cc), F32)
    return pl.pallas_call(
        body, name=name, grid=(ll, rr // tr), in_specs=[spec] * (npart + 3), out_specs=[spec] * 4, out_shape=[out] * 4,
        compiler_params=_params(("parallel", "parallel"), (npart + 7) * tr * cc * 4, 0, 4 * tr * cc * 4),
    )(*parts, w, m, v)


def _sum_devices(allparts):
    _, rr, cc = allparts.shape

    def body(p_ref, o_ref):
        s = p_ref[0]
        for k in range(1, N_DEV):
            s = s + p_ref[k]
        o_ref[...] = s

    return pl.pallas_call(
        body, name="sum_small_grads", grid=(1,), in_specs=[pl.BlockSpec((N_DEV, rr, cc), lambda i: (0, 0, 0))],
        out_specs=pl.BlockSpec((rr, cc), lambda i: (0, 0)), out_shape=_sds((rr, cc), F32),
        compiler_params=_params(("arbitrary",), 9 * rr * cc * 4),
    )(allparts)


def _pack(arrs):
    flat = jnp.concatenate([a.reshape(-1).astype(F32) for a in arrs])
    n = flat.shape[0]
    total = -(-n // 1024) * 1024
    return jnp.pad(flat, (0, total - n)).reshape(total // 128, 128)


def _unpack(block, shapes):
    flat = block.reshape(-1)
    out, off = [], 0
    for sh in shapes:
        n = math.prod(sh)
        out.append(flat[off:off + n].reshape(sh))
        off += n
    return out


_BIG = ["w_in_ab", "w_out_ab", "w_qkv_c", "w_out_c", "mem_wq", "mem_wk", "mem_wv", "mem_wo", "ffn_w1", "ffn_w3", "ffn_w2"]
_BIG_AXIS = {"w_in_ab": 0, "w_out_ab": 1, "w_qkv_c": 0, "w_out_c": 1, "mem_wq": 1, "mem_wk": 1, "mem_wv": 1, "mem_wo": 1,
             "ffn_w1": 0, "ffn_w3": 0, "ffn_w2": 1}
_SMALL_REPL = ["gmlp_ln_g", "gmlp_ln_b", "gmlp_w_s", "gmlp_b_s", "conv_b", "conv_gn_g", "conv_gn_b"]
_SMALL_SHARD = ["conv_w", "ln_g", "ln_b"]
_NAMES = ["w_in_ab", "gmlp_ln_g", "gmlp_ln_b", "gmlp_w_s", "gmlp_b_s", "conv_w", "conv_b", "conv_gn_g", "conv_gn_b",
          "w_out_ab", "w_qkv_c", "w_out_c", "mem_wq", "mem_wk", "mem_wv", "mem_wo", "ffn_w1", "ffn_w3", "ffn_w2",
          "ln_g", "ln_b"]


def _natural(stacked):
    ll, k, rr, cc = stacked.shape
    return stacked.reshape(ll, k * rr, cc)


def _local_step(x, mem, target, wts, small):
    n_ex, s, d = x.shape
    t = n_ex * s
    x2 = x.reshape(t, d)
    mem_a = mem.reshape(-1, d).astype(ACT)
    tgt = target.reshape(t, d)
    one = jnp.ones((1, d), F32)
    zero = jnp.zeros((1, d), F32)
    ln_g, ln_b = small["ln_g"], small["ln_b"]

    def vec(a):
        return a.reshape(1, -1)

    saved = []
    xh, gp, bp = x2, one, zero
    y_act = x2.astype(ACT)
    for l in range(DEPTH):
        sv = {"y0": y_act}
        if l % 2 == 0:
            e = l // 2
            h = _proj_cols(f"in_ab_{l}", y_act, wts["w_in_ab"], e, ACT)
            gl = (vec(small["gmlp_ln_g"][e]), vec(small["gmlp_ln_b"][e]), small["gmlp_w_s"][e],
                  small["gmlp_b_s"][e].reshape(4, CHUNK, 1))
            cl = (small["conv_w"][e], vec(small["conv_b"][e]), vec(small["conv_gn_g"][e]), vec(small["conv_gn_b"][e]))
            ya = _gmlp_fwd(f"gmlp_fwd_{l}", h, *gl)
            yb = _conv_fwd(f"conv_fwd_{l}", h, n_ex, *cl)
            yab = jnp.concatenate([ya, yb], axis=1)
            mix = _dense(f"out_ab_{l}", yab, wts["w_out_ab"][e], F32)
            sv.update(h=h, yab=yab, gl=gl, cl=cl)
        else:
            o = l // 2
            qkv = _proj_cols(f"qkv_{l}", y_act, wts["w_qkv_c"], o, ACT)
            att, ltot = _sb_fwd(f"sb_fwd_{l}", qkv, n_ex)
            att_a = att.astype(ACT)
            mix = _dense(f"out_c_{l}", att_a, wts["w_out_c"][o], F32)
            sv.update(qkv=qkv, att=att_a, ltot=ltot)
        g1, b1 = vec(ln_g[l, 0]), vec(ln_b[l, 0])
        xh1, y1, rstd1 = _ln_fwd(f"ln1_fwd_{l}", xh, gp, bp, mix, g1, b1)
        q = _dense(f"mem_q_{l}", y1, wts["mem_wq"][l], ACT)
        kk = _dense(f"mem_k_{l}", mem_a, wts["mem_wk"][l], ACT)
        vv = _dense(f"mem_v_{l}", mem_a, wts["mem_wv"][l], ACT)
        oc = _xattn_fwd(f"xattn_fwd_{l}", q, kk, vv, n_ex)
        cross = _dense(f"mem_o_{l}", oc, wts["mem_wo"][l], F32)
        g2, b2 = vec(ln_g[l, 1]), vec(ln_b[l, 1])
        xh2, y2, rstd2 = _ln_fwd(f"ln2_fwd_{l}", xh1, g1, b1, cross, g2, b2)
        h1, h3, gact = _ffn_up(f"ffn_up_{l}", y2, wts["ffn_w1"], wts["ffn_w3"], l)
        ffo = _ffn_down(f"ffn_down_{l}", gact, wts["ffn_w2"], l)
        g3, b3 = vec(ln_g[l, 2]), vec(ln_b[l, 2])
        xh3, y3, rstd3 = _ln_fwd(f"ln3_fwd_{l}", xh2, g2, b2, ffo, g3, b3)
        sv.update(xh1=xh1, y1=y1, rstd1=rstd1, g1=g1, q=q, kk=kk, vv=vv, oc=oc, xh2=xh2, y2=y2, rstd2=rstd2, g2=g2,
                  h1=h1, h3=h3, gact=gact, xh3=xh3, rstd3=rstd3, g3=g3)
        saved.append(sv)
        xh, gp, bp, y_act = xh3, g3, b3, y3

    dy, loss = _loss_head(xh, gp, bp, tgt)

    big = {n: [None] * (DEPTH if n.startswith(("mem_", "ffn_")) else DEPTH // 2) for n in _BIG}
    sm = {n: [None] * (DEPTH // 2) for n in _SMALL_REPL + ["conv_w"]}
    d_ln_g = [[None] * 3 for _ in range(DEPTH)]
    d_ln_b = [[None] * 3 for _ in range(DEPTH)]

    def add_res(vals, ex):
        return [vals[0] + ALPHA * ex[0]]

    def add_res2(vals, ex):
        return [vals[0] + ex[0] + ALPHA * ex[1]]

    for l in reversed(range(DEPTH)):
        sv = saved[l]
        dr3, dr3a, d_ln_g[l][2], d_ln_b[l][2] = _ln_bwd(f"ln3_bwd_{l}", dy, sv["xh3"], sv["rstd3"], sv["g3"])
        big["ffn_w2"][l] = _ffn_wgrad_down(f"ffn_w2_grad_{l}", sv["gact"], dr3a)
        dh1, dh3 = _ffn_down_bwd(f"ffn_down_bwd_{l}", dr3a, wts["ffn_w2"], sv["h1"], sv["h3"], l)
        big["ffn_w1"][l], big["ffn_w3"][l] = _ffn_wgrad_up(f"ffn_w13_grad_{l}", sv["y2"], dh1, dh3)
        part = _ffn_up_bwd(f"ffn_up_bwd1_{l}", dh1, wts["ffn_w1"], l, [], None)
        dy = _ffn_up_bwd(f"ffn_up_bwd3_{l}", dh3, wts["ffn_w3"], l, [part, dr3], add_res2)
        dr2, dr2a, d_ln_g[l][1], d_ln_b[l][1] = _ln_bwd(f"ln2_bwd_{l}", dy, sv["xh2"], sv["rstd2"], sv["g2"])
        big["mem_wo"][l] = _dense_tn(f"mem_wo_grad_{l}", sv["oc"], dr2a)
        doc = _dense_nt(f"mem_o_bwd_{l}", dr2a, wts["mem_wo"][l], ACT)
        dq, dkk, dvv = _xattn_bwd(f"xattn_bwd_{l}", sv["q"], sv["kk"], sv["vv"], doc, n_ex)
        big["mem_wq"][l] = _dense_tn(f"mem_wq_grad_{l}", sv["y1"], dq)
        big["mem_wk"][l] = _dense_tn(f"mem_wk_grad_{l}", mem_a, dkk)
        big["mem_wv"][l] = _dense_tn(f"mem_wv_grad_{l}", mem_a, dvv)
        dy = _dense_nt(f"mem_q_bwd_{l}", dq, wts["mem_wq"][l], F32, extras=[dr2], epilogue=add_res)
        dr1, dr1a, d_ln_g[l][0], d_ln_b[l][0] = _ln_bwd(f"ln1_bwd_{l}", dy, sv["xh1"], sv["rstd1"], sv["g1"])
        if l % 2 == 0:
            e = l // 2
            big["w_out_ab"][e] = _dense_tn(f"out_ab_grad_{l}", sv["yab"], dr1a)
            dyab = _dense_nt(f"out_ab_bwd_{l}", dr1a, wts["w_out_ab"][e], ACT)
            duv, dgg, dgb, dws, dbs = _gmlp_bwd(f"gmlp_bwd_{l}", sv["h"], dyab, *sv["gl"])
            da, dgt, dcw, dcb, dng, dnb = _conv_bwd(f"conv_bwd_{l}", sv["h"], dyab, n_ex, *sv["cl"])
            sm["gmlp_ln_g"][e], sm["gmlp_ln_b"][e] = dgg.reshape(-1), dgb.reshape(-1)
            sm["gmlp_w_s"][e], sm["gmlp_b_s"][e] = dws, dbs.reshape(4, CHUNK)
            sm["conv_w"][e], sm["conv_b"][e] = dcw, dcb.reshape(-1)
            sm["conv_gn_g"][e], sm["conv_gn_b"][e] = dng.reshape(-1), dnb.reshape(-1)
            dh = jnp.concatenate([duv, da, dgt], axis=1)
            big["w_in_ab"][e] = _proj_cols_wgrad(f"in_ab_grad_{l}", sv["y0"], dh)
            dy = _proj_cols_bwd(f"in_ab_bwd_{l}", dh, wts["w_in_ab"], e, [dr1], add_res)
        else:
            o = l // 2
            big["w_out_c"][o] = _dense_tn(f"out_c_grad_{l}", sv["att"], dr1a)
            datt = _dense_nt(f"out_c_bwd_{l}", dr1a, wts["w_out_c"][o], ACT)
            dq_, dk_, dv_ = _sb_bwd(f"sb_bwd_{l}", sv["qkv"], datt, sv["ltot"], n_ex)
            dqkv = jnp.concatenate([dq_, dk_.astype(ACT), dv_.astype(ACT)], axis=1)
            big["w_qkv_c"][o] = _proj_cols_wgrad(f"qkv_grad_{l}", sv["y0"], dqkv)
            dy = _proj_cols_bwd(f"qkv_bwd_{l}", dqkv, wts["w_qkv_c"], o, [dr1], add_res)

    grad_x = dy.reshape(n_ex, s, d)
    small_g = {n: jnp.stack(sm[n]) for n in sm}
    small_g["ln_g"] = jnp.stack([jnp.concatenate(r, axis=0) for r in d_ln_g])
    small_g["ln_b"] = jnp.stack([jnp.concatenate(r, axis=0) for r in d_ln_b])
    return loss, grad_x, big, small_g


def _stack_big(name, per_layer):
    if _BIG_AXIS[name] == 0:
        return jnp.stack(per_layer, axis=1)
    st = jnp.stack(per_layer, axis=0)
    if st.ndim == 4:
        return st
    ll, rr, cc = st.shape
    return st.reshape(ll, N_CHIPS, rr // N_CHIPS, cc)


def kernel(x, mem, w_in_ab, gmlp_ln_g, gmlp_ln_b, gmlp_w_s, gmlp_b_s, conv_w, conv_b, conv_gn_g, conv_gn_b, w_out_ab, w_qkv_c, w_out_c, mem_wq, mem_wk, mem_wv, mem_wo, ffn_w1, ffn_w3, ffn_w2, ln_g, ln_b, loss_target, m_w_in_ab, m_gmlp_ln_g, m_gmlp_ln_b, m_gmlp_w_s, m_gmlp_b_s, m_conv_w, m_conv_b, m_conv_gn_g, m_conv_gn_b, m_w_out_ab, m_w_qkv_c, m_w_out_c, m_mem_wq, m_mem_wk, m_mem_wv, m_mem_wo, m_ffn_w1, m_ffn_w3, m_ffn_w2, m_ln_g, m_ln_b, v_w_in_ab, v_gmlp_ln_g, v_gmlp_ln_b, v_gmlp_w_s, v_gmlp_b_s, v_conv_w, v_conv_b, v_conv_gn_g, v_conv_gn_b, v_w_out_ab, v_w_qkv_c, v_w_out_c, v_mem_wq, v_mem_wk, v_mem_wv, v_mem_wo, v_ffn_w1, v_ffn_w3, v_ffn_w2, v_ln_g, v_ln_b):
    w = dict(w_in_ab=w_in_ab, gmlp_ln_g=gmlp_ln_g, gmlp_ln_b=gmlp_ln_b, gmlp_w_s=gmlp_w_s, gmlp_b_s=gmlp_b_s, conv_w=conv_w,
             conv_b=conv_b, conv_gn_g=conv_gn_g, conv_gn_b=conv_gn_b, w_out_ab=w_out_ab, w_qkv_c=w_qkv_c, w_out_c=w_out_c,
             mem_wq=mem_wq, mem_wk=mem_wk, mem_wv=mem_wv, mem_wo=mem_wo, ffn_w1=ffn_w1, ffn_w3=ffn_w3, ffn_w2=ffn_w2,
             ln_g=ln_g, ln_b=ln_b)
    mo = dict(w_in_ab=m_w_in_ab, gmlp_ln_g=m_gmlp_ln_g, gmlp_ln_b=m_gmlp_ln_b, gmlp_w_s=m_gmlp_w_s, gmlp_b_s=m_gmlp_b_s,
              conv_w=m_conv_w, conv_b=m_conv_b, conv_gn_g=m_conv_gn_g, conv_gn_b=m_conv_gn_b, w_out_ab=m_w_out_ab,
              w_qkv_c=m_w_qkv_c, w_out_c=m_w_out_c, mem_wq=m_mem_wq, mem_wk=m_mem_wk, mem_wv=m_mem_wv, mem_wo=m_mem_wo,
              ffn_w1=m_ffn_w1, ffn_w3=m_ffn_w3, ffn_w2=m_ffn_w2, ln_g=m_ln_g, ln_b=m_ln_b)
    vo = dict(w_in_ab=v_w_in_ab, gmlp_ln_g=v_gmlp_ln_g, gmlp_ln_b=v_gmlp_ln_b, gmlp_w_s=v_gmlp_w_s, gmlp_b_s=v_gmlp_b_s,
              conv_w=v_conv_w, conv_b=v_conv_b, conv_gn_g=v_conv_gn_g, conv_gn_b=v_conv_gn_b, w_out_ab=v_w_out_ab,
              w_qkv_c=v_w_qkv_c, w_out_c=v_w_out_c, mem_wq=v_mem_wq, mem_wk=v_mem_wk, mem_wv=v_mem_wv, mem_wo=v_mem_wo,
              ffn_w1=v_ffn_w1, ffn_w3=v_ffn_w3, ffn_w2=v_ffn_w2, ln_g=v_ln_g, ln_b=v_ln_b)
    me = (2 * lax.axis_index("x") + lax.axis_index("y")).astype(jnp.int32).reshape(1)

    shards = [w[n].astype(MM) for n in _BIG] + [w[n] for n in _SMALL_SHARD]
    axes = [_BIG_AXIS[n] for n in _BIG] + [0, 0, 0]
    gathered = _gather_chips(shards, axes)
    wts = {}
    for n, g in zip(_BIG, gathered[:len(_BIG)]):
        wts[n] = g if _BIG_AXIS[n] == 0 else _natural(g)
    wts["ffn_w2"] = gathered[_BIG.index("ffn_w2")]
    small = {n: w[n] for n in _SMALL_REPL}
    cw_g, lg_g, lb_g = gathered[len(_BIG):]
    small["conv_w"] = jnp.moveaxis(cw_g, 0, 2).reshape(cw_g.shape[1], CONV_WIDTH, -1)
    small["ln_g"] = jnp.moveaxis(lg_g, 0, 2).reshape(DEPTH, 3, -1)
    small["ln_b"] = jnp.moveaxis(lb_g, 0, 2).reshape(DEPTH, 3, -1)

    loss, grad_x, big, small_g = _local_step(x, mem, loss_target, wts, small)
    loss = lax.psum(loss, ("x", "y", "c"))

    stacked = [_stack_big(n, big[n]) for n in _BIG]
    baxes = [_BIG_AXIS[n] for n in _BIG]
    recv = _scatter_chips(stacked, baxes)
    sums = []
    for n, st, rc in zip(_BIG, stacked, recv):
        sums.append(_sum_chips(st, rc, _BIG_AXIS[n], me))
    sib = _swap_sibling(sums)

    out = {}
    for n, s_own, s_sib in zip(_BIG, sums, sib):
        out[n] = _adamw(f"adamw_{n}", [s_own, s_sib], w[n], mo[n], vo[n])

    order = _SMALL_REPL + _SMALL_SHARD
    part = _pack([small_g[n] for n in order])
    total = _sum_devices(_gather_all(part))
    full = dict(zip(order, _unpack(total, [small_g[n].shape for n in order])))
    x_i, y_i = lax.axis_index("x"), lax.axis_index("y")
    chip = 2 * x_i + y_i
    loc = {n: full[n] for n in _SMALL_REPL}
    for n in _SMALL_SHARD:
        wd = w[n].shape[-1]
        loc[n] = lax.dynamic_slice_in_dim(full[n], chip * wd, wd, axis=full[n].ndim - 1)
    gp, wp, mp, vp = (_pack([src[n] for n in order]) for src in (loc, w, mo, vo))
    r128 = gp.shape[0]
    res = _adamw("adamw_small", [gp.reshape(1, r128, 128)], wp.reshape(1, r128, 128), mp.reshape(1, r128, 128),
                 vp.reshape(1, r128, 128))
    shapes = [w[n].shape for n in order]
    unp = [_unpack(r.reshape(r128, 128), shapes) for r in res]
    for i, n in enumerate(order):
        out[n] = tuple(u[i] for u in unp)

    grads = [out[n][0] for n in _NAMES]
    deltas = [out[n][1] for n in _NAMES]
    new_m = [out[n][2] for n in _NAMES]
    new_v = [out[n][3] for n in _NAMES]
    return (loss, grad_x, *grads, *deltas, *new_m, *new_v)
```

```python
import functools
import math

import jax
import jax.numpy as jnp
from jax import lax
from jax.experimental import pallas as pl
from jax.experimental.pallas import tpu as pltpu

F32 = jnp.float32
MM = jnp.bfloat16
ACT = jnp.bfloat16
MESH = pl.DeviceIdType.MESH

DEPTH = 4
CHUNK = 128
CONV_WIDTH = 31
HALO = 32
MEM_HEADS = 4
C_HEAD_DIM = 64
ALPHA = (2.0 * DEPTH) ** 0.25
LN_EPS = 1e-5
ADAM_LR, ADAM_B1, ADAM_B2, ADAM_EPS, ADAM_WD, ADAM_STEP = 0.001, 0.9, 0.999, 1e-08, 0.01, 10

VMEM_CAP_V7X = 64 * 1024 * 1024
VMEM_MAX_REQUEST = 56 * 1024 * 1024
N_CHIPS = 4
N_DEV = 8


def _tile(n, pref):
    if n <= pref:
        return n
    for t in range(pref - pref % 8, 7, -8):
        if n % t == 0:
            return t
    return n


def _nbytes(shape, dtype):
    return math.prod(1 if s is None else s for s in shape) * jnp.dtype(dtype).itemsize


def _vmem_limit(block_bytes, scratch_bytes=0, temp_bytes=0):
    est = 2 * block_bytes + scratch_bytes + temp_bytes
    return int(min(VMEM_MAX_REQUEST, max(16 * 1024 * 1024, est * 5 // 4)))


def _params(sem, block_bytes, scratch_bytes=0, temp_bytes=0):
    return pltpu.CompilerParams(dimension_semantics=sem,
                                vmem_limit_bytes=_vmem_limit(block_bytes, scratch_bytes, temp_bytes))


_DIMS = {"nn": (((1,), (0,)), ((), ())), "nt": (((1,), (1,)), ((), ())), "tn": (((0,), (0,)), ((), ()))}


def _mm(name, mode, grid, a, a_spec, bs, b_specs, outs, out_specs, acc_shape,
        extras=(), extra_specs=(), epilogue=None):
    nb, ne, no = len(bs), len(extras), len(outs)
    nk = grid[2]

    def body(*refs):
        a_ref = refs[0]
        b_refs = refs[1:1 + nb]
        e_refs = refs[1 + nb:1 + nb + ne]
        o_refs = refs[1 + nb + ne:1 + nb + ne + no]
        accs = refs[1 + nb + ne + no:]
        k = pl.program_id(2)

        @pl.when(k == 0)
        def _():
            for acc in accs:
                acc[...] = jnp.zeros_like(acc)

        av = a_ref[...].astype(MM)
        for b_ref, acc in zip(b_refs, accs):
            acc[...] += lax.dot_general(av, b_ref[...].astype(MM), _DIMS[mode], preferred_element_type=F32)

        @pl.when(k == nk - 1)
        def _():
            vals = [acc[...] for acc in accs]
            if epilogue is not None:
                vals = epilogue(vals, [e[...].astype(F32) for e in e_refs])
            for o, v in zip(o_refs, vals):
                o[...] = v.astype(o.dtype)

    blocks = (_nbytes(a_spec.block_shape, a.dtype)
              + sum(_nbytes(s.block_shape, b.dtype) for s, b in zip(b_specs, bs))
              + sum(_nbytes(s.block_shape, e.dtype) for s, e in zip(extra_specs, extras))
              + sum(_nbytes(s.block_shape, o.dtype) for s, o in zip(out_specs, outs)))
    acc_bytes = nb * _nbytes(acc_shape, F32)
    res = pl.pallas_call(
        body, name=name, grid=grid,
        in_specs=[a_spec, *b_specs, *extra_specs], out_specs=list(out_specs), out_shape=list(outs),
        scratch_shapes=[pltpu.VMEM(acc_shape, F32)] * nb,
        compiler_params=_params(("parallel", "parallel", "arbitrary"), blocks, acc_bytes, 4 * acc_bytes),
    )(a, *bs, *extras)
    return res


def _sds(shape, dtype):
    return jax.ShapeDtypeStruct(shape, dtype)


def _dense(name, a, w, out_dtype, extras=(), epilogue=None, n_out=None):
    t, kdim = a.shape
    n = w.shape[1]
    tm, tn, tk = _tile(t, 1024), _tile(n, 512), _tile(kdim, 1024)
    grid = (t // tm, n // tn, kdim // tk)
    return _mm(name, "nn", grid, a, pl.BlockSpec((tm, tk), lambda i, j, k: (i, k)),
               [w], [pl.BlockSpec((tk, tn), lambda i, j, k: (k, j))],
               [_sds((t, n), out_dtype)], [pl.BlockSpec((tm, tn), lambda i, j, k: (i, j))], (tm, tn),
               extras=extras, extra_specs=[pl.BlockSpec((tm, tn), lambda i, j, k: (i, j))] * len(extras),
               epilogue=epilogue)[0]


def _dense_nt(name, a, w, out_dtype, extras=(), epilogue=None):
    t, n = a.shape
    kout = w.shape[0]
    tm, tn, tk = _tile(t, 1024), _tile(kout, 512), _tile(n, 1024)
    grid = (t // tm, kout // tn, n // tk)
    return _mm(name, "nt", grid, a, pl.BlockSpec((tm, tk), lambda i, j, k: (i, k)),
               [w], [pl.BlockSpec((tn, tk), lambda i, j, k: (j, k))],
               [_sds((t, kout), out_dtype)], [pl.BlockSpec((tm, tn), lambda i, j, k: (i, j))], (tm, tn),
               extras=extras, extra_specs=[pl.BlockSpec((tm, tn), lambda i, j, k: (i, j))] * len(extras),
               epilogue=epilogue)[0]


def _dense_tn(name, a, b, out_dtype=MM):
    t, m = a.shape
    n = b.shape[1]
    tm, tn, tk = _tile(m, 512), _tile(n, 512), _tile(t, 1024)
    grid = (m // tm, n // tn, t // tk)
    return _mm(name, "tn", grid, a, pl.BlockSpec((tk, tm), lambda i, j, k: (k, i)),
               [b], [pl.BlockSpec((tk, tn), lambda i, j, k: (k, j))],
               [_sds((m, n), out_dtype)], [pl.BlockSpec((tm, tn), lambda i, j, k: (i, j))], (tm, tn))[0]


_INV_SQRT2 = 0.7071067811865476
_INV_SQRT_2PI = 0.3989422804014327


def _gelu(x):
    return 0.5 * x * (1.0 + lax.erf(x * _INV_SQRT2))


def _gelu_grad(x):
    return 0.5 * (1.0 + lax.erf(x * _INV_SQRT2)) + x * jnp.exp(-0.5 * x * x) * _INV_SQRT_2PI


def _sigmoid(x):
    return 1.0 / (1.0 + jnp.exp(-x))


def _norm_stats(x):
    mu = jnp.mean(x, axis=-1, keepdims=True)
    xc = x - mu
    var = jnp.mean(xc * xc, axis=-1, keepdims=True)
    rstd = lax.rsqrt(var + LN_EPS)
    return xc * rstd, rstd


def _norm_bwd(dy_g, xh, rstd):
    m1 = jnp.mean(dy_g, axis=-1, keepdims=True)
    m2 = jnp.mean(dy_g * xh, axis=-1, keepdims=True)
    return rstd * (dy_g - m1 - xh * m2)


def _rows8(x):
    r, c = x.shape
    return jnp.sum(x.reshape(r // 8, 8, c), axis=0)


def _ln_fwd(name, xh_prev, g_prev, b_prev, f, g, b):
    t, d = f.shape
    tm = _tile(t, 512)

    def body(xp_ref, gp_ref, bp_ref, f_ref, g_ref, b_ref, xh_ref, y_ref, rstd_ref):
        r = ALPHA * (xp_ref[...] * gp_ref[...] + bp_ref[...]) + f_ref[...]
        xh, rstd = _norm_stats(r)
        xh_ref[...] = xh
        y_ref[...] = (xh * g_ref[...] + b_ref[...]).astype(y_ref.dtype)
        rstd_ref[...] = rstd

    row = pl.BlockSpec((tm, d), lambda i: (i, 0))
    vec = pl.BlockSpec((1, d), lambda i: (0, 0))
    return pl.pallas_call(
        body, name=name, grid=(t // tm,),
        in_specs=[row, vec, vec, row, vec, vec],
        out_specs=[row, row, pl.BlockSpec((tm, 1), lambda i: (i, 0))],
        out_shape=[_sds((t, d), F32), _sds((t, d), ACT), _sds((t, 1), F32)],
        compiler_params=_params(("parallel",), 4 * tm * d * 4, 0, 4 * tm * d * 4),
    )(xh_prev, g_prev, b_prev, f, g, b)


def _ln_bwd(name, dy, xh, rstd, g):
    t, d = dy.shape
    tm = _tile(t, 512)
    n = t // tm

    def body(dy_ref, xh_ref, rstd_ref, g_ref, dr_ref, dra_ref, dg_ref, db_ref, dg_acc, db_acc):
        i = pl.program_id(0)

        @pl.when(i == 0)
        def _():
            dg_acc[...] = jnp.zeros_like(dg_acc)
            db_acc[...] = jnp.zeros_like(db_acc)

        dyv = dy_ref[...]
        xhv = xh_ref[...]
        dr = _norm_bwd(dyv * g_ref[...], xhv, rstd_ref[...])
        dr_ref[...] = dr
        dra_ref[...] = dr.astype(dra_ref.dtype)
        dg_acc[...] += _rows8(dyv * xhv)
        db_acc[...] += _rows8(dyv)

        @pl.when(i == n - 1)
        def _():
            dg_ref[...] = jnp.sum(dg_acc[...], axis=0, keepdims=True)
            db_ref[...] = jnp.sum(db_acc[...], axis=0, keepdims=True)

    row = pl.BlockSpec((tm, d), lambda i: (i, 0))
    vec = pl.BlockSpec((1, d), lambda i: (0, 0))
    return pl.pallas_call(
        body, name=name, grid=(n,),
        in_specs=[row, row, pl.BlockSpec((tm, 1), lambda i: (i, 0)), vec],
        out_specs=[row, row, vec, vec],
        out_shape=[_sds((t, d), F32), _sds((t, d), ACT), _sds((1, d), F32), _sds((1, d), F32)],
        scratch_shapes=[pltpu.VMEM((8, d), F32), pltpu.VMEM((8, d), F32)],
        compiler_params=_params(("arbitrary",), 4 * tm * d * 4, 0, 4 * tm * d * 4),
    )(dy, xh, rstd, g)


def _loss_head(xh, g, b, target):
    t, d = xh.shape
    tm = _tile(t, 512)
    n = t // tm

    def body(xh_ref, g_ref, b_ref, tg_ref, dy_ref, loss_ref, acc):
        i = pl.program_id(0)

        @pl.when(i == 0)
        def _():
            acc[...] = jnp.zeros_like(acc)

        err = xh_ref[...] * g_ref[...] + b_ref[...] - tg_ref[...]
        dy_ref[...] = err * (1.0 / d)
        acc[...] += _rows8(err * err)

        @pl.when(i == n - 1)
        def _():
            s = jnp.sum(jnp.sum(acc[...], axis=0, keepdims=True), axis=1, keepdims=True)
            loss_ref[...] = jnp.broadcast_to(s * (0.5 / d), loss_ref.shape)

    row = pl.BlockSpec((tm, d), lambda i: (i, 0))
    vec = pl.BlockSpec((1, d), lambda i: (0, 0))
    dy, loss = pl.pallas_call(
        body, name="loss_head", grid=(n,),
        in_specs=[row, vec, vec, row],
        out_specs=[row, pl.BlockSpec((8, 128), lambda i: (0, 0))],
        out_shape=[_sds((t, d), F32), _sds((8, 128), F32)],
        scratch_shapes=[pltpu.VMEM((8, d), F32)],
        compiler_params=_params(("arbitrary",), 3 * tm * d * 4, 0, 2 * tm * d * 4),
    )(xh, g, b, target)
    return dy, loss[0, 0]


def _causal_w(w):
    r = lax.broadcasted_iota(jnp.int32, w.shape, 0)
    c = lax.broadcasted_iota(jnp.int32, w.shape, 1)
    return jnp.where(r >= c, w, 0.0)


def _gmlp_fwd(name, h, ln_g, ln_b, w_s, b_s_col):
    t = h.shape[0]
    tt = _tile(t, 256)
    wd = 4 * CHUNK

    def body(u_ref, v_ref, g_ref, b_ref, w_ref, bs_ref, ya_ref):
        for gi in range(4):
            ln = slice(gi * CHUNK, (gi + 1) * CHUNK)
            u = _gelu(u_ref[:, ln].astype(F32))
            v = _gelu(v_ref[:, ln].astype(F32))
            xh, _ = _norm_stats(v)
            vg = (xh * g_ref[:, ln] + b_ref[:, ln]).astype(MM)
            w = _causal_w(w_ref[gi]).astype(MM)
            for c in range(tt // CHUNK):
                rs = slice(c * CHUNK, (c + 1) * CHUNK)
                mixed = jnp.dot(w, vg[rs], preferred_element_type=F32) + bs_ref[gi]
                ya_ref[rs, ln] = (u[rs] * mixed).astype(ya_ref.dtype)

    vec = pl.BlockSpec((1, wd), lambda i: (0, 0))
    return pl.pallas_call(
        body, name=name, grid=(t // tt,),
        in_specs=[pl.BlockSpec((tt, wd), lambda i: (i, 0)), pl.BlockSpec((tt, wd), lambda i: (i, 1)), vec, vec,
                  pl.BlockSpec((4, CHUNK, CHUNK), lambda i: (0, 0, 0)), pl.BlockSpec((4, CHUNK, 1), lambda i: (0, 0, 0))],
        out_specs=pl.BlockSpec((tt, wd), lambda i: (i, 0)),
        out_shape=_sds((t, wd), ACT),
        compiler_params=_params(("parallel",), 3 * tt * wd * 4, 0, 8 * tt * CHUNK * 4),
    )(h, h, ln_g, ln_b, w_s, b_s_col)


def _gmlp_bwd(name, h, dyab, ln_g, ln_b, w_s, b_s_col):
    t = h.shape[0]
    tt = _tile(t, 256)
    n = t // tt
    wd = 4 * CHUNK

    def body(u_ref, v_ref, dy_ref, g_ref, b_ref, w_ref, bs_ref, duv_ref, dg_ref, db_ref, dw_ref, dbs_ref,
             dg_acc, db_acc):
        i = pl.program_id(0)

        @pl.when(i == 0)
        def _():
            dg_acc[...] = jnp.zeros_like(dg_acc)
            db_acc[...] = jnp.zeros_like(db_acc)
            dw_ref[...] = jnp.zeros_like(dw_ref)
            dbs_ref[...] = jnp.zeros_like(dbs_ref)

        for gi in range(4):
            ln = slice(gi * CHUNK, (gi + 1) * CHUNK)
            upre = u_ref[:, ln].astype(F32)
            vpre = v_ref[:, ln].astype(F32)
            u = _gelu(upre)
            v = _gelu(vpre)
            xh, rstd = _norm_stats(v)
            gv = g_ref[:, ln]
            vg = (xh * gv + b_ref[:, ln]).astype(MM)
            w = _causal_w(w_ref[gi]).astype(MM)
            dya = dy_ref[:, ln].astype(F32)
            dmixed = dya * u
            dmm = dmixed.astype(MM)
            dvg_parts, mixed_parts = [], []
            dw = jnp.zeros((CHUNK, CHUNK), F32)
            dbs = jnp.zeros((CHUNK, 1), F32)
            for c in range(tt // CHUNK):
                rs = slice(c * CHUNK, (c + 1) * CHUNK)
                mixed_parts.append(jnp.dot(w, vg[rs], preferred_element_type=F32) + bs_ref[gi])
                dw = dw + lax.dot_general(dmm[rs], vg[rs], _DIMS["nt"], preferred_element_type=F32)
                dbs = dbs + jnp.sum(dmixed[rs], axis=1, keepdims=True)
                dvg_parts.append(lax.dot_general(w, dmm[rs], _DIMS["tn"], preferred_element_type=F32))
            mixed = jnp.concatenate(mixed_parts, axis=0)
            dvg = jnp.concatenate(dvg_parts, axis=0)
            dw_ref[gi] += _causal_w(dw)
            dbs_ref[gi] += dbs
            dg_acc[:, ln] += _rows8(dvg * xh)
            db_acc[:, ln] += _rows8(dvg)
            dv = _norm_bwd(dvg * gv, xh, rstd) * _gelu_grad(vpre)
            du = dya * mixed * _gelu_grad(upre)
            duv_ref[:, ln] = du.astype(duv_ref.dtype)
            duv_ref[:, wd + gi * CHUNK: wd + (gi + 1) * CHUNK] = dv.astype(duv_ref.dtype)

        @pl.when(i == n - 1)
        def _():
            dg_ref[...] = jnp.sum(dg_acc[...], axis=0, keepdims=True)
            db_ref[...] = jnp.sum(db_acc[...], axis=0, keepdims=True)

    vec = pl.BlockSpec((1, wd), lambda i: (0, 0))
    wspec = pl.BlockSpec((4, CHUNK, CHUNK), lambda i: (0, 0, 0))
    bspec = pl.BlockSpec((4, CHUNK, 1), lambda i: (0, 0, 0))
    return pl.pallas_call(
        body, name=name, grid=(n,),
        in_specs=[pl.BlockSpec((tt, wd), lambda i: (i, 0)), pl.BlockSpec((tt, wd), lambda i: (i, 1)),
                  pl.BlockSpec((tt, wd), lambda i: (i, 0)), vec, vec, wspec, bspec],
        out_specs=[pl.BlockSpec((tt, 2 * wd), lambda i: (i, 0)), vec, vec, wspec, bspec],
        out_shape=[_sds((t, 2 * wd), ACT), _sds((1, wd), F32), _sds((1, wd), F32),
                   _sds((4, CHUNK, CHUNK), F32), _sds((4, CHUNK, 1), F32)],
        scratch_shapes=[pltpu.VMEM((8, wd), F32), pltpu.VMEM((8, wd), F32)],
        compiler_params=_params(("arbitrary",), 5 * tt * wd * 4, 0, 16 * tt * CHUNK * 4),
    )(h, h, dyab, ln_g, ln_b, w_s, b_s_col)


_ROWS = 256


def _conv_taps(win, cw, lo):
    acc = jnp.zeros((_ROWS, CHUNK), F32)
    for w in range(CONV_WIDTH):
        s = lo(w)
        acc = acc + cw[w:w + 1, :] * win[s:s + _ROWS, :]
    return acc


def _conv_fwd(name, h, n_ex, cw, cb, gg, gb):
    t = h.shape[0]
    s = t // n_ex
    nt = s // _ROWS

    def body(a_ref, gt_ref, cw_ref, cb_ref, gg_ref, gb_ref, yb_ref, hh):
        hh[0:HALO, :] = jnp.zeros((HALO, CHUNK), F32)
        hh[HALO:HALO + s, :] = a_ref[...].astype(F32) * _sigmoid(gt_ref[...].astype(F32))
        cwv = cw_ref[...]

        def tile(i, carry):
            r0 = pl.multiple_of(i * _ROWS, _ROWS)
            win = hh[pl.ds(r0, _ROWS + HALO), :]
            c = _conv_taps(win, cwv, lambda w: w + HALO - (CONV_WIDTH - 1)) + cb_ref[...]
            xh, _ = _norm_stats(c)
            hg = xh * gg_ref[...] + gb_ref[...]
            yb_ref[pl.ds(r0, _ROWS), :] = (hg * _sigmoid(hg)).astype(yb_ref.dtype)
            return carry

        lax.fori_loop(0, nt, tile, 0)

    vec = pl.BlockSpec((1, CHUNK), lambda g, b: (0, g))
    return pl.pallas_call(
        body, name=name, grid=(4, n_ex),
        in_specs=[pl.BlockSpec((s, CHUNK), lambda g, b: (b, 8 + g)), pl.BlockSpec((s, CHUNK), lambda g, b: (b, 12 + g)),
                  pl.BlockSpec((CONV_WIDTH, CHUNK), lambda g, b: (0, g)), vec, vec, vec],
        out_specs=pl.BlockSpec((s, CHUNK), lambda g, b: (b, g)),
        out_shape=_sds((t, 4 * CHUNK), ACT),
        scratch_shapes=[pltpu.VMEM((s + HALO, CHUNK), F32)],
        compiler_params=_params(("parallel", "parallel"), 3 * s * CHUNK * 4, (s + HALO) * CHUNK * 4, 4 * s * CHUNK * 4),
    )(h, h, cw, cb, gg, gb)


def _conv_bwd(name, h, dyab, n_ex, cw, cb, gg, gb):
    t = h.shape[0]
    s = t // n_ex
    nt = s // _ROWS

    def body(a_ref, gt_ref, dy_ref, cw_ref, cb_ref, gg_ref, gb_ref,
             da_ref, dgt_ref, dcw_ref, dcb_ref, dgg_ref, dgb_ref, hh, dcs, acc):
        b = pl.program_id(1)

        @pl.when(b == 0)
        def _():
            dcw_ref[...] = jnp.zeros_like(dcw_ref)
            dcb_ref[...] = jnp.zeros_like(dcb_ref)
            dgg_ref[...] = jnp.zeros_like(dgg_ref)
            dgb_ref[...] = jnp.zeros_like(dgb_ref)

        hh[0:HALO, :] = jnp.zeros((HALO, CHUNK), F32)
        hh[HALO:HALO + s, :] = a_ref[...].astype(F32) * _sigmoid(gt_ref[...].astype(F32))
        dcs[s:s + HALO, :] = jnp.zeros((HALO, CHUNK), F32)
        acc[...] = jnp.zeros_like(acc)
        cwv = cw_ref[...]
        off = HALO - (CONV_WIDTH - 1)

        def tile1(i, carry):
            r0 = pl.multiple_of(i * _ROWS, _ROWS)
            win = hh[pl.ds(r0, _ROWS + HALO), :]
            c = _conv_taps(win, cwv, lambda w: w + off) + cb_ref[...]
            xh, rstd = _norm_stats(c)
            hg = xh * gg_ref[...] + gb_ref[...]
            sg = _sigmoid(hg)
            dhg = dy_ref[pl.ds(r0, _ROWS), :].astype(F32) * (sg * (1.0 + hg * (1.0 - sg)))
            acc[32:40, :] += _rows8(dhg * xh)
            acc[40:48, :] += _rows8(dhg)
            dc = _norm_bwd(dhg * gg_ref[...], xh, rstd)
            dcs[pl.ds(r0, _ROWS), :] = dc
            acc[48:56, :] += _rows8(dc)
            for w in range(CONV_WIDTH):
                acc[w:w + 1, :] += jnp.sum(dc * win[w + off:w + off + _ROWS, :], axis=0, keepdims=True)
            return carry

        lax.fori_loop(0, nt, tile1, 0)

        def tile2(i, carry):
            r0 = pl.multiple_of(i * _ROWS, _ROWS)
            win = dcs[pl.ds(r0, _ROWS + HALO), :]
            dhh = _conv_taps(win, cwv, lambda w: CONV_WIDTH - 1 - w)
            av = a_ref[pl.ds(r0, _ROWS), :].astype(F32)
            sg = _sigmoid(gt_ref[pl.ds(r0, _ROWS), :].astype(F32))
            da_ref[pl.ds(r0, _ROWS), :] = (dhh * sg).astype(da_ref.dtype)
            dgt_ref[pl.ds(r0, _ROWS), :] = (dhh * av * sg * (1.0 - sg)).astype(dgt_ref.dtype)
            return carry

        lax.fori_loop(0, nt, tile2, 0)
        dcw_ref[...] += acc[0:CONV_WIDTH, :]
        dgg_ref[...] += jnp.sum(acc[32:40, :], axis=0, keepdims=True)
        dgb_ref[...] += jnp.sum(acc[40:48, :], axis=0, keepdims=True)
        dcb_ref[...] += jnp.sum(acc[48:56, :], axis=0, keepdims=True)

    vec = pl.BlockSpec((1, CHUNK), lambda g, b: (0, g))
    tap = pl.BlockSpec((CONV_WIDTH, CHUNK), lambda g, b: (0, g))
    seq = pl.BlockSpec((s, CHUNK), lambda g, b: (b, g))
    return pl.pallas_call(
        body, name=name, grid=(4, n_ex),
        in_specs=[pl.BlockSpec((s, CHUNK), lambda g, b: (b, 8 + g)), pl.BlockSpec((s, CHUNK), lambda g, b: (b, 12 + g)),
                  pl.BlockSpec((s, CHUNK), lambda g, b: (b, 4 + g)), tap, vec, vec, vec],
        out_specs=[seq, seq, tap, vec, vec, vec],
        out_shape=[_sds((t, 4 * CHUNK), ACT), _sds((t, 4 * CHUNK), ACT), _sds((CONV_WIDTH, 4 * CHUNK), F32),
                   _sds((1, 4 * CHUNK), F32), _sds((1, 4 * CHUNK), F32), _sds((1, 4 * CHUNK), F32)],
        scratch_shapes=[pltpu.VMEM((s + HALO, CHUNK), F32), pltpu.VMEM((s + HALO, CHUNK), F32),
                        pltpu.VMEM((56, CHUNK), F32)],
        compiler_params=_params(("parallel", "arbitrary"), 5 * s * CHUNK * 4, 2 * (s + HALO) * CHUNK * 4, 4 * s * CHUNK * 4),
    )(h, h, dyab, cw, cb, gg, gb)


_TQ = 256
_SB_UNROLL = 2


def _tri(kind):
    r = lax.broadcasted_iota(jnp.int32, (_TQ, _TQ), 0)
    c = lax.broadcasted_iota(jnp.int32, (_TQ, _TQ), 1)
    m = {"gt": r > c, "le": r <= c, "lt": r < c}[kind]
    return jnp.where(m, 1.0, 0.0).astype(jnp.bfloat16)


def _split_dot(x, tri2):
    hi = x.astype(jnp.bfloat16)
    lo = (x - hi.astype(F32)).astype(jnp.bfloat16)
    return jnp.dot(jnp.concatenate([hi, lo], axis=1), tri2, preferred_element_type=F32)


def _neg_abs(x):
    bits = lax.bitcast_convert_type(x, jnp.uint32) | jnp.uint32(0x80000000)
    return lax.bitcast_convert_type(bits, F32)


def _log_not_beta(nz):
    return jnp.minimum(nz, 0.0) - jnp.log(1.0 + jnp.exp(_neg_abs(nz)))


def _sb_fwd(name, qkv, n_ex):
    t = qkv.shape[0]
    d = qkv.shape[1] // 3
    npair = d // CHUNK
    s = t // n_ex
    nq = s // _TQ
    neg_a = -(C_HEAD_DIM ** -0.5)

    def body(q_ref, k_ref, v_ref, o_ref, lt_ref, o_acc, c_acc):
        i = pl.program_id(2)
        first = lax.broadcasted_iota(jnp.int32, (_TQ, CHUNK), 1) < C_HEAD_DIM
        q2 = q_ref[...]
        zero = jnp.zeros_like(q2)
        qh = [jnp.where(first, q2, zero), jnp.where(first, zero, q2)]
        tri2 = jnp.concatenate([_tri("gt")] * 2, axis=0)
        o_acc[...] = jnp.zeros_like(o_acc)
        c_acc[...] = jnp.zeros_like(c_acc)

        def tiles(js, mask):
            work = [(a, hd) for a in range(len(js)) for hd in range(2)]
            rows = [pl.ds(pl.multiple_of(j * _TQ, _TQ), _TQ) for j in js]
            kts = [k_ref[r, :] for r in rows]
            vts = [v_ref[r, :] for r in rows]
            nzs = {w: lax.dot_general(qh[w[1]], kts[w[0]], _DIMS["nt"], preferred_element_type=F32) * neg_a for w in work}
            lns = {w: _log_not_beta(nzs[w]) for w in work}
            if mask is not None:
                lns = {w: jnp.where(mask, lns[w], 0.0) for w in work}
            locs = {w: _split_dot(lns[w], tri2) for w in work}
            laters = {}
            for hd in range(2):
                carry = c_acc[hd]
                for a in range(len(js)):
                    laters[a, hd] = carry + locs[a, hd]
                    carry = laters[a, hd][:, 0:1] + lns[a, hd][:, 0:1]
                c_acc[hd] = carry
            atts = {w: jnp.exp(lns[w] - nzs[w] + laters[w]) for w in work}
            if mask is not None:
                atts = {w: jnp.where(mask, atts[w], 0.0) for w in work}
            for hd in range(2):
                acc = o_acc[hd]
                for a in range(len(js)):
                    acc = acc + jnp.dot(atts[a, hd].astype(MM), vts[a], preferred_element_type=F32)
                o_acc[hd] = acc

        tiles([i], lax.broadcasted_iota(jnp.int32, (_TQ, _TQ), 1) < lax.broadcasted_iota(jnp.int32, (_TQ, _TQ), 0))

        rem = i % _SB_UNROLL

        def single(jj, carry):
            tiles([i - 1 - jj], None)
            return carry

        lax.fori_loop(0, rem, single, 0)

        def step(jj, carry):
            j = i - 1 - rem - _SB_UNROLL * jj
            tiles([j - a for a in range(_SB_UNROLL)], None)
            return carry

        lax.fori_loop(0, i // _SB_UNROLL, step, 0)
        o_ref[...] = jnp.where(first, o_acc[0], o_acc[1])
        lt_ref[:, 0:1] = c_acc[0]
        lt_ref[:, 1:2] = c_acc[1]

    return pl.pallas_call(
        body, name=name, grid=(n_ex, npair, nq),
        in_specs=[pl.BlockSpec((_TQ, CHUNK), lambda b, p, i: (b * nq + i, p)),
                  pl.BlockSpec((s, CHUNK), lambda b, p, i: (b, npair + p)),
                  pl.BlockSpec((s, CHUNK), lambda b, p, i: (b, 2 * npair + p))],
        out_specs=[pl.BlockSpec((_TQ, CHUNK), lambda b, p, i: (b * nq + i, p)),
                   pl.BlockSpec((None, _TQ, 2), lambda b, p, i: (p, b * nq + i, 0))],
        out_shape=[_sds((t, d), F32), _sds((npair, t, 2), F32)],
        scratch_shapes=[pltpu.VMEM((2, _TQ, CHUNK), F32), pltpu.VMEM((2, _TQ, 1), F32)],
        compiler_params=_params(("parallel", "parallel", "arbitrary"), 2 * s * CHUNK * 2 + 4 * _TQ * CHUNK * 4,
                                4 * _TQ * CHUNK * 4, 24 * _TQ * _TQ * 4),
    )(qkv, qkv, qkv)


def _sb_bwd(name, qkv, do, ltot, n_ex):
    t = qkv.shape[0]
    d = qkv.shape[1] // 3
    npair = d // CHUNK
    s = t // n_ex
    nq = s // _TQ
    scale = C_HEAD_DIM ** -0.5
    neg_a = -scale

    def body(q_ref, k_ref, v_ref, do_ref, lt_ref, dq_ref, dk_ref, dv_ref, dq_acc, cp_acc, cg_acc):
        i = pl.program_id(2)

        @pl.when(i == 0)
        def _():
            dk_ref[...] = jnp.zeros_like(dk_ref)
            dv_ref[...] = jnp.zeros_like(dv_ref)

        first = lax.broadcasted_iota(jnp.int32, (_TQ, CHUNK), 1) < C_HEAD_DIM
        q2 = q_ref[...]
        do2 = do_ref[...]
        qs = (q2 * scale).astype(q2.dtype)
        zero = jnp.zeros_like(q2)
        qh = [jnp.where(first, q2, zero), jnp.where(first, zero, q2)]
        doh = [jnp.where(first, do2, zero), jnp.where(first, zero, do2)]
        lt = [lt_ref[:, 0:1], lt_ref[:, 1:2]]
        tri2_le = jnp.concatenate([_tri("le")] * 2, axis=0)
        tri_lt = _tri("lt").astype(MM)
        dq_acc[...] = jnp.zeros_like(dq_acc)
        cp_acc[...] = jnp.zeros_like(cp_acc)
        cg_acc[...] = jnp.zeros_like(cg_acc)

        def tiles(js, mask):
            na = len(js)
            work = [(a, hd) for a in range(na) for hd in range(2)]
            last = slice(_TQ - 1, _TQ)
            rows = [pl.ds(pl.multiple_of(j * _TQ, _TQ), _TQ) for j in js]
            kts = [k_ref[r, :] for r in rows]
            vts = [v_ref[r, :] for r in rows]
            ksc = [(kt * scale).astype(kt.dtype) for kt in kts]
            nzs = {w: lax.dot_general(qh[w[1]], kts[w[0]], _DIMS["nt"], preferred_element_type=F32) * neg_a for w in work}
            datts = {w: lax.dot_general(doh[w[1]], vts[w[0]], _DIMS["nt"], preferred_element_type=F32) for w in work}
            lns = {w: _log_not_beta(nzs[w]) for w in work}
            if mask is not None:
                lns = {w: jnp.where(mask, lns[w], 0.0) for w in work}
            pins = {w: _split_dot(lns[w], tri2_le) for w in work}
            lss = {w: lns[w] - nzs[w] for w in work}
            atts = {}
            for hd in range(2):
                cp = cp_acc[hd]
                for a in range(na):
                    atts[a, hd] = jnp.exp(lss[a, hd] + ((lt[hd] - cp) - pins[a, hd]))
                    cp = cp + pins[a, hd][:, last]
                cp_acc[hd] = cp
            if mask is not None:
                atts = {w: jnp.where(mask, atts[w], 0.0) for w in work}
            gs = {w: datts[w] * atts[w] for w in work}
            locg = {w: jnp.dot(gs[w].astype(MM), tri_lt, preferred_element_type=F32) for w in work}
            dzs = {}
            for hd in range(2):
                carry = cg_acc[hd]
                for a in range(na):
                    big = carry + locg[a, hd]
                    dzs[a, hd] = gs[a, hd] - (gs[a, hd] + big) * jnp.exp(lss[a, hd])
                    carry = big[:, last] + gs[a, hd][:, last]
                cg_acc[hd] = carry
            if mask is not None:
                dzs = {w: jnp.where(mask, dzs[w], 0.0) for w in work}
            dzs = {w: dzs[w].astype(MM) for w in work}
            attm = {w: atts[w].astype(MM) for w in work}
            for hd in range(2):
                acc = dq_acc[hd]
                for a in range(na):
                    acc = acc + jnp.dot(dzs[a, hd], ksc[a], preferred_element_type=F32)
                dq_acc[hd] = acc
            for a in range(na):
                dk0, dk1 = [lax.dot_general(dzs[a, hd], qs, _DIMS["tn"], preferred_element_type=F32) for hd in range(2)]
                dv0, dv1 = [lax.dot_general(attm[a, hd], do2, _DIMS["tn"], preferred_element_type=F32) for hd in range(2)]
                dk_ref[rows[a], :] += jnp.where(first, dk0, dk1)
                dv_ref[rows[a], :] += jnp.where(first, dv0, dv1)

        def step(jj, carry):
            tiles([_SB_UNROLL * jj + a for a in range(_SB_UNROLL)], None)
            return carry

        lax.fori_loop(0, i // _SB_UNROLL, step, 0)

        def single(j, carry):
            tiles([j], None)
            return carry

        lax.fori_loop(i - i % _SB_UNROLL, i, single, 0)
        tiles([i], lax.broadcasted_iota(jnp.int32, (_TQ, _TQ), 1) < lax.broadcasted_iota(jnp.int32, (_TQ, _TQ), 0))
        dq_ref[...] = jnp.where(first, dq_acc[0], dq_acc[1]).astype(dq_ref.dtype)

    qspec = pl.BlockSpec((_TQ, CHUNK), lambda b, p, i: (b * nq + i, p))
    kv_out = pl.BlockSpec((s, CHUNK), lambda b, p, i: (b, p))
    return pl.pallas_call(
        body, name=name, grid=(n_ex, npair, nq),
        in_specs=[qspec, pl.BlockSpec((s, CHUNK), lambda b, p, i: (b, npair + p)),
                  pl.BlockSpec((s, CHUNK), lambda b, p, i: (b, 2 * npair + p)), qspec,
                  pl.BlockSpec((None, _TQ, 2), lambda b, p, i: (p, b * nq + i, 0))],
        out_specs=[qspec, kv_out, kv_out],
        out_shape=[_sds((t, d), ACT), _sds((t, d), F32), _sds((t, d), F32)],
        scratch_shapes=[pltpu.VMEM((2, _TQ, CHUNK), F32), pltpu.VMEM((2, _TQ, 1), F32), pltpu.VMEM((2, _TQ, 1), F32)],
        compiler_params=_params(("parallel", "parallel", "arbitrary"), 2 * s * CHUNK * 2 + 2 * s * CHUNK * 4,
                                4 * _TQ * CHUNK * 4, 32 * _TQ * _TQ * 4),
    )(qkv, qkv, qkv, do, ltot)


def _xattn_fwd(name, q, kk, vv, n_ex):
    t, d = q.shape
    m = kk.shape[0] // n_ex
    s = t // n_ex
    tq = _tile(s, 512)
    nq = s // tq
    hd_dim = d // MEM_HEADS
    scale = hd_dim ** -0.5

    def body(q_ref, k_ref, v_ref, o_ref):
        for hd in range(MEM_HEADS):
            ln = slice(hd * hd_dim, (hd + 1) * hd_dim)
            sc = lax.dot_general(q_ref[:, ln], k_ref[:, ln], _DIMS["nt"], preferred_element_type=F32) * scale
            p = jnp.exp(sc - jnp.max(sc, axis=-1, keepdims=True))
            p = p / jnp.sum(p, axis=-1, keepdims=True)
            o_ref[:, ln] = jnp.dot(p.astype(MM), v_ref[:, ln], preferred_element_type=F32).astype(o_ref.dtype)

    qspec = pl.BlockSpec((tq, d), lambda b, i: (b * nq + i, 0))
    kspec = pl.BlockSpec((m, d), lambda b, i: (b, 0))
    return pl.pallas_call(
        body, name=name, grid=(n_ex, nq), in_specs=[qspec, kspec, kspec], out_specs=qspec,
        out_shape=_sds((t, d), ACT),
        compiler_params=_params(("parallel", "parallel"), 2 * tq * d * 2 + 2 * m * d * 2, 0, 6 * tq * m * 4),
    )(q, kk, vv)


def _xattn_bwd(name, q, kk, vv, do, n_ex):
    t, d = q.shape
    m = kk.shape[0] // n_ex
    s = t // n_ex
    tq = _tile(s, 512)
    nq = s // tq
    hd_dim = d // MEM_HEADS
    scale = hd_dim ** -0.5

    def body(q_ref, k_ref, v_ref, do_ref, dq_ref, dk_ref, dv_ref):
        i = pl.program_id(1)

        @pl.when(i == 0)
        def _():
            dk_ref[...] = jnp.zeros_like(dk_ref)
            dv_ref[...] = jnp.zeros_like(dv_ref)

        for hd in range(MEM_HEADS):
            ln = slice(hd * hd_dim, (hd + 1) * hd_dim)
            qv, kv, vv_, dov = q_ref[:, ln], k_ref[:, ln], v_ref[:, ln], do_ref[:, ln]
            sc = lax.dot_general(qv, kv, _DIMS["nt"], preferred_element_type=F32) * scale
            p = jnp.exp(sc - jnp.max(sc, axis=-1, keepdims=True))
            p = p / jnp.sum(p, axis=-1, keepdims=True)
            dp = lax.dot_general(dov, vv_, _DIMS["nt"], preferred_element_type=F32)
            ds = (p * (dp - jnp.sum(p * dp, axis=-1, keepdims=True)) * scale).astype(MM)
            dq_ref[:, ln] = jnp.dot(ds, kv, preferred_element_type=F32).astype(dq_ref.dtype)
            dk_ref[:, ln] += lax.dot_general(ds, qv, _DIMS["tn"], preferred_element_type=F32)
            dv_ref[:, ln] += lax.dot_general(p.astype(MM), dov, _DIMS["tn"], preferred_element_type=F32)

    qspec = pl.BlockSpec((tq, d), lambda b, i: (b * nq + i, 0))
    kspec = pl.BlockSpec((m, d), lambda b, i: (b, 0))
    return pl.pallas_call(
        body, name=name, grid=(n_ex, nq), in_specs=[qspec, kspec, kspec, qspec], out_specs=[qspec, kspec, kspec],
        out_shape=[_sds((t, d), ACT), _sds((n_ex * m, d), F32), _sds((n_ex * m, d), F32)],
        compiler_params=_params(("parallel", "arbitrary"), 3 * tq * d * 2 + 2 * m * d * 2 + 2 * m * d * 4, 0,
                                8 * tq * m * 4),
    )(q, kk, vv, do)


def _ffn_up(name, y, w1, w3):
    t, d = y.shape
    f = w1.shape[-1]
    tm = _tile(t, 512)
    wspec = pl.BlockSpec((None, d, f), lambda i, j, k: (j, 0, 0))
    hspec = pl.BlockSpec((None, tm, f), lambda i, j, k: (j, i, 0))

    def epi(vals, _):
        h1, h3 = vals
        return [h1, h3, h1 * _sigmoid(h1) * h3]

    return _mm(name, "nn", (t // tm, N_CHIPS, 1), y, pl.BlockSpec((tm, d), lambda i, j, k: (i, 0)),
               [w1, w3], [wspec, wspec], [_sds((N_CHIPS, t, f), ACT)] * 3, [hspec] * 3, (tm, f), epilogue=epi)


def _ffn_down(name, g, w2):
    _, t, f = g.shape
    d = w2.shape[-1]
    tm, tn = _tile(t, 1024), _tile(d, 512)
    return _mm(name, "nn", (t // tm, d // tn, N_CHIPS), g, pl.BlockSpec((None, tm, f), lambda i, j, k: (k, i, 0)),
               [w2], [pl.BlockSpec((None, f, tn), lambda i, j, k: (k, 0, j))],
               [_sds((t, d), F32)], [pl.BlockSpec((tm, tn), lambda i, j, k: (i, j))], (tm, tn))[0]


def _ffn_down_bwd(name, dr, w2, h1, h3):
    t, d = dr.shape
    f = w2.shape[-2]
    tm = _tile(t, 512)
    hspec = pl.BlockSpec((None, tm, f), lambda i, j, k: (j, i, 0))

    def epi(vals, ex):
        dg, = vals
        h1v, h3v = ex
        sg = _sigmoid(h1v)
        return [dg * h3v * (sg * (1.0 + h1v * (1.0 - sg))), dg * h1v * sg]

    return _mm(name, "nt", (t // tm, N_CHIPS, 1), dr, pl.BlockSpec((tm, d), lambda i, j, k: (i, 0)),
               [w2], [pl.BlockSpec((None, f, d), lambda i, j, k: (j, 0, 0))],
               [_sds((N_CHIPS, t, f), ACT)] * 2, [hspec] * 2, (tm, f),
               extras=[h1, h3], extra_specs=[hspec, hspec], epilogue=epi)


def _ffn_up_bwd(name, dh, w, extras, epilogue):
    _, t, f = dh.shape
    d = w.shape[-2]
    tm, tn = _tile(t, 1024), _tile(d, 512)
    ospec = pl.BlockSpec((tm, tn), lambda i, j, k: (i, j))
    return _mm(name, "nt", (t // tm, d // tn, N_CHIPS), dh, pl.BlockSpec((None, tm, f), lambda i, j, k: (k, i, 0)),
               [w], [pl.BlockSpec((None, tn, f), lambda i, j, k: (k, j, 0))],
               [_sds((t, d), F32)], [ospec], (tm, tn),
               extras=extras, extra_specs=[ospec] * len(extras), epilogue=epilogue)[0]


def _ffn_wgrad_up(name, y, dh1, dh3):
    t, d = y.shape
    f = dh1.shape[-1]
    tm, tk = _tile(d, 512), _tile(t, 1024)
    hspec = pl.BlockSpec((None, tk, f), lambda i, j, k: (j, k, 0))
    ospec = pl.BlockSpec((None, tm, f), lambda i, j, k: (j, i, 0))
    return _mm(name, "tn", (d // tm, N_CHIPS, t // tk), y, pl.BlockSpec((tk, tm), lambda i, j, k: (k, i)),
               [dh1, dh3], [hspec, hspec], [_sds((N_CHIPS, d, f), MM)] * 2, [ospec, ospec], (tm, f))


def _ffn_wgrad_down(name, g, dr):
    _, t, f = g.shape
    d = dr.shape[1]
    tn, tk = _tile(d, 512), _tile(t, 1024)
    return _mm(name, "tn", (N_CHIPS, d // tn, t // tk), g, pl.BlockSpec((None, tk, f), lambda i, j, k: (i, k, 0)),
               [dr], [pl.BlockSpec((tk, tn), lambda i, j, k: (k, j))],
               [_sds((N_CHIPS, f, d), MM)], [pl.BlockSpec((None, f, tn), lambda i, j, k: (i, 0, j))], (f, tn))[0]


def _proj_cols(name, y, w, out_dtype):
    t, kdim = y.shape
    wd = w.shape[-1]
    tn = _tile(wd, 512)
    per = wd // tn
    tm = _tile(t, 1024)
    return _mm(name, "nn", (t // tm, N_CHIPS * per, 1), y, pl.BlockSpec((tm, kdim), lambda i, j, k: (i, 0)),
               [w], [pl.BlockSpec((None, kdim, tn), lambda i, j, k: (j // per, 0, j % per))],
               [_sds((t, N_CHIPS * wd), out_dtype)], [pl.BlockSpec((tm, tn), lambda i, j, k: (i, j))], (tm, tn))[0]


def _proj_cols_bwd(name, dh, w, extras, epilogue):
    t = dh.shape[0]
    kdim, wd = w.shape[-2], w.shape[-1]
    tm, tn = _tile(t, 1024), _tile(kdim, 512)
    ospec = pl.BlockSpec((tm, tn), lambda i, j, k: (i, j))
    return _mm(name, "nt", (t // tm, kdim // tn, N_CHIPS), dh, pl.BlockSpec((tm, wd), lambda i, j, k: (i, k)),
               [w], [pl.BlockSpec((None, tn, wd), lambda i, j, k: (k, j, 0))],
               [_sds((t, kdim), F32)], [ospec], (tm, tn),
               extras=extras, extra_specs=[ospec] * len(extras), epilogue=epilogue)[0]


def _proj_cols_wgrad(name, y, dh):
    t, kdim = y.shape
    wd = dh.shape[1] // N_CHIPS
    tm, tk = _tile(kdim, 512), _tile(t, 1024)
    return _mm(name, "tn", (kdim // tm, N_CHIPS, t // tk), y, pl.BlockSpec((tk, tm), lambda i, j, k: (k, i)),
               [dh], [pl.BlockSpec((tk, wd), lambda i, j, k: (k, j))],
               [_sds((N_CHIPS, kdim, wd), MM)], [pl.BlockSpec((None, tm, wd), lambda i, j, k: (j, i, 0))], (tm, wd))[0]


def _coords():
    return lax.axis_index("x"), lax.axis_index("y"), lax.axis_index("c")


def _chip_peers(x, y):
    return [(1 - x, y), (x, 1 - y), (1 - x, 1 - y)]


def _slot(ref, axis, k):
    return ref.at[k] if axis == 0 else ref.at[:, k]


_ANY = pl.BlockSpec(memory_space=pl.ANY)


def _gather_chips(shards, axes):
    n = len(shards)

    def body(*refs):
        ins, outs = refs[:n], refs[n:2 * n]
        send_sems, recv_sems, loc_sems = refs[2 * n:]
        x, y, c = _coords()
        me = 2 * x + y
        copies = []
        for t in range(n):
            dst = _slot(outs[t], axes[t], me)
            cp = pltpu.make_async_copy(ins[t], dst, loc_sems.at[t])
            cp.start()
            copies.append(cp)
            for j, chip in enumerate(_chip_peers(x, y)):
                cp = pltpu.make_async_remote_copy(src_ref=ins[t], dst_ref=dst, send_sem=send_sems.at[t, j],
                                                  recv_sem=recv_sems.at[t, j], device_id=(*chip, c), device_id_type=MESH)
                cp.start()
                copies.append(cp)
        for cp in copies:
            cp.wait()

    def out_shape(a, ax):
        sh = (N_CHIPS, *a.shape) if ax == 0 else (a.shape[0], N_CHIPS, *a.shape[1:])
        return _sds(sh, a.dtype)

    return pl.pallas_call(
        body, name="gather_weights", in_specs=[_ANY] * n, out_specs=[_ANY] * n,
        out_shape=[out_shape(a, ax) for a, ax in zip(shards, axes)],
        scratch_shapes=[pltpu.SemaphoreType.DMA((n, 3)), pltpu.SemaphoreType.DMA((n, 3)), pltpu.SemaphoreType.DMA((n,))],
    )(*shards)


_HBM = pl.BlockSpec(memory_space=pltpu.HBM)
_SEM = pl.BlockSpec(memory_space=pltpu.SEMAPHORE)
_EFFECT = pltpu.SideEffectType.DATAFLOW_SIDE_EFFECTING


def _chip_copies(kind, srcs, lands, send_sems, recv_sems):
    x, y, c = _coords()
    me = 2 * x + y
    copies = []
    for t, (src, land) in enumerate(zip(srcs, lands)):
        for j, chip in enumerate(_chip_peers(x, y)):
            if kind == "gather":
                s_ref, d_ref = src, land.at[me]
            else:
                s_ref, d_ref = src.at[2 * chip[0] + chip[1]], land.at[j]
            copies.append(pltpu.make_async_remote_copy(src_ref=s_ref, dst_ref=d_ref, send_sem=send_sems.at[3 * t + j],
                                                       recv_sem=recv_sems.at[3 * t + j], device_id=(*chip, c), device_id_type=MESH))
    return copies


def _send_start(name, kind, srcs, lands):
    n = len(srcs)

    def body(*refs):
        ins, lz = refs[:n], refs[n:2 * n]
        send_sems, recv_sems = refs[2 * n], refs[2 * n + 1]
        token = refs[-1]
        for cp in _chip_copies(kind, ins, lz, send_sems, recv_sems):
            cp.start()
        token[...] = jnp.zeros_like(token)

    both = [*srcs, *lands]
    res = pl.pallas_call(
        body, name=name,
        out_shape=(pltpu.SemaphoreType.DMA((3 * n,)), pltpu.SemaphoreType.DMA((3 * n,)),
                   *[pltpu.HBM(a.shape, a.dtype) for a in both], _sds((8, 128), F32)),
        in_specs=[_HBM] * (2 * n),
        out_specs=(_SEM, _SEM, *[_HBM] * (2 * n), pl.BlockSpec(memory_space=pltpu.VMEM)),
        input_output_aliases={i: 2 + i for i in range(2 * n)},
        compiler_params=pltpu.CompilerParams(has_side_effects=_EFFECT),
    )(*[pltpu.with_memory_space_constraint(a, pltpu.HBM) for a in both])
    return res[0], res[1], list(res[2:2 + n]), list(res[2 + n:2 + 2 * n]), res[-1]


def _send_wait(name, kind, started, after):
    send_sems, recv_sems, srcs, lands, _ = started
    n = len(srcs)

    def body(*refs):
        ins, lz = refs[:n], refs[n:2 * n]
        s_sems, r_sems = refs[2 * n], refs[2 * n + 1]
        for cp in _chip_copies(kind, ins, lz, s_sems, r_sems):
            cp.wait_send()
            cp.wait_recv()

    both = [*srcs, *lands]
    res = pl.pallas_call(
        body, name=name,
        out_shape=[pltpu.HBM(a.shape, a.dtype) for a in both],
        in_specs=[*[_HBM] * (2 * n), _SEM, _SEM, _ANY], out_specs=[_HBM] * (2 * n),
        input_output_aliases={i: i for i in range(2 * n)},
        compiler_params=pltpu.CompilerParams(has_side_effects=_EFFECT),
    )(*both, send_sems, recv_sems, after)
    return list(res[:n]), list(res[n:])


def _swap_sibling(arrs):
    n = len(arrs)

    def body(*refs):
        ins, outs = refs[:n], refs[n:2 * n]
        send_sems, recv_sems = refs[2 * n:]
        x, y, c = _coords()
        copies = []
        for t in range(n):
            cp = pltpu.make_async_remote_copy(src_ref=ins[t], dst_ref=outs[t], send_sem=send_sems.at[t],
                                              recv_sem=recv_sems.at[t], device_id=(x, y, 1 - c), device_id_type=MESH)
            cp.start()
            copies.append(cp)
        for cp in copies:
            cp.wait()

    return pl.pallas_call(
        body, name="swap_sibling", in_specs=[_ANY] * n, out_specs=[_ANY] * n,
        out_shape=[_sds(a.shape, a.dtype) for a in arrs],
        scratch_shapes=[pltpu.SemaphoreType.DMA((n,)), pltpu.SemaphoreType.DMA((n,))],
    )(*arrs)


def _gather_all(part):
    def body(in_ref, out_ref, send_sems, recv_sems, loc_sem):
        x, y, c = _coords()
        dst = out_ref.at[4 * x + 2 * y + c]
        copies = [pltpu.make_async_copy(in_ref, dst, loc_sem)]
        for r in range(1, N_DEV):
            fx, fy, fc = (r >> 2) & 1, (r >> 1) & 1, r & 1
            peer = (x ^ fx, y ^ fy, c ^ fc)
            copies.append(pltpu.make_async_remote_copy(src_ref=in_ref, dst_ref=dst, send_sem=send_sems.at[r - 1],
                                                       recv_sem=recv_sems.at[r - 1], device_id=peer, device_id_type=MESH))
        for cp in copies:
            cp.start()
        for cp in copies:
            cp.wait()

    return pl.pallas_call(
        body, name="gather_small_grads", in_specs=[_ANY], out_specs=_ANY,
        out_shape=_sds((N_DEV, *part.shape), part.dtype),
        scratch_shapes=[pltpu.SemaphoreType.DMA((N_DEV - 1,)), pltpu.SemaphoreType.DMA((N_DEV - 1,)), pltpu.SemaphoreType.DMA],
    )(part)


def _sum_chips(grad, recv, me):
    _, rr, cc = recv.shape
    tr = _tile(rr, 512)

    def body(me_ref, g_ref, r0_ref, r1_ref, r2_ref, o_ref):
        o_ref[...] = ((g_ref[...].astype(F32) + r0_ref[...].astype(F32)) + r1_ref[...].astype(F32)) + r2_ref[...].astype(F32)

    gspec = pl.BlockSpec((None, tr, cc), lambda r, m: (m[0], r, 0))
    rspecs = [pl.BlockSpec((None, tr, cc), functools.partial(lambda r, m, j: (j, r, 0), j=j)) for j in range(3)]
    return pl.pallas_call(
        body, name="sum_chip_grads",
        grid_spec=pltpu.PrefetchScalarGridSpec(
            num_scalar_prefetch=1, grid=(rr // tr,), in_specs=[gspec, *rspecs],
            out_specs=pl.BlockSpec((tr, cc), lambda r, m: (r, 0))),
        out_shape=_sds((rr, cc), F32),
        compiler_params=_params(("parallel",), 4 * tr * cc * 2 + tr * cc * 4, 0, 2 * tr * cc * 4),
    )(me, grad, recv, recv, recv)


def _adamw_math(w, g, m, v):
    m = ADAM_B1 * m + (1.0 - ADAM_B1) * g
    v = ADAM_B2 * v + (1.0 - ADAM_B2) * (g * g)
    m_hat = m / (1.0 - ADAM_B1 ** ADAM_STEP)
    v_hat = v / (1.0 - ADAM_B2 ** ADAM_STEP)
    delta = -ADAM_LR * (m_hat / (jnp.sqrt(v_hat) + ADAM_EPS) + ADAM_WD * w)
    return delta, m, v


def _adamw(name, parts, w, m, v):
    ll, rr, cc = w.shape
    tr = _tile(rr, 256)
    npart = len(parts)

    def body(*refs):
        p_refs = refs[:npart]
        w_ref, m_ref, v_ref, g_ref, d_ref, nm_ref, nv_ref = refs[npart:]
        g = p_refs[0][...]
        for p in p_refs[1:]:
            g = g + p[...]
        d, nm, nv = _adamw_math(w_ref[...], g, m_ref[...], v_ref[...])
        g_ref[...] = g
        d_ref[...] = d
        nm_ref[...] = nm
        nv_ref[...] = nv

    spec = pl.BlockSpec((None, tr, cc), lambda l, r: (l, r, 0))
    out = _sds((ll, rr, cc), F32)
    return pl.pallas_call(
        body, name=name, grid=(ll, rr // tr), in_specs=[spec] * (npart + 3), out_specs=[spec] * 4, out_shape=[out] * 4,
        compiler_params=_params(("parallel", "parallel"), (npart + 7) * tr * cc * 4, 0, 4 * tr * cc * 4),
    )(*parts, w, m, v)


def _sum_devices(allparts):
    _, rr, cc = allparts.shape

    def body(p_ref, o_ref):
        s = p_ref[0]
        for k in range(1, N_DEV):
            s = s + p_ref[k]
        o_ref[...] = s

    return pl.pallas_call(
        body, name="sum_small_grads", grid=(1,), in_specs=[pl.BlockSpec((N_DEV, rr, cc), lambda i: (0, 0, 0))],
        out_specs=pl.BlockSpec((rr, cc), lambda i: (0, 0)), out_shape=_sds((rr, cc), F32),
        compiler_params=_params(("arbitrary",), 9 * rr * cc * 4),
    )(allparts)


def _pack(arrs):
    flat = jnp.concatenate([a.reshape(-1).astype(F32) for a in arrs])
    n = flat.shape[0]
    total = -(-n // 1024) * 1024
    return jnp.pad(flat, (0, total - n)).reshape(total // 128, 128)


def _unpack(block, shapes):
    flat = block.reshape(-1)
    out, off = [], 0
    for sh in shapes:
        n = math.prod(sh)
        out.append(flat[off:off + n].reshape(sh))
        off += n
    return out


_BIG = ["w_in_ab", "w_out_ab", "w_qkv_c", "w_out_c", "mem_wq", "mem_wk", "mem_wv", "mem_wo", "ffn_w1", "ffn_w3", "ffn_w2"]
_ROW_SHARDED = ("w_out_ab", "w_out_c", "mem_wq", "mem_wk", "mem_wv", "mem_wo")
_SMALL_REPL = ["gmlp_ln_g", "gmlp_ln_b", "gmlp_w_s", "gmlp_b_s", "conv_b", "conv_gn_g", "conv_gn_b"]
_SMALL_SHARD = ["conv_w", "ln_g", "ln_b"]
_NAMES = ["w_in_ab", "gmlp_ln_g", "gmlp_ln_b", "gmlp_w_s", "gmlp_b_s", "conv_w", "conv_b", "conv_gn_g", "conv_gn_b",
          "w_out_ab", "w_qkv_c", "w_out_c", "mem_wq", "mem_wk", "mem_wv", "mem_wo", "ffn_w1", "ffn_w3", "ffn_w2",
          "ln_g", "ln_b"]


def _layer_weights(l):
    mixer = ["w_in_ab", "w_out_ab"] if l % 2 == 0 else ["w_qkv_c", "w_out_c"]
    return [(n, l // 2) for n in mixer] + [(n, l) for n in _BIG if n.startswith(("mem_", "ffn_"))]


def _natural(w):
    return w.reshape(-1, w.shape[-1])


def _local_step(x, mem, target, get_weights, small, on_grads):
    n_ex, s, d = x.shape
    t = n_ex * s
    x2 = x.reshape(t, d)
    mem_a = mem.reshape(-1, d).astype(ACT)
    tgt = target.reshape(t, d)
    one = jnp.ones((1, d), F32)
    zero = jnp.zeros((1, d), F32)
    ln_g, ln_b = small["ln_g"], small["ln_b"]

    def vec(a):
        return a.reshape(1, -1)

    saved = []
    xh, gp, bp = x2, one, zero
    y_act = x2.astype(ACT)
    for l in range(DEPTH):
        wts = get_weights(l, y_act)
        sv = {"y0": y_act, "w": wts}
        if l % 2 == 0:
            e = l // 2
            h = _proj_cols(f"in_ab_{l}", y_act, wts["w_in_ab"], ACT)
            gl = (vec(small["gmlp_ln_g"][e]), vec(small["gmlp_ln_b"][e]), small["gmlp_w_s"][e],
                  small["gmlp_b_s"][e].reshape(4, CHUNK, 1))
            cl = (small["conv_w"][e], vec(small["conv_b"][e]), vec(small["conv_gn_g"][e]), vec(small["conv_gn_b"][e]))
            ya = _gmlp_fwd(f"gmlp_fwd_{l}", h, *gl)
            yb = _conv_fwd(f"conv_fwd_{l}", h, n_ex, *cl)
            yab = jnp.concatenate([ya, yb], axis=1)
            mix = _dense(f"out_ab_{l}", yab, _natural(wts["w_out_ab"]), F32)
            sv.update(h=h, yab=yab, gl=gl, cl=cl)
        else:
            qkv = _proj_cols(f"qkv_{l}", y_act, wts["w_qkv_c"], ACT)
            att, ltot = _sb_fwd(f"sb_fwd_{l}", qkv, n_ex)
            att_a = att.astype(ACT)
            mix = _dense(f"out_c_{l}", att_a, _natural(wts["w_out_c"]), F32)
            sv.update(qkv=qkv, att=att_a, ltot=ltot)
        g1, b1 = vec(ln_g[l, 0]), vec(ln_b[l, 0])
        xh1, y1, rstd1 = _ln_fwd(f"ln1_fwd_{l}", xh, gp, bp, mix, g1, b1)
        q = _dense(f"mem_q_{l}", y1, _natural(wts["mem_wq"]), ACT)
        kk = _dense(f"mem_k_{l}", mem_a, _natural(wts["mem_wk"]), ACT)
        vv = _dense(f"mem_v_{l}", mem_a, _natural(wts["mem_wv"]), ACT)
        oc = _xattn_fwd(f"xattn_fwd_{l}", q, kk, vv, n_ex)
        cross = _dense(f"mem_o_{l}", oc, _natural(wts["mem_wo"]), F32)
        g2, b2 = vec(ln_g[l, 1]), vec(ln_b[l, 1])
        xh2, y2, rstd2 = _ln_fwd(f"ln2_fwd_{l}", xh1, g1, b1, cross, g2, b2)
        h1, h3, gact = _ffn_up(f"ffn_up_{l}", y2, wts["ffn_w1"], wts["ffn_w3"])
        ffo = _ffn_down(f"ffn_down_{l}", gact, wts["ffn_w2"])
        g3, b3 = vec(ln_g[l, 2]), vec(ln_b[l, 2])
        xh3, y3, rstd3 = _ln_fwd(f"ln3_fwd_{l}", xh2, g2, b2, ffo, g3, b3)
        sv.update(xh1=xh1, y1=y1, rstd1=rstd1, g1=g1, q=q, kk=kk, vv=vv, oc=oc, xh2=xh2, y2=y2, rstd2=rstd2, g2=g2,
                  h1=h1, h3=h3, gact=gact, xh3=xh3, rstd3=rstd3, g3=g3)
        saved.append(sv)
        xh, gp, bp, y_act = xh3, g3, b3, y3

    dy, loss = _loss_head(xh, gp, bp, tgt)

    sm = {n: [None] * (DEPTH // 2) for n in _SMALL_REPL + ["conv_w"]}
    d_ln_g = [[None] * 3 for _ in range(DEPTH)]
    d_ln_b = [[None] * 3 for _ in range(DEPTH)]

    def add_res(vals, ex):
        return [vals[0] + ALPHA * ex[0]]

    def add_res2(vals, ex):
        return [vals[0] + ex[0] + ALPHA * ex[1]]

    token = None
    for l in reversed(range(DEPTH)):
        sv = saved[l]
        wts = sv["w"]
        big = {}
        g3 = sv["g3"] if token is None else sv["g3"] + token
        dr3, dr3a, d_ln_g[l][2], d_ln_b[l][2] = _ln_bwd(f"ln3_bwd_{l}", dy, sv["xh3"], sv["rstd3"], g3)
        big["ffn_w2"] = _ffn_wgrad_down(f"ffn_w2_grad_{l}", sv["gact"], dr3a)
        dh1, dh3 = _ffn_down_bwd(f"ffn_down_bwd_{l}", dr3a, wts["ffn_w2"], sv["h1"], sv["h3"])
        big["ffn_w1"], big["ffn_w3"] = _ffn_wgrad_up(f"ffn_w13_grad_{l}", sv["y2"], dh1, dh3)
        part = _ffn_up_bwd(f"ffn_up_bwd1_{l}", dh1, wts["ffn_w1"], [], None)
        dy = _ffn_up_bwd(f"ffn_up_bwd3_{l}", dh3, wts["ffn_w3"], [part, dr3], add_res2)
        dr2, dr2a, d_ln_g[l][1], d_ln_b[l][1] = _ln_bwd(f"ln2_bwd_{l}", dy, sv["xh2"], sv["rstd2"], sv["g2"])
        big["mem_wo"] = _dense_tn(f"mem_wo_grad_{l}", sv["oc"], dr2a)
        doc = _dense_nt(f"mem_o_bwd_{l}", dr2a, _natural(wts["mem_wo"]), ACT)
        dq, dkk, dvv = _xattn_bwd(f"xattn_bwd_{l}", sv["q"], sv["kk"], sv["vv"], doc, n_ex)
        big["mem_wq"] = _dense_tn(f"mem_wq_grad_{l}", sv["y1"], dq)
        big["mem_wk"] = _dense_tn(f"mem_wk_grad_{l}", mem_a, dkk)
        big["mem_wv"] = _dense_tn(f"mem_wv_grad_{l}", mem_a, dvv)
        dy = _dense_nt(f"mem_q_bwd_{l}", dq, _natural(wts["mem_wq"]), F32, extras=[dr2], epilogue=add_res)
        dr1, dr1a, d_ln_g[l][0], d_ln_b[l][0] = _ln_bwd(f"ln1_bwd_{l}", dy, sv["xh1"], sv["rstd1"], sv["g1"])
        if l % 2 == 0:
            e = l // 2
            big["w_out_ab"] = _dense_tn(f"out_ab_grad_{l}", sv["yab"], dr1a)
            dyab = _dense_nt(f"out_ab_bwd_{l}", dr1a, _natural(wts["w_out_ab"]), ACT)
            duv, dgg, dgb, dws, dbs = _gmlp_bwd(f"gmlp_bwd_{l}", sv["h"], dyab, *sv["gl"])
            da, dgt, dcw, dcb, dng, dnb = _conv_bwd(f"conv_bwd_{l}", sv["h"], dyab, n_ex, *sv["cl"])
            sm["gmlp_ln_g"][e], sm["gmlp_ln_b"][e] = dgg.reshape(-1), dgb.reshape(-1)
            sm["gmlp_w_s"][e], sm["gmlp_b_s"][e] = dws, dbs.reshape(4, CHUNK)
            sm["conv_w"][e], sm["conv_b"][e] = dcw, dcb.reshape(-1)
            sm["conv_gn_g"][e], sm["conv_gn_b"][e] = dng.reshape(-1), dnb.reshape(-1)
            dh = jnp.concatenate([duv, da, dgt], axis=1)
            big["w_in_ab"] = _proj_cols_wgrad(f"in_ab_grad_{l}", sv["y0"], dh)
            dy = _proj_cols_bwd(f"in_ab_bwd_{l}", dh, wts["w_in_ab"], [dr1], add_res)
        else:
            big["w_out_c"] = _dense_tn(f"out_c_grad_{l}", sv["att"], dr1a)
            datt = _dense_nt(f"out_c_bwd_{l}", dr1a, _natural(wts["w_out_c"]), ACT)
            dq_, dk_, dv_ = _sb_bwd(f"sb_bwd_{l}", sv["qkv"], datt, sv["ltot"], n_ex)
            dqkv = jnp.concatenate([dq_, dk_.astype(ACT), dv_.astype(ACT)], axis=1)
            big["w_qkv_c"] = _proj_cols_wgrad(f"qkv_grad_{l}", sv["y0"], dqkv)
            dy = _proj_cols_bwd(f"qkv_bwd_{l}", dqkv, wts["w_qkv_c"], [dr1], add_res)
        for n in _ROW_SHARDED:
            if n in big:
                big[n] = big[n].reshape(N_CHIPS, -1, big[n].shape[-1])
        token = on_grads(l, big)

    grad_x = dy.reshape(n_ex, s, d)
    small_g = {n: jnp.stack(sm[n]) for n in sm}
    small_g["ln_g"] = jnp.stack([jnp.concatenate(r, axis=0) for r in d_ln_g])
    small_g["ln_b"] = jnp.stack([jnp.concatenate(r, axis=0) for r in d_ln_b])
    return loss, grad_x, small_g


def kernel(x, mem, w_in_ab, gmlp_ln_g, gmlp_ln_b, gmlp_w_s, gmlp_b_s, conv_w, conv_b, conv_gn_g, conv_gn_b, w_out_ab, w_qkv_c, w_out_c, mem_wq, mem_wk, mem_wv, mem_wo, ffn_w1, ffn_w3, ffn_w2, ln_g, ln_b, loss_target, m_w_in_ab, m_gmlp_ln_g, m_gmlp_ln_b, m_gmlp_w_s, m_gmlp_b_s, m_conv_w, m_conv_b, m_conv_gn_g, m_conv_gn_b, m_w_out_ab, m_w_qkv_c, m_w_out_c, m_mem_wq, m_mem_wk, m_mem_wv, m_mem_wo, m_ffn_w1, m_ffn_w3, m_ffn_w2, m_ln_g, m_ln_b, v_w_in_ab, v_gmlp_ln_g, v_gmlp_ln_b, v_gmlp_w_s, v_gmlp_b_s, v_conv_w, v_conv_b, v_conv_gn_g, v_conv_gn_b, v_w_out_ab, v_w_qkv_c, v_w_out_c, v_mem_wq, v_mem_wk, v_mem_wv, v_mem_wo, v_ffn_w1, v_ffn_w3, v_ffn_w2, v_ln_g, v_ln_b):
    w = dict(w_in_ab=w_in_ab, gmlp_ln_g=gmlp_ln_g, gmlp_ln_b=gmlp_ln_b, gmlp_w_s=gmlp_w_s, gmlp_b_s=gmlp_b_s, conv_w=conv_w,
             conv_b=conv_b, conv_gn_g=conv_gn_g, conv_gn_b=conv_gn_b, w_out_ab=w_out_ab, w_qkv_c=w_qkv_c, w_out_c=w_out_c,
             mem_wq=mem_wq, mem_wk=mem_wk, mem_wv=mem_wv, mem_wo=mem_wo, ffn_w1=ffn_w1, ffn_w3=ffn_w3, ffn_w2=ffn_w2,
             ln_g=ln_g, ln_b=ln_b)
    mo = dict(w_in_ab=m_w_in_ab, gmlp_ln_g=m_gmlp_ln_g, gmlp_ln_b=m_gmlp_ln_b, gmlp_w_s=m_gmlp_w_s, gmlp_b_s=m_gmlp_b_s,
              conv_w=m_conv_w, conv_b=m_conv_b, conv_gn_g=m_conv_gn_g, conv_gn_b=m_conv_gn_b, w_out_ab=m_w_out_ab,
              w_qkv_c=m_w_qkv_c, w_out_c=m_w_out_c, mem_wq=m_mem_wq, mem_wk=m_mem_wk, mem_wv=m_mem_wv, mem_wo=m_mem_wo,
              ffn_w1=m_ffn_w1, ffn_w3=m_ffn_w3, ffn_w2=m_ffn_w2, ln_g=m_ln_g, ln_b=m_ln_b)
    vo = dict(w_in_ab=v_w_in_ab, gmlp_ln_g=v_gmlp_ln_g, gmlp_ln_b=v_gmlp_ln_b, gmlp_w_s=v_gmlp_w_s, gmlp_b_s=v_gmlp_b_s,
              conv_w=v_conv_w, conv_b=v_conv_b, conv_gn_g=v_conv_gn_g, conv_gn_b=v_conv_gn_b, w_out_ab=v_w_out_ab,
              w_qkv_c=v_w_qkv_c, w_out_c=v_w_out_c, mem_wq=v_mem_wq, mem_wk=v_mem_wk, mem_wv=v_mem_wv, mem_wo=v_mem_wo,
              ffn_w1=v_ffn_w1, ffn_w3=v_ffn_w3, ffn_w2=v_ffn_w2, ln_g=v_ln_g, ln_b=v_ln_b)
    me = (2 * lax.axis_index("x") + lax.axis_index("y")).astype(jnp.int32).reshape(1)

    cw_g, lg_g, lb_g = _gather_chips([w[n] for n in _SMALL_SHARD], [0, 0, 0])
    small = {n: w[n] for n in _SMALL_REPL}
    small["conv_w"] = jnp.moveaxis(cw_g, 0, 2).reshape(cw_g.shape[1], CONV_WIDTH, -1)
    small["ln_g"] = jnp.moveaxis(lg_g, 0, 2).reshape(DEPTH, 3, -1)
    small["ln_b"] = jnp.moveaxis(lb_g, 0, 2).reshape(DEPTH, 3, -1)
    started = []
    for l in range(DEPTH):
        order = started[-1][4][0, 0] if started else 0.0
        srcs = [(w[n][i] + order).astype(MM) for n, i in _layer_weights(l)]
        lands = [lax.dynamic_update_index_in_dim(jnp.zeros((N_CHIPS, *s.shape), MM), s[None], me[0], 0) for s in srcs]
        started.append(_send_start(f"gather_start_{l}", "gather", srcs, lands))
    all_started = sum(st[4] for st in started)

    def get_weights(l, after):
        _, lands = _send_wait(f"gather_wait_{l}", "gather", started[l], all_started if l == 0 else after)
        return {n: g for (n, _), g in zip(_layer_weights(l), lands)}

    scattered = [None] * DEPTH

    def on_grads(l, big):
        srcs = [big[n] for n, _ in _layer_weights(l)]
        lands = [lax.empty((3, *s.shape[1:]), s.dtype) for s in srcs]
        scattered[l] = _send_start(f"scatter_start_{l}", "scatter", srcs, lands)
        return scattered[l][4][0, 0]

    loss, grad_x, small_g = _local_step(x, mem, loss_target, get_weights, small, on_grads)
    loss = lax.psum(loss, ("x", "y", "c"))

    per_name = {n: [None] * (DEPTH if n.startswith(("mem_", "ffn_")) else DEPTH // 2) for n in _BIG}
    for l in range(DEPTH):
        own, recv = _send_wait(f"scatter_wait_{l}", "scatter", scattered[l], grad_x)
        for (n, i), g, rc in zip(_layer_weights(l), own, recv):
            per_name[n][i] = _sum_chips(g, rc, me)
    sums = [jnp.stack(per_name[n]) for n in _BIG]
    sib = _swap_sibling(sums)

    out = {}
    for n, s_own, s_sib in zip(_BIG, sums, sib):
        out[n] = _adamw(f"adamw_{n}", [s_own, s_sib], w[n], mo[n], vo[n])

    order = _SMALL_REPL + _SMALL_SHARD
    part = _pack([small_g[n] for n in order])
    total = _sum_devices(_gather_all(part))
    full = dict(zip(order, _unpack(total, [small_g[n].shape for n in order])))
    x_i, y_i = lax.axis_index("x"), lax.axis_index("y")
    chip = 2 * x_i + y_i
    loc = {n: full[n] for n in _SMALL_REPL}
    for n in _SMALL_SHARD:
        wd = w[n].shape[-1]
        loc[n] = lax.dynamic_slice_in_dim(full[n], chip * wd, wd, axis=full[n].ndim - 1)
    gp, wp, mp, vp = (_pack([src[n] for n in order]) for src in (loc, w, mo, vo))
    r128 = gp.shape[0]
    res = _adamw("adamw_small", [gp.reshape(1, r128, 128)], wp.reshape(1, r128, 128), mp.reshape(1, r128, 128),
                 vp.reshape(1, r128, 128))
    shapes = [w[n].shape for n in order]
    unp = [_unpack(r.reshape(r128, 128), shapes) for r in res]
    for i, n in enumerate(order):
        out[n] = tuple(u[i] for u in unp)

    grads = [out[n][0] for n in _NAMES]
    deltas = [out[n][1] for n in _NAMES]
    new_m = [out[n][2] for n in _NAMES]
    new_v = [out[n][3] for n in _NAMES]
    return (loss, grad_x, *grads, *deltas, *new_m, *new_v)
```

```python
import functools
import math

import jax
import jax.numpy as jnp
from jax import lax
from jax.experimental import pallas as pl
from jax.experimental.pallas import tpu as pltpu

F32 = jnp.float32
MM = jnp.bfloat16
ACT = jnp.bfloat16
MESH = pl.DeviceIdType.MESH

DEPTH = 4
CHUNK = 128
CONV_WIDTH = 31
HALO = 32
MEM_HEADS = 4
C_HEAD_DIM = 64
ALPHA = (2.0 * DEPTH) ** 0.25
LN_EPS = 1e-5
ADAM_LR, ADAM_B1, ADAM_B2, ADAM_EPS, ADAM_WD, ADAM_STEP = 0.001, 0.9, 0.999, 1e-08, 0.01, 10

VMEM_CAP_V7X = 64 * 1024 * 1024
VMEM_MAX_REQUEST = 56 * 1024 * 1024
N_CHIPS = 4
N_DEV = 8


def _tile(n, pref):
    if n <= pref:
        return n
    for t in range(pref - pref % 8, 7, -8):
        if n % t == 0:
            return t
    return n


def _nbytes(shape, dtype):
    return math.prod(1 if s is None else s for s in shape) * jnp.dtype(dtype).itemsize


def _vmem_limit(block_bytes, scratch_bytes=0, temp_bytes=0):
    est = 2 * block_bytes + scratch_bytes + temp_bytes
    return int(min(VMEM_MAX_REQUEST, max(16 * 1024 * 1024, est * 5 // 4)))


def _params(sem, block_bytes, scratch_bytes=0, temp_bytes=0):
    return pltpu.CompilerParams(dimension_semantics=sem,
                                vmem_limit_bytes=_vmem_limit(block_bytes, scratch_bytes, temp_bytes))


_DIMS = {"nn": (((1,), (0,)), ((), ())), "nt": (((1,), (1,)), ((), ())), "tn": (((0,), (0,)), ((), ()))}


def _mm(name, mode, grid, a, a_spec, bs, b_specs, outs, out_specs, acc_shape,
        extras=(), extra_specs=(), epilogue=None):
    nb, ne, no = len(bs), len(extras), len(outs)
    nk = grid[2]

    def body(*refs):
        a_ref = refs[0]
        b_refs = refs[1:1 + nb]
        e_refs = refs[1 + nb:1 + nb + ne]
        o_refs = refs[1 + nb + ne:1 + nb + ne + no]
        accs = refs[1 + nb + ne + no:]
        k = pl.program_id(2)

        @pl.when(k == 0)
        def _():
            for acc in accs:
                acc[...] = jnp.zeros_like(acc)

        av = a_ref[...].astype(MM)
        for b_ref, acc in zip(b_refs, accs):
            acc[...] += lax.dot_general(av, b_ref[...].astype(MM), _DIMS[mode], preferred_element_type=F32)

        @pl.when(k == nk - 1)
        def _():
            vals = [acc[...] for acc in accs]
            if epilogue is not None:
                vals = epilogue(vals, [e[...].astype(F32) for e in e_refs])
            for o, v in zip(o_refs, vals):
                o[...] = v.astype(o.dtype)

    blocks = (_nbytes(a_spec.block_shape, a.dtype)
              + sum(_nbytes(s.block_shape, b.dtype) for s, b in zip(b_specs, bs))
              + sum(_nbytes(s.block_shape, e.dtype) for s, e in zip(extra_specs, extras))
              + sum(_nbytes(s.block_shape, o.dtype) for s, o in zip(out_specs, outs)))
    acc_bytes = nb * _nbytes(acc_shape, F32)
    res = pl.pallas_call(
        body, name=name, grid=grid,
        in_specs=[a_spec, *b_specs, *extra_specs], out_specs=list(out_specs), out_shape=list(outs),
        scratch_shapes=[pltpu.VMEM(acc_shape, F32)] * nb,
        compiler_params=_params(("parallel", "parallel", "arbitrary"), blocks, acc_bytes, 4 * acc_bytes),
    )(a, *bs, *extras)
    return res


def _sds(shape, dtype):
    return jax.ShapeDtypeStruct(shape, dtype)


def _dense(name, a, w, out_dtype, extras=(), epilogue=None, n_out=None):
    t, kdim = a.shape
    n = w.shape[1]
    tm, tn, tk = _tile(t, 1024), _tile(n, 512), _tile(kdim, 1024)
    grid = (t // tm, n // tn, kdim // tk)
    return _mm(name, "nn", grid, a, pl.BlockSpec((tm, tk), lambda i, j, k: (i, k)),
               [w], [pl.BlockSpec((tk, tn), lambda i, j, k: (k, j))],
               [_sds((t, n), out_dtype)], [pl.BlockSpec((tm, tn), lambda i, j, k: (i, j))], (tm, tn),
               extras=extras, extra_specs=[pl.BlockSpec((tm, tn), lambda i, j, k: (i, j))] * len(extras),
               epilogue=epilogue)[0]


def _dense_nt(name, a, w, out_dtype, extras=(), epilogue=None):
    t, n = a.shape
    kout = w.shape[0]
    tm, tn, tk = _tile(t, 1024), _tile(kout, 512), _tile(n, 1024)
    grid = (t // tm, kout // tn, n // tk)
    return _mm(name, "nt", grid, a, pl.BlockSpec((tm, tk), lambda i, j, k: (i, k)),
               [w], [pl.BlockSpec((tn, tk), lambda i, j, k: (j, k))],
               [_sds((t, kout), out_dtype)], [pl.BlockSpec((tm, tn), lambda i, j, k: (i, j))], (tm, tn),
               extras=extras, extra_specs=[pl.BlockSpec((tm, tn), lambda i, j, k: (i, j))] * len(extras),
               epilogue=epilogue)[0]


def _dense_tn(name, a, b, out_dtype=MM):
    t, m = a.shape
    n = b.shape[1]
    tm, tn, tk = _tile(m, 512), _tile(n, 512), _tile(t, 1024)
    grid = (m // tm, n // tn, t // tk)
    return _mm(name, "tn", grid, a, pl.BlockSpec((tk, tm), lambda i, j, k: (k, i)),
               [b], [pl.BlockSpec((tk, tn), lambda i, j, k: (k, j))],
               [_sds((m, n), out_dtype)], [pl.BlockSpec((tm, tn), lambda i, j, k: (i, j))], (tm, tn))[0]


_INV_SQRT2 = 0.7071067811865476
_INV_SQRT_2PI = 0.3989422804014327


def _gelu(x):
    return 0.5 * x * (1.0 + lax.erf(x * _INV_SQRT2))


def _gelu_grad(x):
    return 0.5 * (1.0 + lax.erf(x * _INV_SQRT2)) + x * jnp.exp(-0.5 * x * x) * _INV_SQRT_2PI


def _sigmoid(x):
    return 1.0 / (1.0 + jnp.exp(-x))


def _norm_stats(x):
    mu = jnp.mean(x, axis=-1, keepdims=True)
    xc = x - mu
    var = jnp.mean(xc * xc, axis=-1, keepdims=True)
    rstd = lax.rsqrt(var + LN_EPS)
    return xc * rstd, rstd


def _norm_bwd(dy_g, xh, rstd):
    m1 = jnp.mean(dy_g, axis=-1, keepdims=True)
    m2 = jnp.mean(dy_g * xh, axis=-1, keepdims=True)
    return rstd * (dy_g - m1 - xh * m2)


def _rows8(x):
    r, c = x.shape
    return jnp.sum(x.reshape(r // 8, 8, c), axis=0)


def _ln_fwd(name, xh_prev, g_prev, b_prev, f, g, b):
    t, d = f.shape
    tm = _tile(t, 512)

    def body(xp_ref, gp_ref, bp_ref, f_ref, g_ref, b_ref, xh_ref, y_ref, rstd_ref):
        r = ALPHA * (xp_ref[...] * gp_ref[...] + bp_ref[...]) + f_ref[...]
        xh, rstd = _norm_stats(r)
        xh_ref[...] = xh
        y_ref[...] = (xh * g_ref[...] + b_ref[...]).astype(y_ref.dtype)
        rstd_ref[...] = rstd

    row = pl.BlockSpec((tm, d), lambda i: (i, 0))
    vec = pl.BlockSpec((1, d), lambda i: (0, 0))
    return pl.pallas_call(
        body, name=name, grid=(t // tm,),
        in_specs=[row, vec, vec, row, vec, vec],
        out_specs=[row, row, pl.BlockSpec((tm, 1), lambda i: (i, 0))],
        out_shape=[_sds((t, d), F32), _sds((t, d), ACT), _sds((t, 1), F32)],
        compiler_params=_params(("parallel",), 4 * tm * d * 4, 0, 4 * tm * d * 4),
    )(xh_prev, g_prev, b_prev, f, g, b)


def _ln_bwd(name, dy, xh, rstd, g):
    t, d = dy.shape
    tm = _tile(t, 512)
    n = t // tm

    def body(dy_ref, xh_ref, rstd_ref, g_ref, dr_ref, dra_ref, dg_ref, db_ref, dg_acc, db_acc):
        i = pl.program_id(0)

        @pl.when(i == 0)
        def _():
            dg_acc[...] = jnp.zeros_like(dg_acc)
            db_acc[...] = jnp.zeros_like(db_acc)

        dyv = dy_ref[...]
        xhv = xh_ref[...]
        dr = _norm_bwd(dyv * g_ref[...], xhv, rstd_ref[...])
        dr_ref[...] = dr
        dra_ref[...] = dr.astype(dra_ref.dtype)
        dg_acc[...] += _rows8(dyv * xhv)
        db_acc[...] += _rows8(dyv)

        @pl.when(i == n - 1)
        def _():
            dg_ref[...] = jnp.sum(dg_acc[...], axis=0, keepdims=True)
            db_ref[...] = jnp.sum(db_acc[...], axis=0, keepdims=True)

    row = pl.BlockSpec((tm, d), lambda i: (i, 0))
    vec = pl.BlockSpec((1, d), lambda i: (0, 0))
    return pl.pallas_call(
        body, name=name, grid=(n,),
        in_specs=[row, row, pl.BlockSpec((tm, 1), lambda i: (i, 0)), vec],
        out_specs=[row, row, vec, vec],
        out_shape=[_sds((t, d), F32), _sds((t, d), ACT), _sds((1, d), F32), _sds((1, d), F32)],
        scratch_shapes=[pltpu.VMEM((8, d), F32), pltpu.VMEM((8, d), F32)],
        compiler_params=_params(("arbitrary",), 4 * tm * d * 4, 0, 4 * tm * d * 4),
    )(dy, xh, rstd, g)


def _loss_head(xh, g, b, target):
    t, d = xh.shape
    tm = _tile(t, 512)
    n = t // tm

    def body(xh_ref, g_ref, b_ref, tg_ref, dy_ref, loss_ref, acc):
        i = pl.program_id(0)

        @pl.when(i == 0)
        def _():
            acc[...] = jnp.zeros_like(acc)

        err = xh_ref[...] * g_ref[...] + b_ref[...] - tg_ref[...]
        dy_ref[...] = err * (1.0 / d)
        acc[...] += _rows8(err * err)

        @pl.when(i == n - 1)
        def _():
            s = jnp.sum(jnp.sum(acc[...], axis=0, keepdims=True), axis=1, keepdims=True)
            loss_ref[...] = jnp.broadcast_to(s * (0.5 / d), loss_ref.shape)

    row = pl.BlockSpec((tm, d), lambda i: (i, 0))
    vec = pl.BlockSpec((1, d), lambda i: (0, 0))
    dy, loss = pl.pallas_call(
        body, name="loss_head", grid=(n,),
        in_specs=[row, vec, vec, row],
        out_specs=[row, pl.BlockSpec((8, 128), lambda i: (0, 0))],
        out_shape=[_sds((t, d), F32), _sds((8, 128), F32)],
        scratch_shapes=[pltpu.VMEM((8, d), F32)],
        compiler_params=_params(("arbitrary",), 3 * tm * d * 4, 0, 2 * tm * d * 4),
    )(xh, g, b, target)
    return dy, loss[0, 0]


def _causal_w(w):
    r = lax.broadcasted_iota(jnp.int32, w.shape, 0)
    c = lax.broadcasted_iota(jnp.int32, w.shape, 1)
    return jnp.where(r >= c, w, 0.0)


def _gmlp_fwd(name, h, ln_g, ln_b, w_s, b_s_col):
    t = h.shape[0]
    tt = _tile(t, 256)
    wd = 4 * CHUNK

    def body(u_ref, v_ref, g_ref, b_ref, w_ref, bs_ref, ya_ref):
        for gi in range(4):
            ln = slice(gi * CHUNK, (gi + 1) * CHUNK)
            u = _gelu(u_ref[:, ln].astype(F32))
            v = _gelu(v_ref[:, ln].astype(F32))
            xh, _ = _norm_stats(v)
            vg = (xh * g_ref[:, ln] + b_ref[:, ln]).astype(MM)
            w = _causal_w(w_ref[gi]).astype(MM)
            for c in range(tt // CHUNK):
                rs = slice(c * CHUNK, (c + 1) * CHUNK)
                mixed = jnp.dot(w, vg[rs], preferred_element_type=F32) + bs_ref[gi]
                ya_ref[rs, ln] = (u[rs] * mixed).astype(ya_ref.dtype)

    vec = pl.BlockSpec((1, wd), lambda i: (0, 0))
    return pl.pallas_call(
        body, name=name, grid=(t // tt,),
        in_specs=[pl.BlockSpec((tt, wd), lambda i: (i, 0)), pl.BlockSpec((tt, wd), lambda i: (i, 1)), vec, vec,
                  pl.BlockSpec((4, CHUNK, CHUNK), lambda i: (0, 0, 0)), pl.BlockSpec((4, CHUNK, 1), lambda i: (0, 0, 0))],
        out_specs=pl.BlockSpec((tt, wd), lambda i: (i, 0)),
        out_shape=_sds((t, wd), ACT),
        compiler_params=_params(("parallel",), 3 * tt * wd * 4, 0, 8 * tt * CHUNK * 4),
    )(h, h, ln_g, ln_b, w_s, b_s_col)


def _gmlp_bwd(name, h, dyab, ln_g, ln_b, w_s, b_s_col):
    t = h.shape[0]
    tt = _tile(t, 256)
    n = t // tt
    wd = 4 * CHUNK

    def body(u_ref, v_ref, dy_ref, g_ref, b_ref, w_ref, bs_ref, duv_ref, dg_ref, db_ref, dw_ref, dbs_ref,
             dg_acc, db_acc):
        i = pl.program_id(0)

        @pl.when(i == 0)
        def _():
            dg_acc[...] = jnp.zeros_like(dg_acc)
            db_acc[...] = jnp.zeros_like(db_acc)
            dw_ref[...] = jnp.zeros_like(dw_ref)
            dbs_ref[...] = jnp.zeros_like(dbs_ref)

        for gi in range(4):
            ln = slice(gi * CHUNK, (gi + 1) * CHUNK)
            upre = u_ref[:, ln].astype(F32)
            vpre = v_ref[:, ln].astype(F32)
            u = _gelu(upre)
            v = _gelu(vpre)
            xh, rstd = _norm_stats(v)
            gv = g_ref[:, ln]
            vg = (xh * gv + b_ref[:, ln]).astype(MM)
            w = _causal_w(w_ref[gi]).astype(MM)
            dya = dy_ref[:, ln].astype(F32)
            dmixed = dya * u
            dmm = dmixed.astype(MM)
            dvg_parts, mixed_parts = [], []
            dw = jnp.zeros((CHUNK, CHUNK), F32)
            dbs = jnp.zeros((CHUNK, 1), F32)
            for c in range(tt // CHUNK):
                rs = slice(c * CHUNK, (c + 1) * CHUNK)
                mixed_parts.append(jnp.dot(w, vg[rs], preferred_element_type=F32) + bs_ref[gi])
                dw = dw + lax.dot_general(dmm[rs], vg[rs], _DIMS["nt"], preferred_element_type=F32)
                dbs = dbs + jnp.sum(dmixed[rs], axis=1, keepdims=True)
                dvg_parts.append(lax.dot_general(w, dmm[rs], _DIMS["tn"], preferred_element_type=F32))
            mixed = jnp.concatenate(mixed_parts, axis=0)
            dvg = jnp.concatenate(dvg_parts, axis=0)
            dw_ref[gi] += _causal_w(dw)
            dbs_ref[gi] += dbs
            dg_acc[:, ln] += _rows8(dvg * xh)
            db_acc[:, ln] += _rows8(dvg)
            dv = _norm_bwd(dvg * gv, xh, rstd) * _gelu_grad(vpre)
            du = dya * mixed * _gelu_grad(upre)
            duv_ref[:, ln] = du.astype(duv_ref.dtype)
            duv_ref[:, wd + gi * CHUNK: wd + (gi + 1) * CHUNK] = dv.astype(duv_ref.dtype)

        @pl.when(i == n - 1)
        def _():
            dg_ref[...] = jnp.sum(dg_acc[...], axis=0, keepdims=True)
            db_ref[...] = jnp.sum(db_acc[...], axis=0, keepdims=True)

    vec = pl.BlockSpec((1, wd), lambda i: (0, 0))
    wspec = pl.BlockSpec((4, CHUNK, CHUNK), lambda i: (0, 0, 0))
    bspec = pl.BlockSpec((4, CHUNK, 1), lambda i: (0, 0, 0))
    return pl.pallas_call(
        body, name=name, grid=(n,),
        in_specs=[pl.BlockSpec((tt, wd), lambda i: (i, 0)), pl.BlockSpec((tt, wd), lambda i: (i, 1)),
                  pl.BlockSpec((tt, wd), lambda i: (i, 0)), vec, vec, wspec, bspec],
        out_specs=[pl.BlockSpec((tt, 2 * wd), lambda i: (i, 0)), vec, vec, wspec, bspec],
        out_shape=[_sds((t, 2 * wd), ACT), _sds((1, wd), F32), _sds((1, wd), F32),
                   _sds((4, CHUNK, CHUNK), F32), _sds((4, CHUNK, 1), F32)],
        scratch_shapes=[pltpu.VMEM((8, wd), F32), pltpu.VMEM((8, wd), F32)],
        compiler_params=_params(("arbitrary",), 5 * tt * wd * 4, 0, 16 * tt * CHUNK * 4),
    )(h, h, dyab, ln_g, ln_b, w_s, b_s_col)


_ROWS = 256


def _conv_taps(win, cw, lo):
    acc = jnp.zeros((_ROWS, CHUNK), F32)
    for w in range(CONV_WIDTH):
        s = lo(w)
        acc = acc + cw[w:w + 1, :] * win[s:s + _ROWS, :]
    return acc


def _conv_fwd(name, h, n_ex, cw, cb, gg, gb):
    t = h.shape[0]
    s = t // n_ex
    nt = s // _ROWS

    def body(a_ref, gt_ref, cw_ref, cb_ref, gg_ref, gb_ref, yb_ref, hh):
        hh[0:HALO, :] = jnp.zeros((HALO, CHUNK), F32)
        hh[HALO:HALO + s, :] = a_ref[...].astype(F32) * _sigmoid(gt_ref[...].astype(F32))
        cwv = cw_ref[...]

        def tile(i, carry):
            r0 = pl.multiple_of(i * _ROWS, _ROWS)
            win = hh[pl.ds(r0, _ROWS + HALO), :]
            c = _conv_taps(win, cwv, lambda w: w + HALO - (CONV_WIDTH - 1)) + cb_ref[...]
            xh, _ = _norm_stats(c)
            hg = xh * gg_ref[...] + gb_ref[...]
            yb_ref[pl.ds(r0, _ROWS), :] = (hg * _sigmoid(hg)).astype(yb_ref.dtype)
            return carry

        lax.fori_loop(0, nt, tile, 0)

    vec = pl.BlockSpec((1, CHUNK), lambda g, b: (0, g))
    return pl.pallas_call(
        body, name=name, grid=(4, n_ex),
        in_specs=[pl.BlockSpec((s, CHUNK), lambda g, b: (b, 8 + g)), pl.BlockSpec((s, CHUNK), lambda g, b: (b, 12 + g)),
                  pl.BlockSpec((CONV_WIDTH, CHUNK), lambda g, b: (0, g)), vec, vec, vec],
        out_specs=pl.BlockSpec((s, CHUNK), lambda g, b: (b, g)),
        out_shape=_sds((t, 4 * CHUNK), ACT),
        scratch_shapes=[pltpu.VMEM((s + HALO, CHUNK), F32)],
        compiler_params=_params(("parallel", "parallel"), 3 * s * CHUNK * 4, (s + HALO) * CHUNK * 4, 4 * s * CHUNK * 4),
    )(h, h, cw, cb, gg, gb)


def _conv_bwd(name, h, dyab, n_ex, cw, cb, gg, gb):
    t = h.shape[0]
    s = t // n_ex
    nt = s // _ROWS

    def body(a_ref, gt_ref, dy_ref, cw_ref, cb_ref, gg_ref, gb_ref,
             da_ref, dgt_ref, dcw_ref, dcb_ref, dgg_ref, dgb_ref, hh, dcs, acc):
        b = pl.program_id(1)

        @pl.when(b == 0)
        def _():
            dcw_ref[...] = jnp.zeros_like(dcw_ref)
            dcb_ref[...] = jnp.zeros_like(dcb_ref)
            dgg_ref[...] = jnp.zeros_like(dgg_ref)
            dgb_ref[...] = jnp.zeros_like(dgb_ref)

        hh[0:HALO, :] = jnp.zeros((HALO, CHUNK), F32)
        hh[HALO:HALO + s, :] = a_ref[...].astype(F32) * _sigmoid(gt_ref[...].astype(F32))
        dcs[s:s + HALO, :] = jnp.zeros((HALO, CHUNK), F32)
        acc[...] = jnp.zeros_like(acc)
        cwv = cw_ref[...]
        off = HALO - (CONV_WIDTH - 1)

        def tile1(i, carry):
            r0 = pl.multiple_of(i * _ROWS, _ROWS)
            win = hh[pl.ds(r0, _ROWS + HALO), :]
            c = _conv_taps(win, cwv, lambda w: w + off) + cb_ref[...]
            xh, rstd = _norm_stats(c)
            hg = xh * gg_ref[...] + gb_ref[...]
            sg = _sigmoid(hg)
            dhg = dy_ref[pl.ds(r0, _ROWS), :].astype(F32) * (sg * (1.0 + hg * (1.0 - sg)))
            acc[32:40, :] += _rows8(dhg * xh)
            acc[40:48, :] += _rows8(dhg)
            dc = _norm_bwd(dhg * gg_ref[...], xh, rstd)
            dcs[pl.ds(r0, _ROWS), :] = dc
            acc[48:56, :] += _rows8(dc)
            for w in range(CONV_WIDTH):
                acc[w:w + 1, :] += jnp.sum(dc * win[w + off:w + off + _ROWS, :], axis=0, keepdims=True)
            return carry

        lax.fori_loop(0, nt, tile1, 0)

        def tile2(i, carry):
            r0 = pl.multiple_of(i * _ROWS, _ROWS)
            win = dcs[pl.ds(r0, _ROWS + HALO), :]
            dhh = _conv_taps(win, cwv, lambda w: CONV_WIDTH - 1 - w)
            av = a_ref[pl.ds(r0, _ROWS), :].astype(F32)
            sg = _sigmoid(gt_ref[pl.ds(r0, _ROWS), :].astype(F32))
            da_ref[pl.ds(r0, _ROWS), :] = (dhh * sg).astype(da_ref.dtype)
            dgt_ref[pl.ds(r0, _ROWS), :] = (dhh * av * sg * (1.0 - sg)).astype(dgt_ref.dtype)
            return carry

        lax.fori_loop(0, nt, tile2, 0)
        dcw_ref[...] += acc[0:CONV_WIDTH, :]
        dgg_ref[...] += jnp.sum(acc[32:40, :], axis=0, keepdims=True)
        dgb_ref[...] += jnp.sum(acc[40:48, :], axis=0, keepdims=True)
        dcb_ref[...] += jnp.sum(acc[48:56, :], axis=0, keepdims=True)

    vec = pl.BlockSpec((1, CHUNK), lambda g, b: (0, g))
    tap = pl.BlockSpec((CONV_WIDTH, CHUNK), lambda g, b: (0, g))
    seq = pl.BlockSpec((s, CHUNK), lambda g, b: (b, g))
    return pl.pallas_call(
        body, name=name, grid=(4, n_ex),
        in_specs=[pl.BlockSpec((s, CHUNK), lambda g, b: (b, 8 + g)), pl.BlockSpec((s, CHUNK), lambda g, b: (b, 12 + g)),
                  pl.BlockSpec((s, CHUNK), lambda g, b: (b, 4 + g)), tap, vec, vec, vec],
        out_specs=[seq, seq, tap, vec, vec, vec],
        out_shape=[_sds((t, 4 * CHUNK), ACT), _sds((t, 4 * CHUNK), ACT), _sds((CONV_WIDTH, 4 * CHUNK), F32),
                   _sds((1, 4 * CHUNK), F32), _sds((1, 4 * CHUNK), F32), _sds((1, 4 * CHUNK), F32)],
        scratch_shapes=[pltpu.VMEM((s + HALO, CHUNK), F32), pltpu.VMEM((s + HALO, CHUNK), F32),
                        pltpu.VMEM((56, CHUNK), F32)],
        compiler_params=_params(("parallel", "arbitrary"), 5 * s * CHUNK * 4, 2 * (s + HALO) * CHUNK * 4, 4 * s * CHUNK * 4),
    )(h, h, dyab, cw, cb, gg, gb)


_TQ = 256
_SB_DEAD = -110.0


def _tri(kind):
    r = lax.broadcasted_iota(jnp.int32, (_TQ, _TQ), 0)
    c = lax.broadcasted_iota(jnp.int32, (_TQ, _TQ), 1)
    m = {"gt": r > c, "le": r <= c, "lt": r < c}[kind]
    return jnp.where(m, 1.0, 0.0).astype(jnp.bfloat16)


def _split_dot(x, tri2):
    hi = x.astype(jnp.bfloat16)
    lo = (x - hi.astype(F32)).astype(jnp.bfloat16)
    return jnp.dot(jnp.concatenate([hi, lo], axis=1), tri2, preferred_element_type=F32)


def _neg_abs(x):
    bits = lax.bitcast_convert_type(x, jnp.uint32) | jnp.uint32(0x80000000)
    return lax.bitcast_convert_type(bits, F32)


def _log_not_beta(nz):
    return jnp.minimum(nz, 0.0) - jnp.log(1.0 + jnp.exp(_neg_abs(nz)))


def _sb_fwd(name, qkv, n_ex):
    t = qkv.shape[0]
    d = qkv.shape[1] // 3
    npair = d // CHUNK
    s = t // n_ex
    nq = s // _TQ
    neg_a = -(C_HEAD_DIM ** -0.5)

    def body(q_ref, k_ref, v_ref, o_ref, lt_ref, o_acc, c_acc):
        i = pl.program_id(2)
        first = lax.broadcasted_iota(jnp.int32, (_TQ, CHUNK), 1) < C_HEAD_DIM
        q2 = q_ref[...]
        zero = jnp.zeros_like(q2)
        qh = [jnp.where(first, q2, zero), jnp.where(first, zero, q2)]
        tri2 = jnp.concatenate([_tri("gt")] * 2, axis=0)
        o_acc[...] = jnp.zeros_like(o_acc)
        c_acc[...] = jnp.zeros_like(c_acc)

        def tiles(js, mask):
            work = [(a, hd) for a in range(len(js)) for hd in range(2)]
            rows = [pl.ds(pl.multiple_of(j * _TQ, _TQ), _TQ) for j in js]
            kts = [k_ref[r, :] for r in rows]
            vts = [v_ref[r, :] for r in rows]
            nzs = {w: lax.dot_general(qh[w[1]], kts[w[0]], _DIMS["nt"], preferred_element_type=F32) * neg_a for w in work}
            lns = {w: _log_not_beta(nzs[w]) for w in work}
            if mask is not None:
                lns = {w: jnp.where(mask, lns[w], 0.0) for w in work}
            locs = {w: _split_dot(lns[w], tri2) for w in work}
            laters = {}
            for hd in range(2):
                carry = c_acc[hd]
                for a in range(len(js)):
                    laters[a, hd] = carry + locs[a, hd]
                    carry = laters[a, hd][:, 0:1] + lns[a, hd][:, 0:1]
                c_acc[hd] = carry
            atts = {w: jnp.exp(lns[w] - nzs[w] + laters[w]) for w in work}
            if mask is not None:
                atts = {w: jnp.where(mask, atts[w], 0.0) for w in work}
            for hd in range(2):
                acc = o_acc[hd]
                for a in range(len(js)):
                    acc = acc + jnp.dot(atts[a, hd].astype(MM), vts[a], preferred_element_type=F32)
                o_acc[hd] = acc

        tiles([i], lax.broadcasted_iota(jnp.int32, (_TQ, _TQ), 1) < lax.broadcasted_iota(jnp.int32, (_TQ, _TQ), 0))

        def alive():
            return jnp.max(jnp.maximum(c_acc[0], c_acc[1])) >= _SB_DEAD

        def cond(st):
            return jnp.logical_and(st[0] >= 0, st[1])

        def step(st):
            tiles([st[0]], None)
            return st[0] - 1, alive()

        j_last, _ = lax.while_loop(cond, step, (i - 1, alive()))
        o_ref[...] = jnp.where(first, o_acc[0], o_acc[1])
        lt_ref[:, 0:1] = c_acc[0]
        lt_ref[:, 1:2] = c_acc[1]
        lt_ref[:, 2:3] = jnp.full((_TQ, 1), j_last + 1, jnp.int32).astype(F32)

    return pl.pallas_call(
        body, name=name, grid=(n_ex, npair, nq),
        in_specs=[pl.BlockSpec((_TQ, CHUNK), lambda b, p, i: (b * nq + i, p)),
                  pl.BlockSpec((s, CHUNK), lambda b, p, i: (b, npair + p)),
                  pl.BlockSpec((s, CHUNK), lambda b, p, i: (b, 2 * npair + p))],
        out_specs=[pl.BlockSpec((_TQ, CHUNK), lambda b, p, i: (b * nq + i, p)),
                   pl.BlockSpec((None, _TQ, 3), lambda b, p, i: (p, b * nq + i, 0))],
        out_shape=[_sds((t, d), F32), _sds((npair, t, 3), F32)],
        scratch_shapes=[pltpu.VMEM((2, _TQ, CHUNK), F32), pltpu.VMEM((2, _TQ, 1), F32)],
        compiler_params=_params(("parallel", "parallel", "arbitrary"), 2 * s * CHUNK * 2 + 4 * _TQ * CHUNK * 4,
                                4 * _TQ * CHUNK * 4, 24 * _TQ * _TQ * 4),
    )(qkv, qkv, qkv)


def _sb_bwd(name, qkv, do, ltot, n_ex):
    t = qkv.shape[0]
    d = qkv.shape[1] // 3
    npair = d // CHUNK
    s = t // n_ex
    nq = s // _TQ
    scale = C_HEAD_DIM ** -0.5
    neg_a = -scale

    def body(q_ref, k_ref, v_ref, do_ref, lt_ref, dq_ref, dk_ref, dv_ref, dq_acc, cp_acc, cg_acc):
        i = pl.program_id(2)

        @pl.when(i == 0)
        def _():
            dk_ref[...] = jnp.zeros_like(dk_ref)
            dv_ref[...] = jnp.zeros_like(dv_ref)

        first = lax.broadcasted_iota(jnp.int32, (_TQ, CHUNK), 1) < C_HEAD_DIM
        q2 = q_ref[...]
        do2 = do_ref[...]
        qs = (q2 * scale).astype(q2.dtype)
        zero = jnp.zeros_like(q2)
        qh = [jnp.where(first, q2, zero), jnp.where(first, zero, q2)]
        doh = [jnp.where(first, do2, zero), jnp.where(first, zero, do2)]
        lt = [lt_ref[:, 0:1], lt_ref[:, 1:2]]
        tri2_le = jnp.concatenate([_tri("le")] * 2, axis=0)
        tri_lt = _tri("lt").astype(MM)
        dq_acc[...] = jnp.zeros_like(dq_acc)
        cp_acc[...] = jnp.zeros_like(cp_acc)
        cg_acc[...] = jnp.zeros_like(cg_acc)

        def tiles(js, mask):
            na = len(js)
            work = [(a, hd) for a in range(na) for hd in range(2)]
            last = slice(_TQ - 1, _TQ)
            rows = [pl.ds(pl.multiple_of(j * _TQ, _TQ), _TQ) for j in js]
            kts = [k_ref[r, :] for r in rows]
            vts = [v_ref[r, :] for r in rows]
            ksc = [(kt * scale).astype(kt.dtype) for kt in kts]
            nzs = {w: lax.dot_general(qh[w[1]], kts[w[0]], _DIMS["nt"], preferred_element_type=F32) * neg_a for w in work}
            datts = {w: lax.dot_general(doh[w[1]], vts[w[0]], _DIMS["nt"], preferred_element_type=F32) for w in work}
            lns = {w: _log_not_beta(nzs[w]) for w in work}
            if mask is not None:
                lns = {w: jnp.where(mask, lns[w], 0.0) for w in work}
            pins = {w: _split_dot(lns[w], tri2_le) for w in work}
            lss = {w: lns[w] - nzs[w] for w in work}
            atts = {}
            for hd in range(2):
                cp = cp_acc[hd]
                for a in range(na):
                    atts[a, hd] = jnp.exp(lss[a, hd] + ((lt[hd] - cp) - pins[a, hd]))
                    cp = cp + pins[a, hd][:, last]
                cp_acc[hd] = cp
            if mask is not None:
                atts = {w: jnp.where(mask, atts[w], 0.0) for w in work}
            gs = {w: datts[w] * atts[w] for w in work}
            locg = {w: jnp.dot(gs[w].astype(MM), tri_lt, preferred_element_type=F32) for w in work}
            dzs = {}
            for hd in range(2):
                carry = cg_acc[hd]
                for a in range(na):
                    big = carry + locg[a, hd]
                    dzs[a, hd] = gs[a, hd] - (gs[a, hd] + big) * jnp.exp(lss[a, hd])
                    carry = big[:, last] + gs[a, hd][:, last]
                cg_acc[hd] = carry
            if mask is not None:
                dzs = {w: jnp.where(mask, dzs[w], 0.0) for w in work}
            dzs = {w: dzs[w].astype(MM) for w in work}
            attm = {w: atts[w].astype(MM) for w in work}
            for hd in range(2):
                acc = dq_acc[hd]
                for a in range(na):
                    acc = acc + jnp.dot(dzs[a, hd], ksc[a], preferred_element_type=F32)
                dq_acc[hd] = acc
            for a in range(na):
                dk0, dk1 = [lax.dot_general(dzs[a, hd], qs, _DIMS["tn"], preferred_element_type=F32) for hd in range(2)]
                dv0, dv1 = [lax.dot_general(attm[a, hd], do2, _DIMS["tn"], preferred_element_type=F32) for hd in range(2)]
                dk_ref[rows[a], :] += jnp.where(first, dk0, dk1)
                dv_ref[rows[a], :] += jnp.where(first, dv0, dv1)

        def single(j, carry):
            tiles([j], None)
            return carry

        j_first = jnp.clip(jnp.max(lt_ref[:, 2:3]).astype(jnp.int32), 0, i)
        lax.fori_loop(j_first, i, single, 0)
        tiles([i], lax.broadcasted_iota(jnp.int32, (_TQ, _TQ), 1) < lax.broadcasted_iota(jnp.int32, (_TQ, _TQ), 0))
        dq_ref[...] = jnp.where(first, dq_acc[0], dq_acc[1]).astype(dq_ref.dtype)

    qspec = pl.BlockSpec((_TQ, CHUNK), lambda b, p, i: (b * nq + i, p))
    kv_out = pl.BlockSpec((s, CHUNK), lambda b, p, i: (b, p))
    return pl.pallas_call(
        body, name=name, grid=(n_ex, npair, nq),
        in_specs=[qspec, pl.BlockSpec((s, CHUNK), lambda b, p, i: (b, npair + p)),
                  pl.BlockSpec((s, CHUNK), lambda b, p, i: (b, 2 * npair + p)), qspec,
                  pl.BlockSpec((None, _TQ, 3), lambda b, p, i: (p, b * nq + i, 0))],
        out_specs=[qspec, kv_out, kv_out],
        out_shape=[_sds((t, d), ACT), _sds((t, d), F32), _sds((t, d), F32)],
        scratch_shapes=[pltpu.VMEM((2, _TQ, CHUNK), F32), pltpu.VMEM((2, _TQ, 1), F32), pltpu.VMEM((2, _TQ, 1), F32)],
        compiler_params=_params(("parallel", "parallel", "arbitrary"), 2 * s * CHUNK * 2 + 2 * s * CHUNK * 4,
                                4 * _TQ * CHUNK * 4, 32 * _TQ * _TQ * 4),
    )(qkv, qkv, qkv, do, ltot)


def _xattn_fwd(name, q, kk, vv, n_ex):
    t, d = q.shape
    m = kk.shape[0] // n_ex
    s = t // n_ex
    tq = _tile(s, 512)
    nq = s // tq
    hd_dim = d // MEM_HEADS
    scale = hd_dim ** -0.5

    def body(q_ref, k_ref, v_ref, o_ref):
        for hd in range(MEM_HEADS):
            ln = slice(hd * hd_dim, (hd + 1) * hd_dim)
            sc = lax.dot_general(q_ref[:, ln], k_ref[:, ln], _DIMS["nt"], preferred_element_type=F32) * scale
            p = jnp.exp(sc - jnp.max(sc, axis=-1, keepdims=True))
            p = p / jnp.sum(p, axis=-1, keepdims=True)
            o_ref[:, ln] = jnp.dot(p.astype(MM), v_ref[:, ln], preferred_element_type=F32).astype(o_ref.dtype)

    qspec = pl.BlockSpec((tq, d), lambda b, i: (b * nq + i, 0))
    kspec = pl.BlockSpec((m, d), lambda b, i: (b, 0))
    return pl.pallas_call(
        body, name=name, grid=(n_ex, nq), in_specs=[qspec, kspec, kspec], out_specs=qspec,
        out_shape=_sds((t, d), ACT),
        compiler_params=_params(("parallel", "parallel"), 2 * tq * d * 2 + 2 * m * d * 2, 0, 6 * tq * m * 4),
    )(q, kk, vv)


def _xattn_bwd(name, q, kk, vv, do, n_ex):
    t, d = q.shape
    m = kk.shape[0] // n_ex
    s = t // n_ex
    tq = _tile(s, 512)
    nq = s // tq
    hd_dim = d // MEM_HEADS
    scale = hd_dim ** -0.5

    def body(q_ref, k_ref, v_ref, do_ref, dq_ref, dk_ref, dv_ref):
        i = pl.program_id(1)

        @pl.when(i == 0)
        def _():
            dk_ref[...] = jnp.zeros_like(dk_ref)
            dv_ref[...] = jnp.zeros_like(dv_ref)

        for hd in range(MEM_HEADS):
            ln = slice(hd * hd_dim, (hd + 1) * hd_dim)
            qv, kv, vv_, dov = q_ref[:, ln], k_ref[:, ln], v_ref[:, ln], do_ref[:, ln]
            sc = lax.dot_general(qv, kv, _DIMS["nt"], preferred_element_type=F32) * scale
            p = jnp.exp(sc - jnp.max(sc, axis=-1, keepdims=True))
            p = p / jnp.sum(p, axis=-1, keepdims=True)
            dp = lax.dot_general(dov, vv_, _DIMS["nt"], preferred_element_type=F32)
            ds = (p * (dp - jnp.sum(p * dp, axis=-1, keepdims=True)) * scale).astype(MM)
            dq_ref[:, ln] = jnp.dot(ds, kv, preferred_element_type=F32).astype(dq_ref.dtype)
            dk_ref[:, ln] += lax.dot_general(ds, qv, _DIMS["tn"], preferred_element_type=F32)
            dv_ref[:, ln] += lax.dot_general(p.astype(MM), dov, _DIMS["tn"], preferred_element_type=F32)

    qspec = pl.BlockSpec((tq, d), lambda b, i: (b * nq + i, 0))
    kspec = pl.BlockSpec((m, d), lambda b, i: (b, 0))
    return pl.pallas_call(
        body, name=name, grid=(n_ex, nq), in_specs=[qspec, kspec, kspec, qspec], out_specs=[qspec, kspec, kspec],
        out_shape=[_sds((t, d), ACT), _sds((n_ex * m, d), F32), _sds((n_ex * m, d), F32)],
        compiler_params=_params(("parallel", "arbitrary"), 3 * tq * d * 2 + 2 * m * d * 2 + 2 * m * d * 4, 0,
                                8 * tq * m * 4),
    )(q, kk, vv, do)


def _ffn_up(name, y, w1, w3):
    t, d = y.shape
    f = w1.shape[-1]
    tm = _tile(t, 512)
    wspec = pl.BlockSpec((None, d, f), lambda i, j, k: (j, 0, 0))
    hspec = pl.BlockSpec((None, tm, f), lambda i, j, k: (j, i, 0))

    def epi(vals, _):
        h1, h3 = vals
        return [h1, h3, h1 * _sigmoid(h1) * h3]

    return _mm(name, "nn", (t // tm, N_CHIPS, 1), y, pl.BlockSpec((tm, d), lambda i, j, k: (i, 0)),
               [w1, w3], [wspec, wspec], [_sds((N_CHIPS, t, f), ACT)] * 3, [hspec] * 3, (tm, f), epilogue=epi)


def _ffn_down(name, g, w2):
    _, t, f = g.shape
    d = w2.shape[-1]
    tm, tn = _tile(t, 1024), _tile(d, 512)
    return _mm(name, "nn", (t // tm, d // tn, N_CHIPS), g, pl.BlockSpec((None, tm, f), lambda i, j, k: (k, i, 0)),
               [w2], [pl.BlockSpec((None, f, tn), lambda i, j, k: (k, 0, j))],
               [_sds((t, d), F32)], [pl.BlockSpec((tm, tn), lambda i, j, k: (i, j))], (tm, tn))[0]


def _ffn_down_bwd(name, dr, w2, h1, h3):
    t, d = dr.shape
    f = w2.shape[-2]
    tm = _tile(t, 512)
    hspec = pl.BlockSpec((None, tm, f), lambda i, j, k: (j, i, 0))

    def epi(vals, ex):
        dg, = vals
        h1v, h3v = ex
        sg = _sigmoid(h1v)
        return [dg * h3v * (sg * (1.0 + h1v * (1.0 - sg))), dg * h1v * sg]

    return _mm(name, "nt", (t // tm, N_CHIPS, 1), dr, pl.BlockSpec((tm, d), lambda i, j, k: (i, 0)),
               [w2], [pl.BlockSpec((None, f, d), lambda i, j, k: (j, 0, 0))],
               [_sds((N_CHIPS, t, f), ACT)] * 2, [hspec] * 2, (tm, f),
               extras=[h1, h3], extra_specs=[hspec, hspec], epilogue=epi)


def _ffn_up_bwd(name, dh, w, extras, epilogue):
    _, t, f = dh.shape
    d = w.shape[-2]
    tm, tn = _tile(t, 1024), _tile(d, 512)
    ospec = pl.BlockSpec((tm, tn), lambda i, j, k: (i, j))
    return _mm(name, "nt", (t // tm, d // tn, N_CHIPS), dh, pl.BlockSpec((None, tm, f), lambda i, j, k: (k, i, 0)),
               [w], [pl.BlockSpec((None, tn, f), lambda i, j, k: (k, j, 0))],
               [_sds((t, d), F32)], [ospec], (tm, tn),
               extras=extras, extra_specs=[ospec] * len(extras), epilogue=epilogue)[0]


def _ffn_wgrad_up(name, y, dh1, dh3):
    t, d = y.shape
    f = dh1.shape[-1]
    tm, tk = _tile(d, 512), _tile(t, 1024)
    hspec = pl.BlockSpec((None, tk, f), lambda i, j, k: (j, k, 0))
    ospec = pl.BlockSpec((None, tm, f), lambda i, j, k: (j, i, 0))
    return _mm(name, "tn", (d // tm, N_CHIPS, t // tk), y, pl.BlockSpec((tk, tm), lambda i, j, k: (k, i)),
               [dh1, dh3], [hspec, hspec], [_sds((N_CHIPS, d, f), MM)] * 2, [ospec, ospec], (tm, f))


def _ffn_wgrad_down(name, g, dr):
    _, t, f = g.shape
    d = dr.shape[1]
    tn, tk = _tile(d, 512), _tile(t, 1024)
    return _mm(name, "tn", (N_CHIPS, d // tn, t // tk), g, pl.BlockSpec((None, tk, f), lambda i, j, k: (i, k, 0)),
               [dr], [pl.BlockSpec((tk, tn), lambda i, j, k: (k, j))],
               [_sds((N_CHIPS, f, d), MM)], [pl.BlockSpec((None, f, tn), lambda i, j, k: (i, 0, j))], (f, tn))[0]


def _proj_cols(name, y, w, out_dtype):
    t, kdim = y.shape
    wd = w.shape[-1]
    tn = _tile(wd, 512)
    per = wd // tn
    tm = _tile(t, 1024)
    return _mm(name, "nn", (t // tm, N_CHIPS * per, 1), y, pl.BlockSpec((tm, kdim), lambda i, j, k: (i, 0)),
               [w], [pl.BlockSpec((None, kdim, tn), lambda i, j, k: (j // per, 0, j % per))],
               [_sds((t, N_CHIPS * wd), out_dtype)], [pl.BlockSpec((tm, tn), lambda i, j, k: (i, j))], (tm, tn))[0]


def _proj_cols_bwd(name, dh, w, extras, epilogue):
    t = dh.shape[0]
    kdim, wd = w.shape[-2], w.shape[-1]
    tm, tn = _tile(t, 1024), _tile(kdim, 512)
    ospec = pl.BlockSpec((tm, tn), lambda i, j, k: (i, j))
    return _mm(name, "nt", (t // tm, kdim // tn, N_CHIPS), dh, pl.BlockSpec((tm, wd), lambda i, j, k: (i, k)),
               [w], [pl.BlockSpec((None, tn, wd), lambda i, j, k: (k, j, 0))],
               [_sds((t, kdim), F32)], [ospec], (tm, tn),
               extras=extras, extra_specs=[ospec] * len(extras), epilogue=epilogue)[0]


def _proj_cols_wgrad(name, y, dh):
    t, kdim = y.shape
    wd = dh.shape[1] // N_CHIPS
    tm, tk = _tile(kdim, 512), _tile(t, 1024)
    return _mm(name, "tn", (kdim // tm, N_CHIPS, t // tk), y, pl.BlockSpec((tk, tm), lambda i, j, k: (k, i)),
               [dh], [pl.BlockSpec((tk, wd), lambda i, j, k: (k, j))],
               [_sds((N_CHIPS, kdim, wd), MM)], [pl.BlockSpec((None, tm, wd), lambda i, j, k: (j, i, 0))], (tm, wd))[0]


def _coords():
    return lax.axis_index("x"), lax.axis_index("y"), lax.axis_index("c")


def _chip_peers(x, y):
    return [(1 - x, y), (x, 1 - y), (1 - x, 1 - y)]


_ANY = pl.BlockSpec(memory_space=pl.ANY)


def _chip_copies(kind, srcs, lands, send_sems, recv_sems):
    x, y, c = _coords()
    me = 2 * x + y
    copies = []
    for t, (src, land) in enumerate(zip(srcs, lands)):
        for j, chip in enumerate(_chip_peers(x, y)):
            if kind == "gather":
                s_ref, d_ref = src, land.at[me]
            else:
                s_ref, d_ref = src.at[2 * chip[0] + chip[1]], land.at[j]
            copies.append(pltpu.make_async_remote_copy(src_ref=s_ref, dst_ref=d_ref, send_sem=send_sems.at[3 * t + j],
                                                       recv_sem=recv_sems.at[3 * t + j], device_id=(*chip, c), device_id_type=MESH))
    return copies


def _exchange(name, kind, srcs, lands):
    n = len(srcs)

    def body(*refs):
        ins, lz = refs[:n], refs[2 * n:3 * n]
        send_sems, recv_sems = refs[3 * n:]
        copies = _chip_copies(kind, ins, lz, send_sems, recv_sems)
        for cp in copies:
            cp.start()
        for cp in copies:
            cp.wait()

    res = pl.pallas_call(
        body, name=name, in_specs=[_ANY] * (2 * n), out_specs=[_ANY] * n,
        out_shape=[_sds(a.shape, a.dtype) for a in lands], input_output_aliases={n + i: i for i in range(n)},
        scratch_shapes=[pltpu.SemaphoreType.DMA((3 * n,)), pltpu.SemaphoreType.DMA((3 * n,))],
    )(*srcs, *lands)
    return list(res)


def _swap_sibling(arrs):
    n = len(arrs)

    def body(*refs):
        ins, outs = refs[:n], refs[n:2 * n]
        send_sems, recv_sems = refs[2 * n:]
        x, y, c = _coords()
        copies = []
        for t in range(n):
            cp = pltpu.make_async_remote_copy(src_ref=ins[t], dst_ref=outs[t], send_sem=send_sems.at[t],
                                              recv_sem=recv_sems.at[t], device_id=(x, y, 1 - c), device_id_type=MESH)
            cp.start()
            copies.append(cp)
        for cp in copies:
            cp.wait()

    return pl.pallas_call(
        body, name="swap_sibling", in_specs=[_ANY] * n, out_specs=[_ANY] * n,
        out_shape=[_sds(a.shape, a.dtype) for a in arrs],
        scratch_shapes=[pltpu.SemaphoreType.DMA((n,)), pltpu.SemaphoreType.DMA((n,))],
    )(*arrs)


def _gather_all(part):
    def body(in_ref, out_ref, send_sems, recv_sems, loc_sem):
        x, y, c = _coords()
        dst = out_ref.at[4 * x + 2 * y + c]
        copies = [pltpu.make_async_copy(in_ref, dst, loc_sem)]
        for r in range(1, N_DEV):
            fx, fy, fc = (r >> 2) & 1, (r >> 1) & 1, r & 1
            peer = (x ^ fx, y ^ fy, c ^ fc)
            copies.append(pltpu.make_async_remote_copy(src_ref=in_ref, dst_ref=dst, send_sem=send_sems.at[r - 1],
                                                       recv_sem=recv_sems.at[r - 1], device_id=peer, device_id_type=MESH))
        for cp in copies:
            cp.start()
        for cp in copies:
            cp.wait()

    return pl.pallas_call(
        body, name="gather_small_grads", in_specs=[_ANY], out_specs=_ANY,
        out_shape=_sds((N_DEV, *part.shape), part.dtype),
        scratch_shapes=[pltpu.SemaphoreType.DMA((N_DEV - 1,)), pltpu.SemaphoreType.DMA((N_DEV - 1,)), pltpu.SemaphoreType.DMA],
    )(part)


def _sum_chips(grad, recv, me):
    _, rr, cc = recv.shape
    tr = _tile(rr, 512)

    def body(me_ref, g_ref, r0_ref, r1_ref, r2_ref, o_ref):
        o_ref[...] = ((g_ref[...].astype(F32) + r0_ref[...].astype(F32)) + r1_ref[...].astype(F32)) + r2_ref[...].astype(F32)

    gspec = pl.BlockSpec((None, tr, cc), lambda r, m: (m[0], r, 0))
    rspecs = [pl.BlockSpec((None, tr, cc), functools.partial(lambda r, m, j: (j, r, 0), j=j)) for j in range(3)]
    return pl.pallas_call(
        body, name="sum_chip_grads",
        grid_spec=pltpu.PrefetchScalarGridSpec(
            num_scalar_prefetch=1, grid=(rr // tr,), in_specs=[gspec, *rspecs],
            out_specs=pl.BlockSpec((tr, cc), lambda r, m: (r, 0))),
        out_shape=_sds((rr, cc), F32),
        compiler_params=_params(("parallel",), 4 * tr * cc * 2 + tr * cc * 4, 0, 2 * tr * cc * 4),
    )(me, grad, recv, recv, recv)


def _adamw_math(w, g, m, v):
    m = ADAM_B1 * m + (1.0 - ADAM_B1) * g
    v = ADAM_B2 * v + (1.0 - ADAM_B2) * (g * g)
    m_hat = m / (1.0 - ADAM_B1 ** ADAM_STEP)
    v_hat = v / (1.0 - ADAM_B2 ** ADAM_STEP)
    delta = -ADAM_LR * (m_hat / (jnp.sqrt(v_hat) + ADAM_EPS) + ADAM_WD * w)
    return delta, m, v


def _adamw(name, parts, w, m, v):
    ll, rr, cc = w.shape
    tr = _tile(rr, 256)
    npart = len(parts)

    def body(*refs):
        p_refs = refs[:npart]
        w_ref, m_ref, v_ref, g_ref, d_ref, nm_ref, nv_ref = refs[npart:]
        g = p_refs[0][...]
        for p in p_refs[1:]:
            g = g + p[...]
        d, nm, nv = _adamw_math(w_ref[...], g, m_ref[...], v_ref[...])
        g_ref[...] = g
        d_ref[...] = d
        nm_ref[...] = nm
        nv_ref[...] = nv

    spec = pl.BlockSpec((None, tr, cc), lambda l, r: (l, r, 0))
    out = _sds((ll, rr, cc), F32)
    return pl.pallas_call(
        body, name=name, grid=(ll, rr // tr), in_specs=[spec] * (npart + 3), out_specs=[spec] * 4, out_shape=[out] * 4,
        compiler_params=_params(("parallel", "parallel"), (npart + 7) * tr * cc * 4, 0, 4 * tr * cc * 4),
    )(*parts, w, m, v)


def _sum_devices(allparts):
    _, rr, cc = allparts.shape

    def body(p_ref, o_ref):
        s = p_ref[0]
        for k in range(1, N_DEV):
            s = s + p_ref[k]
        o_ref[...] = s

    return pl.pallas_call(
        body, name="sum_small_grads", grid=(1,), in_specs=[pl.BlockSpec((N_DEV, rr, cc), lambda i: (0, 0, 0))],
        out_specs=pl.BlockSpec((rr, cc), lambda i: (0, 0)), out_shape=_sds((rr, cc), F32),
        compiler_params=_params(("arbitrary",), 9 * rr * cc * 4),
    )(allparts)


def _pack(arrs):
    flat = jnp.concatenate([a.reshape(-1).astype(F32) for a in arrs])
    n = flat.shape[0]
    total = -(-n // 1024) * 1024
    return jnp.pad(flat, (0, total - n)).reshape(total // 128, 128)


def _unpack(block, shapes):
    flat = block.reshape(-1)
    out, off = [], 0
    for sh in shapes:
        n = math.prod(sh)
        out.append(flat[off:off + n].reshape(sh))
        off += n
    return out


_BIG = ["w_in_ab", "w_out_ab", "w_qkv_c", "w_out_c", "mem_wq", "mem_wk", "mem_wv", "mem_wo", "ffn_w1", "ffn_w3", "ffn_w2"]
_ROW_SHARDED = ("w_out_ab", "w_out_c", "mem_wq", "mem_wk", "mem_wv", "mem_wo")
_SMALL_REPL = ["gmlp_ln_g", "gmlp_ln_b", "gmlp_w_s", "gmlp_b_s", "conv_b", "conv_gn_g", "conv_gn_b"]
_SMALL_SHARD = ["conv_w", "ln_g", "ln_b"]
_NAMES = ["w_in_ab", "gmlp_ln_g", "gmlp_ln_b", "gmlp_w_s", "gmlp_b_s", "conv_w", "conv_b", "conv_gn_g", "conv_gn_b",
          "w_out_ab", "w_qkv_c", "w_out_c", "mem_wq", "mem_wk", "mem_wv", "mem_wo", "ffn_w1", "ffn_w3", "ffn_w2",
          "ln_g", "ln_b"]


def _layer_weights(l):
    mixer = ["w_in_ab", "w_out_ab"] if l % 2 == 0 else ["w_qkv_c", "w_out_c"]
    return [(n, l // 2) for n in mixer] + [(n, l) for n in _BIG if n.startswith(("mem_", "ffn_"))]


def _natural(w):
    return w.reshape(-1, w.shape[-1])


def _local_step(x, mem, target, layer_w, small):
    n_ex, s, d = x.shape
    t = n_ex * s
    x2 = x.reshape(t, d)
    mem_a = mem.reshape(-1, d).astype(ACT)
    tgt = target.reshape(t, d)
    one = jnp.ones((1, d), F32)
    zero = jnp.zeros((1, d), F32)
    ln_g, ln_b = small["ln_g"], small["ln_b"]

    def vec(a):
        return a.reshape(1, -1)

    saved = []
    xh, gp, bp = x2, one, zero
    y_act = x2.astype(ACT)
    for l in range(DEPTH):
        wts = layer_w[l]
        sv = {"y0": y_act, "w": wts}
        if l % 2 == 0:
            e = l // 2
            h = _proj_cols(f"in_ab_{l}", y_act, wts["w_in_ab"], ACT)
            gl = (vec(small["gmlp_ln_g"][e]), vec(small["gmlp_ln_b"][e]), small["gmlp_w_s"][e],
                  small["gmlp_b_s"][e].reshape(4, CHUNK, 1))
            cl = (small["conv_w"][e], vec(small["conv_b"][e]), vec(small["conv_gn_g"][e]), vec(small["conv_gn_b"][e]))
            ya = _gmlp_fwd(f"gmlp_fwd_{l}", h, *gl)
            yb = _conv_fwd(f"conv_fwd_{l}", h, n_ex, *cl)
            yab = jnp.concatenate([ya, yb], axis=1)
            mix = _dense(f"out_ab_{l}", yab, _natural(wts["w_out_ab"]), F32)
            sv.update(h=h, yab=yab, gl=gl, cl=cl)
        else:
            qkv = _proj_cols(f"qkv_{l}", y_act, wts["w_qkv_c"], ACT)
            att, ltot = _sb_fwd(f"sb_fwd_{l}", qkv, n_ex)
            att_a = att.astype(ACT)
            mix = _dense(f"out_c_{l}", att_a, _natural(wts["w_out_c"]), F32)
            sv.update(qkv=qkv, att=att_a, ltot=ltot)
        g1, b1 = vec(ln_g[l, 0]), vec(ln_b[l, 0])
        xh1, y1, rstd1 = _ln_fwd(f"ln1_fwd_{l}", xh, gp, bp, mix, g1, b1)
        q = _dense(f"mem_q_{l}", y1, _natural(wts["mem_wq"]), ACT)
        kk = _dense(f"mem_k_{l}", mem_a, _natural(wts["mem_wk"]), ACT)
        vv = _dense(f"mem_v_{l}", mem_a, _natural(wts["mem_wv"]), ACT)
        oc = _xattn_fwd(f"xattn_fwd_{l}", q, kk, vv, n_ex)
        cross = _dense(f"mem_o_{l}", oc, _natural(wts["mem_wo"]), F32)
        g2, b2 = vec(ln_g[l, 1]), vec(ln_b[l, 1])
        xh2, y2, rstd2 = _ln_fwd(f"ln2_fwd_{l}", xh1, g1, b1, cross, g2, b2)
        h1, h3, gact = _ffn_up(f"ffn_up_{l}", y2, wts["ffn_w1"], wts["ffn_w3"])
        ffo = _ffn_down(f"ffn_down_{l}", gact, wts["ffn_w2"])
        g3, b3 = vec(ln_g[l, 2]), vec(ln_b[l, 2])
        xh3, y3, rstd3 = _ln_fwd(f"ln3_fwd_{l}", xh2, g2, b2, ffo, g3, b3)
        sv.update(xh1=xh1, y1=y1, rstd1=rstd1, g1=g1, q=q, kk=kk, vv=vv, oc=oc, xh2=xh2, y2=y2, rstd2=rstd2, g2=g2,
                  h1=h1, h3=h3, gact=gact, xh3=xh3, rstd3=rstd3, g3=g3)
        saved.append(sv)
        xh, gp, bp, y_act = xh3, g3, b3, y3

    dy, loss = _loss_head(xh, gp, bp, tgt)

    sm = {n: [None] * (DEPTH // 2) for n in _SMALL_REPL + ["conv_w"]}
    d_ln_g = [[None] * 3 for _ in range(DEPTH)]
    d_ln_b = [[None] * 3 for _ in range(DEPTH)]

    def add_res(vals, ex):
        return [vals[0] + ALPHA * ex[0]]

    def add_res2(vals, ex):
        return [vals[0] + ex[0] + ALPHA * ex[1]]

    layer_g = [None] * DEPTH
    for l in reversed(range(DEPTH)):
        sv = saved[l]
        wts = sv["w"]
        big = {}
        dr3, dr3a, d_ln_g[l][2], d_ln_b[l][2] = _ln_bwd(f"ln3_bwd_{l}", dy, sv["xh3"], sv["rstd3"], sv["g3"])
        big["ffn_w2"] = _ffn_wgrad_down(f"ffn_w2_grad_{l}", sv["gact"], dr3a)
        dh1, dh3 = _ffn_down_bwd(f"ffn_down_bwd_{l}", dr3a, wts["ffn_w2"], sv["h1"], sv["h3"])
        big["ffn_w1"], big["ffn_w3"] = _ffn_wgrad_up(f"ffn_w13_grad_{l}", sv["y2"], dh1, dh3)
        part = _ffn_up_bwd(f"ffn_up_bwd1_{l}", dh1, wts["ffn_w1"], [], None)
        dy = _ffn_up_bwd(f"ffn_up_bwd3_{l}", dh3, wts["ffn_w3"], [part, dr3], add_res2)
        dr2, dr2a, d_ln_g[l][1], d_ln_b[l][1] = _ln_bwd(f"ln2_bwd_{l}", dy, sv["xh2"], sv["rstd2"], sv["g2"])
        big["mem_wo"] = _dense_tn(f"mem_wo_grad_{l}", sv["oc"], dr2a)
        doc = _dense_nt(f"mem_o_bwd_{l}", dr2a, _natural(wts["mem_wo"]), ACT)
        dq, dkk, dvv = _xattn_bwd(f"xattn_bwd_{l}", sv["q"], sv["kk"], sv["vv"], doc, n_ex)
        big["mem_wq"] = _dense_tn(f"mem_wq_grad_{l}", sv["y1"], dq)
        big["mem_wk"] = _dense_tn(f"mem_wk_grad_{l}", mem_a, dkk)
        big["mem_wv"] = _dense_tn(f"mem_wv_grad_{l}", mem_a, dvv)
        dy = _dense_nt(f"mem_q_bwd_{l}", dq, _natural(wts["mem_wq"]), F32, extras=[dr2], epilogue=add_res)
        dr1, dr1a, d_ln_g[l][0], d_ln_b[l][0] = _ln_bwd(f"ln1_bwd_{l}", dy, sv["xh1"], sv["rstd1"], sv["g1"])
        if l % 2 == 0:
            e = l // 2
            big["w_out_ab"] = _dense_tn(f"out_ab_grad_{l}", sv["yab"], dr1a)
            dyab = _dense_nt(f"out_ab_bwd_{l}", dr1a, _natural(wts["w_out_ab"]), ACT)
            duv, dgg, dgb, dws, dbs = _gmlp_bwd(f"gmlp_bwd_{l}", sv["h"], dyab, *sv["gl"])
            da, dgt, dcw, dcb, dng, dnb = _conv_bwd(f"conv_bwd_{l}", sv["h"], dyab, n_ex, *sv["cl"])
            sm["gmlp_ln_g"][e], sm["gmlp_ln_b"][e] = dgg.reshape(-1), dgb.reshape(-1)
            sm["gmlp_w_s"][e], sm["gmlp_b_s"][e] = dws, dbs.reshape(4, CHUNK)
            sm["conv_w"][e], sm["conv_b"][e] = dcw, dcb.reshape(-1)
            sm["conv_gn_g"][e], sm["conv_gn_b"][e] = dng.reshape(-1), dnb.reshape(-1)
            dh = jnp.concatenate([duv, da, dgt], axis=1)
            big["w_in_ab"] = _proj_cols_wgrad(f"in_ab_grad_{l}", sv["y0"], dh)
            dy = _proj_cols_bwd(f"in_ab_bwd_{l}", dh, wts["w_in_ab"], [dr1], add_res)
        else:
            big["w_out_c"] = _dense_tn(f"out_c_grad_{l}", sv["att"], dr1a)
            datt = _dense_nt(f"out_c_bwd_{l}", dr1a, _natural(wts["w_out_c"]), ACT)
            dq_, dk_, dv_ = _sb_bwd(f"sb_bwd_{l}", sv["qkv"], datt, sv["ltot"], n_ex)
            dqkv = jnp.concatenate([dq_, dk_.astype(ACT), dv_.astype(ACT)], axis=1)
            big["w_qkv_c"] = _proj_cols_wgrad(f"qkv_grad_{l}", sv["y0"], dqkv)
            dy = _proj_cols_bwd(f"qkv_bwd_{l}", dqkv, wts["w_qkv_c"], [dr1], add_res)
        for n in _ROW_SHARDED:
            if n in big:
                big[n] = big[n].reshape(N_CHIPS, -1, big[n].shape[-1])
        layer_g[l] = big

    grad_x = dy.reshape(n_ex, s, d)
    small_g = {n: jnp.stack(sm[n]) for n in sm}
    small_g["ln_g"] = jnp.stack([jnp.concatenate(r, axis=0) for r in d_ln_g])
    small_g["ln_b"] = jnp.stack([jnp.concatenate(r, axis=0) for r in d_ln_b])
    return loss, grad_x, layer_g, small_g


def kernel(x, mem, w_in_ab, gmlp_ln_g, gmlp_ln_b, gmlp_w_s, gmlp_b_s, conv_w, conv_b, conv_gn_g, conv_gn_b, w_out_ab, w_qkv_c, w_out_c, mem_wq, mem_wk, mem_wv, mem_wo, ffn_w1, ffn_w3, ffn_w2, ln_g, ln_b, loss_target, m_w_in_ab, m_gmlp_ln_g, m_gmlp_ln_b, m_gmlp_w_s, m_gmlp_b_s, m_conv_w, m_conv_b, m_conv_gn_g, m_conv_gn_b, m_w_out_ab, m_w_qkv_c, m_w_out_c, m_mem_wq, m_mem_wk, m_mem_wv, m_mem_wo, m_ffn_w1, m_ffn_w3, m_ffn_w2, m_ln_g, m_ln_b, v_w_in_ab, v_gmlp_ln_g, v_gmlp_ln_b, v_gmlp_w_s, v_gmlp_b_s, v_conv_w, v_conv_b, v_conv_gn_g, v_conv_gn_b, v_w_out_ab, v_w_qkv_c, v_w_out_c, v_mem_wq, v_mem_wk, v_mem_wv, v_mem_wo, v_ffn_w1, v_ffn_w3, v_ffn_w2, v_ln_g, v_ln_b):
    w = dict(w_in_ab=w_in_ab, gmlp_ln_g=gmlp_ln_g, gmlp_ln_b=gmlp_ln_b, gmlp_w_s=gmlp_w_s, gmlp_b_s=gmlp_b_s, conv_w=conv_w,
             conv_b=conv_b, conv_gn_g=conv_gn_g, conv_gn_b=conv_gn_b, w_out_ab=w_out_ab, w_qkv_c=w_qkv_c, w_out_c=w_out_c,
             mem_wq=mem_wq, mem_wk=mem_wk, mem_wv=mem_wv, mem_wo=mem_wo, ffn_w1=ffn_w1, ffn_w3=ffn_w3, ffn_w2=ffn_w2,
             ln_g=ln_g, ln_b=ln_b)
    mo = dict(w_in_ab=m_w_in_ab, gmlp_ln_g=m_gmlp_ln_g, gmlp_ln_b=m_gmlp_ln_b, gmlp_w_s=m_gmlp_w_s, gmlp_b_s=m_gmlp_b_s,
              conv_w=m_conv_w, conv_b=m_conv_b, conv_gn_g=m_conv_gn_g, conv_gn_b=m_conv_gn_b, w_out_ab=m_w_out_ab,
              w_qkv_c=m_w_qkv_c, w_out_c=m_w_out_c, mem_wq=m_mem_wq, mem_wk=m_mem_wk, mem_wv=m_mem_wv, mem_wo=m_mem_wo,
              ffn_w1=m_ffn_w1, ffn_w3=m_ffn_w3, ffn_w2=m_ffn_w2, ln_g=m_ln_g, ln_b=m_ln_b)
    vo = dict(w_in_ab=v_w_in_ab, gmlp_ln_g=v_gmlp_ln_g, gmlp_ln_b=v_gmlp_ln_b, gmlp_w_s=v_gmlp_w_s, gmlp_b_s=v_gmlp_b_s,
              conv_w=v_conv_w, conv_b=v_conv_b, conv_gn_g=v_conv_gn_g, conv_gn_b=v_conv_gn_b, w_out_ab=v_w_out_ab,
              w_qkv_c=v_w_qkv_c, w_out_c=v_w_out_c, mem_wq=v_mem_wq, mem_wk=v_mem_wk, mem_wv=v_mem_wv, mem_wo=v_mem_wo,
              ffn_w1=v_ffn_w1, ffn_w3=v_ffn_w3, ffn_w2=v_ffn_w2, ln_g=v_ln_g, ln_b=v_ln_b)
    me = (2 * lax.axis_index("x") + lax.axis_index("y")).astype(jnp.int32).reshape(1)

    per_layer = [_layer_weights(l) for l in range(DEPTH)]
    srcs = [w[n] for n in _SMALL_SHARD] + [w[n][i].astype(MM) for lw in per_layer for n, i in lw]
    lands = [lax.dynamic_update_index_in_dim(jnp.zeros((N_CHIPS, *s.shape), s.dtype), s[None], me[0], 0) for s in srcs]
    gathered = _exchange("gather_weights", "gather", srcs, lands)
    cw_g, lg_g, lb_g = gathered[:3]
    small = {n: w[n] for n in _SMALL_REPL}
    small["conv_w"] = jnp.moveaxis(cw_g, 0, 2).reshape(cw_g.shape[1], CONV_WIDTH, -1)
    small["ln_g"] = jnp.moveaxis(lg_g, 0, 2).reshape(DEPTH, 3, -1)
    small["ln_b"] = jnp.moveaxis(lb_g, 0, 2).reshape(DEPTH, 3, -1)
    it = iter(gathered[3:])
    layer_w = [{n: next(it) for n, _ in lw} for lw in per_layer]

    loss, grad_x, layer_g, small_g = _local_step(x, mem, loss_target, layer_w, small)
    loss = lax.psum(loss, ("x", "y", "c"))

    grads_flat = [layer_g[l][n] for l in range(DEPTH) for n, _ in per_layer[l]]
    recv = _exchange("scatter_grads", "scatter", grads_flat, [lax.empty((3, *g.shape[1:]), g.dtype) for g in grads_flat])
    per_name = {n: [None] * (DEPTH if n.startswith(("mem_", "ffn_")) else DEPTH // 2) for n in _BIG}
    flat_names = [ni for lw in per_layer for ni in lw]
    for (n, i), g, rc in zip(flat_names, grads_flat, recv):
        per_name[n][i] = _sum_chips(g, rc, me)
    sums = [jnp.stack(per_name[n]) for n in _BIG]
    sib = _swap_sibling(sums)

    out = {}
    for n, s_own, s_sib in zip(_BIG, sums, sib):
        out[n] = _adamw(f"adamw_{n}", [s_own, s_sib], w[n], mo[n], vo[n])

    order = _SMALL_REPL + _SMALL_SHARD
    part = _pack([small_g[n] for n in order])
    total = _sum_devices(_gather_all(part))
    full = dict(zip(order, _unpack(total, [small_g[n].shape for n in order])))
    x_i, y_i = lax.axis_index("x"), lax.axis_index("y")
    chip = 2 * x_i + y_i
    loc = {n: full[n] for n in _SMALL_REPL}
    for n in _SMALL_SHARD:
        wd = w[n].shape[-1]
        loc[n] = lax.dynamic_slice_in_dim(full[n], chip * wd, wd, axis=full[n].ndim - 1)
    gp, wp, mp, vp = (_pack([src[n] for n in order]) for src in (loc, w, mo, vo))
    r128 = gp.shape[0]
    res = _adamw("adamw_small", [gp.reshape(1, r128, 128)], wp.reshape(1, r128, 128), mp.reshape(1, r128, 128),
                 vp.reshape(1, r128, 128))
    shapes = [w[n].shape for n in order]
    unp = [_unpack(r.reshape(r128, 128), shapes) for r in res]
    for i, n in enumerate(order):
        out[n] = tuple(u[i] for u in unp)

    grads = [out[n][0] for n in _NAMES]
    deltas = [out[n][1] for n in _NAMES]
    new_m = [out[n][2] for n in _NAMES]
    new_v = [out[n][3] for n in _NAMES]
    return (loss, grad_x, *grads, *deltas, *new_m, *new_v)
```

```python
import functools
import math

import jax
import jax.numpy as jnp
from jax import lax
from jax.experimental import pallas as pl
from jax.experimental.pallas import tpu as pltpu

F32 = jnp.float32
MM = jnp.bfloat16
ACT = jnp.bfloat16
MESH = pl.DeviceIdType.MESH

DEPTH = 4
CHUNK = 128
CONV_WIDTH = 31
HALO = 32
MEM_HEADS = 4
C_HEAD_DIM = 64
ALPHA = (2.0 * DEPTH) ** 0.25
LN_EPS = 1e-5
ADAM_LR, ADAM_B1, ADAM_B2, ADAM_EPS, ADAM_WD, ADAM_STEP = 0.001, 0.9, 0.999, 1e-08, 0.01, 10

VMEM_CAP_V7X = 64 * 1024 * 1024
VMEM_MAX_REQUEST = 56 * 1024 * 1024
N_CHIPS = 4
N_DEV = 8


def _tile(n, pref):
    if n <= pref:
        return n
    for t in range(pref - pref % 8, 7, -8):
        if n % t == 0:
            return t
    return n


def _nbytes(shape, dtype):
    return math.prod(1 if s is None else s for s in shape) * jnp.dtype(dtype).itemsize


def _vmem_limit(block_bytes, scratch_bytes=0, temp_bytes=0):
    est = 2 * block_bytes + scratch_bytes + temp_bytes
    return int(min(VMEM_MAX_REQUEST, max(16 * 1024 * 1024, est * 5 // 4)))


def _params(sem, block_bytes, scratch_bytes=0, temp_bytes=0):
    return pltpu.CompilerParams(dimension_semantics=sem,
                                vmem_limit_bytes=_vmem_limit(block_bytes, scratch_bytes, temp_bytes))


_DIMS = {"nn": (((1,), (0,)), ((), ())), "nt": (((1,), (1,)), ((), ())), "tn": (((0,), (0,)), ((), ()))}


def _mm(name, mode, grid, a, a_spec, bs, b_specs, outs, out_specs, acc_shape,
        extras=(), extra_specs=(), epilogue=None):
    nb, ne, no = len(bs), len(extras), len(outs)
    nk = grid[2]

    def body(*refs):
        a_ref = refs[0]
        b_refs = refs[1:1 + nb]
        e_refs = refs[1 + nb:1 + nb + ne]
        o_refs = refs[1 + nb + ne:1 + nb + ne + no]
        accs = refs[1 + nb + ne + no:]
        k = pl.program_id(2)

        @pl.when(k == 0)
        def _():
            for acc in accs:
                acc[...] = jnp.zeros_like(acc)

        av = a_ref[...].astype(MM)
        for b_ref, acc in zip(b_refs, accs):
            acc[...] += lax.dot_general(av, b_ref[...].astype(MM), _DIMS[mode], preferred_element_type=F32)

        @pl.when(k == nk - 1)
        def _():
            vals = [acc[...] for acc in accs]
            if epilogue is not None:
                vals = epilogue(vals, [e[...].astype(F32) for e in e_refs])
            for o, v in zip(o_refs, vals):
                o[...] = v.astype(o.dtype)

    blocks = (_nbytes(a_spec.block_shape, a.dtype)
              + sum(_nbytes(s.block_shape, b.dtype) for s, b in zip(b_specs, bs))
              + sum(_nbytes(s.block_shape, e.dtype) for s, e in zip(extra_specs, extras))
              + sum(_nbytes(s.block_shape, o.dtype) for s, o in zip(out_specs, outs)))
    acc_bytes = nb * _nbytes(acc_shape, F32)
    res = pl.pallas_call(
        body, name=name, grid=grid,
        in_specs=[a_spec, *b_specs, *extra_specs], out_specs=list(out_specs), out_shape=list(outs),
        scratch_shapes=[pltpu.VMEM(acc_shape, F32)] * nb,
        compiler_params=_params(("parallel", "parallel", "arbitrary"), blocks, acc_bytes, 4 * acc_bytes),
    )(a, *bs, *extras)
    return res


def _sds(shape, dtype):
    return jax.ShapeDtypeStruct(shape, dtype)


def _dense(name, a, w, out_dtype, extras=(), epilogue=None, n_out=None):
    t, kdim = a.shape
    n = w.shape[1]
    tm, tn, tk = _tile(t, 1024), _tile(n, 512), _tile(kdim, 1024)
    grid = (t // tm, n // tn, kdim // tk)
    return _mm(name, "nn", grid, a, pl.BlockSpec((tm, tk), lambda i, j, k: (i, k)),
               [w], [pl.BlockSpec((tk, tn), lambda i, j, k: (k, j))],
               [_sds((t, n), out_dtype)], [pl.BlockSpec((tm, tn), lambda i, j, k: (i, j))], (tm, tn),
               extras=extras, extra_specs=[pl.BlockSpec((tm, tn), lambda i, j, k: (i, j))] * len(extras),
               epilogue=epilogue)[0]


def _dense_nt(name, a, w, out_dtype, extras=(), epilogue=None):
    t, n = a.shape
    kout = w.shape[0]
    tm, tn, tk = _tile(t, 1024), _tile(kout, 512), _tile(n, 1024)
    grid = (t // tm, kout // tn, n // tk)
    return _mm(name, "nt", grid, a, pl.BlockSpec((tm, tk), lambda i, j, k: (i, k)),
               [w], [pl.BlockSpec((tn, tk), lambda i, j, k: (j, k))],
               [_sds((t, kout), out_dtype)], [pl.BlockSpec((tm, tn), lambda i, j, k: (i, j))], (tm, tn),
               extras=extras, extra_specs=[pl.BlockSpec((tm, tn), lambda i, j, k: (i, j))] * len(extras),
               epilogue=epilogue)[0]


def _dense_tn(name, a, b, out_dtype=MM):
    t, m = a.shape
    n = b.shape[1]
    tm, tn, tk = _tile(m, 512), _tile(n, 512), _tile(t, 1024)
    grid = (m // tm, n // tn, t // tk)
    return _mm(name, "tn", grid, a, pl.BlockSpec((tk, tm), lambda i, j, k: (k, i)),
               [b], [pl.BlockSpec((tk, tn), lambda i, j, k: (k, j))],
               [_sds((m, n), out_dtype)], [pl.BlockSpec((tm, tn), lambda i, j, k: (i, j))], (tm, tn))[0]


_INV_SQRT2 = 0.7071067811865476
_INV_SQRT_2PI = 0.3989422804014327


def _gelu(x):
    return 0.5 * x * (1.0 + lax.erf(x * _INV_SQRT2))


def _gelu_grad(x):
    return 0.5 * (1.0 + lax.erf(x * _INV_SQRT2)) + x * jnp.exp(-0.5 * x * x) * _INV_SQRT_2PI


def _sigmoid(x):
    return 1.0 / (1.0 + jnp.exp(-x))


def _norm_stats(x):
    mu = jnp.mean(x, axis=-1, keepdims=True)
    xc = x - mu
    var = jnp.mean(xc * xc, axis=-1, keepdims=True)
    rstd = lax.rsqrt(var + LN_EPS)
    return xc * rstd, rstd


def _norm_bwd(dy_g, xh, rstd):
    m1 = jnp.mean(dy_g, axis=-1, keepdims=True)
    m2 = jnp.mean(dy_g * xh, axis=-1, keepdims=True)
    return rstd * (dy_g - m1 - xh * m2)


def _rows8(x):
    r, c = x.shape
    return jnp.sum(x.reshape(r // 8, 8, c), axis=0)


def _ln_fwd(name, xh_prev, g_prev, b_prev, f, g, b):
    t, d = f.shape
    tm = _tile(t, 512)

    def body(xp_ref, gp_ref, bp_ref, f_ref, g_ref, b_ref, xh_ref, y_ref, rstd_ref):
        r = ALPHA * (xp_ref[...] * gp_ref[...] + bp_ref[...]) + f_ref[...]
        xh, rstd = _norm_stats(r)
        xh_ref[...] = xh
        y_ref[...] = (xh * g_ref[...] + b_ref[...]).astype(y_ref.dtype)
        rstd_ref[...] = rstd

    row = pl.BlockSpec((tm, d), lambda i: (i, 0))
    vec = pl.BlockSpec((1, d), lambda i: (0, 0))
    return pl.pallas_call(
        body, name=name, grid=(t // tm,),
        in_specs=[row, vec, vec, row, vec, vec],
        out_specs=[row, row, pl.BlockSpec((tm, 1), lambda i: (i, 0))],
        out_shape=[_sds((t, d), F32), _sds((t, d), ACT), _sds((t, 1), F32)],
        compiler_params=_params(("parallel",), 4 * tm * d * 4, 0, 4 * tm * d * 4),
    )(xh_prev, g_prev, b_prev, f, g, b)


def _ln_bwd(name, dy, xh, rstd, g):
    t, d = dy.shape
    tm = _tile(t, 512)
    n = t // tm

    def body(dy_ref, xh_ref, rstd_ref, g_ref, dr_ref, dra_ref, dg_ref, db_ref, dg_acc, db_acc):
        i = pl.program_id(0)

        @pl.when(i == 0)
        def _():
            dg_acc[...] = jnp.zeros_like(dg_acc)
            db_acc[...] = jnp.zeros_like(db_acc)

        dyv = dy_ref[...]
        xhv = xh_ref[...]
        dr = _norm_bwd(dyv * g_ref[...], xhv, rstd_ref[...])
        dr_ref[...] = dr
        dra_ref[...] = dr.astype(dra_ref.dtype)
        dg_acc[...] += _rows8(dyv * xhv)
        db_acc[...] += _rows8(dyv)

        @pl.when(i == n - 1)
        def _():
            dg_ref[...] = jnp.sum(dg_acc[...], axis=0, keepdims=True)
            db_ref[...] = jnp.sum(db_acc[...], axis=0, keepdims=True)

    row = pl.BlockSpec((tm, d), lambda i: (i, 0))
    vec = pl.BlockSpec((1, d), lambda i: (0, 0))
    return pl.pallas_call(
        body, name=name, grid=(n,),
        in_specs=[row, row, pl.BlockSpec((tm, 1), lambda i: (i, 0)), vec],
        out_specs=[row, row, vec, vec],
        out_shape=[_sds((t, d), F32), _sds((t, d), ACT), _sds((1, d), F32), _sds((1, d), F32)],
        scratch_shapes=[pltpu.VMEM((8, d), F32), pltpu.VMEM((8, d), F32)],
        compiler_params=_params(("arbitrary",), 4 * tm * d * 4, 0, 4 * tm * d * 4),
    )(dy, xh, rstd, g)


def _loss_head(xh, g, b, target):
    t, d = xh.shape
    tm = _tile(t, 512)
    n = t // tm

    def body(xh_ref, g_ref, b_ref, tg_ref, dy_ref, loss_ref, acc):
        i = pl.program_id(0)

        @pl.when(i == 0)
        def _():
            acc[...] = jnp.zeros_like(acc)

        err = xh_ref[...] * g_ref[...] + b_ref[...] - tg_ref[...]
        dy_ref[...] = err * (1.0 / d)
        acc[...] += _rows8(err * err)

        @pl.when(i == n - 1)
        def _():
            s = jnp.sum(jnp.sum(acc[...], axis=0, keepdims=True), axis=1, keepdims=True)
            loss_ref[...] = jnp.broadcast_to(s * (0.5 / d), loss_ref.shape)

    row = pl.BlockSpec((tm, d), lambda i: (i, 0))
    vec = pl.BlockSpec((1, d), lambda i: (0, 0))
    dy, loss = pl.pallas_call(
        body, name="loss_head", grid=(n,),
        in_specs=[row, vec, vec, row],
        out_specs=[row, pl.BlockSpec((8, 128), lambda i: (0, 0))],
        out_shape=[_sds((t, d), F32), _sds((8, 128), F32)],
        scratch_shapes=[pltpu.VMEM((8, d), F32)],
        compiler_params=_params(("arbitrary",), 3 * tm * d * 4, 0, 2 * tm * d * 4),
    )(xh, g, b, target)
    return dy, loss[0, 0]


def _causal_w(w):
    r = lax.broadcasted_iota(jnp.int32, w.shape, 0)
    c = lax.broadcasted_iota(jnp.int32, w.shape, 1)
    return jnp.where(r >= c, w, 0.0)


def _gmlp_fwd(name, h, ln_g, ln_b, w_s, b_s_col):
    t = h.shape[0]
    tt = _tile(t, 256)
    wd = 4 * CHUNK

    def body(u_ref, v_ref, g_ref, b_ref, w_ref, bs_ref, ya_ref):
        for gi in range(4):
            ln = slice(gi * CHUNK, (gi + 1) * CHUNK)
            u = _gelu(u_ref[:, ln].astype(F32))
            v = _gelu(v_ref[:, ln].astype(F32))
            xh, _ = _norm_stats(v)
            vg = (xh * g_ref[:, ln] + b_ref[:, ln]).astype(MM)
            w = _causal_w(w_ref[gi]).astype(MM)
            for c in range(tt // CHUNK):
                rs = slice(c * CHUNK, (c + 1) * CHUNK)
                mixed = jnp.dot(w, vg[rs], preferred_element_type=F32) + bs_ref[gi]
                ya_ref[rs, ln] = (u[rs] * mixed).astype(ya_ref.dtype)

    vec = pl.BlockSpec((1, wd), lambda i: (0, 0))
    return pl.pallas_call(
        body, name=name, grid=(t // tt,),
        in_specs=[pl.BlockSpec((tt, wd), lambda i: (i, 0)), pl.BlockSpec((tt, wd), lambda i: (i, 1)), vec, vec,
                  pl.BlockSpec((4, CHUNK, CHUNK), lambda i: (0, 0, 0)), pl.BlockSpec((4, CHUNK, 1), lambda i: (0, 0, 0))],
        out_specs=pl.BlockSpec((tt, wd), lambda i: (i, 0)),
        out_shape=_sds((t, wd), ACT),
        compiler_params=_params(("parallel",), 3 * tt * wd * 4, 0, 8 * tt * CHUNK * 4),
    )(h, h, ln_g, ln_b, w_s, b_s_col)


def _gmlp_bwd(name, h, dyab, ln_g, ln_b, w_s, b_s_col):
    t = h.shape[0]
    tt = _tile(t, 256)
    n = t // tt
    wd = 4 * CHUNK

    def body(u_ref, v_ref, dy_ref, g_ref, b_ref, w_ref, bs_ref, duv_ref, dg_ref, db_ref, dw_ref, dbs_ref,
             dg_acc, db_acc):
        i = pl.program_id(0)

        @pl.when(i == 0)
        def _():
            dg_acc[...] = jnp.zeros_like(dg_acc)
            db_acc[...] = jnp.zeros_like(db_acc)
            dw_ref[...] = jnp.zeros_like(dw_ref)
            dbs_ref[...] = jnp.zeros_like(dbs_ref)

        for gi in range(4):
            ln = slice(gi * CHUNK, (gi + 1) * CHUNK)
            upre = u_ref[:, ln].astype(F32)
            vpre = v_ref[:, ln].astype(F32)
            u = _gelu(upre)
            v = _gelu(vpre)
            xh, rstd = _norm_stats(v)
            gv = g_ref[:, ln]
            vg = (xh * gv + b_ref[:, ln]).astype(MM)
            w = _causal_w(w_ref[gi]).astype(MM)
            dya = dy_ref[:, ln].astype(F32)
            dmixed = dya * u
            dmm = dmixed.astype(MM)
            dvg_parts, mixed_parts = [], []
            dw = jnp.zeros((CHUNK, CHUNK), F32)
            dbs = jnp.zeros((CHUNK, 1), F32)
            for c in range(tt // CHUNK):
                rs = slice(c * CHUNK, (c + 1) * CHUNK)
                mixed_parts.append(jnp.dot(w, vg[rs], preferred_element_type=F32) + bs_ref[gi])
                dw = dw + lax.dot_general(dmm[rs], vg[rs], _DIMS["nt"], preferred_element_type=F32)
                dbs = dbs + jnp.sum(dmixed[rs], axis=1, keepdims=True)
                dvg_parts.append(lax.dot_general(w, dmm[rs], _DIMS["tn"], preferred_element_type=F32))
            mixed = jnp.concatenate(mixed_parts, axis=0)
            dvg = jnp.concatenate(dvg_parts, axis=0)
            dw_ref[gi] += _causal_w(dw)
            dbs_ref[gi] += dbs
            dg_acc[:, ln] += _rows8(dvg * xh)
            db_acc[:, ln] += _rows8(dvg)
            dv = _norm_bwd(dvg * gv, xh, rstd) * _gelu_grad(vpre)
            du = dya * mixed * _gelu_grad(upre)
            duv_ref[:, ln] = du.astype(duv_ref.dtype)
            duv_ref[:, wd + gi * CHUNK: wd + (gi + 1) * CHUNK] = dv.astype(duv_ref.dtype)

        @pl.when(i == n - 1)
        def _():
            dg_ref[...] = jnp.sum(dg_acc[...], axis=0, keepdims=True)
            db_ref[...] = jnp.sum(db_acc[...], axis=0, keepdims=True)

    vec = pl.BlockSpec((1, wd), lambda i: (0, 0))
    wspec = pl.BlockSpec((4, CHUNK, CHUNK), lambda i: (0, 0, 0))
    bspec = pl.BlockSpec((4, CHUNK, 1), lambda i: (0, 0, 0))
    return pl.pallas_call(
        body, name=name, grid=(n,),
        in_specs=[pl.BlockSpec((tt, wd), lambda i: (i, 0)), pl.BlockSpec((tt, wd), lambda i: (i, 1)),
                  pl.BlockSpec((tt, wd), lambda i: (i, 0)), vec, vec, wspec, bspec],
        out_specs=[pl.BlockSpec((tt, 2 * wd), lambda i: (i, 0)), vec, vec, wspec, bspec],
        out_shape=[_sds((t, 2 * wd), ACT), _sds((1, wd), F32), _sds((1, wd), F32),
                   _sds((4, CHUNK, CHUNK), F32), _sds((4, CHUNK, 1), F32)],
        scratch_shapes=[pltpu.VMEM((8, wd), F32), pltpu.VMEM((8, wd), F32)],
        compiler_params=_params(("arbitrary",), 5 * tt * wd * 4, 0, 16 * tt * CHUNK * 4),
    )(h, h, dyab, ln_g, ln_b, w_s, b_s_col)


_ROWS = 256


def _conv_taps(win, cw, lo):
    acc = jnp.zeros((_ROWS, CHUNK), F32)
    for w in range(CONV_WIDTH):
        s = lo(w)
        acc = acc + cw[w:w + 1, :] * win[s:s + _ROWS, :]
    return acc


def _conv_fwd(name, h, n_ex, cw, cb, gg, gb):
    t = h.shape[0]
    s = t // n_ex
    nt = s // _ROWS

    def body(a_ref, gt_ref, cw_ref, cb_ref, gg_ref, gb_ref, yb_ref, hh):
        hh[0:HALO, :] = jnp.zeros((HALO, CHUNK), F32)
        hh[HALO:HALO + s, :] = a_ref[...].astype(F32) * _sigmoid(gt_ref[...].astype(F32))
        cwv = cw_ref[...]

        def tile(i, carry):
            r0 = pl.multiple_of(i * _ROWS, _ROWS)
            win = hh[pl.ds(r0, _ROWS + HALO), :]
            c = _conv_taps(win, cwv, lambda w: w + HALO - (CONV_WIDTH - 1)) + cb_ref[...]
            xh, _ = _norm_stats(c)
            hg = xh * gg_ref[...] + gb_ref[...]
            yb_ref[pl.ds(r0, _ROWS), :] = (hg * _sigmoid(hg)).astype(yb_ref.dtype)
            return carry

        lax.fori_loop(0, nt, tile, 0)

    vec = pl.BlockSpec((1, CHUNK), lambda g, b: (0, g))
    return pl.pallas_call(
        body, name=name, grid=(4, n_ex),
        in_specs=[pl.BlockSpec((s, CHUNK), lambda g, b: (b, 8 + g)), pl.BlockSpec((s, CHUNK), lambda g, b: (b, 12 + g)),
                  pl.BlockSpec((CONV_WIDTH, CHUNK), lambda g, b: (0, g)), vec, vec, vec],
        out_specs=pl.BlockSpec((s, CHUNK), lambda g, b: (b, g)),
        out_shape=_sds((t, 4 * CHUNK), ACT),
        scratch_shapes=[pltpu.VMEM((s + HALO, CHUNK), F32)],
        compiler_params=_params(("parallel", "parallel"), 3 * s * CHUNK * 4, (s + HALO) * CHUNK * 4, 4 * s * CHUNK * 4),
    )(h, h, cw, cb, gg, gb)


def _conv_bwd(name, h, dyab, n_ex, cw, cb, gg, gb):
    t = h.shape[0]
    s = t // n_ex
    nt = s // _ROWS

    def body(a_ref, gt_ref, dy_ref, cw_ref, cb_ref, gg_ref, gb_ref,
             da_ref, dgt_ref, dcw_ref, dcb_ref, dgg_ref, dgb_ref, hh, dcs, acc):
        b = pl.program_id(1)

        @pl.when(b == 0)
        def _():
            dcw_ref[...] = jnp.zeros_like(dcw_ref)
            dcb_ref[...] = jnp.zeros_like(dcb_ref)
            dgg_ref[...] = jnp.zeros_like(dgg_ref)
            dgb_ref[...] = jnp.zeros_like(dgb_ref)

        hh[0:HALO, :] = jnp.zeros((HALO, CHUNK), F32)
        hh[HALO:HALO + s, :] = a_ref[...].astype(F32) * _sigmoid(gt_ref[...].astype(F32))
        dcs[s:s + HALO, :] = jnp.zeros((HALO, CHUNK), F32)
        acc[...] = jnp.zeros_like(acc)
        cwv = cw_ref[...]
        off = HALO - (CONV_WIDTH - 1)

        def tile1(i, carry):
            r0 = pl.multiple_of(i * _ROWS, _ROWS)
            win = hh[pl.ds(r0, _ROWS + HALO), :]
            c = _conv_taps(win, cwv, lambda w: w + off) + cb_ref[...]
            xh, rstd = _norm_stats(c)
            hg = xh * gg_ref[...] + gb_ref[...]
            sg = _sigmoid(hg)
            dhg = dy_ref[pl.ds(r0, _ROWS), :].astype(F32) * (sg * (1.0 + hg * (1.0 - sg)))
            acc[32:40, :] += _rows8(dhg * xh)
            acc[40:48, :] += _rows8(dhg)
            dc = _norm_bwd(dhg * gg_ref[...], xh, rstd)
            dcs[pl.ds(r0, _ROWS), :] = dc
            acc[48:56, :] += _rows8(dc)
            for w in range(CONV_WIDTH):
                acc[w:w + 1, :] += jnp.sum(dc * win[w + off:w + off + _ROWS, :], axis=0, keepdims=True)
            return carry

        lax.fori_loop(0, nt, tile1, 0)

        def tile2(i, carry):
            r0 = pl.multiple_of(i * _ROWS, _ROWS)
            win = dcs[pl.ds(r0, _ROWS + HALO), :]
            dhh = _conv_taps(win, cwv, lambda w: CONV_WIDTH - 1 - w)
            av = a_ref[pl.ds(r0, _ROWS), :].astype(F32)
            sg = _sigmoid(gt_ref[pl.ds(r0, _ROWS), :].astype(F32))
            da_ref[pl.ds(r0, _ROWS), :] = (dhh * sg).astype(da_ref.dtype)
            dgt_ref[pl.ds(r0, _ROWS), :] = (dhh * av * sg * (1.0 - sg)).astype(dgt_ref.dtype)
            return carry

        lax.fori_loop(0, nt, tile2, 0)
        dcw_ref[...] += acc[0:CONV_WIDTH, :]
        dgg_ref[...] += jnp.sum(acc[32:40, :], axis=0, keepdims=True)
        dgb_ref[...] += jnp.sum(acc[40:48, :], axis=0, keepdims=True)
        dcb_ref[...] += jnp.sum(acc[48:56, :], axis=0, keepdims=True)

    vec = pl.BlockSpec((1, CHUNK), lambda g, b: (0, g))
    tap = pl.BlockSpec((CONV_WIDTH, CHUNK), lambda g, b: (0, g))
    seq = pl.BlockSpec((s, CHUNK), lambda g, b: (b, g))
    return pl.pallas_call(
        body, name=name, grid=(4, n_ex),
        in_specs=[pl.BlockSpec((s, CHUNK), lambda g, b: (b, 8 + g)), pl.BlockSpec((s, CHUNK), lambda g, b: (b, 12 + g)),
                  pl.BlockSpec((s, CHUNK), lambda g, b: (b, 4 + g)), tap, vec, vec, vec],
        out_specs=[seq, seq, tap, vec, vec, vec],
        out_shape=[_sds((t, 4 * CHUNK), ACT), _sds((t, 4 * CHUNK), ACT), _sds((CONV_WIDTH, 4 * CHUNK), F32),
                   _sds((1, 4 * CHUNK), F32), _sds((1, 4 * CHUNK), F32), _sds((1, 4 * CHUNK), F32)],
        scratch_shapes=[pltpu.VMEM((s + HALO, CHUNK), F32), pltpu.VMEM((s + HALO, CHUNK), F32),
                        pltpu.VMEM((56, CHUNK), F32)],
        compiler_params=_params(("parallel", "arbitrary"), 5 * s * CHUNK * 4, 2 * (s + HALO) * CHUNK * 4, 4 * s * CHUNK * 4),
    )(h, h, dyab, cw, cb, gg, gb)


_TQ = 256
_SB_DEAD = -110.0


def _tri(kind):
    r = lax.broadcasted_iota(jnp.int32, (_TQ, _TQ), 0)
    c = lax.broadcasted_iota(jnp.int32, (_TQ, _TQ), 1)
    m = {"gt": r > c, "le": r <= c, "lt": r < c}[kind]
    return jnp.where(m, 1.0, 0.0).astype(jnp.bfloat16)


def _split_dot(x, tri2):
    hi = x.astype(jnp.bfloat16)
    lo = (x - hi.astype(F32)).astype(jnp.bfloat16)
    return jnp.dot(jnp.concatenate([hi, lo], axis=1), tri2, preferred_element_type=F32)


def _neg_abs(x):
    bits = lax.bitcast_convert_type(x, jnp.uint32) | jnp.uint32(0x80000000)
    return lax.bitcast_convert_type(bits, F32)


def _log_not_beta(nz):
    return jnp.minimum(nz, 0.0) - jnp.log(1.0 + jnp.exp(_neg_abs(nz)))


def _sb_fwd(name, qkv, n_ex):
    t = qkv.shape[0]
    d = qkv.shape[1] // 3
    npair = d // CHUNK
    s = t // n_ex
    nq = s // _TQ
    neg_a = -(C_HEAD_DIM ** -0.5)

    def body(q_ref, k_ref, v_ref, o_ref, lt_ref, o_acc, c_acc):
        i = pl.program_id(2)
        first = lax.broadcasted_iota(jnp.int32, (_TQ, CHUNK), 1) < C_HEAD_DIM
        q2 = q_ref[...]
        zero = jnp.zeros_like(q2)
        qh = [jnp.where(first, q2, zero), jnp.where(first, zero, q2)]
        tri2 = jnp.concatenate([_tri("gt")] * 2, axis=0)
        o_acc[...] = jnp.zeros_like(o_acc)
        c_acc[...] = jnp.zeros_like(c_acc)

        def tiles(js, mask):
            work = [(a, hd) for a in range(len(js)) for hd in range(2)]
            rows = [pl.ds(pl.multiple_of(j * _TQ, _TQ), _TQ) for j in js]
            kts = [k_ref[r, :] for r in rows]
            vts = [v_ref[r, :] for r in rows]
            nzs = {w: lax.dot_general(qh[w[1]], kts[w[0]], _DIMS["nt"], preferred_element_type=F32) * neg_a for w in work}
            lns = {w: _log_not_beta(nzs[w]) for w in work}
            if mask is not None:
                lns = {w: jnp.where(mask, lns[w], 0.0) for w in work}
            locs = {w: _split_dot(lns[w], tri2) for w in work}
            laters = {}
            for hd in range(2):
                carry = c_acc[hd]
                for a in range(len(js)):
                    laters[a, hd] = carry + locs[a, hd]
                    carry = laters[a, hd][:, 0:1] + lns[a, hd][:, 0:1]
                c_acc[hd] = carry
            atts = {w: jnp.exp(lns[w] - nzs[w] + laters[w]) for w in work}
            if mask is not None:
                atts = {w: jnp.where(mask, atts[w], 0.0) for w in work}
            for hd in range(2):
                acc = o_acc[hd]
                for a in range(len(js)):
                    acc = acc + jnp.dot(atts[a, hd].astype(MM), vts[a], preferred_element_type=F32)
                o_acc[hd] = acc

        tiles([i], lax.broadcasted_iota(jnp.int32, (_TQ, _TQ), 1) < lax.broadcasted_iota(jnp.int32, (_TQ, _TQ), 0))

        def alive():
            return jnp.max(jnp.maximum(c_acc[0], c_acc[1])) >= _SB_DEAD

        def cond(st):
            return jnp.logical_and(st[0] >= 0, st[1])

        def step(st):
            tiles([st[0]], None)
            return st[0] - 1, alive()

        j_last, _ = lax.while_loop(cond, step, (i - 1, alive()))
        o_ref[...] = jnp.where(first, o_acc[0], o_acc[1])
        lt_ref[:, 0:1] = c_acc[0]
        lt_ref[:, 1:2] = c_acc[1]
        lt_ref[:, 2:3] = jnp.full((_TQ, 1), j_last + 1, jnp.int32).astype(F32)

    return pl.pallas_call(
        body, name=name, grid=(n_ex, npair, nq),
        in_specs=[pl.BlockSpec((_TQ, CHUNK), lambda b, p, i: (b * nq + i, p)),
                  pl.BlockSpec((s, CHUNK), lambda b, p, i: (b, npair + p)),
                  pl.BlockSpec((s, CHUNK), lambda b, p, i: (b, 2 * npair + p))],
        out_specs=[pl.BlockSpec((_TQ, CHUNK), lambda b, p, i: (b * nq + i, p)),
                   pl.BlockSpec((None, _TQ, 3), lambda b, p, i: (p, b * nq + i, 0))],
        out_shape=[_sds((t, d), F32), _sds((npair, t, 3), F32)],
        scratch_shapes=[pltpu.VMEM((2, _TQ, CHUNK), F32), pltpu.VMEM((2, _TQ, 1), F32)],
        compiler_params=_params(("parallel", "parallel", "arbitrary"), 2 * s * CHUNK * 2 + 4 * _TQ * CHUNK * 4,
                                4 * _TQ * CHUNK * 4, 24 * _TQ * _TQ * 4),
    )(qkv, qkv, qkv)


def _sb_bwd(name, qkv, do, ltot, n_ex):
    t = qkv.shape[0]
    d = qkv.shape[1] // 3
    npair = d // CHUNK
    s = t // n_ex
    nq = s // _TQ
    scale = C_HEAD_DIM ** -0.5
    neg_a = -scale

    def body(q_ref, k_ref, v_ref, do_ref, lt_ref, dq_ref, dk_ref, dv_ref, dq_acc, cp_acc, cg_acc):
        i = pl.program_id(2)

        @pl.when(i == 0)
        def _():
            dk_ref[...] = jnp.zeros_like(dk_ref)
            dv_ref[...] = jnp.zeros_like(dv_ref)

        first = lax.broadcasted_iota(jnp.int32, (_TQ, CHUNK), 1) < C_HEAD_DIM
        q2 = q_ref[...]
        do2 = do_ref[...]
        qs = (q2 * scale).astype(q2.dtype)
        zero = jnp.zeros_like(q2)
        qh = [jnp.where(first, q2, zero), jnp.where(first, zero, q2)]
        doh = [jnp.where(first, do2, zero), jnp.where(first, zero, do2)]
        lt = [lt_ref[:, 0:1], lt_ref[:, 1:2]]
        tri2_le = jnp.concatenate([_tri("le")] * 2, axis=0)
        tri_lt = _tri("lt").astype(MM)
        dq_acc[...] = jnp.zeros_like(dq_acc)
        cp_acc[...] = jnp.zeros_like(cp_acc)
        cg_acc[...] = jnp.zeros_like(cg_acc)

        def tiles(js, mask):
            na = len(js)
            work = [(a, hd) for a in range(na) for hd in range(2)]
            last = slice(_TQ - 1, _TQ)
            rows = [pl.ds(pl.multiple_of(j * _TQ, _TQ), _TQ) for j in js]
            kts = [k_ref[r, :] for r in rows]
            vts = [v_ref[r, :] for r in rows]
            ksc = [(kt * scale).astype(kt.dtype) for kt in kts]
            nzs = {w: lax.dot_general(qh[w[1]], kts[w[0]], _DIMS["nt"], preferred_element_type=F32) * neg_a for w in work}
            datts = {w: lax.dot_general(doh[w[1]], vts[w[0]], _DIMS["nt"], preferred_element_type=F32) for w in work}
            lns = {w: _log_not_beta(nzs[w]) for w in work}
            if mask is not None:
                lns = {w: jnp.where(mask, lns[w], 0.0) for w in work}
            pins = {w: _split_dot(lns[w], tri2_le) for w in work}
            lss = {w: lns[w] - nzs[w] for w in work}
            atts = {}
            for hd in range(2):
                cp = cp_acc[hd]
                for a in range(na):
                    atts[a, hd] = jnp.exp(lss[a, hd] + ((lt[hd] - cp) - pins[a, hd]))
                    cp = cp + pins[a, hd][:, last]
                cp_acc[hd] = cp
            if mask is not None:
                atts = {w: jnp.where(mask, atts[w], 0.0) for w in work}
            gs = {w: datts[w] * atts[w] for w in work}
            locg = {w: jnp.dot(gs[w].astype(MM), tri_lt, preferred_element_type=F32) for w in work}
            dzs = {}
            for hd in range(2):
                carry = cg_acc[hd]
                for a in range(na):
                    big = carry + locg[a, hd]
                    dzs[a, hd] = gs[a, hd] - (gs[a, hd] + big) * jnp.exp(lss[a, hd])
                    carry = big[:, last] + gs[a, hd][:, last]
                cg_acc[hd] = carry
            if mask is not None:
                dzs = {w: jnp.where(mask, dzs[w], 0.0) for w in work}
            dzs = {w: dzs[w].astype(MM) for w in work}
            attm = {w: atts[w].astype(MM) for w in work}
            for hd in range(2):
                acc = dq_acc[hd]
                for a in range(na):
                    acc = acc + jnp.dot(dzs[a, hd], ksc[a], preferred_element_type=F32)
                dq_acc[hd] = acc
            for a in range(na):
                dk0, dk1 = [lax.dot_general(dzs[a, hd], qs, _DIMS["tn"], preferred_element_type=F32) for hd in range(2)]
                dv0, dv1 = [lax.dot_general(attm[a, hd], do2, _DIMS["tn"], preferred_element_type=F32) for hd in range(2)]
                dk_ref[rows[a], :] += jnp.where(first, dk0, dk1)
                dv_ref[rows[a], :] += jnp.where(first, dv0, dv1)

        def single(j, carry):
            tiles([j], None)
            return carry

        j_first = jnp.clip(jnp.max(lt_ref[:, 2:3]).astype(jnp.int32), 0, i)
        lax.fori_loop(j_first, i, single, 0)
        tiles([i], lax.broadcasted_iota(jnp.int32, (_TQ, _TQ), 1) < lax.broadcasted_iota(jnp.int32, (_TQ, _TQ), 0))
        dq_ref[...] = jnp.where(first, dq_acc[0], dq_acc[1]).astype(dq_ref.dtype)

    qspec = pl.BlockSpec((_TQ, CHUNK), lambda b, p, i: (b * nq + i, p))
    kv_out = pl.BlockSpec((s, CHUNK), lambda b, p, i: (b, p))
    return pl.pallas_call(
        body, name=name, grid=(n_ex, npair, nq),
        in_specs=[qspec, pl.BlockSpec((s, CHUNK), lambda b, p, i: (b, npair + p)),
                  pl.BlockSpec((s, CHUNK), lambda b, p, i: (b, 2 * npair + p)), qspec,
                  pl.BlockSpec((None, _TQ, 3), lambda b, p, i: (p, b * nq + i, 0))],
        out_specs=[qspec, kv_out, kv_out],
        out_shape=[_sds((t, d), ACT), _sds((t, d), F32), _sds((t, d), F32)],
        scratch_shapes=[pltpu.VMEM((2, _TQ, CHUNK), F32), pltpu.VMEM((2, _TQ, 1), F32), pltpu.VMEM((2, _TQ, 1), F32)],
        compiler_params=_params(("parallel", "parallel", "arbitrary"), 2 * s * CHUNK * 2 + 2 * s * CHUNK * 4,
                                4 * _TQ * CHUNK * 4, 32 * _TQ * _TQ * 4),
    )(qkv, qkv, qkv, do, ltot)


def _xattn_fwd(name, q, kk, vv, n_ex):
    t, d = q.shape
    m = kk.shape[0] // n_ex
    s = t // n_ex
    tq = _tile(s, 512)
    nq = s // tq
    hd_dim = d // MEM_HEADS
    scale = hd_dim ** -0.5

    def body(q_ref, k_ref, v_ref, o_ref):
        for hd in range(MEM_HEADS):
            ln = slice(hd * hd_dim, (hd + 1) * hd_dim)
            sc = lax.dot_general(q_ref[:, ln], k_ref[:, ln], _DIMS["nt"], preferred_element_type=F32) * scale
            p = jnp.exp(sc - jnp.max(sc, axis=-1, keepdims=True))
            p = p / jnp.sum(p, axis=-1, keepdims=True)
            o_ref[:, ln] = jnp.dot(p.astype(MM), v_ref[:, ln], preferred_element_type=F32).astype(o_ref.dtype)

    qspec = pl.BlockSpec((tq, d), lambda b, i: (b * nq + i, 0))
    kspec = pl.BlockSpec((m, d), lambda b, i: (b, 0))
    return pl.pallas_call(
        body, name=name, grid=(n_ex, nq), in_specs=[qspec, kspec, kspec], out_specs=qspec,
        out_shape=_sds((t, d), ACT),
        compiler_params=_params(("parallel", "parallel"), 2 * tq * d * 2 + 2 * m * d * 2, 0, 6 * tq * m * 4),
    )(q, kk, vv)


def _xattn_bwd(name, q, kk, vv, do, n_ex):
    t, d = q.shape
    m = kk.shape[0] // n_ex
    s = t // n_ex
    tq = _tile(s, 512)
    nq = s // tq
    hd_dim = d // MEM_HEADS
    scale = hd_dim ** -0.5

    def body(q_ref, k_ref, v_ref, do_ref, dq_ref, dk_ref, dv_ref):
        i = pl.program_id(1)

        @pl.when(i == 0)
        def _():
            dk_ref[...] = jnp.zeros_like(dk_ref)
            dv_ref[...] = jnp.zeros_like(dv_ref)

        for hd in range(MEM_HEADS):
            ln = slice(hd * hd_dim, (hd + 1) * hd_dim)
            qv, kv, vv_, dov = q_ref[:, ln], k_ref[:, ln], v_ref[:, ln], do_ref[:, ln]
            sc = lax.dot_general(qv, kv, _DIMS["nt"], preferred_element_type=F32) * scale
            p = jnp.exp(sc - jnp.max(sc, axis=-1, keepdims=True))
            p = p / jnp.sum(p, axis=-1, keepdims=True)
            dp = lax.dot_general(dov, vv_, _DIMS["nt"], preferred_element_type=F32)
            ds = (p * (dp - jnp.sum(p * dp, axis=-1, keepdims=True)) * scale).astype(MM)
            dq_ref[:, ln] = jnp.dot(ds, kv, preferred_element_type=F32).astype(dq_ref.dtype)
            dk_ref[:, ln] += lax.dot_general(ds, qv, _DIMS["tn"], preferred_element_type=F32)
            dv_ref[:, ln] += lax.dot_general(p.astype(MM), dov, _DIMS["tn"], preferred_element_type=F32)

    qspec = pl.BlockSpec((tq, d), lambda b, i: (b * nq + i, 0))
    kspec = pl.BlockSpec((m, d), lambda b, i: (b, 0))
    return pl.pallas_call(
        body, name=name, grid=(n_ex, nq), in_specs=[qspec, kspec, kspec, qspec], out_specs=[qspec, kspec, kspec],
        out_shape=[_sds((t, d), ACT), _sds((n_ex * m, d), F32), _sds((n_ex * m, d), F32)],
        compiler_params=_params(("parallel", "arbitrary"), 3 * tq * d * 2 + 2 * m * d * 2 + 2 * m * d * 4, 0,
                                8 * tq * m * 4),
    )(q, kk, vv, do)


def _ffn_up(name, y, w1, w3):
    t, d = y.shape
    f = w1.shape[-1]
    tm = _tile(t, 512)
    wspec = pl.BlockSpec((None, d, f), lambda i, j, k: (j, 0, 0))
    hspec = pl.BlockSpec((None, tm, f), lambda i, j, k: (j, i, 0))

    def epi(vals, _):
        h1, h3 = vals
        return [h1, h3, h1 * _sigmoid(h1) * h3]

    return _mm(name, "nn", (t // tm, N_CHIPS, 1), y, pl.BlockSpec((tm, d), lambda i, j, k: (i, 0)),
               [w1, w3], [wspec, wspec], [_sds((N_CHIPS, t, f), ACT)] * 3, [hspec] * 3, (tm, f), epilogue=epi)


def _ffn_down(name, g, w2):
    _, t, f = g.shape
    d = w2.shape[-1]
    tm, tn = _tile(t, 1024), _tile(d, 512)
    return _mm(name, "nn", (t // tm, d // tn, N_CHIPS), g, pl.BlockSpec((None, tm, f), lambda i, j, k: (k, i, 0)),
               [w2], [pl.BlockSpec((None, f, tn), lambda i, j, k: (k, 0, j))],
               [_sds((t, d), F32)], [pl.BlockSpec((tm, tn), lambda i, j, k: (i, j))], (tm, tn))[0]


def _ffn_down_bwd(name, dr, w2, h1, h3):
    t, d = dr.shape
    f = w2.shape[-2]
    tm = _tile(t, 512)
    hspec = pl.BlockSpec((None, tm, f), lambda i, j, k: (j, i, 0))

    def epi(vals, ex):
        dg, = vals
        h1v, h3v = ex
        sg = _sigmoid(h1v)
        return [dg * h3v * (sg * (1.0 + h1v * (1.0 - sg))), dg * h1v * sg]

    return _mm(name, "nt", (t // tm, N_CHIPS, 1), dr, pl.BlockSpec((tm, d), lambda i, j, k: (i, 0)),
               [w2], [pl.BlockSpec((None, f, d), lambda i, j, k: (j, 0, 0))],
               [_sds((N_CHIPS, t, f), ACT)] * 2, [hspec] * 2, (tm, f),
               extras=[h1, h3], extra_specs=[hspec, hspec], epilogue=epi)


def _ffn_up_bwd(name, dh, w, extras, epilogue):
    _, t, f = dh.shape
    d = w.shape[-2]
    tm, tn = _tile(t, 1024), _tile(d, 512)
    ospec = pl.BlockSpec((tm, tn), lambda i, j, k: (i, j))
    return _mm(name, "nt", (t // tm, d // tn, N_CHIPS), dh, pl.BlockSpec((None, tm, f), lambda i, j, k: (k, i, 0)),
               [w], [pl.BlockSpec((None, tn, f), lambda i, j, k: (k, j, 0))],
               [_sds((t, d), F32)], [ospec], (tm, tn),
               extras=extras, extra_specs=[ospec] * len(extras), epilogue=epilogue)[0]


def _ffn_wgrad_up(name, y, dh1, dh3):
    t, d = y.shape
    f = dh1.shape[-1]
    tm, tk = _tile(d, 512), _tile(t, 1024)
    hspec = pl.BlockSpec((None, tk, f), lambda i, j, k: (j, k, 0))
    ospec = pl.BlockSpec((None, tm, f), lambda i, j, k: (j, i, 0))
    return _mm(name, "tn", (d // tm, N_CHIPS, t // tk), y, pl.BlockSpec((tk, tm), lambda i, j, k: (k, i)),
               [dh1, dh3], [hspec, hspec], [_sds((N_CHIPS, d, f), MM)] * 2, [ospec, ospec], (tm, f))


def _ffn_wgrad_down(name, g, dr):
    _, t, f = g.shape
    d = dr.shape[1]
    tn, tk = _tile(d, 512), _tile(t, 1024)
    return _mm(name, "tn", (N_CHIPS, d // tn, t // tk), g, pl.BlockSpec((None, tk, f), lambda i, j, k: (i, k, 0)),
               [dr], [pl.BlockSpec((tk, tn), lambda i, j, k: (k, j))],
               [_sds((N_CHIPS, f, d), MM)], [pl.BlockSpec((None, f, tn), lambda i, j, k: (i, 0, j))], (f, tn))[0]


def _proj_cols(name, y, w, out_dtype):
    t, kdim = y.shape
    wd = w.shape[-1]
    tn = _tile(wd, 512)
    per = wd // tn
    tm = _tile(t, 1024)
    return _mm(name, "nn", (t // tm, N_CHIPS * per, 1), y, pl.BlockSpec((tm, kdim), lambda i, j, k: (i, 0)),
               [w], [pl.BlockSpec((None, kdim, tn), lambda i, j, k: (j // per, 0, j % per))],
               [_sds((t, N_CHIPS * wd), out_dtype)], [pl.BlockSpec((tm, tn), lambda i, j, k: (i, j))], (tm, tn))[0]


def _proj_cols_bwd(name, dh, w, extras, epilogue):
    t = dh.shape[0]
    kdim, wd = w.shape[-2], w.shape[-1]
    tm, tn = _tile(t, 1024), _tile(kdim, 512)
    ospec = pl.BlockSpec((tm, tn), lambda i, j, k: (i, j))
    return _mm(name, "nt", (t // tm, kdim // tn, N_CHIPS), dh, pl.BlockSpec((tm, wd), lambda i, j, k: (i, k)),
               [w], [pl.BlockSpec((None, tn, wd), lambda i, j, k: (k, j, 0))],
               [_sds((t, kdim), F32)], [ospec], (tm, tn),
               extras=extras, extra_specs=[ospec] * len(extras), epilogue=epilogue)[0]


def _proj_cols_wgrad(name, y, dh):
    t, kdim = y.shape
    wd = dh.shape[1] // N_CHIPS
    tm, tk = _tile(kdim, 512), _tile(t, 1024)
    return _mm(name, "tn", (kdim // tm, N_CHIPS, t // tk), y, pl.BlockSpec((tk, tm), lambda i, j, k: (k, i)),
               [dh], [pl.BlockSpec((tk, wd), lambda i, j, k: (k, j))],
               [_sds((N_CHIPS, kdim, wd), MM)], [pl.BlockSpec((None, tm, wd), lambda i, j, k: (j, i, 0))], (tm, wd))[0]


def _coords():
    return lax.axis_index("x"), lax.axis_index("y"), lax.axis_index("c")


def _chip_peers(x, y):
    return [(1 - x, y), (x, 1 - y), (1 - x, 1 - y)]


_ANY = pl.BlockSpec(memory_space=pl.ANY)


def _half(ref, c):
    h = ref.shape[0] // 2
    return ref.at[pl.ds(c * h, h)]


def _gather_two_level(srcs, lands):
    n = len(srcs)

    def body(*refs):
        ins, lz = refs[:n], refs[2 * n:3 * n]
        ici_send, ici_recv, d2d_send, d2d_recv = refs[3 * n:]
        x, y, c = _coords()
        me = 2 * x + y
        peers = _chip_peers(x, y)
        ici, d2d = [], []
        for t in range(n):
            for j, chip in enumerate(peers):
                k = 3 * t + j
                ici.append(pltpu.make_async_remote_copy(
                    src_ref=_half(ins[t], c), dst_ref=_half(lz[t].at[me], c), send_sem=ici_send.at[k], recv_sem=ici_recv.at[k],
                    device_id=(*chip, c), device_id_type=MESH))
                got = _half(lz[t].at[2 * chip[0] + chip[1]], c)
                d2d.append(pltpu.make_async_remote_copy(
                    src_ref=got, dst_ref=got, send_sem=d2d_send.at[k], recv_sem=d2d_recv.at[k],
                    device_id=(x, y, 1 - c), device_id_type=MESH))
        for cp in ici:
            cp.start()
        for cp, fw in zip(ici, d2d):
            cp.wait_recv()
            fw.start()
        for cp, fw in zip(ici, d2d):
            cp.wait_send()
            fw.wait_send()
            fw.wait_recv()

    res = pl.pallas_call(
        body, name="gather_weights", in_specs=[_ANY] * (2 * n), out_specs=[_ANY] * n,
        out_shape=[_sds(a.shape, a.dtype) for a in lands], input_output_aliases={n + i: i for i in range(n)},
        scratch_shapes=[pltpu.SemaphoreType.DMA((3 * n,))] * 4,
    )(*srcs, *lands)
    return list(res)


def _scatter_chips(parts):
    n = len(parts)

    def body(*refs):
        ins, outs = refs[:n], refs[n:2 * n]
        send_sems, recv_sems = refs[2 * n:]
        x, y, c = _coords()
        copies = []
        for t in range(n):
            for j, chip in enumerate(_chip_peers(x, y)):
                copies.append(pltpu.make_async_remote_copy(
                    src_ref=ins[t].at[2 * chip[0] + chip[1]], dst_ref=outs[t].at[j], send_sem=send_sems.at[3 * t + j],
                    recv_sem=recv_sems.at[3 * t + j], device_id=(*chip, c), device_id_type=MESH))
        for cp in copies:
            cp.start()
        for cp in copies:
            cp.wait()

    return pl.pallas_call(
        body, name="scatter_grads", in_specs=[_ANY] * n, out_specs=[_ANY] * n,
        out_shape=[_sds((3, *a.shape[1:]), a.dtype) for a in parts],
        scratch_shapes=[pltpu.SemaphoreType.DMA((3 * n,))] * 2,
    )(*parts)


def _swap_halves(grads):
    n = len(grads)

    def body(*refs):
        ins, outs = refs[:n], refs[n:2 * n]
        send_sems, recv_sems = refs[2 * n:]
        x, y, c = _coords()
        copies = []
        for t in range(n):
            h = ins[t].shape[1] // 2
            copies.append(pltpu.make_async_remote_copy(
                src_ref=ins[t].at[:, pl.ds((1 - c) * h, h)], dst_ref=outs[t], send_sem=send_sems.at[t], recv_sem=recv_sems.at[t],
                device_id=(x, y, 1 - c), device_id_type=MESH))
        for cp in copies:
            cp.start()
        for cp in copies:
            cp.wait()

    return pl.pallas_call(
        body, name="swap_grad_halves", in_specs=[_ANY] * n, out_specs=[_ANY] * n,
        out_shape=[_sds((a.shape[0], a.shape[1] // 2, a.shape[2]), a.dtype) for a in grads],
        scratch_shapes=[pltpu.SemaphoreType.DMA((n,))] * 2,
    )(*grads)


def _swap_sibling(arrs):
    n = len(arrs)

    def body(*refs):
        ins, outs = refs[:n], refs[n:2 * n]
        send_sems, recv_sems = refs[2 * n:]
        x, y, c = _coords()
        copies = []
        for t in range(n):
            cp = pltpu.make_async_remote_copy(src_ref=ins[t], dst_ref=outs[t], send_sem=send_sems.at[t],
                                              recv_sem=recv_sems.at[t], device_id=(x, y, 1 - c), device_id_type=MESH)
            cp.start()
            copies.append(cp)
        for cp in copies:
            cp.wait()

    return pl.pallas_call(
        body, name="swap_sibling", in_specs=[_ANY] * n, out_specs=[_ANY] * n,
        out_shape=[_sds(a.shape, a.dtype) for a in arrs],
        scratch_shapes=[pltpu.SemaphoreType.DMA((n,)), pltpu.SemaphoreType.DMA((n,))],
    )(*arrs)


def _gather_all(part):
    def body(in_ref, out_ref, send_sems, recv_sems, loc_sem):
        x, y, c = _coords()
        dst = out_ref.at[4 * x + 2 * y + c]
        copies = [pltpu.make_async_copy(in_ref, dst, loc_sem)]
        for r in range(1, N_DEV):
            fx, fy, fc = (r >> 2) & 1, (r >> 1) & 1, r & 1
            peer = (x ^ fx, y ^ fy, c ^ fc)
            copies.append(pltpu.make_async_remote_copy(src_ref=in_ref, dst_ref=dst, send_sem=send_sems.at[r - 1],
                                                       recv_sem=recv_sems.at[r - 1], device_id=peer, device_id_type=MESH))
        for cp in copies:
            cp.start()
        for cp in copies:
            cp.wait()

    return pl.pallas_call(
        body, name="gather_small_grads", in_specs=[_ANY], out_specs=_ANY,
        out_shape=_sds((N_DEV, *part.shape), part.dtype),
        scratch_shapes=[pltpu.SemaphoreType.DMA((N_DEV - 1,)), pltpu.SemaphoreType.DMA((N_DEV - 1,)), pltpu.SemaphoreType.DMA],
    )(part)


def _sum_chips(grad, recv, me):
    _, rr, cc = recv.shape
    tr = _tile(rr, 512)

    def body(me_ref, g_ref, r0_ref, r1_ref, r2_ref, o_ref):
        o_ref[...] = ((g_ref[...].astype(F32) + r0_ref[...].astype(F32)) + r1_ref[...].astype(F32)) + r2_ref[...].astype(F32)

    gspec = pl.BlockSpec((None, tr, cc), lambda r, m: (m[0], r, 0))
    rspecs = [pl.BlockSpec((None, tr, cc), functools.partial(lambda r, m, j: (j, r, 0), j=j)) for j in range(3)]
    return pl.pallas_call(
        body, name="sum_chip_grads",
        grid_spec=pltpu.PrefetchScalarGridSpec(
            num_scalar_prefetch=1, grid=(rr // tr,), in_specs=[gspec, *rspecs],
            out_specs=pl.BlockSpec((tr, cc), lambda r, m: (r, 0))),
        out_shape=_sds((rr, cc), F32),
        compiler_params=_params(("parallel",), 4 * tr * cc * 2 + tr * cc * 4, 0, 2 * tr * cc * 4),
    )(me, grad, recv, recv, recv)


def _sum_pair(grad, sib, core):
    k, h, cc = sib.shape
    tr = _tile(h, 512)
    nb = h // tr

    def body(c_ref, g_ref, a_ref, o_ref):
        o_ref[...] = (g_ref[...].astype(F32) + a_ref[...].astype(F32)).astype(o_ref.dtype)

    spec = pl.BlockSpec((None, tr, cc), lambda s, r, c: (s, r, 0))
    return pl.pallas_call(
        body, name="sum_core_grads",
        grid_spec=pltpu.PrefetchScalarGridSpec(
            num_scalar_prefetch=1, grid=(k, nb),
            in_specs=[pl.BlockSpec((None, tr, cc), lambda s, r, c: (s, c[0] * nb + r, 0)), spec], out_specs=spec),
        out_shape=_sds(sib.shape, sib.dtype),
        compiler_params=_params(("parallel", "parallel"), 3 * tr * cc * 2, 0, 2 * tr * cc * 4),
    )(core, grad, sib)


def _adamw_math(w, g, m, v):
    m = ADAM_B1 * m + (1.0 - ADAM_B1) * g
    v = ADAM_B2 * v + (1.0 - ADAM_B2) * (g * g)
    m_hat = m / (1.0 - ADAM_B1 ** ADAM_STEP)
    v_hat = v / (1.0 - ADAM_B2 ** ADAM_STEP)
    delta = -ADAM_LR * (m_hat / (jnp.sqrt(v_hat) + ADAM_EPS) + ADAM_WD * w)
    return delta, m, v


def _adamw(name, parts, w, m, v):
    ll, rr, cc = w.shape
    tr = _tile(rr, 256)
    npart = len(parts)

    def body(*refs):
        p_refs = refs[:npart]
        w_ref, m_ref, v_ref, g_ref, d_ref, nm_ref, nv_ref = refs[npart:]
        g = p_refs[0][...]
        for p in p_refs[1:]:
            g = g + p[...]
        d, nm, nv = _adamw_math(w_ref[...], g, m_ref[...], v_ref[...])
        g_ref[...] = g
        d_ref[...] = d
        nm_ref[...] = nm
        nv_ref[...] = nv

    spec = pl.BlockSpec((None, tr, cc), lambda l, r: (l, r, 0))
    out = _sds((ll, rr, cc), F32)
    return pl.pallas_call(
        body, name=name, grid=(ll, rr // tr), in_specs=[spec] * (npart + 3), out_specs=[spec] * 4, out_shape=[out] * 4,
        compiler_params=_params(("parallel", "parallel"), (npart + 7) * tr * cc * 4, 0, 4 * tr * cc * 4),
    )(*parts, w, m, v)


def _adamw_halves(name, s_own, s_sib, core, w, m, v):
    ll, rr, cc = w.shape
    h = rr // 2
    tr = _tile(h, 256)
    nb = h // tr

    def body(c_ref, own_ref, sib_ref, w_ref, m_ref, v_ref, g_ref, d_ref, nm_ref, nv_ref):
        g = jnp.where(pl.program_id(1) == c_ref[0], own_ref[...], sib_ref[...])
        d, nm, nv = _adamw_math(w_ref[...], g, m_ref[...], v_ref[...])
        g_ref[...] = g
        d_ref[...] = d
        nm_ref[...] = nm
        nv_ref[...] = nv

    half = pl.BlockSpec((None, tr, cc), lambda l, hf, r, c: (l, r, 0))
    full = pl.BlockSpec((None, tr, cc), lambda l, hf, r, c: (l, hf * nb + r, 0))
    out = _sds((ll, rr, cc), F32)
    return pl.pallas_call(
        body, name=name,
        grid_spec=pltpu.PrefetchScalarGridSpec(num_scalar_prefetch=1, grid=(ll, 2, nb), in_specs=[half, half, full, full, full],
                                               out_specs=[full] * 4),
        out_shape=[out] * 4,
        compiler_params=_params(("parallel", "parallel", "parallel"), 9 * tr * cc * 4, 0, 4 * tr * cc * 4),
    )(core, s_own, s_sib, w, m, v)


def _sum_devices(allparts):
    _, rr, cc = allparts.shape

    def body(p_ref, o_ref):
        s = p_ref[0]
        for k in range(1, N_DEV):
            s = s + p_ref[k]
        o_ref[...] = s

    return pl.pallas_call(
        body, name="sum_small_grads", grid=(1,), in_specs=[pl.BlockSpec((N_DEV, rr, cc), lambda i: (0, 0, 0))],
        out_specs=pl.BlockSpec((rr, cc), lambda i: (0, 0)), out_shape=_sds((rr, cc), F32),
        compiler_params=_params(("arbitrary",), 9 * rr * cc * 4),
    )(allparts)


def _pack(arrs):
    flat = jnp.concatenate([a.reshape(-1).astype(F32) for a in arrs])
    n = flat.shape[0]
    total = -(-n // 1024) * 1024
    return jnp.pad(flat, (0, total - n)).reshape(total // 128, 128)


def _unpack(block, shapes):
    flat = block.reshape(-1)
    out, off = [], 0
    for sh in shapes:
        n = math.prod(sh)
        out.append(flat[off:off + n].reshape(sh))
        off += n
    return out


_BIG = ["w_in_ab", "w_out_ab", "w_qkv_c", "w_out_c", "mem_wq", "mem_wk", "mem_wv", "mem_wo", "ffn_w1", "ffn_w3", "ffn_w2"]
_ROW_SHARDED = ("w_out_ab", "w_out_c", "mem_wq", "mem_wk", "mem_wv", "mem_wo")
_SMALL_REPL = ["gmlp_ln_g", "gmlp_ln_b", "gmlp_w_s", "gmlp_b_s", "conv_b", "conv_gn_g", "conv_gn_b"]
_SMALL_SHARD = ["conv_w", "ln_g", "ln_b"]
_NAMES = ["w_in_ab", "gmlp_ln_g", "gmlp_ln_b", "gmlp_w_s", "gmlp_b_s", "conv_w", "conv_b", "conv_gn_g", "conv_gn_b",
          "w_out_ab", "w_qkv_c", "w_out_c", "mem_wq", "mem_wk", "mem_wv", "mem_wo", "ffn_w1", "ffn_w3", "ffn_w2",
          "ln_g", "ln_b"]


def _layer_weights(l):
    mixer = ["w_in_ab", "w_out_ab"] if l % 2 == 0 else ["w_qkv_c", "w_out_c"]
    return [(n, l // 2) for n in mixer] + [(n, l) for n in _BIG if n.startswith(("mem_", "ffn_"))]


def _natural(w):
    return w.reshape(-1, w.shape[-1])


def _local_step(x, mem, target, layer_w, small):
    n_ex, s, d = x.shape
    t = n_ex * s
    x2 = x.reshape(t, d)
    mem_a = mem.reshape(-1, d).astype(ACT)
    tgt = target.reshape(t, d)
    one = jnp.ones((1, d), F32)
    zero = jnp.zeros((1, d), F32)
    ln_g, ln_b = small["ln_g"], small["ln_b"]

    def vec(a):
        return a.reshape(1, -1)

    saved = []
    xh, gp, bp = x2, one, zero
    y_act = x2.astype(ACT)
    for l in range(DEPTH):
        wts = layer_w[l]
        sv = {"y0": y_act, "w": wts}
        if l % 2 == 0:
            e = l // 2
            h = _proj_cols(f"in_ab_{l}", y_act, wts["w_in_ab"], ACT)
            gl = (vec(small["gmlp_ln_g"][e]), vec(small["gmlp_ln_b"][e]), small["gmlp_w_s"][e],
                  small["gmlp_b_s"][e].reshape(4, CHUNK, 1))
            cl = (small["conv_w"][e], vec(small["conv_b"][e]), vec(small["conv_gn_g"][e]), vec(small["conv_gn_b"][e]))
            ya = _gmlp_fwd(f"gmlp_fwd_{l}", h, *gl)
            yb = _conv_fwd(f"conv_fwd_{l}", h, n_ex, *cl)
            yab = jnp.concatenate([ya, yb], axis=1)
            mix = _dense(f"out_ab_{l}", yab, _natural(wts["w_out_ab"]), F32)
            sv.update(h=h, yab=yab, gl=gl, cl=cl)
        else:
            qkv = _proj_cols(f"qkv_{l}", y_act, wts["w_qkv_c"], ACT)
            att, ltot = _sb_fwd(f"sb_fwd_{l}", qkv, n_ex)
            att_a = att.astype(ACT)
            mix = _dense(f"out_c_{l}", att_a, _natural(wts["w_out_c"]), F32)
            sv.update(qkv=qkv, att=att_a, ltot=ltot)
        g1, b1 = vec(ln_g[l, 0]), vec(ln_b[l, 0])
        xh1, y1, rstd1 = _ln_fwd(f"ln1_fwd_{l}", xh, gp, bp, mix, g1, b1)
        q = _dense(f"mem_q_{l}", y1, _natural(wts["mem_wq"]), ACT)
        kk = _dense(f"mem_k_{l}", mem_a, _natural(wts["mem_wk"]), ACT)
        vv = _dense(f"mem_v_{l}", mem_a, _natural(wts["mem_wv"]), ACT)
        oc = _xattn_fwd(f"xattn_fwd_{l}", q, kk, vv, n_ex)
        cross = _dense(f"mem_o_{l}", oc, _natural(wts["mem_wo"]), F32)
        g2, b2 = vec(ln_g[l, 1]), vec(ln_b[l, 1])
        xh2, y2, rstd2 = _ln_fwd(f"ln2_fwd_{l}", xh1, g1, b1, cross, g2, b2)
        h1, h3, gact = _ffn_up(f"ffn_up_{l}", y2, wts["ffn_w1"], wts["ffn_w3"])
        ffo = _ffn_down(f"ffn_down_{l}", gact, wts["ffn_w2"])
        g3, b3 = vec(ln_g[l, 2]), vec(ln_b[l, 2])
        xh3, y3, rstd3 = _ln_fwd(f"ln3_fwd_{l}", xh2, g2, b2, ffo, g3, b3)
        sv.update(xh1=xh1, y1=y1, rstd1=rstd1, g1=g1, q=q, kk=kk, vv=vv, oc=oc, xh2=xh2, y2=y2, rstd2=rstd2, g2=g2,
                  h1=h1, h3=h3, gact=gact, xh3=xh3, rstd3=rstd3, g3=g3)
        saved.append(sv)
        xh, gp, bp, y_act = xh3, g3, b3, y3

    dy, loss = _loss_head(xh, gp, bp, tgt)

    sm = {n: [None] * (DEPTH // 2) for n in _SMALL_REPL + ["conv_w"]}
    d_ln_g = [[None] * 3 for _ in range(DEPTH)]
    d_ln_b = [[None] * 3 for _ in range(DEPTH)]

    def add_res(vals, ex):
        return [vals[0] + ALPHA * ex[0]]

    def add_res2(vals, ex):
        return [vals[0] + ex[0] + ALPHA * ex[1]]

    layer_g = [None] * DEPTH
    for l in reversed(range(DEPTH)):
        sv = saved[l]
        wts = sv["w"]
        big = {}
        dr3, dr3a, d_ln_g[l][2], d_ln_b[l][2] = _ln_bwd(f"ln3_bwd_{l}", dy, sv["xh3"], sv["rstd3"], sv["g3"])
        big["ffn_w2"] = _ffn_wgrad_down(f"ffn_w2_grad_{l}", sv["gact"], dr3a)
        dh1, dh3 = _ffn_down_bwd(f"ffn_down_bwd_{l}", dr3a, wts["ffn_w2"], sv["h1"], sv["h3"])
        big["ffn_w1"], big["ffn_w3"] = _ffn_wgrad_up(f"ffn_w13_grad_{l}", sv["y2"], dh1, dh3)
        part = _ffn_up_bwd(f"ffn_up_bwd1_{l}", dh1, wts["ffn_w1"], [], None)
        dy = _ffn_up_bwd(f"ffn_up_bwd3_{l}", dh3, wts["ffn_w3"], [part, dr3], add_res2)
        dr2, dr2a, d_ln_g[l][1], d_ln_b[l][1] = _ln_bwd(f"ln2_bwd_{l}", dy, sv["xh2"], sv["rstd2"], sv["g2"])
        big["mem_wo"] = _dense_tn(f"mem_wo_grad_{l}", sv["oc"], dr2a)
        doc = _dense_nt(f"mem_o_bwd_{l}", dr2a, _natural(wts["mem_wo"]), ACT)
        dq, dkk, dvv = _xattn_bwd(f"xattn_bwd_{l}", sv["q"], sv["kk"], sv["vv"], doc, n_ex)
        big["mem_wq"] = _dense_tn(f"mem_wq_grad_{l}", sv["y1"], dq)
        big["mem_wk"] = _dense_tn(f"mem_wk_grad_{l}", mem_a, dkk)
        big["mem_wv"] = _dense_tn(f"mem_wv_grad_{l}", mem_a, dvv)
        dy = _dense_nt(f"mem_q_bwd_{l}", dq, _natural(wts["mem_wq"]), F32, extras=[dr2], epilogue=add_res)
        dr1, dr1a, d_ln_g[l][0], d_ln_b[l][0] = _ln_bwd(f"ln1_bwd_{l}", dy, sv["xh1"], sv["rstd1"], sv["g1"])
        if l % 2 == 0:
            e = l // 2
            big["w_out_ab"] = _dense_tn(f"out_ab_grad_{l}", sv["yab"], dr1a)
            dyab = _dense_nt(f"out_ab_bwd_{l}", dr1a, _natural(wts["w_out_ab"]), ACT)
            duv, dgg, dgb, dws, dbs = _gmlp_bwd(f"gmlp_bwd_{l}", sv["h"], dyab, *sv["gl"])
            da, dgt, dcw, dcb, dng, dnb = _conv_bwd(f"conv_bwd_{l}", sv["h"], dyab, n_ex, *sv["cl"])
            sm["gmlp_ln_g"][e], sm["gmlp_ln_b"][e] = dgg.reshape(-1), dgb.reshape(-1)
            sm["gmlp_w_s"][e], sm["gmlp_b_s"][e] = dws, dbs.reshape(4, CHUNK)
            sm["conv_w"][e], sm["conv_b"][e] = dcw, dcb.reshape(-1)
            sm["conv_gn_g"][e], sm["conv_gn_b"][e] = dng.reshape(-1), dnb.reshape(-1)
            dh = jnp.concatenate([duv, da, dgt], axis=1)
            big["w_in_ab"] = _proj_cols_wgrad(f"in_ab_grad_{l}", sv["y0"], dh)
            dy = _proj_cols_bwd(f"in_ab_bwd_{l}", dh, wts["w_in_ab"], [dr1], add_res)
        else:
            big["w_out_c"] = _dense_tn(f"out_c_grad_{l}", sv["att"], dr1a)
            datt = _dense_nt(f"out_c_bwd_{l}", dr1a, _natural(wts["w_out_c"]), ACT)
            dq_, dk_, dv_ = _sb_bwd(f"sb_bwd_{l}", sv["qkv"], datt, sv["ltot"], n_ex)
            dqkv = jnp.concatenate([dq_, dk_.astype(ACT), dv_.astype(ACT)], axis=1)
            big["w_qkv_c"] = _proj_cols_wgrad(f"qkv_grad_{l}", sv["y0"], dqkv)
            dy = _proj_cols_bwd(f"qkv_bwd_{l}", dqkv, wts["w_qkv_c"], [dr1], add_res)
        for n in _ROW_SHARDED:
            if n in big:
                big[n] = big[n].reshape(N_CHIPS, -1, big[n].shape[-1])
        layer_g[l] = big

    grad_x = dy.reshape(n_ex, s, d)
    small_g = {n: jnp.stack(sm[n]) for n in sm}
    small_g["ln_g"] = jnp.stack([jnp.concatenate(r, axis=0) for r in d_ln_g])
    small_g["ln_b"] = jnp.stack([jnp.concatenate(r, axis=0) for r in d_ln_b])
    return loss, grad_x, layer_g, small_g


def kernel(x, mem, w_in_ab, gmlp_ln_g, gmlp_ln_b, gmlp_w_s, gmlp_b_s, conv_w, conv_b, conv_gn_g, conv_gn_b, w_out_ab, w_qkv_c, w_out_c, mem_wq, mem_wk, mem_wv, mem_wo, ffn_w1, ffn_w3, ffn_w2, ln_g, ln_b, loss_target, m_w_in_ab, m_gmlp_ln_g, m_gmlp_ln_b, m_gmlp_w_s, m_gmlp_b_s, m_conv_w, m_conv_b, m_conv_gn_g, m_conv_gn_b, m_w_out_ab, m_w_qkv_c, m_w_out_c, m_mem_wq, m_mem_wk, m_mem_wv, m_mem_wo, m_ffn_w1, m_ffn_w3, m_ffn_w2, m_ln_g, m_ln_b, v_w_in_ab, v_gmlp_ln_g, v_gmlp_ln_b, v_gmlp_w_s, v_gmlp_b_s, v_conv_w, v_conv_b, v_conv_gn_g, v_conv_gn_b, v_w_out_ab, v_w_qkv_c, v_w_out_c, v_mem_wq, v_mem_wk, v_mem_wv, v_mem_wo, v_ffn_w1, v_ffn_w3, v_ffn_w2, v_ln_g, v_ln_b):
    w = dict(w_in_ab=w_in_ab, gmlp_ln_g=gmlp_ln_g, gmlp_ln_b=gmlp_ln_b, gmlp_w_s=gmlp_w_s, gmlp_b_s=gmlp_b_s, conv_w=conv_w,
             conv_b=conv_b, conv_gn_g=conv_gn_g, conv_gn_b=conv_gn_b, w_out_ab=w_out_ab, w_qkv_c=w_qkv_c, w_out_c=w_out_c,
             mem_wq=mem_wq, mem_wk=mem_wk, mem_wv=mem_wv, mem_wo=mem_wo, ffn_w1=ffn_w1, ffn_w3=ffn_w3, ffn_w2=ffn_w2,
             ln_g=ln_g, ln_b=ln_b)
    mo = dict(w_in_ab=m_w_in_ab, gmlp_ln_g=m_gmlp_ln_g, gmlp_ln_b=m_gmlp_ln_b, gmlp_w_s=m_gmlp_w_s, gmlp_b_s=m_gmlp_b_s,
              conv_w=m_conv_w, conv_b=m_conv_b, conv_gn_g=m_conv_gn_g, conv_gn_b=m_conv_gn_b, w_out_ab=m_w_out_ab,
              w_qkv_c=m_w_qkv_c, w_out_c=m_w_out_c, mem_wq=m_mem_wq, mem_wk=m_mem_wk, mem_wv=m_mem_wv, mem_wo=m_mem_wo,
              ffn_w1=m_ffn_w1, ffn_w3=m_ffn_w3, ffn_w2=m_ffn_w2, ln_g=m_ln_g, ln_b=m_ln_b)
    vo = dict(w_in_ab=v_w_in_ab, gmlp_ln_g=v_gmlp_ln_g, gmlp_ln_b=v_gmlp_ln_b, gmlp_w_s=v_gmlp_w_s, gmlp_b_s=v_gmlp_b_s,
              conv_w=v_conv_w, conv_b=v_conv_b, conv_gn_g=v_conv_gn_g, conv_gn_b=v_conv_gn_b, w_out_ab=v_w_out_ab,
              w_qkv_c=v_w_qkv_c, w_out_c=v_w_out_c, mem_wq=v_mem_wq, mem_wk=v_mem_wk, mem_wv=v_mem_wv, mem_wo=v_mem_wo,
              ffn_w1=v_ffn_w1, ffn_w3=v_ffn_w3, ffn_w2=v_ffn_w2, ln_g=v_ln_g, ln_b=v_ln_b)
    me = (2 * lax.axis_index("x") + lax.axis_index("y")).astype(jnp.int32).reshape(1)

    per_layer = [_layer_weights(l) for l in range(DEPTH)]
    srcs = [w[n] for n in _SMALL_SHARD] + [w[n][i].astype(MM) for lw in per_layer for n, i in lw]
    lands = [lax.dynamic_update_index_in_dim(jnp.zeros((N_CHIPS, *s.shape), s.dtype), s[None], me[0], 0) for s in srcs]
    gathered = _gather_two_level(srcs, lands)
    cw_g, lg_g, lb_g = gathered[:3]
    small = {n: w[n] for n in _SMALL_REPL}
    small["conv_w"] = jnp.moveaxis(cw_g, 0, 2).reshape(cw_g.shape[1], CONV_WIDTH, -1)
    small["ln_g"] = jnp.moveaxis(lg_g, 0, 2).reshape(DEPTH, 3, -1)
    small["ln_b"] = jnp.moveaxis(lb_g, 0, 2).reshape(DEPTH, 3, -1)
    it = iter(gathered[3:])
    layer_w = [{n: next(it) for n, _ in lw} for lw in per_layer]

    loss, grad_x, layer_g, small_g = _local_step(x, mem, loss_target, layer_w, small)
    loss = lax.psum(loss, ("x", "y", "c"))

    core = lax.axis_index("c").astype(jnp.int32).reshape(1)
    grads_flat = [layer_g[l][n] for l in range(DEPTH) for n, _ in per_layer[l]]
    pair = [_sum_pair(g, a, core) for g, a in zip(grads_flat, _swap_halves(grads_flat))]
    recv = _scatter_chips(pair)
    per_name = {n: [None] * (DEPTH if n.startswith(("mem_", "ffn_")) else DEPTH // 2) for n in _BIG}
    flat_names = [ni for lw in per_layer for ni in lw]
    for (n, i), p, rc in zip(flat_names, pair, recv):
        per_name[n][i] = _sum_chips(p, rc, me)
    sums = [jnp.stack(per_name[n]) for n in _BIG]
    sib = _swap_sibling(sums)

    out = {}
    for n, s_own, s_sib in zip(_BIG, sums, sib):
        out[n] = _adamw_halves(f"adamw_{n}", s_own, s_sib, core, w[n], mo[n], vo[n])

    order = _SMALL_REPL + _SMALL_SHARD
    part = _pack([small_g[n] for n in order])
    total = _sum_devices(_gather_all(part))
    full = dict(zip(order, _unpack(total, [small_g[n].shape for n in order])))
    x_i, y_i = lax.axis_index("x"), lax.axis_index("y")
    chip = 2 * x_i + y_i
    loc = {n: full[n] for n in _SMALL_REPL}
    for n in _SMALL_SHARD:
        wd = w[n].shape[-1]
        loc[n] = lax.dynamic_slice_in_dim(full[n], chip * wd, wd, axis=full[n].ndim - 1)
    gp, wp, mp, vp = (_pack([src[n] for n in order]) for src in (loc, w, mo, vo))
    r128 = gp.shape[0]
    res = _adamw("adamw_small", [gp.reshape(1, r128, 128)], wp.reshape(1, r128, 128), mp.reshape(1, r128, 128),
                 vp.reshape(1, r128, 128))
    shapes = [w[n].shape for n in order]
    unp = [_unpack(r.reshape(r128, 128), shapes) for r in res]
    for i, n in enumerate(order):
        out[n] = tuple(u[i] for u in unp)

    grads = [out[n][0] for n in _NAMES]
    deltas = [out[n][1] for n in _NAMES]
    new_m = [out[n][2] for n in _NAMES]
    new_v = [out[n][3] for n in _NAMES]
    return (loss, grad_x, *grads, *deltas, *new_m, *new_v)
```

```python
import functools
import math

import jax
import jax.numpy as jnp
from jax import lax
from jax.experimental import pallas as pl
from jax.experimental.pallas import tpu as pltpu

F32 = jnp.float32
MM = jnp.bfloat16
ACT = jnp.bfloat16
MESH = pl.DeviceIdType.MESH

DEPTH = 4
CHUNK = 128
CONV_WIDTH = 31
HALO = 32
MEM_HEADS = 4
C_HEAD_DIM = 64
ALPHA = (2.0 * DEPTH) ** 0.25
LN_EPS = 1e-5
ADAM_LR, ADAM_B1, ADAM_B2, ADAM_EPS, ADAM_WD, ADAM_STEP = 0.001, 0.9, 0.999, 1e-08, 0.01, 10

VMEM_CAP_V7X = 64 * 1024 * 1024
VMEM_MAX_REQUEST = 56 * 1024 * 1024
N_CHIPS = 4
N_DEV = 8


def _tile(n, pref):
    if n <= pref:
        return n
    for t in range(pref - pref % 8, 7, -8):
        if n % t == 0:
            return t
    return n


def _nbytes(shape, dtype):
    return math.prod(1 if s is None else s for s in shape) * jnp.dtype(dtype).itemsize


def _vmem_limit(block_bytes, scratch_bytes=0, temp_bytes=0):
    est = 2 * block_bytes + scratch_bytes + temp_bytes
    return int(min(VMEM_MAX_REQUEST, max(16 * 1024 * 1024, est * 5 // 4)))


def _params(sem, block_bytes, scratch_bytes=0, temp_bytes=0):
    return pltpu.CompilerParams(dimension_semantics=sem,
                                vmem_limit_bytes=_vmem_limit(block_bytes, scratch_bytes, temp_bytes))


_DIMS = {"nn": (((1,), (0,)), ((), ())), "nt": (((1,), (1,)), ((), ())), "tn": (((0,), (0,)), ((), ()))}


def _mm(name, mode, grid, a, a_spec, bs, b_specs, outs, out_specs, acc_shape,
        extras=(), extra_specs=(), epilogue=None):
    nb, ne, no = len(bs), len(extras), len(outs)
    nk = grid[2]

    def body(*refs):
        a_ref = refs[0]
        b_refs = refs[1:1 + nb]
        e_refs = refs[1 + nb:1 + nb + ne]
        o_refs = refs[1 + nb + ne:1 + nb + ne + no]
        accs = refs[1 + nb + ne + no:]
        k = pl.program_id(2)

        @pl.when(k == 0)
        def _():
            for acc in accs:
                acc[...] = jnp.zeros_like(acc)

        av = a_ref[...].astype(MM)
        for b_ref, acc in zip(b_refs, accs):
            acc[...] += lax.dot_general(av, b_ref[...].astype(MM), _DIMS[mode], preferred_element_type=F32)

        @pl.when(k == nk - 1)
        def _():
            vals = [acc[...] for acc in accs]
            if epilogue is not None:
                vals = epilogue(vals, [e[...].astype(F32) for e in e_refs])
            for o, v in zip(o_refs, vals):
                o[...] = v.astype(o.dtype)

    blocks = (_nbytes(a_spec.block_shape, a.dtype)
              + sum(_nbytes(s.block_shape, b.dtype) for s, b in zip(b_specs, bs))
              + sum(_nbytes(s.block_shape, e.dtype) for s, e in zip(extra_specs, extras))
              + sum(_nbytes(s.block_shape, o.dtype) for s, o in zip(out_specs, outs)))
    acc_bytes = nb * _nbytes(acc_shape, F32)
    res = pl.pallas_call(
        body, name=name, grid=grid,
        in_specs=[a_spec, *b_specs, *extra_specs], out_specs=list(out_specs), out_shape=list(outs),
        scratch_shapes=[pltpu.VMEM(acc_shape, F32)] * nb,
        compiler_params=_params(("parallel", "parallel", "arbitrary"), blocks, acc_bytes, 4 * acc_bytes),
    )(a, *bs, *extras)
    return res


def _sds(shape, dtype):
    return jax.ShapeDtypeStruct(shape, dtype)


def _dense(name, a, w, out_dtype, extras=(), epilogue=None, n_out=None):
    t, kdim = a.shape
    n = w.shape[1]
    tm, tn, tk = _tile(t, 1024), _tile(n, 1024), _tile(kdim, 1024)
    grid = (t // tm, n // tn, kdim // tk)
    return _mm(name, "nn", grid, a, pl.BlockSpec((tm, tk), lambda i, j, k: (i, k)),
               [w], [pl.BlockSpec((tk, tn), lambda i, j, k: (k, j))],
               [_sds((t, n), out_dtype)], [pl.BlockSpec((tm, tn), lambda i, j, k: (i, j))], (tm, tn),
               extras=extras, extra_specs=[pl.BlockSpec((tm, tn), lambda i, j, k: (i, j))] * len(extras),
               epilogue=epilogue)[0]


def _dense_nt(name, a, w, out_dtype, extras=(), epilogue=None):
    t, n = a.shape
    kout = w.shape[0]
    tm, tn, tk = _tile(t, 1024), _tile(kout, 1024), _tile(n, 1024)
    grid = (t // tm, kout // tn, n // tk)
    return _mm(name, "nt", grid, a, pl.BlockSpec((tm, tk), lambda i, j, k: (i, k)),
               [w], [pl.BlockSpec((tn, tk), lambda i, j, k: (j, k))],
               [_sds((t, kout), out_dtype)], [pl.BlockSpec((tm, tn), lambda i, j, k: (i, j))], (tm, tn),
               extras=extras, extra_specs=[pl.BlockSpec((tm, tn), lambda i, j, k: (i, j))] * len(extras),
               epilogue=epilogue)[0]


def _dense_tn(name, a, b, out_dtype=MM):
    t, m = a.shape
    n = b.shape[1]
    tm, tn, tk = _tile(m, 1024), _tile(n, 1024), _tile(t, 1024)
    grid = (m // tm, n // tn, t // tk)
    return _mm(name, "tn", grid, a, pl.BlockSpec((tk, tm), lambda i, j, k: (k, i)),
               [b], [pl.BlockSpec((tk, tn), lambda i, j, k: (k, j))],
               [_sds((m, n), out_dtype)], [pl.BlockSpec((tm, tn), lambda i, j, k: (i, j))], (tm, tn))[0]


_INV_SQRT2 = 0.7071067811865476
_INV_SQRT_2PI = 0.3989422804014327


def _gelu(x):
    return 0.5 * x * (1.0 + lax.erf(x * _INV_SQRT2))


def _gelu_grad(x):
    return 0.5 * (1.0 + lax.erf(x * _INV_SQRT2)) + x * jnp.exp(-0.5 * x * x) * _INV_SQRT_2PI


def _sigmoid(x):
    return 1.0 / (1.0 + jnp.exp(-x))


def _norm_stats(x):
    mu = jnp.mean(x, axis=-1, keepdims=True)
    xc = x - mu
    var = jnp.mean(xc * xc, axis=-1, keepdims=True)
    rstd = lax.rsqrt(var + LN_EPS)
    return xc * rstd, rstd


def _norm_bwd(dy_g, xh, rstd):
    m1 = jnp.mean(dy_g, axis=-1, keepdims=True)
    m2 = jnp.mean(dy_g * xh, axis=-1, keepdims=True)
    return rstd * (dy_g - m1 - xh * m2)


def _rows8(x):
    r, c = x.shape
    return jnp.sum(x.reshape(r // 8, 8, c), axis=0)


def _ln_fwd(name, xh_prev, g_prev, b_prev, f, g, b):
    t, d = f.shape
    tm = _tile(t, 512)

    def body(xp_ref, gp_ref, bp_ref, f_ref, g_ref, b_ref, xh_ref, y_ref, rstd_ref):
        r = ALPHA * (xp_ref[...] * gp_ref[...] + bp_ref[...]) + f_ref[...]
        xh, rstd = _norm_stats(r)
        xh_ref[...] = xh
        y_ref[...] = (xh * g_ref[...] + b_ref[...]).astype(y_ref.dtype)
        rstd_ref[...] = rstd

    row = pl.BlockSpec((tm, d), lambda i: (i, 0))
    vec = pl.BlockSpec((1, d), lambda i: (0, 0))
    return pl.pallas_call(
        body, name=name, grid=(t // tm,),
        in_specs=[row, vec, vec, row, vec, vec],
        out_specs=[row, row, pl.BlockSpec((tm, 1), lambda i: (i, 0))],
        out_shape=[_sds((t, d), F32), _sds((t, d), ACT), _sds((t, 1), F32)],
        compiler_params=_params(("parallel",), 4 * tm * d * 4, 0, 4 * tm * d * 4),
    )(xh_prev, g_prev, b_prev, f, g, b)


def _ln_bwd(name, dy, xh, rstd, g):
    t, d = dy.shape
    tm = _tile(t, 512)
    n = t // tm

    def body(dy_ref, xh_ref, rstd_ref, g_ref, dr_ref, dra_ref, dg_ref, db_ref, dg_acc, db_acc):
        i = pl.program_id(0)

        @pl.when(i == 0)
        def _():
            dg_acc[...] = jnp.zeros_like(dg_acc)
            db_acc[...] = jnp.zeros_like(db_acc)

        dyv = dy_ref[...]
        xhv = xh_ref[...]
        dr = _norm_bwd(dyv * g_ref[...], xhv, rstd_ref[...])
        dr_ref[...] = dr
        dra_ref[...] = dr.astype(dra_ref.dtype)
        dg_acc[...] += _rows8(dyv * xhv)
        db_acc[...] += _rows8(dyv)

        @pl.when(i == n - 1)
        def _():
            dg_ref[...] = jnp.sum(dg_acc[...], axis=0, keepdims=True)
            db_ref[...] = jnp.sum(db_acc[...], axis=0, keepdims=True)

    row = pl.BlockSpec((tm, d), lambda i: (i, 0))
    vec = pl.BlockSpec((1, d), lambda i: (0, 0))
    return pl.pallas_call(
        body, name=name, grid=(n,),
        in_specs=[row, row, pl.BlockSpec((tm, 1), lambda i: (i, 0)), vec],
        out_specs=[row, row, vec, vec],
        out_shape=[_sds((t, d), F32), _sds((t, d), ACT), _sds((1, d), F32), _sds((1, d), F32)],
        scratch_shapes=[pltpu.VMEM((8, d), F32), pltpu.VMEM((8, d), F32)],
        compiler_params=_params(("arbitrary",), 4 * tm * d * 4, 0, 4 * tm * d * 4),
    )(dy, xh, rstd, g)


def _loss_head(xh, g, b, target):
    t, d = xh.shape
    tm = _tile(t, 512)
    n = t // tm

    def body(xh_ref, g_ref, b_ref, tg_ref, dy_ref, loss_ref, acc):
        i = pl.program_id(0)

        @pl.when(i == 0)
        def _():
            acc[...] = jnp.zeros_like(acc)

        err = xh_ref[...] * g_ref[...] + b_ref[...] - tg_ref[...]
        dy_ref[...] = err * (1.0 / d)
        acc[...] += _rows8(err * err)

        @pl.when(i == n - 1)
        def _():
            s = jnp.sum(jnp.sum(acc[...], axis=0, keepdims=True), axis=1, keepdims=True)
            loss_ref[...] = jnp.broadcast_to(s * (0.5 / d), loss_ref.shape)

    row = pl.BlockSpec((tm, d), lambda i: (i, 0))
    vec = pl.BlockSpec((1, d), lambda i: (0, 0))
    dy, loss = pl.pallas_call(
        body, name="loss_head", grid=(n,),
        in_specs=[row, vec, vec, row],
        out_specs=[row, pl.BlockSpec((8, 128), lambda i: (0, 0))],
        out_shape=[_sds((t, d), F32), _sds((8, 128), F32)],
        scratch_shapes=[pltpu.VMEM((8, d), F32)],
        compiler_params=_params(("arbitrary",), 3 * tm * d * 4, 0, 2 * tm * d * 4),
    )(xh, g, b, target)
    return dy, loss[0, 0]


def _causal_w(w):
    r = lax.broadcasted_iota(jnp.int32, w.shape, 0)
    c = lax.broadcasted_iota(jnp.int32, w.shape, 1)
    return jnp.where(r >= c, w, 0.0)


def _gmlp_fwd(name, h, ln_g, ln_b, w_s, b_s_col):
    t = h.shape[0]
    tt = _tile(t, 256)
    wd = 4 * CHUNK

    def body(u_ref, v_ref, g_ref, b_ref, w_ref, bs_ref, ya_ref):
        for gi in range(4):
            ln = slice(gi * CHUNK, (gi + 1) * CHUNK)
            u = _gelu(u_ref[:, ln].astype(F32))
            v = _gelu(v_ref[:, ln].astype(F32))
            xh, _ = _norm_stats(v)
            vg = (xh * g_ref[:, ln] + b_ref[:, ln]).astype(MM)
            w = _causal_w(w_ref[gi]).astype(MM)
            for c in range(tt // CHUNK):
                rs = slice(c * CHUNK, (c + 1) * CHUNK)
                mixed = jnp.dot(w, vg[rs], preferred_element_type=F32) + bs_ref[gi]
                ya_ref[rs, ln] = (u[rs] * mixed).astype(ya_ref.dtype)

    vec = pl.BlockSpec((1, wd), lambda i: (0, 0))
    return pl.pallas_call(
        body, name=name, grid=(t // tt,),
        in_specs=[pl.BlockSpec((tt, wd), lambda i: (i, 0)), pl.BlockSpec((tt, wd), lambda i: (i, 1)), vec, vec,
                  pl.BlockSpec((4, CHUNK, CHUNK), lambda i: (0, 0, 0)), pl.BlockSpec((4, CHUNK, 1), lambda i: (0, 0, 0))],
        out_specs=pl.BlockSpec((tt, wd), lambda i: (i, 0)),
        out_shape=_sds((t, wd), ACT),
        compiler_params=_params(("parallel",), 3 * tt * wd * 4, 0, 8 * tt * CHUNK * 4),
    )(h, h, ln_g, ln_b, w_s, b_s_col)


def _gmlp_bwd(name, h, dyab, ln_g, ln_b, w_s, b_s_col):
    t = h.shape[0]
    tt = _tile(t, 256)
    n = t // tt
    wd = 4 * CHUNK

    def body(u_ref, v_ref, dy_ref, g_ref, b_ref, w_ref, bs_ref, duv_ref, dg_ref, db_ref, dw_ref, dbs_ref,
             dg_acc, db_acc):
        i = pl.program_id(0)

        @pl.when(i == 0)
        def _():
            dg_acc[...] = jnp.zeros_like(dg_acc)
            db_acc[...] = jnp.zeros_like(db_acc)
            dw_ref[...] = jnp.zeros_like(dw_ref)
            dbs_ref[...] = jnp.zeros_like(dbs_ref)

        for gi in range(4):
            ln = slice(gi * CHUNK, (gi + 1) * CHUNK)
            upre = u_ref[:, ln].astype(F32)
            vpre = v_ref[:, ln].astype(F32)
            u = _gelu(upre)
            v = _gelu(vpre)
            xh, rstd = _norm_stats(v)
            gv = g_ref[:, ln]
            vg = (xh * gv + b_ref[:, ln]).astype(MM)
            w = _causal_w(w_ref[gi]).astype(MM)
            dya = dy_ref[:, ln].astype(F32)
            dmixed = dya * u
            dmm = dmixed.astype(MM)
            dvg_parts, mixed_parts = [], []
            dw = jnp.zeros((CHUNK, CHUNK), F32)
            dbs = jnp.zeros((CHUNK, 1), F32)
            for c in range(tt // CHUNK):
                rs = slice(c * CHUNK, (c + 1) * CHUNK)
                mixed_parts.append(jnp.dot(w, vg[rs], preferred_element_type=F32) + bs_ref[gi])
                dw = dw + lax.dot_general(dmm[rs], vg[rs], _DIMS["nt"], preferred_element_type=F32)
                dbs = dbs + jnp.sum(dmixed[rs], axis=1, keepdims=True)
                dvg_parts.append(lax.dot_general(w, dmm[rs], _DIMS["tn"], preferred_element_type=F32))
            mixed = jnp.concatenate(mixed_parts, axis=0)
            dvg = jnp.concatenate(dvg_parts, axis=0)
            dw_ref[gi] += _causal_w(dw)
            dbs_ref[gi] += dbs
            dg_acc[:, ln] += _rows8(dvg * xh)
            db_acc[:, ln] += _rows8(dvg)
            dv = _norm_bwd(dvg * gv, xh, rstd) * _gelu_grad(vpre)
            du = dya * mixed * _gelu_grad(upre)
            duv_ref[:, ln] = du.astype(duv_ref.dtype)
            duv_ref[:, wd + gi * CHUNK: wd + (gi + 1) * CHUNK] = dv.astype(duv_ref.dtype)

        @pl.when(i == n - 1)
        def _():
            dg_ref[...] = jnp.sum(dg_acc[...], axis=0, keepdims=True)
            db_ref[...] = jnp.sum(db_acc[...], axis=0, keepdims=True)

    vec = pl.BlockSpec((1, wd), lambda i: (0, 0))
    wspec = pl.BlockSpec((4, CHUNK, CHUNK), lambda i: (0, 0, 0))
    bspec = pl.BlockSpec((4, CHUNK, 1), lambda i: (0, 0, 0))
    return pl.pallas_call(
        body, name=name, grid=(n,),
        in_specs=[pl.BlockSpec((tt, wd), lambda i: (i, 0)), pl.BlockSpec((tt, wd), lambda i: (i, 1)),
                  pl.BlockSpec((tt, wd), lambda i: (i, 0)), vec, vec, wspec, bspec],
        out_specs=[pl.BlockSpec((tt, 2 * wd), lambda i: (i, 0)), vec, vec, wspec, bspec],
        out_shape=[_sds((t, 2 * wd), ACT), _sds((1, wd), F32), _sds((1, wd), F32),
                   _sds((4, CHUNK, CHUNK), F32), _sds((4, CHUNK, 1), F32)],
        scratch_shapes=[pltpu.VMEM((8, wd), F32), pltpu.VMEM((8, wd), F32)],
        compiler_params=_params(("arbitrary",), 5 * tt * wd * 4, 0, 16 * tt * CHUNK * 4),
    )(h, h, dyab, ln_g, ln_b, w_s, b_s_col)


_ROWS = 256


def _conv_taps(win, cw, lo):
    acc = jnp.zeros((_ROWS, CHUNK), F32)
    for w in range(CONV_WIDTH):
        s = lo(w)
        acc = acc + cw[w:w + 1, :] * win[s:s + _ROWS, :]
    return acc


def _conv_fwd(name, h, n_ex, cw, cb, gg, gb):
    t = h.shape[0]
    s = t // n_ex
    nt = s // _ROWS

    def body(a_ref, gt_ref, cw_ref, cb_ref, gg_ref, gb_ref, yb_ref, hh):
        hh[0:HALO, :] = jnp.zeros((HALO, CHUNK), F32)
        hh[HALO:HALO + s, :] = a_ref[...].astype(F32) * _sigmoid(gt_ref[...].astype(F32))
        cwv = cw_ref[...]

        def tile(i, carry):
            r0 = pl.multiple_of(i * _ROWS, _ROWS)
            win = hh[pl.ds(r0, _ROWS + HALO), :]
            c = _conv_taps(win, cwv, lambda w: w + HALO - (CONV_WIDTH - 1)) + cb_ref[...]
            xh, _ = _norm_stats(c)
            hg = xh * gg_ref[...] + gb_ref[...]
            yb_ref[pl.ds(r0, _ROWS), :] = (hg * _sigmoid(hg)).astype(yb_ref.dtype)
            return carry

        lax.fori_loop(0, nt, tile, 0)

    vec = pl.BlockSpec((1, CHUNK), lambda g, b: (0, g))
    return pl.pallas_call(
        body, name=name, grid=(4, n_ex),
        in_specs=[pl.BlockSpec((s, CHUNK), lambda g, b: (b, 8 + g)), pl.BlockSpec((s, CHUNK), lambda g, b: (b, 12 + g)),
                  pl.BlockSpec((CONV_WIDTH, CHUNK), lambda g, b: (0, g)), vec, vec, vec],
        out_specs=pl.BlockSpec((s, CHUNK), lambda g, b: (b, g)),
        out_shape=_sds((t, 4 * CHUNK), ACT),
        scratch_shapes=[pltpu.VMEM((s + HALO, CHUNK), F32)],
        compiler_params=_params(("parallel", "parallel"), 3 * s * CHUNK * 4, (s + HALO) * CHUNK * 4, 4 * s * CHUNK * 4),
    )(h, h, cw, cb, gg, gb)


def _conv_bwd(name, h, dyab, n_ex, cw, cb, gg, gb):
    t = h.shape[0]
    s = t // n_ex
    nt = s // _ROWS

    def body(a_ref, gt_ref, dy_ref, cw_ref, cb_ref, gg_ref, gb_ref,
             da_ref, dgt_ref, dcw_ref, dcb_ref, dgg_ref, dgb_ref, hh, dcs, acc):
        b = pl.program_id(1)

        @pl.when(b == 0)
        def _():
            dcw_ref[...] = jnp.zeros_like(dcw_ref)
            dcb_ref[...] = jnp.zeros_like(dcb_ref)
            dgg_ref[...] = jnp.zeros_like(dgg_ref)
            dgb_ref[...] = jnp.zeros_like(dgb_ref)

        hh[0:HALO, :] = jnp.zeros((HALO, CHUNK), F32)
        hh[HALO:HALO + s, :] = a_ref[...].astype(F32) * _sigmoid(gt_ref[...].astype(F32))
        dcs[s:s + HALO, :] = jnp.zeros((HALO, CHUNK), F32)
        acc[...] = jnp.zeros_like(acc)
        cwv = cw_ref[...]
        off = HALO - (CONV_WIDTH - 1)

        def tile1(i, carry):
            r0 = pl.multiple_of(i * _ROWS, _ROWS)
            win = hh[pl.ds(r0, _ROWS + HALO), :]
            c = _conv_taps(win, cwv, lambda w: w + off) + cb_ref[...]
            xh, rstd = _norm_stats(c)
            hg = xh * gg_ref[...] + gb_ref[...]
            sg = _sigmoid(hg)
            dhg = dy_ref[pl.ds(r0, _ROWS), :].astype(F32) * (sg * (1.0 + hg * (1.0 - sg)))
            acc[32:40, :] += _rows8(dhg * xh)
            acc[40:48, :] += _rows8(dhg)
            dc = _norm_bwd(dhg * gg_ref[...], xh, rstd)
            dcs[pl.ds(r0, _ROWS), :] = dc
            acc[48:56, :] += _rows8(dc)
            for w in range(CONV_WIDTH):
                acc[w:w + 1, :] += jnp.sum(dc * win[w + off:w + off + _ROWS, :], axis=0, keepdims=True)
            return carry

        lax.fori_loop(0, nt, tile1, 0)

        def tile2(i, carry):
            r0 = pl.multiple_of(i * _ROWS, _ROWS)
            win = dcs[pl.ds(r0, _ROWS + HALO), :]
            dhh = _conv_taps(win, cwv, lambda w: CONV_WIDTH - 1 - w)
            av = a_ref[pl.ds(r0, _ROWS), :].astype(F32)
            sg = _sigmoid(gt_ref[pl.ds(r0, _ROWS), :].astype(F32))
            da_ref[pl.ds(r0, _ROWS), :] = (dhh * sg).astype(da_ref.dtype)
            dgt_ref[pl.ds(r0, _ROWS), :] = (dhh * av * sg * (1.0 - sg)).astype(dgt_ref.dtype)
            return carry

        lax.fori_loop(0, nt, tile2, 0)
        dcw_ref[...] += acc[0:CONV_WIDTH, :]
        dgg_ref[...] += jnp.sum(acc[32:40, :], axis=0, keepdims=True)
        dgb_ref[...] += jnp.sum(acc[40:48, :], axis=0, keepdims=True)
        dcb_ref[...] += jnp.sum(acc[48:56, :], axis=0, keepdims=True)

    vec = pl.BlockSpec((1, CHUNK), lambda g, b: (0, g))
    tap = pl.BlockSpec((CONV_WIDTH, CHUNK), lambda g, b: (0, g))
    seq = pl.BlockSpec((s, CHUNK), lambda g, b: (b, g))
    return pl.pallas_call(
        body, name=name, grid=(4, n_ex),
        in_specs=[pl.BlockSpec((s, CHUNK), lambda g, b: (b, 8 + g)), pl.BlockSpec((s, CHUNK), lambda g, b: (b, 12 + g)),
                  pl.BlockSpec((s, CHUNK), lambda g, b: (b, 4 + g)), tap, vec, vec, vec],
        out_specs=[seq, seq, tap, vec, vec, vec],
        out_shape=[_sds((t, 4 * CHUNK), ACT), _sds((t, 4 * CHUNK), ACT), _sds((CONV_WIDTH, 4 * CHUNK), F32),
                   _sds((1, 4 * CHUNK), F32), _sds((1, 4 * CHUNK), F32), _sds((1, 4 * CHUNK), F32)],
        scratch_shapes=[pltpu.VMEM((s + HALO, CHUNK), F32), pltpu.VMEM((s + HALO, CHUNK), F32),
                        pltpu.VMEM((56, CHUNK), F32)],
        compiler_params=_params(("parallel", "arbitrary"), 5 * s * CHUNK * 4, 2 * (s + HALO) * CHUNK * 4, 4 * s * CHUNK * 4),
    )(h, h, dyab, cw, cb, gg, gb)


_TQ = 256
_SB_DEAD = -110.0


def _tri(kind):
    r = lax.broadcasted_iota(jnp.int32, (_TQ, _TQ), 0)
    c = lax.broadcasted_iota(jnp.int32, (_TQ, _TQ), 1)
    m = {"gt": r > c, "le": r <= c, "lt": r < c}[kind]
    return jnp.where(m, 1.0, 0.0).astype(jnp.bfloat16)


def _split_dot(x, tri2):
    hi = x.astype(jnp.bfloat16)
    lo = (x - hi.astype(F32)).astype(jnp.bfloat16)
    return jnp.dot(jnp.concatenate([hi, lo], axis=1), tri2, preferred_element_type=F32)


def _neg_abs(x):
    bits = lax.bitcast_convert_type(x, jnp.uint32) | jnp.uint32(0x80000000)
    return lax.bitcast_convert_type(bits, F32)


def _log_not_beta(nz):
    return jnp.minimum(nz, 0.0) - jnp.log(1.0 + jnp.exp(_neg_abs(nz)))


def _sb_fwd(name, qkv, n_ex):
    t = qkv.shape[0]
    d = qkv.shape[1] // 3
    npair = d // CHUNK
    s = t // n_ex
    nq = s // _TQ
    neg_a = -(C_HEAD_DIM ** -0.5)

    def body(q_ref, k_ref, v_ref, o_ref, lt_ref, o_acc, c_acc):
        i = pl.program_id(2)
        first = lax.broadcasted_iota(jnp.int32, (_TQ, CHUNK), 1) < C_HEAD_DIM
        q2 = q_ref[...]
        zero = jnp.zeros_like(q2)
        qh = [jnp.where(first, q2, zero), jnp.where(first, zero, q2)]
        tri2 = jnp.concatenate([_tri("gt")] * 2, axis=0)
        o_acc[...] = jnp.zeros_like(o_acc)
        c_acc[...] = jnp.zeros_like(c_acc)

        def tiles(js, mask):
            work = [(a, hd) for a in range(len(js)) for hd in range(2)]
            rows = [pl.ds(pl.multiple_of(j * _TQ, _TQ), _TQ) for j in js]
            kts = [k_ref[r, :] for r in rows]
            vts = [v_ref[r, :] for r in rows]
            nzs = {w: lax.dot_general(qh[w[1]], kts[w[0]], _DIMS["nt"], preferred_element_type=F32) * neg_a for w in work}
            lns = {w: _log_not_beta(nzs[w]) for w in work}
            if mask is not None:
                lns = {w: jnp.where(mask, lns[w], 0.0) for w in work}
            locs = {w: _split_dot(lns[w], tri2) for w in work}
            laters = {}
            for hd in range(2):
                carry = c_acc[hd]
                for a in range(len(js)):
                    laters[a, hd] = carry + locs[a, hd]
                    carry = laters[a, hd][:, 0:1] + lns[a, hd][:, 0:1]
                c_acc[hd] = carry
            atts = {w: jnp.exp(lns[w] - nzs[w] + laters[w]) for w in work}
            if mask is not None:
                atts = {w: jnp.where(mask, atts[w], 0.0) for w in work}
            for hd in range(2):
                acc = o_acc[hd]
                for a in range(len(js)):
                    acc = acc + jnp.dot(atts[a, hd].astype(MM), vts[a], preferred_element_type=F32)
                o_acc[hd] = acc

        tiles([i], lax.broadcasted_iota(jnp.int32, (_TQ, _TQ), 1) < lax.broadcasted_iota(jnp.int32, (_TQ, _TQ), 0))

        def alive():
            return jnp.max(jnp.maximum(c_acc[0], c_acc[1])) >= _SB_DEAD

        def cond(st):
            return jnp.logical_and(st[0] >= 0, st[1])

        def step(st):
            tiles([st[0]], None)
            return st[0] - 1, alive()

        j_last, _ = lax.while_loop(cond, step, (i - 1, alive()))
        o_ref[...] = jnp.where(first, o_acc[0], o_acc[1])
        lt_ref[:, 0:1] = c_acc[0]
        lt_ref[:, 1:2] = c_acc[1]
        lt_ref[:, 2:3] = jnp.full((_TQ, 1), j_last + 1, jnp.int32).astype(F32)

    return pl.pallas_call(
        body, name=name, grid=(n_ex, npair, nq),
        in_specs=[pl.BlockSpec((_TQ, CHUNK), lambda b, p, i: (b * nq + i, p)),
                  pl.BlockSpec((s, CHUNK), lambda b, p, i: (b, npair + p)),
                  pl.BlockSpec((s, CHUNK), lambda b, p, i: (b, 2 * npair + p))],
        out_specs=[pl.BlockSpec((_TQ, CHUNK), lambda b, p, i: (b * nq + i, p)),
                   pl.BlockSpec((None, _TQ, 3), lambda b, p, i: (p, b * nq + i, 0))],
        out_shape=[_sds((t, d), F32), _sds((npair, t, 3), F32)],
        scratch_shapes=[pltpu.VMEM((2, _TQ, CHUNK), F32), pltpu.VMEM((2, _TQ, 1), F32)],
        compiler_params=_params(("parallel", "parallel", "arbitrary"), 2 * s * CHUNK * 2 + 4 * _TQ * CHUNK * 4,
                                4 * _TQ * CHUNK * 4, 24 * _TQ * _TQ * 4),
    )(qkv, qkv, qkv)


def _sb_bwd(name, qkv, do, ltot, n_ex):
    t = qkv.shape[0]
    d = qkv.shape[1] // 3
    npair = d // CHUNK
    s = t // n_ex
    nq = s // _TQ
    scale = C_HEAD_DIM ** -0.5
    neg_a = -scale

    def body(q_ref, k_ref, v_ref, do_ref, lt_ref, dq_ref, dk_ref, dv_ref, dq_acc, cp_acc, cg_acc, dk_acc, dv_acc):
        i = pl.program_id(2)

        @pl.when(i == 0)
        def _():
            dk_acc[...] = jnp.zeros_like(dk_acc)
            dv_acc[...] = jnp.zeros_like(dv_acc)

        first = lax.broadcasted_iota(jnp.int32, (_TQ, CHUNK), 1) < C_HEAD_DIM
        q2 = q_ref[...]
        do2 = do_ref[...]
        qs = (q2 * scale).astype(q2.dtype)
        zero = jnp.zeros_like(q2)
        qh = [jnp.where(first, q2, zero), jnp.where(first, zero, q2)]
        doh = [jnp.where(first, do2, zero), jnp.where(first, zero, do2)]
        lt = [lt_ref[:, 0:1], lt_ref[:, 1:2]]
        tri2_le = jnp.concatenate([_tri("le")] * 2, axis=0)
        tri_lt = _tri("lt").astype(MM)
        dq_acc[...] = jnp.zeros_like(dq_acc)
        cp_acc[...] = jnp.zeros_like(cp_acc)
        cg_acc[...] = jnp.zeros_like(cg_acc)

        def tiles(js, mask):
            na = len(js)
            work = [(a, hd) for a in range(na) for hd in range(2)]
            last = slice(_TQ - 1, _TQ)
            rows = [pl.ds(pl.multiple_of(j * _TQ, _TQ), _TQ) for j in js]
            kts = [k_ref[r, :] for r in rows]
            vts = [v_ref[r, :] for r in rows]
            ksc = [(kt * scale).astype(kt.dtype) for kt in kts]
            nzs = {w: lax.dot_general(qh[w[1]], kts[w[0]], _DIMS["nt"], preferred_element_type=F32) * neg_a for w in work}
            datts = {w: lax.dot_general(doh[w[1]], vts[w[0]], _DIMS["nt"], preferred_element_type=F32) for w in work}
            lns = {w: _log_not_beta(nzs[w]) for w in work}
            if mask is not None:
                lns = {w: jnp.where(mask, lns[w], 0.0) for w in work}
            pins = {w: _split_dot(lns[w], tri2_le) for w in work}
            lss = {w: lns[w] - nzs[w] for w in work}
            atts = {}
            for hd in range(2):
                cp = cp_acc[hd]
                for a in range(na):
                    atts[a, hd] = jnp.exp(lss[a, hd] + ((lt[hd] - cp) - pins[a, hd]))
                    cp = cp + pins[a, hd][:, last]
                cp_acc[hd] = cp
            if mask is not None:
                atts = {w: jnp.where(mask, atts[w], 0.0) for w in work}
            gs = {w: datts[w] * atts[w] for w in work}
            locg = {w: jnp.dot(gs[w].astype(MM), tri_lt, preferred_element_type=F32) for w in work}
            dzs = {}
            for hd in range(2):
                carry = cg_acc[hd]
                for a in range(na):
                    big = carry + locg[a, hd]
                    dzs[a, hd] = gs[a, hd] - (gs[a, hd] + big) * jnp.exp(lss[a, hd])
                    carry = big[:, last] + gs[a, hd][:, last]
                cg_acc[hd] = carry
            if mask is not None:
                dzs = {w: jnp.where(mask, dzs[w], 0.0) for w in work}
            dzs = {w: dzs[w].astype(MM) for w in work}
            attm = {w: atts[w].astype(MM) for w in work}
            for hd in range(2):
                acc = dq_acc[hd]
                for a in range(na):
                    acc = acc + jnp.dot(dzs[a, hd], ksc[a], preferred_element_type=F32)
                dq_acc[hd] = acc
            for a in range(na):
                dk0, dk1 = [lax.dot_general(dzs[a, hd], qs, _DIMS["tn"], preferred_element_type=F32) for hd in range(2)]
                dv0, dv1 = [lax.dot_general(attm[a, hd], do2, _DIMS["tn"], preferred_element_type=F32) for hd in range(2)]
                dk_acc[rows[a], :] += jnp.where(first, dk0, dk1)
                dv_acc[rows[a], :] += jnp.where(first, dv0, dv1)

        def single(j, carry):
            tiles([j], None)
            return carry

        j_first = jnp.clip(jnp.max(lt_ref[:, 2:3]).astype(jnp.int32), 0, i)
        lax.fori_loop(j_first, i, single, 0)
        tiles([i], lax.broadcasted_iota(jnp.int32, (_TQ, _TQ), 1) < lax.broadcasted_iota(jnp.int32, (_TQ, _TQ), 0))
        dq_ref[...] = jnp.where(first, dq_acc[0], dq_acc[1]).astype(dq_ref.dtype)

        @pl.when(i == nq - 1)
        def _():
            dk_ref[...] = dk_acc[...].astype(dk_ref.dtype)
            dv_ref[...] = dv_acc[...].astype(dv_ref.dtype)

    qspec = pl.BlockSpec((_TQ, CHUNK), lambda b, p, i: (b * nq + i, p))
    kv_out = pl.BlockSpec((s, CHUNK), lambda b, p, i: (b, p))
    return pl.pallas_call(
        body, name=name, grid=(n_ex, npair, nq),
        in_specs=[qspec, pl.BlockSpec((s, CHUNK), lambda b, p, i: (b, npair + p)),
                  pl.BlockSpec((s, CHUNK), lambda b, p, i: (b, 2 * npair + p)), qspec,
                  pl.BlockSpec((None, _TQ, 3), lambda b, p, i: (p, b * nq + i, 0))],
        out_specs=[qspec, kv_out, kv_out],
        out_shape=[_sds((t, d), ACT)] * 3,
        scratch_shapes=[pltpu.VMEM((2, _TQ, CHUNK), F32), pltpu.VMEM((2, _TQ, 1), F32), pltpu.VMEM((2, _TQ, 1), F32),
                        pltpu.VMEM((s, CHUNK), F32), pltpu.VMEM((s, CHUNK), F32)],
        compiler_params=_params(("parallel", "parallel", "arbitrary"), 4 * s * CHUNK * 2,
                                4 * _TQ * CHUNK * 4 + 2 * s * CHUNK * 4, 32 * _TQ * _TQ * 4),
    )(qkv, qkv, qkv, do, ltot)


def _xattn_fwd(name, q, kk, vv, n_ex):
    t, d = q.shape
    m = kk.shape[0] // n_ex
    s = t // n_ex
    tq = _tile(s, 512)
    nq = s // tq
    hd_dim = d // MEM_HEADS
    scale = hd_dim ** -0.5

    def body(q_ref, k_ref, v_ref, o_ref):
        for hd in range(MEM_HEADS):
            ln = slice(hd * hd_dim, (hd + 1) * hd_dim)
            sc = lax.dot_general(q_ref[:, ln], k_ref[:, ln], _DIMS["nt"], preferred_element_type=F32) * scale
            p = jnp.exp(sc - jnp.max(sc, axis=-1, keepdims=True))
            p = p / jnp.sum(p, axis=-1, keepdims=True)
            o_ref[:, ln] = jnp.dot(p.astype(MM), v_ref[:, ln], preferred_element_type=F32).astype(o_ref.dtype)

    qspec = pl.BlockSpec((tq, d), lambda b, i: (b * nq + i, 0))
    kspec = pl.BlockSpec((m, d), lambda b, i: (b, 0))
    return pl.pallas_call(
        body, name=name, grid=(n_ex, nq), in_specs=[qspec, kspec, kspec], out_specs=qspec,
        out_shape=_sds((t, d), ACT),
        compiler_params=_params(("parallel", "parallel"), 2 * tq * d * 2 + 2 * m * d * 2, 0, 6 * tq * m * 4),
    )(q, kk, vv)


def _xattn_bwd(name, q, kk, vv, do, n_ex):
    t, d = q.shape
    m = kk.shape[0] // n_ex
    s = t // n_ex
    tq = _tile(s, 512)
    nq = s // tq
    hd_dim = d // MEM_HEADS
    scale = hd_dim ** -0.5

    def body(q_ref, k_ref, v_ref, do_ref, dq_ref, dk_ref, dv_ref):
        i = pl.program_id(1)

        @pl.when(i == 0)
        def _():
            dk_ref[...] = jnp.zeros_like(dk_ref)
            dv_ref[...] = jnp.zeros_like(dv_ref)

        for hd in range(MEM_HEADS):
            ln = slice(hd * hd_dim, (hd + 1) * hd_dim)
            qv, kv, vv_, dov = q_ref[:, ln], k_ref[:, ln], v_ref[:, ln], do_ref[:, ln]
            sc = lax.dot_general(qv, kv, _DIMS["nt"], preferred_element_type=F32) * scale
            p = jnp.exp(sc - jnp.max(sc, axis=-1, keepdims=True))
            p = p / jnp.sum(p, axis=-1, keepdims=True)
            dp = lax.dot_general(dov, vv_, _DIMS["nt"], preferred_element_type=F32)
            ds = (p * (dp - jnp.sum(p * dp, axis=-1, keepdims=True)) * scale).astype(MM)
            dq_ref[:, ln] = jnp.dot(ds, kv, preferred_element_type=F32).astype(dq_ref.dtype)
            dk_ref[:, ln] += lax.dot_general(ds, qv, _DIMS["tn"], preferred_element_type=F32)
            dv_ref[:, ln] += lax.dot_general(p.astype(MM), dov, _DIMS["tn"], preferred_element_type=F32)

    qspec = pl.BlockSpec((tq, d), lambda b, i: (b * nq + i, 0))
    kspec = pl.BlockSpec((m, d), lambda b, i: (b, 0))
    return pl.pallas_call(
        body, name=name, grid=(n_ex, nq), in_specs=[qspec, kspec, kspec, qspec], out_specs=[qspec, kspec, kspec],
        out_shape=[_sds((t, d), ACT), _sds((n_ex * m, d), F32), _sds((n_ex * m, d), F32)],
        compiler_params=_params(("parallel", "arbitrary"), 3 * tq * d * 2 + 2 * m * d * 2 + 2 * m * d * 4, 0,
                                8 * tq * m * 4),
    )(q, kk, vv, do)


def _ffn_up(name, y, w1, w3):
    t, d = y.shape
    f = w1.shape[-1]
    tm = _tile(t, 1024)
    wspec = pl.BlockSpec((None, d, f), lambda i, j, k: (j, 0, 0))
    hspec = pl.BlockSpec((None, tm, f), lambda i, j, k: (j, i, 0))

    def epi(vals, _):
        h1, h3 = vals
        return [h1, h3, h1 * _sigmoid(h1) * h3]

    return _mm(name, "nn", (t // tm, N_CHIPS, 1), y, pl.BlockSpec((tm, d), lambda i, j, k: (i, 0)),
               [w1, w3], [wspec, wspec], [_sds((N_CHIPS, t, f), ACT)] * 3, [hspec] * 3, (tm, f), epilogue=epi)


def _ffn_down(name, g, w2):
    _, t, f = g.shape
    d = w2.shape[-1]
    tm, tn = _tile(t, 1024), _tile(d, 1024)
    return _mm(name, "nn", (t // tm, d // tn, N_CHIPS), g, pl.BlockSpec((None, tm, f), lambda i, j, k: (k, i, 0)),
               [w2], [pl.BlockSpec((None, f, tn), lambda i, j, k: (k, 0, j))],
               [_sds((t, d), F32)], [pl.BlockSpec((tm, tn), lambda i, j, k: (i, j))], (tm, tn))[0]


def _ffn_down_bwd(name, dr, w2, h1, h3):
    t, d = dr.shape
    f = w2.shape[-2]
    tm = _tile(t, 1024)
    hspec = pl.BlockSpec((None, tm, f), lambda i, j, k: (j, i, 0))

    def epi(vals, ex):
        dg, = vals
        h1v, h3v = ex
        sg = _sigmoid(h1v)
        return [dg * h3v * (sg * (1.0 + h1v * (1.0 - sg))), dg * h1v * sg]

    return _mm(name, "nt", (t // tm, N_CHIPS, 1), dr, pl.BlockSpec((tm, d), lambda i, j, k: (i, 0)),
               [w2], [pl.BlockSpec((None, f, d), lambda i, j, k: (j, 0, 0))],
               [_sds((N_CHIPS, t, f), ACT)] * 2, [hspec] * 2, (tm, f),
               extras=[h1, h3], extra_specs=[hspec, hspec], epilogue=epi)


def _ffn_up_bwd(name, dh, w, extras, epilogue):
    _, t, f = dh.shape
    d = w.shape[-2]
    tm, tn = _tile(t, 1024), _tile(d, 1024)
    ospec = pl.BlockSpec((tm, tn), lambda i, j, k: (i, j))
    return _mm(name, "nt", (t // tm, d // tn, N_CHIPS), dh, pl.BlockSpec((None, tm, f), lambda i, j, k: (k, i, 0)),
               [w], [pl.BlockSpec((None, tn, f), lambda i, j, k: (k, j, 0))],
               [_sds((t, d), F32)], [ospec], (tm, tn),
               extras=extras, extra_specs=[ospec] * len(extras), epilogue=epilogue)[0]


def _ffn_wgrad_up(name, y, dh1, dh3):
    t, d = y.shape
    f = dh1.shape[-1]
    tm, tk = _tile(d, 1024), _tile(t, 1024)
    hspec = pl.BlockSpec((None, tk, f), lambda i, j, k: (j, k, 0))
    ospec = pl.BlockSpec((None, tm, f), lambda i, j, k: (j, i, 0))
    return _mm(name, "tn", (d // tm, N_CHIPS, t // tk), y, pl.BlockSpec((tk, tm), lambda i, j, k: (k, i)),
               [dh1, dh3], [hspec, hspec], [_sds((N_CHIPS, d, f), MM)] * 2, [ospec, ospec], (tm, f))


def _ffn_wgrad_down(name, g, dr):
    _, t, f = g.shape
    d = dr.shape[1]
    tn, tk = _tile(d, 1024), _tile(t, 1024)
    return _mm(name, "tn", (N_CHIPS, d // tn, t // tk), g, pl.BlockSpec((None, tk, f), lambda i, j, k: (i, k, 0)),
               [dr], [pl.BlockSpec((tk, tn), lambda i, j, k: (k, j))],
               [_sds((N_CHIPS, f, d), MM)], [pl.BlockSpec((None, f, tn), lambda i, j, k: (i, 0, j))], (f, tn))[0]


def _proj_cols(name, y, w, out_dtype):
    t, kdim = y.shape
    wd = w.shape[-1]
    tn = _tile(wd, 512)
    per = wd // tn
    tm = _tile(t, 1024)
    return _mm(name, "nn", (t // tm, N_CHIPS * per, 1), y, pl.BlockSpec((tm, kdim), lambda i, j, k: (i, 0)),
               [w], [pl.BlockSpec((None, kdim, tn), lambda i, j, k: (j // per, 0, j % per))],
               [_sds((t, N_CHIPS * wd), out_dtype)], [pl.BlockSpec((tm, tn), lambda i, j, k: (i, j))], (tm, tn))[0]


def _proj_cols_bwd(name, dh, w, extras, epilogue):
    t = dh.shape[0]
    kdim, wd = w.shape[-2], w.shape[-1]
    tm, tn = _tile(t, 1024), _tile(kdim, 1024)
    ospec = pl.BlockSpec((tm, tn), lambda i, j, k: (i, j))
    return _mm(name, "nt", (t // tm, kdim // tn, N_CHIPS), dh, pl.BlockSpec((tm, wd), lambda i, j, k: (i, k)),
               [w], [pl.BlockSpec((None, tn, wd), lambda i, j, k: (k, j, 0))],
               [_sds((t, kdim), F32)], [ospec], (tm, tn),
               extras=extras, extra_specs=[ospec] * len(extras), epilogue=epilogue)[0]


def _proj_cols_wgrad(name, y, dh):
    t, kdim = y.shape
    wd = dh.shape[1] // N_CHIPS
    tm, tk = _tile(kdim, 1024), _tile(t, 1024)
    return _mm(name, "tn", (kdim // tm, N_CHIPS, t // tk), y, pl.BlockSpec((tk, tm), lambda i, j, k: (k, i)),
               [dh], [pl.BlockSpec((tk, wd), lambda i, j, k: (k, j))],
               [_sds((N_CHIPS, kdim, wd), MM)], [pl.BlockSpec((None, tm, wd), lambda i, j, k: (j, i, 0))], (tm, wd))[0]


def _coords():
    return lax.axis_index("x"), lax.axis_index("y"), lax.axis_index("c")


def _chip_peers(x, y):
    return [(1 - x, y), (x, 1 - y), (1 - x, 1 - y)]


_ANY = pl.BlockSpec(memory_space=pl.ANY)


def _half(ref, c):
    h = ref.shape[0] // 2
    return ref.at[pl.ds(c * h, h)]


def _gather_two_level(srcs, lands):
    n = len(srcs)

    def body(*refs):
        ins, lz = refs[:n], refs[2 * n:3 * n]
        ici_send, ici_recv, d2d_send, d2d_recv = refs[3 * n:]
        x, y, c = _coords()
        me = 2 * x + y
        peers = _chip_peers(x, y)
        ici, d2d = [], []
        for t in range(n):
            for j, chip in enumerate(peers):
                k = 3 * t + j
                ici.append(pltpu.make_async_remote_copy(
                    src_ref=_half(ins[t], c), dst_ref=_half(lz[t].at[me], c), send_sem=ici_send.at[k], recv_sem=ici_recv.at[k],
                    device_id=(*chip, c), device_id_type=MESH))
                got = _half(lz[t].at[2 * chip[0] + chip[1]], c)
                d2d.append(pltpu.make_async_remote_copy(
                    src_ref=got, dst_ref=got, send_sem=d2d_send.at[k], recv_sem=d2d_recv.at[k],
                    device_id=(x, y, 1 - c), device_id_type=MESH))
        for cp in ici:
            cp.start()
        for cp, fw in zip(ici, d2d):
            cp.wait_recv()
            fw.start()
        for cp, fw in zip(ici, d2d):
            cp.wait_send()
            fw.wait_send()
            fw.wait_recv()

    res = pl.pallas_call(
        body, name="gather_weights", in_specs=[_ANY] * (2 * n), out_specs=[_ANY] * n,
        out_shape=[_sds(a.shape, a.dtype) for a in lands], input_output_aliases={n + i: i for i in range(n)},
        scratch_shapes=[pltpu.SemaphoreType.DMA((3 * n,))] * 4,
    )(*srcs, *lands)
    return list(res)


def _scatter_chips(parts):
    n = len(parts)

    def body(*refs):
        ins, outs = refs[:n], refs[n:2 * n]
        send_sems, recv_sems = refs[2 * n:]
        x, y, c = _coords()
        copies = []
        for t in range(n):
            for j, chip in enumerate(_chip_peers(x, y)):
                copies.append(pltpu.make_async_remote_copy(
                    src_ref=ins[t].at[2 * chip[0] + chip[1]], dst_ref=outs[t].at[j], send_sem=send_sems.at[3 * t + j],
                    recv_sem=recv_sems.at[3 * t + j], device_id=(*chip, c), device_id_type=MESH))
        for cp in copies:
            cp.start()
        for cp in copies:
            cp.wait()

    return pl.pallas_call(
        body, name="scatter_grads", in_specs=[_ANY] * n, out_specs=[_ANY] * n,
        out_shape=[_sds((3, *a.shape[1:]), a.dtype) for a in parts],
        scratch_shapes=[pltpu.SemaphoreType.DMA((3 * n,))] * 2,
    )(*parts)


def _swap_halves(grads):
    n = len(grads)

    def body(*refs):
        ins, outs = refs[:n], refs[n:2 * n]
        send_sems, recv_sems = refs[2 * n:]
        x, y, c = _coords()
        copies = []
        for t in range(n):
            h = ins[t].shape[1] // 2
            copies.append(pltpu.make_async_remote_copy(
                src_ref=ins[t].at[:, pl.ds((1 - c) * h, h)], dst_ref=outs[t], send_sem=send_sems.at[t], recv_sem=recv_sems.at[t],
                device_id=(x, y, 1 - c), device_id_type=MESH))
        for cp in copies:
            cp.start()
        for cp in copies:
            cp.wait()

    return pl.pallas_call(
        body, name="swap_grad_halves", in_specs=[_ANY] * n, out_specs=[_ANY] * n,
        out_shape=[_sds((a.shape[0], a.shape[1] // 2, a.shape[2]), a.dtype) for a in grads],
        scratch_shapes=[pltpu.SemaphoreType.DMA((n,))] * 2,
    )(*grads)


def _swap_sibling(arrs):
    n = len(arrs)

    def body(*refs):
        ins, outs = refs[:n], refs[n:2 * n]
        send_sems, recv_sems = refs[2 * n:]
        x, y, c = _coords()
        copies = []
        for t in range(n):
            cp = pltpu.make_async_remote_copy(src_ref=ins[t], dst_ref=outs[t], send_sem=send_sems.at[t],
                                              recv_sem=recv_sems.at[t], device_id=(x, y, 1 - c), device_id_type=MESH)
            cp.start()
            copies.append(cp)
        for cp in copies:
            cp.wait()

    return pl.pallas_call(
        body, name="swap_sibling", in_specs=[_ANY] * n, out_specs=[_ANY] * n,
        out_shape=[_sds(a.shape, a.dtype) for a in arrs],
        scratch_shapes=[pltpu.SemaphoreType.DMA((n,)), pltpu.SemaphoreType.DMA((n,))],
    )(*arrs)


def _gather_all(part):
    def body(in_ref, out_ref, send_sems, recv_sems, loc_sem):
        x, y, c = _coords()
        dst = out_ref.at[4 * x + 2 * y + c]
        copies = [pltpu.make_async_copy(in_ref, dst, loc_sem)]
        for r in range(1, N_DEV):
            fx, fy, fc = (r >> 2) & 1, (r >> 1) & 1, r & 1
            peer = (x ^ fx, y ^ fy, c ^ fc)
            copies.append(pltpu.make_async_remote_copy(src_ref=in_ref, dst_ref=dst, send_sem=send_sems.at[r - 1],
                                                       recv_sem=recv_sems.at[r - 1], device_id=peer, device_id_type=MESH))
        for cp in copies:
            cp.start()
        for cp in copies:
            cp.wait()

    return pl.pallas_call(
        body, name="gather_small_grads", in_specs=[_ANY], out_specs=_ANY,
        out_shape=_sds((N_DEV, *part.shape), part.dtype),
        scratch_shapes=[pltpu.SemaphoreType.DMA((N_DEV - 1,)), pltpu.SemaphoreType.DMA((N_DEV - 1,)), pltpu.SemaphoreType.DMA],
    )(part)


def _sum_chips(grad, recv, me):
    _, rr, cc = recv.shape
    tr = _tile(rr, 512)

    def body(me_ref, g_ref, r0_ref, r1_ref, r2_ref, o_ref):
        o_ref[...] = ((g_ref[...].astype(F32) + r0_ref[...].astype(F32)) + r1_ref[...].astype(F32)) + r2_ref[...].astype(F32)

    gspec = pl.BlockSpec((None, tr, cc), lambda r, m: (m[0], r, 0))
    rspecs = [pl.BlockSpec((None, tr, cc), functools.partial(lambda r, m, j: (j, r, 0), j=j)) for j in range(3)]
    return pl.pallas_call(
        body, name="sum_chip_grads",
        grid_spec=pltpu.PrefetchScalarGridSpec(
            num_scalar_prefetch=1, grid=(rr // tr,), in_specs=[gspec, *rspecs],
            out_specs=pl.BlockSpec((tr, cc), lambda r, m: (r, 0))),
        out_shape=_sds((rr, cc), F32),
        compiler_params=_params(("parallel",), 4 * tr * cc * 2 + tr * cc * 4, 0, 2 * tr * cc * 4),
    )(me, grad, recv, recv, recv)


def _sum_pair(grad, sib, core):
    k, h, cc = sib.shape
    tr = _tile(h, 512)
    nb = h // tr

    def body(c_ref, g_ref, a_ref, o_ref):
        o_ref[...] = (g_ref[...].astype(F32) + a_ref[...].astype(F32)).astype(o_ref.dtype)

    spec = pl.BlockSpec((None, tr, cc), lambda s, r, c: (s, r, 0))
    return pl.pallas_call(
        body, name="sum_core_grads",
        grid_spec=pltpu.PrefetchScalarGridSpec(
            num_scalar_prefetch=1, grid=(k, nb),
            in_specs=[pl.BlockSpec((None, tr, cc), lambda s, r, c: (s, c[0] * nb + r, 0)), spec], out_specs=spec),
        out_shape=_sds(sib.shape, sib.dtype),
        compiler_params=_params(("parallel", "parallel"), 3 * tr * cc * 2, 0, 2 * tr * cc * 4),
    )(core, grad, sib)


def _adamw_math(w, g, m, v):
    m = ADAM_B1 * m + (1.0 - ADAM_B1) * g
    v = ADAM_B2 * v + (1.0 - ADAM_B2) * (g * g)
    m_hat = m / (1.0 - ADAM_B1 ** ADAM_STEP)
    v_hat = v / (1.0 - ADAM_B2 ** ADAM_STEP)
    delta = -ADAM_LR * (m_hat / (jnp.sqrt(v_hat) + ADAM_EPS) + ADAM_WD * w)
    return delta, m, v


def _adamw(name, parts, w, m, v):
    ll, rr, cc = w.shape
    tr = _tile(rr, 256)
    npart = len(parts)

    def body(*refs):
        p_refs = refs[:npart]
        w_ref, m_ref, v_ref, g_ref, d_ref, nm_ref, nv_ref = refs[npart:]
        g = p_refs[0][...]
        for p in p_refs[1:]:
            g = g + p[...]
        d, nm, nv = _adamw_math(w_ref[...], g, m_ref[...], v_ref[...])
        g_ref[...] = g
        d_ref[...] = d
        nm_ref[...] = nm
        nv_ref[...] = nv

    spec = pl.BlockSpec((None, tr, cc), lambda l, r: (l, r, 0))
    out = _sds((ll, rr, cc), F32)
    return pl.pallas_call(
        body, name=name, grid=(ll, rr // tr), in_specs=[spec] * (npart + 3), out_specs=[spec] * 4, out_shape=[out] * 4,
        compiler_params=_params(("parallel", "parallel"), (npart + 7) * tr * cc * 4, 0, 4 * tr * cc * 4),
    )(*parts, w, m, v)


def _adamw_halves(name, s_own, s_sib, core, w, m, v):
    ll, rr, cc = w.shape
    h = rr // 2
    tr = _tile(h, 256)
    nb = h // tr

    def body(c_ref, own_ref, sib_ref, w_ref, m_ref, v_ref, g_ref, d_ref, nm_ref, nv_ref):
        g = jnp.where(pl.program_id(1) == c_ref[0], own_ref[...], sib_ref[...])
        d, nm, nv = _adamw_math(w_ref[...], g, m_ref[...], v_ref[...])
        g_ref[...] = g
        d_ref[...] = d
        nm_ref[...] = nm
        nv_ref[...] = nv

    half = pl.BlockSpec((None, tr, cc), lambda l, hf, r, c: (l, r, 0))
    full = pl.BlockSpec((None, tr, cc), lambda l, hf, r, c: (l, hf * nb + r, 0))
    out = _sds((ll, rr, cc), F32)
    return pl.pallas_call(
        body, name=name,
        grid_spec=pltpu.PrefetchScalarGridSpec(num_scalar_prefetch=1, grid=(ll, 2, nb), in_specs=[half, half, full, full, full],
                                               out_specs=[full] * 4),
        out_shape=[out] * 4,
        compiler_params=_params(("parallel", "parallel", "parallel"), 9 * tr * cc * 4, 0, 4 * tr * cc * 4),
    )(core, s_own, s_sib, w, m, v)


def _sum_devices(allparts):
    _, rr, cc = allparts.shape

    def body(p_ref, o_ref):
        s = p_ref[0]
        for k in range(1, N_DEV):
            s = s + p_ref[k]
        o_ref[...] = s

    return pl.pallas_call(
        body, name="sum_small_grads", grid=(1,), in_specs=[pl.BlockSpec((N_DEV, rr, cc), lambda i: (0, 0, 0))],
        out_specs=pl.BlockSpec((rr, cc), lambda i: (0, 0)), out_shape=_sds((rr, cc), F32),
        compiler_params=_params(("arbitrary",), 9 * rr * cc * 4),
    )(allparts)


def _pack(arrs):
    flat = jnp.concatenate([a.reshape(-1).astype(F32) for a in arrs])
    n = flat.shape[0]
    total = -(-n // 1024) * 1024
    return jnp.pad(flat, (0, total - n)).reshape(total // 128, 128)


def _unpack(block, shapes):
    flat = block.reshape(-1)
    out, off = [], 0
    for sh in shapes:
        n = math.prod(sh)
        out.append(flat[off:off + n].reshape(sh))
        off += n
    return out


_BIG = ["w_in_ab", "w_out_ab", "w_qkv_c", "w_out_c", "mem_wq", "mem_wk", "mem_wv", "mem_wo", "ffn_w1", "ffn_w3", "ffn_w2"]
_ROW_SHARDED = ("w_out_ab", "w_out_c", "mem_wq", "mem_wk", "mem_wv", "mem_wo")
_SMALL_REPL = ["gmlp_ln_g", "gmlp_ln_b", "gmlp_w_s", "gmlp_b_s", "conv_b", "conv_gn_g", "conv_gn_b"]
_SMALL_SHARD = ["conv_w", "ln_g", "ln_b"]
_NAMES = ["w_in_ab", "gmlp_ln_g", "gmlp_ln_b", "gmlp_w_s", "gmlp_b_s", "conv_w", "conv_b", "conv_gn_g", "conv_gn_b",
          "w_out_ab", "w_qkv_c", "w_out_c", "mem_wq", "mem_wk", "mem_wv", "mem_wo", "ffn_w1", "ffn_w3", "ffn_w2",
          "ln_g", "ln_b"]


def _layer_weights(l):
    mixer = ["w_in_ab", "w_out_ab"] if l % 2 == 0 else ["w_qkv_c", "w_out_c"]
    return [(n, l // 2) for n in mixer] + [(n, l) for n in _BIG if n.startswith(("mem_", "ffn_"))]


def _natural(w):
    return w.reshape(-1, w.shape[-1])


def _local_step(x, mem, target, layer_w, small):
    n_ex, s, d = x.shape
    t = n_ex * s
    x2 = x.reshape(t, d)
    mem_a = mem.reshape(-1, d).astype(ACT)
    tgt = target.reshape(t, d)
    one = jnp.ones((1, d), F32)
    zero = jnp.zeros((1, d), F32)
    ln_g, ln_b = small["ln_g"], small["ln_b"]

    def vec(a):
        return a.reshape(1, -1)

    saved = []
    xh, gp, bp = x2, one, zero
    y_act = x2.astype(ACT)
    for l in range(DEPTH):
        wts = layer_w[l]
        sv = {"y0": y_act, "w": wts}
        if l % 2 == 0:
            e = l // 2
            h = _proj_cols(f"in_ab_{l}", y_act, wts["w_in_ab"], ACT)
            gl = (vec(small["gmlp_ln_g"][e]), vec(small["gmlp_ln_b"][e]), small["gmlp_w_s"][e],
                  small["gmlp_b_s"][e].reshape(4, CHUNK, 1))
            cl = (small["conv_w"][e], vec(small["conv_b"][e]), vec(small["conv_gn_g"][e]), vec(small["conv_gn_b"][e]))
            ya = _gmlp_fwd(f"gmlp_fwd_{l}", h, *gl)
            yb = _conv_fwd(f"conv_fwd_{l}", h, n_ex, *cl)
            yab = jnp.concatenate([ya, yb], axis=1)
            mix = _dense(f"out_ab_{l}", yab, _natural(wts["w_out_ab"]), F32)
            sv.update(h=h, yab=yab, gl=gl, cl=cl)
        else:
            qkv = _proj_cols(f"qkv_{l}", y_act, wts["w_qkv_c"], ACT)
            att, ltot = _sb_fwd(f"sb_fwd_{l}", qkv, n_ex)
            att_a = att.astype(ACT)
            mix = _dense(f"out_c_{l}", att_a, _natural(wts["w_out_c"]), F32)
            sv.update(qkv=qkv, att=att_a, ltot=ltot)
        g1, b1 = vec(ln_g[l, 0]), vec(ln_b[l, 0])
        xh1, y1, rstd1 = _ln_fwd(f"ln1_fwd_{l}", xh, gp, bp, mix, g1, b1)
        q = _dense(f"mem_q_{l}", y1, _natural(wts["mem_wq"]), ACT)
        kk = _dense(f"mem_k_{l}", mem_a, _natural(wts["mem_wk"]), ACT)
        vv = _dense(f"mem_v_{l}", mem_a, _natural(wts["mem_wv"]), ACT)
        oc = _xattn_fwd(f"xattn_fwd_{l}", q, kk, vv, n_ex)
        cross = _dense(f"mem_o_{l}", oc, _natural(wts["mem_wo"]), F32)
        g2, b2 = vec(ln_g[l, 1]), vec(ln_b[l, 1])
        xh2, y2, rstd2 = _ln_fwd(f"ln2_fwd_{l}", xh1, g1, b1, cross, g2, b2)
        h1, h3, gact = _ffn_up(f"ffn_up_{l}", y2, wts["ffn_w1"], wts["ffn_w3"])
        ffo = _ffn_down(f"ffn_down_{l}", gact, wts["ffn_w2"])
        g3, b3 = vec(ln_g[l, 2]), vec(ln_b[l, 2])
        xh3, y3, rstd3 = _ln_fwd(f"ln3_fwd_{l}", xh2, g2, b2, ffo, g3, b3)
        sv.update(xh1=xh1, y1=y1, rstd1=rstd1, g1=g1, q=q, kk=kk, vv=vv, oc=oc, xh2=xh2, y2=y2, rstd2=rstd2, g2=g2,
                  h1=h1, h3=h3, gact=gact, xh3=xh3, rstd3=rstd3, g3=g3)
        saved.append(sv)
        xh, gp, bp, y_act = xh3, g3, b3, y3

    dy, loss = _loss_head(xh, gp, bp, tgt)

    sm = {n: [None] * (DEPTH // 2) for n in _SMALL_REPL + ["conv_w"]}
    d_ln_g = [[None] * 3 for _ in range(DEPTH)]
    d_ln_b = [[None] * 3 for _ in range(DEPTH)]

    def add_res(vals, ex):
        return [vals[0] + ALPHA * ex[0]]

    def add_res2(vals, ex):
        return [vals[0] + ex[0] + ALPHA * ex[1]]

    layer_g = [None] * DEPTH
    for l in reversed(range(DEPTH)):
        sv = saved[l]
        wts = sv["w"]
        big = {}
        dr3, dr3a, d_ln_g[l][2], d_ln_b[l][2] = _ln_bwd(f"ln3_bwd_{l}", dy, sv["xh3"], sv["rstd3"], sv["g3"])
        big["ffn_w2"] = _ffn_wgrad_down(f"ffn_w2_grad_{l}", sv["gact"], dr3a)
        dh1, dh3 = _ffn_down_bwd(f"ffn_down_bwd_{l}", dr3a, wts["ffn_w2"], sv["h1"], sv["h3"])
        big["ffn_w1"], big["ffn_w3"] = _ffn_wgrad_up(f"ffn_w13_grad_{l}", sv["y2"], dh1, dh3)
        part = _ffn_up_bwd(f"ffn_up_bwd1_{l}", dh1, wts["ffn_w1"], [], None)
        dy = _ffn_up_bwd(f"ffn_up_bwd3_{l}", dh3, wts["ffn_w3"], [part, dr3], add_res2)
        dr2, dr2a, d_ln_g[l][1], d_ln_b[l][1] = _ln_bwd(f"ln2_bwd_{l}", dy, sv["xh2"], sv["rstd2"], sv["g2"])
        big["mem_wo"] = _dense_tn(f"mem_wo_grad_{l}", sv["oc"], dr2a)
        doc = _dense_nt(f"mem_o_bwd_{l}", dr2a, _natural(wts["mem_wo"]), ACT)
        dq, dkk, dvv = _xattn_bwd(f"xattn_bwd_{l}", sv["q"], sv["kk"], sv["vv"], doc, n_ex)
        big["mem_wq"] = _dense_tn(f"mem_wq_grad_{l}", sv["y1"], dq)
        big["mem_wk"] = _dense_tn(f"mem_wk_grad_{l}", mem_a, dkk)
        big["mem_wv"] = _dense_tn(f"mem_wv_grad_{l}", mem_a, dvv)
        dy = _dense_nt(f"mem_q_bwd_{l}", dq, _natural(wts["mem_wq"]), F32, extras=[dr2], epilogue=add_res)
        dr1, dr1a, d_ln_g[l][0], d_ln_b[l][0] = _ln_bwd(f"ln1_bwd_{l}", dy, sv["xh1"], sv["rstd1"], sv["g1"])
        if l % 2 == 0:
            e = l // 2
            big["w_out_ab"] = _dense_tn(f"out_ab_grad_{l}", sv["yab"], dr1a)
            dyab = _dense_nt(f"out_ab_bwd_{l}", dr1a, _natural(wts["w_out_ab"]), ACT)
            duv, dgg, dgb, dws, dbs = _gmlp_bwd(f"gmlp_bwd_{l}", sv["h"], dyab, *sv["gl"])
            da, dgt, dcw, dcb, dng, dnb = _conv_bwd(f"conv_bwd_{l}", sv["h"], dyab, n_ex, *sv["cl"])
            sm["gmlp_ln_g"][e], sm["gmlp_ln_b"][e] = dgg.reshape(-1), dgb.reshape(-1)
            sm["gmlp_w_s"][e], sm["gmlp_b_s"][e] = dws, dbs.reshape(4, CHUNK)
            sm["conv_w"][e], sm["conv_b"][e] = dcw, dcb.reshape(-1)
            sm["conv_gn_g"][e], sm["conv_gn_b"][e] = dng.reshape(-1), dnb.reshape(-1)
            dh = jnp.concatenate([duv, da, dgt], axis=1)
            big["w_in_ab"] = _proj_cols_wgrad(f"in_ab_grad_{l}", sv["y0"], dh)
            dy = _proj_cols_bwd(f"in_ab_bwd_{l}", dh, wts["w_in_ab"], [dr1], add_res)
        else:
            big["w_out_c"] = _dense_tn(f"out_c_grad_{l}", sv["att"], dr1a)
            datt = _dense_nt(f"out_c_bwd_{l}", dr1a, _natural(wts["w_out_c"]), ACT)
            dq_, dk_, dv_ = _sb_bwd(f"sb_bwd_{l}", sv["qkv"], datt, sv["ltot"], n_ex)
            dqkv = jnp.concatenate([dq_, dk_, dv_], axis=1)
            big["w_qkv_c"] = _proj_cols_wgrad(f"qkv_grad_{l}", sv["y0"], dqkv)
            dy = _proj_cols_bwd(f"qkv_bwd_{l}", dqkv, wts["w_qkv_c"], [dr1], add_res)
        for n in _ROW_SHARDED:
            if n in big:
                big[n] = big[n].reshape(N_CHIPS, -1, big[n].shape[-1])
        layer_g[l] = big

    grad_x = dy.reshape(n_ex, s, d)
    small_g = {n: jnp.stack(sm[n]) for n in sm}
    small_g["ln_g"] = jnp.stack([jnp.concatenate(r, axis=0) for r in d_ln_g])
    small_g["ln_b"] = jnp.stack([jnp.concatenate(r, axis=0) for r in d_ln_b])
    return loss, grad_x, layer_g, small_g


def kernel(x, mem, w_in_ab, gmlp_ln_g, gmlp_ln_b, gmlp_w_s, gmlp_b_s, conv_w, conv_b, conv_gn_g, conv_gn_b, w_out_ab, w_qkv_c, w_out_c, mem_wq, mem_wk, mem_wv, mem_wo, ffn_w1, ffn_w3, ffn_w2, ln_g, ln_b, loss_target, m_w_in_ab, m_gmlp_ln_g, m_gmlp_ln_b, m_gmlp_w_s, m_gmlp_b_s, m_conv_w, m_conv_b, m_conv_gn_g, m_conv_gn_b, m_w_out_ab, m_w_qkv_c, m_w_out_c, m_mem_wq, m_mem_wk, m_mem_wv, m_mem_wo, m_ffn_w1, m_ffn_w3, m_ffn_w2, m_ln_g, m_ln_b, v_w_in_ab, v_gmlp_ln_g, v_gmlp_ln_b, v_gmlp_w_s, v_gmlp_b_s, v_conv_w, v_conv_b, v_conv_gn_g, v_conv_gn_b, v_w_out_ab, v_w_qkv_c, v_w_out_c, v_mem_wq, v_mem_wk, v_mem_wv, v_mem_wo, v_ffn_w1, v_ffn_w3, v_ffn_w2, v_ln_g, v_ln_b):
    w = dict(w_in_ab=w_in_ab, gmlp_ln_g=gmlp_ln_g, gmlp_ln_b=gmlp_ln_b, gmlp_w_s=gmlp_w_s, gmlp_b_s=gmlp_b_s, conv_w=conv_w,
             conv_b=conv_b, conv_gn_g=conv_gn_g, conv_gn_b=conv_gn_b, w_out_ab=w_out_ab, w_qkv_c=w_qkv_c, w_out_c=w_out_c,
             mem_wq=mem_wq, mem_wk=mem_wk, mem_wv=mem_wv, mem_wo=mem_wo, ffn_w1=ffn_w1, ffn_w3=ffn_w3, ffn_w2=ffn_w2,
             ln_g=ln_g, ln_b=ln_b)
    mo = dict(w_in_ab=m_w_in_ab, gmlp_ln_g=m_gmlp_ln_g, gmlp_ln_b=m_gmlp_ln_b, gmlp_w_s=m_gmlp_w_s, gmlp_b_s=m_gmlp_b_s,
              conv_w=m_conv_w, conv_b=m_conv_b, conv_gn_g=m_conv_gn_g, conv_gn_b=m_conv_gn_b, w_out_ab=m_w_out_ab,
              w_qkv_c=m_w_qkv_c, w_out_c=m_w_out_c, mem_wq=m_mem_wq, mem_wk=m_mem_wk, mem_wv=m_mem_wv, mem_wo=m_mem_wo,
              ffn_w1=m_ffn_w1, ffn_w3=m_ffn_w3, ffn_w2=m_ffn_w2, ln_g=m_ln_g, ln_b=m_ln_b)
    vo = dict(w_in_ab=v_w_in_ab, gmlp_ln_g=v_gmlp_ln_g, gmlp_ln_b=v_gmlp_ln_b, gmlp_w_s=v_gmlp_w_s, gmlp_b_s=v_gmlp_b_s,
              conv_w=v_conv_w, conv_b=v_conv_b, conv_gn_g=v_conv_gn_g, conv_gn_b=v_conv_gn_b, w_out_ab=v_w_out_ab,
              w_qkv_c=v_w_qkv_c, w_out_c=v_w_out_c, mem_wq=v_mem_wq, mem_wk=v_mem_wk, mem_wv=v_mem_wv, mem_wo=v_mem_wo,
              ffn_w1=v_ffn_w1, ffn_w3=v_ffn_w3, ffn_w2=v_ffn_w2, ln_g=v_ln_g, ln_b=v_ln_b)
    me = (2 * lax.axis_index("x") + lax.axis_index("y")).astype(jnp.int32).reshape(1)

    per_layer = [_layer_weights(l) for l in range(DEPTH)]
    srcs = [w[n] for n in _SMALL_SHARD] + [w[n][i].astype(MM) for lw in per_layer for n, i in lw]
    lands = [lax.dynamic_update_index_in_dim(jnp.zeros((N_CHIPS, *s.shape), s.dtype), s[None], me[0], 0) for s in srcs]
    gathered = _gather_two_level(srcs, lands)
    cw_g, lg_g, lb_g = gathered[:3]
    small = {n: w[n] for n in _SMALL_REPL}
    small["conv_w"] = jnp.moveaxis(cw_g, 0, 2).reshape(cw_g.shape[1], CONV_WIDTH, -1)
    small["ln_g"] = jnp.moveaxis(lg_g, 0, 2).reshape(DEPTH, 3, -1)
    small["ln_b"] = jnp.moveaxis(lb_g, 0, 2).reshape(DEPTH, 3, -1)
    it = iter(gathered[3:])
    layer_w = [{n: next(it) for n, _ in lw} for lw in per_layer]

    loss, grad_x, layer_g, small_g = _local_step(x, mem, loss_target, layer_w, small)
    loss = lax.psum(loss, ("x", "y", "c"))

    core = lax.axis_index("c").astype(jnp.int32).reshape(1)
    grads_flat = [layer_g[l][n] for l in range(DEPTH) for n, _ in per_layer[l]]
    pair = [_sum_pair(g, a, core) for g, a in zip(grads_flat, _swap_halves(grads_flat))]
    recv = _scatter_chips(pair)
    per_name = {n: [None] * (DEPTH if n.startswith(("mem_", "ffn_")) else DEPTH // 2) for n in _BIG}
    flat_names = [ni for lw in per_layer for ni in lw]
    for (n, i), p, rc in zip(flat_names, pair, recv):
        per_name[n][i] = _sum_chips(p, rc, me)
    sums = [jnp.stack(per_name[n]) for n in _BIG]
    sib = _swap_sibling(sums)

    out = {}
    for n, s_own, s_sib in zip(_BIG, sums, sib):
        out[n] = _adamw_halves(f"adamw_{n}", s_own, s_sib, core, w[n], mo[n], vo[n])

    order = _SMALL_REPL + _SMALL_SHARD
    part = _pack([small_g[n] for n in order])
    total = _sum_devices(_gather_all(part))
    full = dict(zip(order, _unpack(total, [small_g[n].shape for n in order])))
    x_i, y_i = lax.axis_index("x"), lax.axis_index("y")
    chip = 2 * x_i + y_i
    loc = {n: full[n] for n in _SMALL_REPL}
    for n in _SMALL_SHARD:
        wd = w[n].shape[-1]
        loc[n] = lax.dynamic_slice_in_dim(full[n], chip * wd, wd, axis=full[n].ndim - 1)
    gp, wp, mp, vp = (_pack([src[n] for n in order]) for src in (loc, w, mo, vo))
    r128 = gp.shape[0]
    res = _adamw("adamw_small", [gp.reshape(1, r128, 128)], wp.reshape(1, r128, 128), mp.reshape(1, r128, 128),
                 vp.reshape(1, r128, 128))
    shapes = [w[n].shape for n in order]
    unp = [_unpack(r.reshape(r128, 128), shapes) for r in res]
    for i, n in enumerate(order):
        out[n] = tuple(u[i] for u in unp)

    grads = [out[n][0] for n in _NAMES]
    deltas = [out[n][1] for n in _NAMES]
    new_m = [out[n][2] for n in _NAMES]
    new_v = [out[n][3] for n in _NAMES]
    return (loss, grad_x, *grads, *deltas, *new_m, *new_v)
```

```python
import functools
import math

import jax
import jax.numpy as jnp
from jax import lax
from jax.experimental import pallas as pl
from jax.experimental.pallas import tpu as pltpu

F32 = jnp.float32
MM = jnp.bfloat16
ACT = jnp.bfloat16
MESH = pl.DeviceIdType.MESH

DEPTH = 4
CHUNK = 128
CONV_WIDTH = 31
HALO = 32
MEM_HEADS = 4
C_HEAD_DIM = 64
ALPHA = (2.0 * DEPTH) ** 0.25
LN_EPS = 1e-5
ADAM_LR, ADAM_B1, ADAM_B2, ADAM_EPS, ADAM_WD, ADAM_STEP = 0.001, 0.9, 0.999, 1e-08, 0.01, 10

VMEM_CAP_V7X = 64 * 1024 * 1024
VMEM_MAX_REQUEST = 56 * 1024 * 1024
N_CHIPS = 4
N_DEV = 8


def _tile(n, pref):
    if n <= pref:
        return n
    for t in range(pref - pref % 8, 7, -8):
        if n % t == 0:
            return t
    return n


def _nbytes(shape, dtype):
    return math.prod(1 if s is None else s for s in shape) * jnp.dtype(dtype).itemsize


def _vmem_limit(block_bytes, scratch_bytes=0, temp_bytes=0):
    est = 2 * block_bytes + scratch_bytes + temp_bytes
    return int(min(VMEM_MAX_REQUEST, max(16 * 1024 * 1024, est * 5 // 4)))


def _params(sem, block_bytes, scratch_bytes=0, temp_bytes=0):
    return pltpu.CompilerParams(dimension_semantics=sem,
                                vmem_limit_bytes=_vmem_limit(block_bytes, scratch_bytes, temp_bytes))


_DIMS = {"nn": (((1,), (0,)), ((), ())), "nt": (((1,), (1,)), ((), ())), "tn": (((0,), (0,)), ((), ()))}


def _mm(name, mode, grid, a, a_spec, bs, b_specs, outs, out_specs, acc_shape,
        extras=(), extra_specs=(), epilogue=None):
    nb, ne, no = len(bs), len(extras), len(outs)
    nk = grid[2]

    def body(*refs):
        a_ref = refs[0]
        b_refs = refs[1:1 + nb]
        e_refs = refs[1 + nb:1 + nb + ne]
        o_refs = refs[1 + nb + ne:1 + nb + ne + no]
        accs = refs[1 + nb + ne + no:]

        def finish(vals):
            if epilogue is not None:
                vals = epilogue(vals, [e[...].astype(F32) for e in e_refs])
            for o, v in zip(o_refs, vals):
                o[...] = v.astype(o.dtype)

        av = a_ref[...].astype(MM)
        if nk == 1:
            finish([lax.dot_general(av, b_ref[...].astype(MM), _DIMS[mode], preferred_element_type=F32) for b_ref in b_refs])
            return
        k = pl.program_id(2)

        @pl.when(k == 0)
        def _():
            for acc in accs:
                acc[...] = jnp.zeros_like(acc)

        for b_ref, acc in zip(b_refs, accs):
            acc[...] += lax.dot_general(av, b_ref[...].astype(MM), _DIMS[mode], preferred_element_type=F32)

        @pl.when(k == nk - 1)
        def _():
            finish([acc[...] for acc in accs])

    blocks = (_nbytes(a_spec.block_shape, a.dtype)
              + sum(_nbytes(s.block_shape, b.dtype) for s, b in zip(b_specs, bs))
              + sum(_nbytes(s.block_shape, e.dtype) for s, e in zip(extra_specs, extras))
              + sum(_nbytes(s.block_shape, o.dtype) for s, o in zip(out_specs, outs)))
    acc_bytes = nb * _nbytes(acc_shape, F32)
    res = pl.pallas_call(
        body, name=name, grid=grid,
        in_specs=[a_spec, *b_specs, *extra_specs], out_specs=list(out_specs), out_shape=list(outs),
        scratch_shapes=[pltpu.VMEM(acc_shape, F32)] * (nb if nk > 1 else 0),
        compiler_params=_params(("parallel", "parallel", "arbitrary"), blocks, acc_bytes if nk > 1 else 0, 4 * acc_bytes),
    )(a, *bs, *extras)
    return res


def _sds(shape, dtype):
    return jax.ShapeDtypeStruct(shape, dtype)


def _dense(name, a, w, out_dtype):
    t, kdim = a.shape
    n = w.shape[1]
    tm, tn, tk = _tile(t, 1024), _tile(n, 1024), _tile(kdim, 1024)
    grid = (t // tm, n // tn, kdim // tk)
    return _mm(name, "nn", grid, a, pl.BlockSpec((tm, tk), lambda i, j, k: (i, k)),
               [w], [pl.BlockSpec((tk, tn), lambda i, j, k: (k, j))],
               [_sds((t, n), out_dtype)], [pl.BlockSpec((tm, tn), lambda i, j, k: (i, j))], (tm, tn))[0]


def _dense_nt(name, a, w, out_dtype, extras=(), epilogue=None):
    t, n = a.shape
    kout = w.shape[0]
    tm, tn, tk = _tile(t, 1024), _tile(kout, 1024), _tile(n, 1024)
    grid = (t // tm, kout // tn, n // tk)
    return _mm(name, "nt", grid, a, pl.BlockSpec((tm, tk), lambda i, j, k: (i, k)),
               [w], [pl.BlockSpec((tn, tk), lambda i, j, k: (j, k))],
               [_sds((t, kout), out_dtype)], [pl.BlockSpec((tm, tn), lambda i, j, k: (i, j))], (tm, tn),
               extras=extras, extra_specs=[pl.BlockSpec((tm, tn), lambda i, j, k: (i, j))] * len(extras),
               epilogue=epilogue)[0]


def _dense_tn(name, a, b, out_dtype=MM):
    t, m = a.shape
    n = b.shape[1]
    tm, tn, tk = _tile(m, 1024), _tile(n, 1024), _tile(t, 1024)
    grid = (m // tm, n // tn, t // tk)
    return _mm(name, "tn", grid, a, pl.BlockSpec((tk, tm), lambda i, j, k: (k, i)),
               [b], [pl.BlockSpec((tk, tn), lambda i, j, k: (k, j))],
               [_sds((m, n), out_dtype)], [pl.BlockSpec((tm, tn), lambda i, j, k: (i, j))], (tm, tn))[0]


_INV_SQRT2 = 0.7071067811865476
_INV_SQRT_2PI = 0.3989422804014327


def _gelu(x):
    return 0.5 * x * (1.0 + lax.erf(x * _INV_SQRT2))


def _gelu_grad(x):
    return 0.5 * (1.0 + lax.erf(x * _INV_SQRT2)) + x * jnp.exp(-0.5 * x * x) * _INV_SQRT_2PI


def _sigmoid(x):
    return 1.0 / (1.0 + jnp.exp(-x))


def _norm_stats(x):
    mu = jnp.mean(x, axis=-1, keepdims=True)
    xc = x - mu
    var = jnp.mean(xc * xc, axis=-1, keepdims=True)
    rstd = lax.rsqrt(var + LN_EPS)
    return xc * rstd, rstd


def _norm_bwd(dy_g, xh, rstd):
    m1 = jnp.mean(dy_g, axis=-1, keepdims=True)
    m2 = jnp.mean(dy_g * xh, axis=-1, keepdims=True)
    return rstd * (dy_g - m1 - xh * m2)


def _rows8(x):
    r, c = x.shape
    return jnp.sum(x.reshape(r // 8, 8, c), axis=0)


def _ln_epilogue(vals, ex):
    xp, gp, bp, g, b = ex
    xh, rstd = _norm_stats(ALPHA * (xp * gp + bp) + vals[0])
    return [xh, xh * g + b, rstd]


def _proj_ln(name, a, a_spec_of, w, w_spec, nk, xh_prev, g_prev, b_prev, g, b):
    t, d = xh_prev.shape
    tm = _tile(t, 512)
    row = pl.BlockSpec((tm, d), lambda i, j, k: (i, 0))
    vec = pl.BlockSpec((1, d), lambda i, j, k: (0, 0))
    return _mm(name, "nn", (t // tm, 1, nk), a, a_spec_of(tm), [w], [w_spec],
               [_sds((t, d), F32), _sds((t, d), ACT), _sds((t, 1), F32)],
               [row, row, pl.BlockSpec((tm, 1), lambda i, j, k: (i, 0))], (tm, d),
               extras=[xh_prev, g_prev, b_prev, g, b], extra_specs=[row, vec, vec, vec, vec], epilogue=_ln_epilogue)


def _dense_ln(name, a, w, *ln_args):
    kdim, d = w.shape
    return _proj_ln(name, a, lambda tm: pl.BlockSpec((tm, kdim), lambda i, j, k: (i, 0)), w,
                    pl.BlockSpec((kdim, d), lambda i, j, k: (0, 0)), 1, *ln_args)


def _ffn_down_ln(name, gact, w2, *ln_args):
    f, d = w2.shape[-2:]
    return _proj_ln(name, gact, lambda tm: pl.BlockSpec((None, tm, f), lambda i, j, k: (k, i, 0)), w2,
                    pl.BlockSpec((None, f, d), lambda i, j, k: (k, 0, 0)), N_CHIPS, *ln_args)


def _ln_bwd(name, dy, xh, rstd, g):
    t, d = dy.shape
    tm = _tile(t, 512)
    n = t // tm

    def body(dy_ref, xh_ref, rstd_ref, g_ref, dr_ref, dra_ref, dg_ref, db_ref, dg_acc, db_acc):
        i = pl.program_id(0)

        @pl.when(i == 0)
        def _():
            dg_acc[...] = jnp.zeros_like(dg_acc)
            db_acc[...] = jnp.zeros_like(db_acc)

        dyv = dy_ref[...]
        xhv = xh_ref[...]
        dr = _norm_bwd(dyv * g_ref[...], xhv, rstd_ref[...])
        dr_ref[...] = dr
        dra_ref[...] = dr.astype(dra_ref.dtype)
        dg_acc[...] += _rows8(dyv * xhv)
        db_acc[...] += _rows8(dyv)

        @pl.when(i == n - 1)
        def _():
            dg_ref[...] = jnp.sum(dg_acc[...], axis=0, keepdims=True)
            db_ref[...] = jnp.sum(db_acc[...], axis=0, keepdims=True)

    row = pl.BlockSpec((tm, d), lambda i: (i, 0))
    vec = pl.BlockSpec((1, d), lambda i: (0, 0))
    return pl.pallas_call(
        body, name=name, grid=(n,),
        in_specs=[row, row, pl.BlockSpec((tm, 1), lambda i: (i, 0)), vec],
        out_specs=[row, row, vec, vec],
        out_shape=[_sds((t, d), F32), _sds((t, d), ACT), _sds((1, d), F32), _sds((1, d), F32)],
        scratch_shapes=[pltpu.VMEM((8, d), F32), pltpu.VMEM((8, d), F32)],
        compiler_params=_params(("arbitrary",), 4 * tm * d * 4, 0, 4 * tm * d * 4),
    )(dy, xh, rstd, g)


def _loss_head(xh, g, b, target):
    t, d = xh.shape
    tm = _tile(t, 512)
    n = t // tm

    def body(xh_ref, g_ref, b_ref, tg_ref, dy_ref, loss_ref, acc):
        i = pl.program_id(0)

        @pl.when(i == 0)
        def _():
            acc[...] = jnp.zeros_like(acc)

        err = xh_ref[...] * g_ref[...] + b_ref[...] - tg_ref[...]
        dy_ref[...] = err * (1.0 / d)
        acc[...] += _rows8(err * err)

        @pl.when(i == n - 1)
        def _():
            s = jnp.sum(jnp.sum(acc[...], axis=0, keepdims=True), axis=1, keepdims=True)
            loss_ref[...] = jnp.broadcast_to(s * (0.5 / d), loss_ref.shape)

    row = pl.BlockSpec((tm, d), lambda i: (i, 0))
    vec = pl.BlockSpec((1, d), lambda i: (0, 0))
    dy, loss = pl.pallas_call(
        body, name="loss_head", grid=(n,),
        in_specs=[row, vec, vec, row],
        out_specs=[row, pl.BlockSpec((8, 128), lambda i: (0, 0))],
        out_shape=[_sds((t, d), F32), _sds((8, 128), F32)],
        scratch_shapes=[pltpu.VMEM((8, d), F32)],
        compiler_params=_params(("arbitrary",), 3 * tm * d * 4, 0, 2 * tm * d * 4),
    )(xh, g, b, target)
    return dy, loss[0, 0]


def _causal_w(w):
    r = lax.broadcasted_iota(jnp.int32, w.shape, 0)
    c = lax.broadcasted_iota(jnp.int32, w.shape, 1)
    return jnp.where(r >= c, w, 0.0)


def _gmlp_fwd(name, h, ln_g, ln_b, w_s, b_s_col):
    t = h.shape[0]
    tt = _tile(t, 256)
    wd = 4 * CHUNK

    def body(u_ref, v_ref, g_ref, b_ref, w_ref, bs_ref, ya_ref):
        for gi in range(4):
            ln = slice(gi * CHUNK, (gi + 1) * CHUNK)
            u = _gelu(u_ref[:, ln].astype(F32))
            v = _gelu(v_ref[:, ln].astype(F32))
            xh, _ = _norm_stats(v)
            vg = (xh * g_ref[:, ln] + b_ref[:, ln]).astype(MM)
            w = _causal_w(w_ref[gi]).astype(MM)
            for c in range(tt // CHUNK):
                rs = slice(c * CHUNK, (c + 1) * CHUNK)
                mixed = jnp.dot(w, vg[rs], preferred_element_type=F32) + bs_ref[gi]
                ya_ref[rs, ln] = (u[rs] * mixed).astype(ya_ref.dtype)

    vec = pl.BlockSpec((1, wd), lambda i: (0, 0))
    return pl.pallas_call(
        body, name=name, grid=(t // tt,),
        in_specs=[pl.BlockSpec((tt, wd), lambda i: (i, 0)), pl.BlockSpec((tt, wd), lambda i: (i, 1)), vec, vec,
                  pl.BlockSpec((4, CHUNK, CHUNK), lambda i: (0, 0, 0)), pl.BlockSpec((4, CHUNK, 1), lambda i: (0, 0, 0))],
        out_specs=pl.BlockSpec((tt, wd), lambda i: (i, 0)),
        out_shape=_sds((t, wd), ACT),
        compiler_params=_params(("parallel",), 3 * tt * wd * 4, 0, 8 * tt * CHUNK * 4),
    )(h, h, ln_g, ln_b, w_s, b_s_col)


def _gmlp_bwd(name, h, dyab, ln_g, ln_b, w_s, b_s_col):
    t = h.shape[0]
    tt = _tile(t, 256)
    n = t // tt
    wd = 4 * CHUNK

    def body(u_ref, v_ref, dy_ref, g_ref, b_ref, w_ref, bs_ref, duv_ref, dg_ref, db_ref, dw_ref, dbs_ref,
             dg_acc, db_acc):
        i = pl.program_id(0)

        @pl.when(i == 0)
        def _():
            dg_acc[...] = jnp.zeros_like(dg_acc)
            db_acc[...] = jnp.zeros_like(db_acc)
            dw_ref[...] = jnp.zeros_like(dw_ref)
            dbs_ref[...] = jnp.zeros_like(dbs_ref)

        for gi in range(4):
            ln = slice(gi * CHUNK, (gi + 1) * CHUNK)
            upre = u_ref[:, ln].astype(F32)
            vpre = v_ref[:, ln].astype(F32)
            u = _gelu(upre)
            v = _gelu(vpre)
            xh, rstd = _norm_stats(v)
            gv = g_ref[:, ln]
            vg = (xh * gv + b_ref[:, ln]).astype(MM)
            w = _causal_w(w_ref[gi]).astype(MM)
            dya = dy_ref[:, ln].astype(F32)
            dmixed = dya * u
            dmm = dmixed.astype(MM)
            dvg_parts, mixed_parts = [], []
            dw = jnp.zeros((CHUNK, CHUNK), F32)
            dbs = jnp.zeros((CHUNK, 1), F32)
            for c in range(tt // CHUNK):
                rs = slice(c * CHUNK, (c + 1) * CHUNK)
                mixed_parts.append(jnp.dot(w, vg[rs], preferred_element_type=F32) + bs_ref[gi])
                dw = dw + lax.dot_general(dmm[rs], vg[rs], _DIMS["nt"], preferred_element_type=F32)
                dbs = dbs + jnp.sum(dmixed[rs], axis=1, keepdims=True)
                dvg_parts.append(lax.dot_general(w, dmm[rs], _DIMS["tn"], preferred_element_type=F32))
            mixed = jnp.concatenate(mixed_parts, axis=0)
            dvg = jnp.concatenate(dvg_parts, axis=0)
            dw_ref[gi] += _causal_w(dw)
            dbs_ref[gi] += dbs
            dg_acc[:, ln] += _rows8(dvg * xh)
            db_acc[:, ln] += _rows8(dvg)
            dv = _norm_bwd(dvg * gv, xh, rstd) * _gelu_grad(vpre)
            du = dya * mixed * _gelu_grad(upre)
            duv_ref[:, ln] = du.astype(duv_ref.dtype)
            duv_ref[:, wd + gi * CHUNK: wd + (gi + 1) * CHUNK] = dv.astype(duv_ref.dtype)

        @pl.when(i == n - 1)
        def _():
            dg_ref[...] = jnp.sum(dg_acc[...], axis=0, keepdims=True)
            db_ref[...] = jnp.sum(db_acc[...], axis=0, keepdims=True)

    vec = pl.BlockSpec((1, wd), lambda i: (0, 0))
    wspec = pl.BlockSpec((4, CHUNK, CHUNK), lambda i: (0, 0, 0))
    bspec = pl.BlockSpec((4, CHUNK, 1), lambda i: (0, 0, 0))
    return pl.pallas_call(
        body, name=name, grid=(n,),
        in_specs=[pl.BlockSpec((tt, wd), lambda i: (i, 0)), pl.BlockSpec((tt, wd), lambda i: (i, 1)),
                  pl.BlockSpec((tt, wd), lambda i: (i, 0)), vec, vec, wspec, bspec],
        out_specs=[pl.BlockSpec((tt, 2 * wd), lambda i: (i, 0)), vec, vec, wspec, bspec],
        out_shape=[_sds((t, 2 * wd), ACT), _sds((1, wd), F32), _sds((1, wd), F32),
                   _sds((4, CHUNK, CHUNK), F32), _sds((4, CHUNK, 1), F32)],
        scratch_shapes=[pltpu.VMEM((8, wd), F32), pltpu.VMEM((8, wd), F32)],
        compiler_params=_params(("arbitrary",), 5 * tt * wd * 4, 0, 16 * tt * CHUNK * 4),
    )(h, h, dyab, ln_g, ln_b, w_s, b_s_col)


_ROWS = 256


def _conv_taps(win, cw, lo):
    acc = jnp.zeros((_ROWS, CHUNK), F32)
    for w in range(CONV_WIDTH):
        s = lo(w)
        acc = acc + cw[w:w + 1, :] * win[s:s + _ROWS, :]
    return acc


def _conv_fwd(name, h, n_ex, cw, cb, gg, gb):
    t = h.shape[0]
    s = t // n_ex
    nt = s // _ROWS

    def body(a_ref, gt_ref, cw_ref, cb_ref, gg_ref, gb_ref, yb_ref, hh):
        hh[0:HALO, :] = jnp.zeros((HALO, CHUNK), F32)
        hh[HALO:HALO + s, :] = a_ref[...].astype(F32) * _sigmoid(gt_ref[...].astype(F32))
        cwv = cw_ref[...]

        def tile(i, carry):
            r0 = pl.multiple_of(i * _ROWS, _ROWS)
            win = hh[pl.ds(r0, _ROWS + HALO), :]
            c = _conv_taps(win, cwv, lambda w: w + HALO - (CONV_WIDTH - 1)) + cb_ref[...]
            xh, _ = _norm_stats(c)
            hg = xh * gg_ref[...] + gb_ref[...]
            yb_ref[pl.ds(r0, _ROWS), :] = (hg * _sigmoid(hg)).astype(yb_ref.dtype)
            return carry

        lax.fori_loop(0, nt, tile, 0)

    vec = pl.BlockSpec((1, CHUNK), lambda g, b: (0, g))
    return pl.pallas_call(
        body, name=name, grid=(4, n_ex),
        in_specs=[pl.BlockSpec((s, CHUNK), lambda g, b: (b, 8 + g)), pl.BlockSpec((s, CHUNK), lambda g, b: (b, 12 + g)),
                  pl.BlockSpec((CONV_WIDTH, CHUNK), lambda g, b: (0, g)), vec, vec, vec],
        out_specs=pl.BlockSpec((s, CHUNK), lambda g, b: (b, g)),
        out_shape=_sds((t, 4 * CHUNK), ACT),
        scratch_shapes=[pltpu.VMEM((s + HALO, CHUNK), F32)],
        compiler_params=_params(("parallel", "parallel"), 3 * s * CHUNK * 4, (s + HALO) * CHUNK * 4, 4 * s * CHUNK * 4),
    )(h, h, cw, cb, gg, gb)


def _conv_bwd(name, h, dyab, n_ex, cw, cb, gg, gb):
    t = h.shape[0]
    s = t // n_ex
    nt = s // _ROWS

    def body(a_ref, gt_ref, dy_ref, cw_ref, cb_ref, gg_ref, gb_ref,
             da_ref, dgt_ref, dcw_ref, dcb_ref, dgg_ref, dgb_ref, hh, dcs, acc):
        b = pl.program_id(1)

        @pl.when(b == 0)
        def _():
            dcw_ref[...] = jnp.zeros_like(dcw_ref)
            dcb_ref[...] = jnp.zeros_like(dcb_ref)
            dgg_ref[...] = jnp.zeros_like(dgg_ref)
            dgb_ref[...] = jnp.zeros_like(dgb_ref)

        hh[0:HALO, :] = jnp.zeros((HALO, CHUNK), F32)
        hh[HALO:HALO + s, :] = a_ref[...].astype(F32) * _sigmoid(gt_ref[...].astype(F32))
        dcs[s:s + HALO, :] = jnp.zeros((HALO, CHUNK), F32)
        acc[...] = jnp.zeros_like(acc)
        cwv = cw_ref[...]
        off = HALO - (CONV_WIDTH - 1)

        def tile1(i, carry):
            r0 = pl.multiple_of(i * _ROWS, _ROWS)
            win = hh[pl.ds(r0, _ROWS + HALO), :]
            c = _conv_taps(win, cwv, lambda w: w + off) + cb_ref[...]
            xh, rstd = _norm_stats(c)
            hg = xh * gg_ref[...] + gb_ref[...]
            sg = _sigmoid(hg)
            dhg = dy_ref[pl.ds(r0, _ROWS), :].astype(F32) * (sg * (1.0 + hg * (1.0 - sg)))
            acc[32:40, :] += _rows8(dhg * xh)
            acc[40:48, :] += _rows8(dhg)
            dc = _norm_bwd(dhg * gg_ref[...], xh, rstd)
            dcs[pl.ds(r0, _ROWS), :] = dc
            acc[48:56, :] += _rows8(dc)
            for w in range(CONV_WIDTH):
                acc[w:w + 1, :] += jnp.sum(dc * win[w + off:w + off + _ROWS, :], axis=0, keepdims=True)
            return carry

        lax.fori_loop(0, nt, tile1, 0)

        def tile2(i, carry):
            r0 = pl.multiple_of(i * _ROWS, _ROWS)
            win = dcs[pl.ds(r0, _ROWS + HALO), :]
            dhh = _conv_taps(win, cwv, lambda w: CONV_WIDTH - 1 - w)
            av = a_ref[pl.ds(r0, _ROWS), :].astype(F32)
            sg = _sigmoid(gt_ref[pl.ds(r0, _ROWS), :].astype(F32))
            da_ref[pl.ds(r0, _ROWS), :] = (dhh * sg).astype(da_ref.dtype)
            dgt_ref[pl.ds(r0, _ROWS), :] = (dhh * av * sg * (1.0 - sg)).astype(dgt_ref.dtype)
            return carry

        lax.fori_loop(0, nt, tile2, 0)
        dcw_ref[...] += acc[0:CONV_WIDTH, :]
        dgg_ref[...] += jnp.sum(acc[32:40, :], axis=0, keepdims=True)
        dgb_ref[...] += jnp.sum(acc[40:48, :], axis=0, keepdims=True)
        dcb_ref[...] += jnp.sum(acc[48:56, :], axis=0, keepdims=True)

    vec = pl.BlockSpec((1, CHUNK), lambda g, b: (0, g))
    tap = pl.BlockSpec((CONV_WIDTH, CHUNK), lambda g, b: (0, g))
    seq = pl.BlockSpec((s, CHUNK), lambda g, b: (b, g))
    return pl.pallas_call(
        body, name=name, grid=(4, n_ex),
        in_specs=[pl.BlockSpec((s, CHUNK), lambda g, b: (b, 8 + g)), pl.BlockSpec((s, CHUNK), lambda g, b: (b, 12 + g)),
                  pl.BlockSpec((s, CHUNK), lambda g, b: (b, 4 + g)), tap, vec, vec, vec],
        out_specs=[seq, seq, tap, vec, vec, vec],
        out_shape=[_sds((t, 4 * CHUNK), ACT), _sds((t, 4 * CHUNK), ACT), _sds((CONV_WIDTH, 4 * CHUNK), F32),
                   _sds((1, 4 * CHUNK), F32), _sds((1, 4 * CHUNK), F32), _sds((1, 4 * CHUNK), F32)],
        scratch_shapes=[pltpu.VMEM((s + HALO, CHUNK), F32), pltpu.VMEM((s + HALO, CHUNK), F32),
                        pltpu.VMEM((56, CHUNK), F32)],
        compiler_params=_params(("parallel", "arbitrary"), 5 * s * CHUNK * 4, 2 * (s + HALO) * CHUNK * 4, 4 * s * CHUNK * 4),
    )(h, h, dyab, cw, cb, gg, gb)


_TQ = 256
_SB_DEAD = -110.0


def _tri(kind):
    r = lax.broadcasted_iota(jnp.int32, (_TQ, _TQ), 0)
    c = lax.broadcasted_iota(jnp.int32, (_TQ, _TQ), 1)
    m = {"gt": r > c, "le": r <= c, "lt": r < c}[kind]
    return jnp.where(m, 1.0, 0.0).astype(jnp.bfloat16)


def _split_dot(x, tri2):
    hi = x.astype(jnp.bfloat16)
    lo = (x - hi.astype(F32)).astype(jnp.bfloat16)
    return jnp.dot(jnp.concatenate([hi, lo], axis=1), tri2, preferred_element_type=F32)


def _neg_abs(x):
    bits = lax.bitcast_convert_type(x, jnp.uint32) | jnp.uint32(0x80000000)
    return lax.bitcast_convert_type(bits, F32)


def _log_not_beta(nz):
    return jnp.minimum(nz, 0.0) - jnp.log(1.0 + jnp.exp(_neg_abs(nz)))


def _sb_fwd(name, qkv, n_ex):
    t = qkv.shape[0]
    d = qkv.shape[1] // 3
    npair = d // CHUNK
    s = t // n_ex
    nq = s // _TQ
    neg_a = -(C_HEAD_DIM ** -0.5)

    def body(q_ref, k_ref, v_ref, o_ref, lt_ref, o_acc, c_acc):
        i = pl.program_id(2)
        first = lax.broadcasted_iota(jnp.int32, (_TQ, CHUNK), 1) < C_HEAD_DIM
        q2 = q_ref[...]
        zero = jnp.zeros_like(q2)
        qh = [jnp.where(first, q2, zero), jnp.where(first, zero, q2)]
        tri2 = jnp.concatenate([_tri("gt")] * 2, axis=0)
        o_acc[...] = jnp.zeros_like(o_acc)
        c_acc[...] = jnp.zeros_like(c_acc)

        def tiles(js, mask):
            work = [(a, hd) for a in range(len(js)) for hd in range(2)]
            rows = [pl.ds(pl.multiple_of(j * _TQ, _TQ), _TQ) for j in js]
            kts = [k_ref[r, :] for r in rows]
            vts = [v_ref[r, :] for r in rows]
            nzs = {w: lax.dot_general(qh[w[1]], kts[w[0]], _DIMS["nt"], preferred_element_type=F32) * neg_a for w in work}
            lns = {w: _log_not_beta(nzs[w]) for w in work}
            if mask is not None:
                lns = {w: jnp.where(mask, lns[w], 0.0) for w in work}
            locs = {w: _split_dot(lns[w], tri2) for w in work}
            laters = {}
            for hd in range(2):
                carry = c_acc[hd]
                for a in range(len(js)):
                    laters[a, hd] = carry + locs[a, hd]
                    carry = laters[a, hd][:, 0:1] + lns[a, hd][:, 0:1]
                c_acc[hd] = carry
            atts = {w: jnp.exp(lns[w] - nzs[w] + laters[w]) for w in work}
            if mask is not None:
                atts = {w: jnp.where(mask, atts[w], 0.0) for w in work}
            for hd in range(2):
                acc = o_acc[hd]
                for a in range(len(js)):
                    acc = acc + jnp.dot(atts[a, hd].astype(MM), vts[a], preferred_element_type=F32)
                o_acc[hd] = acc

        tiles([i], lax.broadcasted_iota(jnp.int32, (_TQ, _TQ), 1) < lax.broadcasted_iota(jnp.int32, (_TQ, _TQ), 0))

        def alive():
            return jnp.max(jnp.maximum(c_acc[0], c_acc[1])) >= _SB_DEAD

        def cond(st):
            return jnp.logical_and(st[0] >= 0, st[1])

        def step(st):
            tiles([st[0]], None)
            return st[0] - 1, alive()

        j_last, _ = lax.while_loop(cond, step, (i - 1, alive()))
        o_ref[...] = jnp.where(first, o_acc[0], o_acc[1])
        lt_ref[:, 0:1] = c_acc[0]
        lt_ref[:, 1:2] = c_acc[1]
        lt_ref[:, 2:3] = jnp.full((_TQ, 1), j_last + 1, jnp.int32).astype(F32)

    return pl.pallas_call(
        body, name=name, grid=(n_ex, npair, nq),
        in_specs=[pl.BlockSpec((_TQ, CHUNK), lambda b, p, i: (b * nq + i, p)),
                  pl.BlockSpec((s, CHUNK), lambda b, p, i: (b, npair + p)),
                  pl.BlockSpec((s, CHUNK), lambda b, p, i: (b, 2 * npair + p))],
        out_specs=[pl.BlockSpec((_TQ, CHUNK), lambda b, p, i: (b * nq + i, p)),
                   pl.BlockSpec((None, _TQ, 3), lambda b, p, i: (p, b * nq + i, 0))],
        out_shape=[_sds((t, d), F32), _sds((npair, t, 3), F32)],
        scratch_shapes=[pltpu.VMEM((2, _TQ, CHUNK), F32), pltpu.VMEM((2, _TQ, 1), F32)],
        compiler_params=_params(("parallel", "parallel", "arbitrary"), 2 * s * CHUNK * 2 + 4 * _TQ * CHUNK * 4,
                                4 * _TQ * CHUNK * 4, 24 * _TQ * _TQ * 4),
    )(qkv, qkv, qkv)


def _sb_bwd(name, qkv, do, ltot, n_ex):
    t = qkv.shape[0]
    d = qkv.shape[1] // 3
    npair = d // CHUNK
    s = t // n_ex
    nq = s // _TQ
    scale = C_HEAD_DIM ** -0.5
    neg_a = -scale

    def body(q_ref, k_ref, v_ref, do_ref, lt_ref, dq_ref, dk_ref, dv_ref, dq_acc, cp_acc, cg_acc, dk_acc, dv_acc):
        i = pl.program_id(2)

        @pl.when(i == 0)
        def _():
            dk_acc[...] = jnp.zeros_like(dk_acc)
            dv_acc[...] = jnp.zeros_like(dv_acc)

        first = lax.broadcasted_iota(jnp.int32, (_TQ, CHUNK), 1) < C_HEAD_DIM
        q2 = q_ref[...]
        do2 = do_ref[...]
        qs = (q2 * scale).astype(q2.dtype)
        zero = jnp.zeros_like(q2)
        qh = [jnp.where(first, q2, zero), jnp.where(first, zero, q2)]
        doh = [jnp.where(first, do2, zero), jnp.where(first, zero, do2)]
        lt = [lt_ref[:, 0:1], lt_ref[:, 1:2]]
        tri2_le = jnp.concatenate([_tri("le")] * 2, axis=0)
        tri_lt = _tri("lt").astype(MM)
        dq_acc[...] = jnp.zeros_like(dq_acc)
        cp_acc[...] = jnp.zeros_like(cp_acc)
        cg_acc[...] = jnp.zeros_like(cg_acc)

        def tiles(js, mask):
            na = len(js)
            work = [(a, hd) for a in range(na) for hd in range(2)]
            last = slice(_TQ - 1, _TQ)
            rows = [pl.ds(pl.multiple_of(j * _TQ, _TQ), _TQ) for j in js]
            kts = [k_ref[r, :] for r in rows]
            vts = [v_ref[r, :] for r in rows]
            ksc = [(kt * scale).astype(kt.dtype) for kt in kts]
            nzs = {w: lax.dot_general(qh[w[1]], kts[w[0]], _DIMS["nt"], preferred_element_type=F32) * neg_a for w in work}
            datts = {w: lax.dot_general(doh[w[1]], vts[w[0]], _DIMS["nt"], preferred_element_type=F32) for w in work}
            lns = {w: _log_not_beta(nzs[w]) for w in work}
            if mask is not None:
                lns = {w: jnp.where(mask, lns[w], 0.0) for w in work}
            pins = {w: _split_dot(lns[w], tri2_le) for w in work}
            lss = {w: lns[w] - nzs[w] for w in work}
            atts = {}
            for hd in range(2):
                cp = cp_acc[hd]
                for a in range(na):
                    atts[a, hd] = jnp.exp(lss[a, hd] + ((lt[hd] - cp) - pins[a, hd]))
                    cp = cp + pins[a, hd][:, last]
                cp_acc[hd] = cp
            if mask is not None:
                atts = {w: jnp.where(mask, atts[w], 0.0) for w in work}
            gs = {w: datts[w] * atts[w] for w in work}
            locg = {w: jnp.dot(gs[w].astype(MM), tri_lt, preferred_element_type=F32) for w in work}
            dzs = {}
            for hd in range(2):
                carry = cg_acc[hd]
                for a in range(na):
                    big = carry + locg[a, hd]
                    dzs[a, hd] = gs[a, hd] - (gs[a, hd] + big) * jnp.exp(lss[a, hd])
                    carry = big[:, last] + gs[a, hd][:, last]
                cg_acc[hd] = carry
            if mask is not None:
                dzs = {w: jnp.where(mask, dzs[w], 0.0) for w in work}
            dzs = {w: dzs[w].astype(MM) for w in work}
            attm = {w: atts[w].astype(MM) for w in work}
            for hd in range(2):
                acc = dq_acc[hd]
                for a in range(na):
                    acc = acc + jnp.dot(dzs[a, hd], ksc[a], preferred_element_type=F32)
                dq_acc[hd] = acc
            for a in range(na):
                dk0, dk1 = [lax.dot_general(dzs[a, hd], qs, _DIMS["tn"], preferred_element_type=F32) for hd in range(2)]
                dv0, dv1 = [lax.dot_general(attm[a, hd], do2, _DIMS["tn"], preferred_element_type=F32) for hd in range(2)]
                dk_acc[rows[a], :] += jnp.where(first, dk0, dk1)
                dv_acc[rows[a], :] += jnp.where(first, dv0, dv1)

        def single(j, carry):
            tiles([j], None)
            return carry

        j_first = jnp.clip(jnp.max(lt_ref[:, 2:3]).astype(jnp.int32), 0, i)
        lax.fori_loop(j_first, i, single, 0)
        tiles([i], lax.broadcasted_iota(jnp.int32, (_TQ, _TQ), 1) < lax.broadcasted_iota(jnp.int32, (_TQ, _TQ), 0))
        dq_ref[...] = jnp.where(first, dq_acc[0], dq_acc[1]).astype(dq_ref.dtype)

        @pl.when(i == nq - 1)
        def _():
            dk_ref[...] = dk_acc[...].astype(dk_ref.dtype)
            dv_ref[...] = dv_acc[...].astype(dv_ref.dtype)

    qspec = pl.BlockSpec((_TQ, CHUNK), lambda b, p, i: (b * nq + i, p))
    kv_out = pl.BlockSpec((s, CHUNK), lambda b, p, i: (b, p))
    return pl.pallas_call(
        body, name=name, grid=(n_ex, npair, nq),
        in_specs=[qspec, pl.BlockSpec((s, CHUNK), lambda b, p, i: (b, npair + p)),
                  pl.BlockSpec((s, CHUNK), lambda b, p, i: (b, 2 * npair + p)), qspec,
                  pl.BlockSpec((None, _TQ, 3), lambda b, p, i: (p, b * nq + i, 0))],
        out_specs=[qspec, kv_out, kv_out],
        out_shape=[_sds((t, d), ACT)] * 3,
        scratch_shapes=[pltpu.VMEM((2, _TQ, CHUNK), F32), pltpu.VMEM((2, _TQ, 1), F32), pltpu.VMEM((2, _TQ, 1), F32),
                        pltpu.VMEM((s, CHUNK), F32), pltpu.VMEM((s, CHUNK), F32)],
        compiler_params=_params(("parallel", "parallel", "arbitrary"), 4 * s * CHUNK * 2,
                                4 * _TQ * CHUNK * 4 + 2 * s * CHUNK * 4, 32 * _TQ * _TQ * 4),
    )(qkv, qkv, qkv, do, ltot)


def _xattn_fwd(name, q, kk, vv, n_ex):
    t, d = q.shape
    m = kk.shape[0] // n_ex
    s = t // n_ex
    tq = _tile(s, 512)
    nq = s // tq
    hd_dim = d // MEM_HEADS
    scale = hd_dim ** -0.5

    def body(q_ref, k_ref, v_ref, o_ref):
        for hd in range(MEM_HEADS):
            ln = slice(hd * hd_dim, (hd + 1) * hd_dim)
            sc = lax.dot_general(q_ref[:, ln], k_ref[:, ln], _DIMS["nt"], preferred_element_type=F32) * scale
            p = jnp.exp(sc - jnp.max(sc, axis=-1, keepdims=True))
            p = p / jnp.sum(p, axis=-1, keepdims=True)
            o_ref[:, ln] = jnp.dot(p.astype(MM), v_ref[:, ln], preferred_element_type=F32).astype(o_ref.dtype)

    qspec = pl.BlockSpec((tq, d), lambda b, i: (b * nq + i, 0))
    kspec = pl.BlockSpec((m, d), lambda b, i: (b, 0))
    return pl.pallas_call(
        body, name=name, grid=(n_ex, nq), in_specs=[qspec, kspec, kspec], out_specs=qspec,
        out_shape=_sds((t, d), ACT),
        compiler_params=_params(("parallel", "parallel"), 2 * tq * d * 2 + 2 * m * d * 2, 0, 6 * tq * m * 4),
    )(q, kk, vv)


def _xattn_bwd(name, q, kk, vv, do, n_ex):
    t, d = q.shape
    m = kk.shape[0] // n_ex
    s = t // n_ex
    tq = _tile(s, 512)
    nq = s // tq
    hd_dim = d // MEM_HEADS
    scale = hd_dim ** -0.5

    def body(q_ref, k_ref, v_ref, do_ref, dq_ref, dk_ref, dv_ref):
        i = pl.program_id(1)

        @pl.when(i == 0)
        def _():
            dk_ref[...] = jnp.zeros_like(dk_ref)
            dv_ref[...] = jnp.zeros_like(dv_ref)

        for hd in range(MEM_HEADS):
            ln = slice(hd * hd_dim, (hd + 1) * hd_dim)
            qv, kv, vv_, dov = q_ref[:, ln], k_ref[:, ln], v_ref[:, ln], do_ref[:, ln]
            sc = lax.dot_general(qv, kv, _DIMS["nt"], preferred_element_type=F32) * scale
            p = jnp.exp(sc - jnp.max(sc, axis=-1, keepdims=True))
            p = p / jnp.sum(p, axis=-1, keepdims=True)
            dp = lax.dot_general(dov, vv_, _DIMS["nt"], preferred_element_type=F32)
            ds = (p * (dp - jnp.sum(p * dp, axis=-1, keepdims=True)) * scale).astype(MM)
            dq_ref[:, ln] = jnp.dot(ds, kv, preferred_element_type=F32).astype(dq_ref.dtype)
            dk_ref[:, ln] += lax.dot_general(ds, qv, _DIMS["tn"], preferred_element_type=F32)
            dv_ref[:, ln] += lax.dot_general(p.astype(MM), dov, _DIMS["tn"], preferred_element_type=F32)

    qspec = pl.BlockSpec((tq, d), lambda b, i: (b * nq + i, 0))
    kspec = pl.BlockSpec((m, d), lambda b, i: (b, 0))
    return pl.pallas_call(
        body, name=name, grid=(n_ex, nq), in_specs=[qspec, kspec, kspec, qspec], out_specs=[qspec, kspec, kspec],
        out_shape=[_sds((t, d), ACT), _sds((n_ex * m, d), F32), _sds((n_ex * m, d), F32)],
        compiler_params=_params(("parallel", "arbitrary"), 3 * tq * d * 2 + 2 * m * d * 2 + 2 * m * d * 4, 0,
                                8 * tq * m * 4),
    )(q, kk, vv, do)


def _ffn_up(name, y, w1, w3):
    t, d = y.shape
    f = w1.shape[-1]
    tm = _tile(t, 1024)
    wspec = pl.BlockSpec((None, d, f), lambda i, j, k: (j, 0, 0))
    hspec = pl.BlockSpec((None, tm, f), lambda i, j, k: (j, i, 0))

    def epi(vals, _):
        h1, h3 = vals
        return [h1, h3, h1 * _sigmoid(h1) * h3]

    return _mm(name, "nn", (t // tm, N_CHIPS, 1), y, pl.BlockSpec((tm, d), lambda i, j, k: (i, 0)),
               [w1, w3], [wspec, wspec], [_sds((N_CHIPS, t, f), ACT)] * 3, [hspec] * 3, (tm, f), epilogue=epi)


def _ffn_down_bwd(name, dr, w2, h1, h3):
    t, d = dr.shape
    f = w2.shape[-2]
    tm = _tile(t, 1024)
    hspec = pl.BlockSpec((None, tm, f), lambda i, j, k: (j, i, 0))

    def epi(vals, ex):
        dg, = vals
        h1v, h3v = ex
        sg = _sigmoid(h1v)
        return [dg * h3v * (sg * (1.0 + h1v * (1.0 - sg))), dg * h1v * sg]

    return _mm(name, "nt", (t // tm, N_CHIPS, 1), dr, pl.BlockSpec((tm, d), lambda i, j, k: (i, 0)),
               [w2], [pl.BlockSpec((None, f, d), lambda i, j, k: (j, 0, 0))],
               [_sds((N_CHIPS, t, f), ACT)] * 2, [hspec] * 2, (tm, f),
               extras=[h1, h3], extra_specs=[hspec, hspec], epilogue=epi)


def _ffn_up_bwd(name, dh, w, extras, epilogue):
    _, t, f = dh.shape
    d = w.shape[-2]
    tm, tn = _tile(t, 1024), _tile(d, 1024)
    ospec = pl.BlockSpec((tm, tn), lambda i, j, k: (i, j))
    return _mm(name, "nt", (t // tm, d // tn, N_CHIPS), dh, pl.BlockSpec((None, tm, f), lambda i, j, k: (k, i, 0)),
               [w], [pl.BlockSpec((None, tn, f), lambda i, j, k: (k, j, 0))],
               [_sds((t, d), F32)], [ospec], (tm, tn),
               extras=extras, extra_specs=[ospec] * len(extras), epilogue=epilogue)[0]


def _ffn_wgrad_up(name, y, dh1, dh3):
    t, d = y.shape
    f = dh1.shape[-1]
    tm, tk = _tile(d, 1024), _tile(t, 1024)
    hspec = pl.BlockSpec((None, tk, f), lambda i, j, k: (j, k, 0))
    ospec = pl.BlockSpec((None, tm, f), lambda i, j, k: (j, i, 0))
    return _mm(name, "tn", (d // tm, N_CHIPS, t // tk), y, pl.BlockSpec((tk, tm), lambda i, j, k: (k, i)),
               [dh1, dh3], [hspec, hspec], [_sds((N_CHIPS, d, f), MM)] * 2, [ospec, ospec], (tm, f))


def _ffn_wgrad_down(name, g, dr):
    _, t, f = g.shape
    d = dr.shape[1]
    tn, tk = _tile(d, 1024), _tile(t, 1024)
    return _mm(name, "tn", (N_CHIPS, d // tn, t // tk), g, pl.BlockSpec((None, tk, f), lambda i, j, k: (i, k, 0)),
               [dr], [pl.BlockSpec((tk, tn), lambda i, j, k: (k, j))],
               [_sds((N_CHIPS, f, d), MM)], [pl.BlockSpec((None, f, tn), lambda i, j, k: (i, 0, j))], (f, tn))[0]


def _proj_cols(name, y, w, out_dtype):
    t, kdim = y.shape
    wd = w.shape[-1]
    tn = _tile(wd, 512)
    per = wd // tn
    tm = _tile(t, 1024)
    return _mm(name, "nn", (t // tm, N_CHIPS * per, 1), y, pl.BlockSpec((tm, kdim), lambda i, j, k: (i, 0)),
               [w], [pl.BlockSpec((None, kdim, tn), lambda i, j, k: (j // per, 0, j % per))],
               [_sds((t, N_CHIPS * wd), out_dtype)], [pl.BlockSpec((tm, tn), lambda i, j, k: (i, j))], (tm, tn))[0]


def _proj_cols_bwd(name, dh, w, extras, epilogue):
    t = dh.shape[0]
    kdim, wd = w.shape[-2], w.shape[-1]
    tm, tn = _tile(t, 1024), _tile(kdim, 1024)
    ospec = pl.BlockSpec((tm, tn), lambda i, j, k: (i, j))
    return _mm(name, "nt", (t // tm, kdim // tn, N_CHIPS), dh, pl.BlockSpec((tm, wd), lambda i, j, k: (i, k)),
               [w], [pl.BlockSpec((None, tn, wd), lambda i, j, k: (k, j, 0))],
               [_sds((t, kdim), F32)], [ospec], (tm, tn),
               extras=extras, extra_specs=[ospec] * len(extras), epilogue=epilogue)[0]


def _proj_cols_wgrad(name, y, dh):
    t, kdim = y.shape
    wd = dh.shape[1] // N_CHIPS
    tm, tk = _tile(kdim, 1024), _tile(t, 1024)
    return _mm(name, "tn", (kdim // tm, N_CHIPS, t // tk), y, pl.BlockSpec((tk, tm), lambda i, j, k: (k, i)),
               [dh], [pl.BlockSpec((tk, wd), lambda i, j, k: (k, j))],
               [_sds((N_CHIPS, kdim, wd), MM)], [pl.BlockSpec((None, tm, wd), lambda i, j, k: (j, i, 0))], (tm, wd))[0]


def _coords():
    return lax.axis_index("x"), lax.axis_index("y"), lax.axis_index("c")


def _chip_peers(x, y):
    return [(1 - x, y), (x, 1 - y), (1 - x, 1 - y)]


_ANY = pl.BlockSpec(memory_space=pl.ANY)


def _half(ref, c):
    h = ref.shape[0] // 2
    return ref.at[pl.ds(c * h, h)]


def _gather_two_level(srcs, lands):
    n = len(srcs)

    def body(*refs):
        ins, lz = refs[:n], refs[2 * n:3 * n]
        ici_send, ici_recv, d2d_send, d2d_recv = refs[3 * n:]
        x, y, c = _coords()
        me = 2 * x + y
        peers = _chip_peers(x, y)
        ici, d2d = [], []
        for t in range(n):
            for j, chip in enumerate(peers):
                k = 3 * t + j
                ici.append(pltpu.make_async_remote_copy(
                    src_ref=_half(ins[t], c), dst_ref=_half(lz[t].at[me], c), send_sem=ici_send.at[k], recv_sem=ici_recv.at[k],
                    device_id=(*chip, c), device_id_type=MESH))
                got = _half(lz[t].at[2 * chip[0] + chip[1]], c)
                d2d.append(pltpu.make_async_remote_copy(
                    src_ref=got, dst_ref=got, send_sem=d2d_send.at[k], recv_sem=d2d_recv.at[k],
                    device_id=(x, y, 1 - c), device_id_type=MESH))
        for cp in ici:
            cp.start()
        for cp, fw in zip(ici, d2d):
            cp.wait_recv()
            fw.start()
        for cp, fw in zip(ici, d2d):
            cp.wait_send()
            fw.wait_send()
            fw.wait_recv()

    res = pl.pallas_call(
        body, name="gather_weights", in_specs=[_ANY] * (2 * n), out_specs=[_ANY] * n,
        out_shape=[_sds(a.shape, a.dtype) for a in lands], input_output_aliases={n + i: i for i in range(n)},
        scratch_shapes=[pltpu.SemaphoreType.DMA((3 * n,))] * 4,
    )(*srcs, *lands)
    return list(res)


def _scatter_chips(parts):
    n = len(parts)

    def body(*refs):
        ins, outs = refs[:n], refs[n:2 * n]
        send_sems, recv_sems = refs[2 * n:]
        x, y, c = _coords()
        copies = []
        for t in range(n):
            for j, chip in enumerate(_chip_peers(x, y)):
                copies.append(pltpu.make_async_remote_copy(
                    src_ref=ins[t].at[2 * chip[0] + chip[1]], dst_ref=outs[t].at[j], send_sem=send_sems.at[3 * t + j],
                    recv_sem=recv_sems.at[3 * t + j], device_id=(*chip, c), device_id_type=MESH))
        for cp in copies:
            cp.start()
        for cp in copies:
            cp.wait()

    return pl.pallas_call(
        body, name="scatter_grads", in_specs=[_ANY] * n, out_specs=[_ANY] * n,
        out_shape=[_sds((3, *a.shape[1:]), a.dtype) for a in parts],
        scratch_shapes=[pltpu.SemaphoreType.DMA((3 * n,))] * 2,
    )(*parts)


def _swap_halves(grads):
    n = len(grads)

    def body(*refs):
        ins, outs = refs[:n], refs[n:2 * n]
        send_sems, recv_sems = refs[2 * n:]
        x, y, c = _coords()
        copies = []
        for t in range(n):
            h = ins[t].shape[1] // 2
            copies.append(pltpu.make_async_remote_copy(
                src_ref=ins[t].at[:, pl.ds((1 - c) * h, h)], dst_ref=outs[t], send_sem=send_sems.at[t], recv_sem=recv_sems.at[t],
                device_id=(x, y, 1 - c), device_id_type=MESH))
        for cp in copies:
            cp.start()
        for cp in copies:
            cp.wait()

    return pl.pallas_call(
        body, name="swap_grad_halves", in_specs=[_ANY] * n, out_specs=[_ANY] * n,
        out_shape=[_sds((a.shape[0], a.shape[1] // 2, a.shape[2]), a.dtype) for a in grads],
        scratch_shapes=[pltpu.SemaphoreType.DMA((n,))] * 2,
    )(*grads)


def _swap_sibling(arrs):
    n = len(arrs)

    def body(*refs):
        ins, outs = refs[:n], refs[n:2 * n]
        send_sems, recv_sems = refs[2 * n:]
        x, y, c = _coords()
        copies = []
        for t in range(n):
            cp = pltpu.make_async_remote_copy(src_ref=ins[t], dst_ref=outs[t], send_sem=send_sems.at[t],
                                              recv_sem=recv_sems.at[t], device_id=(x, y, 1 - c), device_id_type=MESH)
            cp.start()
            copies.append(cp)
        for cp in copies:
            cp.wait()

    return pl.pallas_call(
        body, name="swap_sibling", in_specs=[_ANY] * n, out_specs=[_ANY] * n,
        out_shape=[_sds(a.shape, a.dtype) for a in arrs],
        scratch_shapes=[pltpu.SemaphoreType.DMA((n,)), pltpu.SemaphoreType.DMA((n,))],
    )(*arrs)


def _gather_all(part):
    def body(in_ref, out_ref, send_sems, recv_sems, loc_sem):
        x, y, c = _coords()
        dst = out_ref.at[4 * x + 2 * y + c]
        copies = [pltpu.make_async_copy(in_ref, dst, loc_sem)]
        for r in range(1, N_DEV):
            fx, fy, fc = (r >> 2) & 1, (r >> 1) & 1, r & 1
            peer = (x ^ fx, y ^ fy, c ^ fc)
            copies.append(pltpu.make_async_remote_copy(src_ref=in_ref, dst_ref=dst, send_sem=send_sems.at[r - 1],
                                                       recv_sem=recv_sems.at[r - 1], device_id=peer, device_id_type=MESH))
        for cp in copies:
            cp.start()
        for cp in copies:
            cp.wait()

    return pl.pallas_call(
        body, name="gather_small_grads", in_specs=[_ANY], out_specs=_ANY,
        out_shape=_sds((N_DEV, *part.shape), part.dtype),
        scratch_shapes=[pltpu.SemaphoreType.DMA((N_DEV - 1,)), pltpu.SemaphoreType.DMA((N_DEV - 1,)), pltpu.SemaphoreType.DMA],
    )(part)


def _sum_chips(grad, recv, me):
    _, rr, cc = recv.shape
    tr = _tile(rr, 512)

    def body(me_ref, g_ref, r0_ref, r1_ref, r2_ref, o_ref):
        o_ref[...] = ((g_ref[...].astype(F32) + r0_ref[...].astype(F32)) + r1_ref[...].astype(F32)) + r2_ref[...].astype(F32)

    gspec = pl.BlockSpec((None, tr, cc), lambda r, m: (m[0], r, 0))
    rspecs = [pl.BlockSpec((None, tr, cc), functools.partial(lambda r, m, j: (j, r, 0), j=j)) for j in range(3)]
    return pl.pallas_call(
        body, name="sum_chip_grads",
        grid_spec=pltpu.PrefetchScalarGridSpec(
            num_scalar_prefetch=1, grid=(rr // tr,), in_specs=[gspec, *rspecs],
            out_specs=pl.BlockSpec((tr, cc), lambda r, m: (r, 0))),
        out_shape=_sds((rr, cc), F32),
        compiler_params=_params(("parallel",), 4 * tr * cc * 2 + tr * cc * 4, 0, 2 * tr * cc * 4),
    )(me, grad, recv, recv, recv)


def _sum_pair(grad, sib, core):
    k, h, cc = sib.shape
    tr = _tile(h, 512)
    nb = h // tr

    def body(c_ref, g_ref, a_ref, o_ref):
        o_ref[...] = (g_ref[...].astype(F32) + a_ref[...].astype(F32)).astype(o_ref.dtype)

    spec = pl.BlockSpec((None, tr, cc), lambda s, r, c: (s, r, 0))
    return pl.pallas_call(
        body, name="sum_core_grads",
        grid_spec=pltpu.PrefetchScalarGridSpec(
            num_scalar_prefetch=1, grid=(k, nb),
            in_specs=[pl.BlockSpec((None, tr, cc), lambda s, r, c: (s, c[0] * nb + r, 0)), spec], out_specs=spec),
        out_shape=_sds(sib.shape, sib.dtype),
        compiler_params=_params(("parallel", "parallel"), 3 * tr * cc * 2, 0, 2 * tr * cc * 4),
    )(core, grad, sib)


def _adamw_math(w, g, m, v):
    m = ADAM_B1 * m + (1.0 - ADAM_B1) * g
    v = ADAM_B2 * v + (1.0 - ADAM_B2) * (g * g)
    m_hat = m / (1.0 - ADAM_B1 ** ADAM_STEP)
    v_hat = v / (1.0 - ADAM_B2 ** ADAM_STEP)
    delta = -ADAM_LR * (m_hat / (jnp.sqrt(v_hat) + ADAM_EPS) + ADAM_WD * w)
    return delta, m, v


def _adamw(name, parts, w, m, v):
    ll, rr, cc = w.shape
    tr = _tile(rr, 256)
    npart = len(parts)

    def body(*refs):
        p_refs = refs[:npart]
        w_ref, m_ref, v_ref, g_ref, d_ref, nm_ref, nv_ref = refs[npart:]
        g = p_refs[0][...]
        for p in p_refs[1:]:
            g = g + p[...]
        d, nm, nv = _adamw_math(w_ref[...], g, m_ref[...], v_ref[...])
        g_ref[...] = g
        d_ref[...] = d
        nm_ref[...] = nm
        nv_ref[...] = nv

    spec = pl.BlockSpec((None, tr, cc), lambda l, r: (l, r, 0))
    out = _sds((ll, rr, cc), F32)
    return pl.pallas_call(
        body, name=name, grid=(ll, rr // tr), in_specs=[spec] * (npart + 3), out_specs=[spec] * 4, out_shape=[out] * 4,
        compiler_params=_params(("parallel", "parallel"), (npart + 7) * tr * cc * 4, 0, 4 * tr * cc * 4),
    )(*parts, w, m, v)


def _adamw_halves(name, s_own, s_sib, core, w, m, v):
    ll, rr, cc = w.shape
    h = rr // 2
    tr = _tile(h, 256)
    nb = h // tr

    def body(c_ref, own_ref, sib_ref, w_ref, m_ref, v_ref, g_ref, d_ref, nm_ref, nv_ref):
        g = jnp.where(pl.program_id(1) == c_ref[0], own_ref[...], sib_ref[...])
        d, nm, nv = _adamw_math(w_ref[...], g, m_ref[...], v_ref[...])
        g_ref[...] = g
        d_ref[...] = d
        nm_ref[...] = nm
        nv_ref[...] = nv

    half = pl.BlockSpec((None, tr, cc), lambda l, hf, r, c: (l, r, 0))
    full = pl.BlockSpec((None, tr, cc), lambda l, hf, r, c: (l, hf * nb + r, 0))
    out = _sds((ll, rr, cc), F32)
    return pl.pallas_call(
        body, name=name,
        grid_spec=pltpu.PrefetchScalarGridSpec(num_scalar_prefetch=1, grid=(ll, 2, nb), in_specs=[half, half, full, full, full],
                                               out_specs=[full] * 4),
        out_shape=[out] * 4,
        compiler_params=_params(("parallel", "parallel", "parallel"), 9 * tr * cc * 4, 0, 4 * tr * cc * 4),
    )(core, s_own, s_sib, w, m, v)


def _sum_devices(allparts):
    _, rr, cc = allparts.shape

    def body(p_ref, o_ref):
        s = p_ref[0]
        for k in range(1, N_DEV):
            s = s + p_ref[k]
        o_ref[...] = s

    return pl.pallas_call(
        body, name="sum_small_grads", grid=(1,), in_specs=[pl.BlockSpec((N_DEV, rr, cc), lambda i: (0, 0, 0))],
        out_specs=pl.BlockSpec((rr, cc), lambda i: (0, 0)), out_shape=_sds((rr, cc), F32),
        compiler_params=_params(("arbitrary",), 9 * rr * cc * 4),
    )(allparts)


def _pack(arrs):
    flat = jnp.concatenate([a.reshape(-1).astype(F32) for a in arrs])
    n = flat.shape[0]
    total = -(-n // 1024) * 1024
    return jnp.pad(flat, (0, total - n)).reshape(total // 128, 128)


def _unpack(block, shapes):
    flat = block.reshape(-1)
    out, off = [], 0
    for sh in shapes:
        n = math.prod(sh)
        out.append(flat[off:off + n].reshape(sh))
        off += n
    return out


_BIG = ["w_in_ab", "w_out_ab", "w_qkv_c", "w_out_c", "mem_wq", "mem_wk", "mem_wv", "mem_wo", "ffn_w1", "ffn_w3", "ffn_w2"]
_ROW_SHARDED = ("w_out_ab", "w_out_c", "mem_wq", "mem_wk", "mem_wv", "mem_wo")
_SMALL_REPL = ["gmlp_ln_g", "gmlp_ln_b", "gmlp_w_s", "gmlp_b_s", "conv_b", "conv_gn_g", "conv_gn_b"]
_SMALL_SHARD = ["conv_w", "ln_g", "ln_b"]
_NAMES = ["w_in_ab", "gmlp_ln_g", "gmlp_ln_b", "gmlp_w_s", "gmlp_b_s", "conv_w", "conv_b", "conv_gn_g", "conv_gn_b",
          "w_out_ab", "w_qkv_c", "w_out_c", "mem_wq", "mem_wk", "mem_wv", "mem_wo", "ffn_w1", "ffn_w3", "ffn_w2",
          "ln_g", "ln_b"]


def _layer_weights(l):
    mixer = ["w_in_ab", "w_out_ab"] if l % 2 == 0 else ["w_qkv_c", "w_out_c"]
    return [(n, l // 2) for n in mixer] + [(n, l) for n in _BIG if n.startswith(("mem_", "ffn_"))]


def _natural(w):
    return w.reshape(-1, w.shape[-1])


def _local_step(x, mem, target, layer_w, small):
    n_ex, s, d = x.shape
    t = n_ex * s
    x2 = x.reshape(t, d)
    mem_a = mem.reshape(-1, d).astype(ACT)
    tgt = target.reshape(t, d)
    one = jnp.ones((1, d), F32)
    zero = jnp.zeros((1, d), F32)
    ln_g, ln_b = small["ln_g"], small["ln_b"]

    def vec(a):
        return a.reshape(1, -1)

    saved = []
    xh, gp, bp = x2, one, zero
    y_act = x2.astype(ACT)
    for l in range(DEPTH):
        wts = layer_w[l]
        sv = {"y0": y_act, "w": wts}
        if l % 2 == 0:
            e = l // 2
            h = _proj_cols(f"in_ab_{l}", y_act, wts["w_in_ab"], ACT)
            gl = (vec(small["gmlp_ln_g"][e]), vec(small["gmlp_ln_b"][e]), small["gmlp_w_s"][e],
                  small["gmlp_b_s"][e].reshape(4, CHUNK, 1))
            cl = (small["conv_w"][e], vec(small["conv_b"][e]), vec(small["conv_gn_g"][e]), vec(small["conv_gn_b"][e]))
            ya = _gmlp_fwd(f"gmlp_fwd_{l}", h, *gl)
            yb = _conv_fwd(f"conv_fwd_{l}", h, n_ex, *cl)
            yab = jnp.concatenate([ya, yb], axis=1)
            mixed, w_mix = yab, _natural(wts["w_out_ab"])
            sv.update(h=h, yab=yab, gl=gl, cl=cl)
        else:
            qkv = _proj_cols(f"qkv_{l}", y_act, wts["w_qkv_c"], ACT)
            att, ltot = _sb_fwd(f"sb_fwd_{l}", qkv, n_ex)
            att_a = att.astype(ACT)
            mixed, w_mix = att_a, _natural(wts["w_out_c"])
            sv.update(qkv=qkv, att=att_a, ltot=ltot)
        g1, b1 = vec(ln_g[l, 0]), vec(ln_b[l, 0])
        xh1, y1, rstd1 = _dense_ln(f"mix_out_ln1_{l}", mixed, w_mix, xh, gp, bp, g1, b1)
        q = _dense(f"mem_q_{l}", y1, _natural(wts["mem_wq"]), ACT)
        kk = _dense(f"mem_k_{l}", mem_a, _natural(wts["mem_wk"]), ACT)
        vv = _dense(f"mem_v_{l}", mem_a, _natural(wts["mem_wv"]), ACT)
        oc = _xattn_fwd(f"xattn_fwd_{l}", q, kk, vv, n_ex)
        g2, b2 = vec(ln_g[l, 1]), vec(ln_b[l, 1])
        xh2, y2, rstd2 = _dense_ln(f"mem_o_ln2_{l}", oc, _natural(wts["mem_wo"]), xh1, g1, b1, g2, b2)
        h1, h3, gact = _ffn_up(f"ffn_up_{l}", y2, wts["ffn_w1"], wts["ffn_w3"])
        g3, b3 = vec(ln_g[l, 2]), vec(ln_b[l, 2])
        xh3, y3, rstd3 = _ffn_down_ln(f"ffn_down_ln3_{l}", gact, wts["ffn_w2"], xh2, g2, b2, g3, b3)
        sv.update(xh1=xh1, y1=y1, rstd1=rstd1, g1=g1, q=q, kk=kk, vv=vv, oc=oc, xh2=xh2, y2=y2, rstd2=rstd2, g2=g2,
                  h1=h1, h3=h3, gact=gact, xh3=xh3, rstd3=rstd3, g3=g3)
        saved.append(sv)
        xh, gp, bp, y_act = xh3, g3, b3, y3

    dy, loss = _loss_head(xh, gp, bp, tgt)

    sm = {n: [None] * (DEPTH // 2) for n in _SMALL_REPL + ["conv_w"]}
    d_ln_g = [[None] * 3 for _ in range(DEPTH)]
    d_ln_b = [[None] * 3 for _ in range(DEPTH)]

    def add_res(vals, ex):
        return [vals[0] + ALPHA * ex[0]]

    def add_res2(vals, ex):
        return [vals[0] + ex[0] + ALPHA * ex[1]]

    layer_g = [None] * DEPTH
    for l in reversed(range(DEPTH)):
        sv = saved[l]
        wts = sv["w"]
        big = {}
        dr3, dr3a, d_ln_g[l][2], d_ln_b[l][2] = _ln_bwd(f"ln3_bwd_{l}", dy, sv["xh3"], sv["rstd3"], sv["g3"])
        big["ffn_w2"] = _ffn_wgrad_down(f"ffn_w2_grad_{l}", sv["gact"], dr3a)
        dh1, dh3 = _ffn_down_bwd(f"ffn_down_bwd_{l}", dr3a, wts["ffn_w2"], sv["h1"], sv["h3"])
        big["ffn_w1"], big["ffn_w3"] = _ffn_wgrad_up(f"ffn_w13_grad_{l}", sv["y2"], dh1, dh3)
        part = _ffn_up_bwd(f"ffn_up_bwd1_{l}", dh1, wts["ffn_w1"], [], None)
        dy = _ffn_up_bwd(f"ffn_up_bwd3_{l}", dh3, wts["ffn_w3"], [part, dr3], add_res2)
        dr2, dr2a, d_ln_g[l][1], d_ln_b[l][1] = _ln_bwd(f"ln2_bwd_{l}", dy, sv["xh2"], sv["rstd2"], sv["g2"])
        big["mem_wo"] = _dense_tn(f"mem_wo_grad_{l}", sv["oc"], dr2a)
        doc = _dense_nt(f"mem_o_bwd_{l}", dr2a, _natural(wts["mem_wo"]), ACT)
        dq, dkk, dvv = _xattn_bwd(f"xattn_bwd_{l}", sv["q"], sv["kk"], sv["vv"], doc, n_ex)
        big["mem_wq"] = _dense_tn(f"mem_wq_grad_{l}", sv["y1"], dq)
        big["mem_wk"] = _dense_tn(f"mem_wk_grad_{l}", mem_a, dkk)
        big["mem_wv"] = _dense_tn(f"mem_wv_grad_{l}", mem_a, dvv)
        dy = _dense_nt(f"mem_q_bwd_{l}", dq, _natural(wts["mem_wq"]), F32, extras=[dr2], epilogue=add_res)
        dr1, dr1a, d_ln_g[l][0], d_ln_b[l][0] = _ln_bwd(f"ln1_bwd_{l}", dy, sv["xh1"], sv["rstd1"], sv["g1"])
        if l % 2 == 0:
            e = l // 2
            big["w_out_ab"] = _dense_tn(f"out_ab_grad_{l}", sv["yab"], dr1a)
            dyab = _dense_nt(f"out_ab_bwd_{l}", dr1a, _natural(wts["w_out_ab"]), ACT)
            duv, dgg, dgb, dws, dbs = _gmlp_bwd(f"gmlp_bwd_{l}", sv["h"], dyab, *sv["gl"])
            da, dgt, dcw, dcb, dng, dnb = _conv_bwd(f"conv_bwd_{l}", sv["h"], dyab, n_ex, *sv["cl"])
            sm["gmlp_ln_g"][e], sm["gmlp_ln_b"][e] = dgg.reshape(-1), dgb.reshape(-1)
            sm["gmlp_w_s"][e], sm["gmlp_b_s"][e] = dws, dbs.reshape(4, CHUNK)
            sm["conv_w"][e], sm["conv_b"][e] = dcw, dcb.reshape(-1)
            sm["conv_gn_g"][e], sm["conv_gn_b"][e] = dng.reshape(-1), dnb.reshape(-1)
            dh = jnp.concatenate([duv, da, dgt], axis=1)
            big["w_in_ab"] = _proj_cols_wgrad(f"in_ab_grad_{l}", sv["y0"], dh)
            dy = _proj_cols_bwd(f"in_ab_bwd_{l}", dh, wts["w_in_ab"], [dr1], add_res)
        else:
            big["w_out_c"] = _dense_tn(f"out_c_grad_{l}", sv["att"], dr1a)
            datt = _dense_nt(f"out_c_bwd_{l}", dr1a, _natural(wts["w_out_c"]), ACT)
            dq_, dk_, dv_ = _sb_bwd(f"sb_bwd_{l}", sv["qkv"], datt, sv["ltot"], n_ex)
            dqkv = jnp.concatenate([dq_, dk_, dv_], axis=1)
            big["w_qkv_c"] = _proj_cols_wgrad(f"qkv_grad_{l}", sv["y0"], dqkv)
            dy = _proj_cols_bwd(f"qkv_bwd_{l}", dqkv, wts["w_qkv_c"], [dr1], add_res)
        for n in _ROW_SHARDED:
            if n in big:
                big[n] = big[n].reshape(N_CHIPS, -1, big[n].shape[-1])
        layer_g[l] = big

    grad_x = dy.reshape(n_ex, s, d)
    small_g = {n: jnp.stack(sm[n]) for n in sm}
    small_g["ln_g"] = jnp.stack([jnp.concatenate(r, axis=0) for r in d_ln_g])
    small_g["ln_b"] = jnp.stack([jnp.concatenate(r, axis=0) for r in d_ln_b])
    return loss, grad_x, layer_g, small_g


def kernel(x, mem, w_in_ab, gmlp_ln_g, gmlp_ln_b, gmlp_w_s, gmlp_b_s, conv_w, conv_b, conv_gn_g, conv_gn_b, w_out_ab, w_qkv_c, w_out_c, mem_wq, mem_wk, mem_wv, mem_wo, ffn_w1, ffn_w3, ffn_w2, ln_g, ln_b, loss_target, m_w_in_ab, m_gmlp_ln_g, m_gmlp_ln_b, m_gmlp_w_s, m_gmlp_b_s, m_conv_w, m_conv_b, m_conv_gn_g, m_conv_gn_b, m_w_out_ab, m_w_qkv_c, m_w_out_c, m_mem_wq, m_mem_wk, m_mem_wv, m_mem_wo, m_ffn_w1, m_ffn_w3, m_ffn_w2, m_ln_g, m_ln_b, v_w_in_ab, v_gmlp_ln_g, v_gmlp_ln_b, v_gmlp_w_s, v_gmlp_b_s, v_conv_w, v_conv_b, v_conv_gn_g, v_conv_gn_b, v_w_out_ab, v_w_qkv_c, v_w_out_c, v_mem_wq, v_mem_wk, v_mem_wv, v_mem_wo, v_ffn_w1, v_ffn_w3, v_ffn_w2, v_ln_g, v_ln_b):
    w = dict(w_in_ab=w_in_ab, gmlp_ln_g=gmlp_ln_g, gmlp_ln_b=gmlp_ln_b, gmlp_w_s=gmlp_w_s, gmlp_b_s=gmlp_b_s, conv_w=conv_w,
             conv_b=conv_b, conv_gn_g=conv_gn_g, conv_gn_b=conv_gn_b, w_out_ab=w_out_ab, w_qkv_c=w_qkv_c, w_out_c=w_out_c,
             mem_wq=mem_wq, mem_wk=mem_wk, mem_wv=mem_wv, mem_wo=mem_wo, ffn_w1=ffn_w1, ffn_w3=ffn_w3, ffn_w2=ffn_w2,
             ln_g=ln_g, ln_b=ln_b)
    mo = dict(w_in_ab=m_w_in_ab, gmlp_ln_g=m_gmlp_ln_g, gmlp_ln_b=m_gmlp_ln_b, gmlp_w_s=m_gmlp_w_s, gmlp_b_s=m_gmlp_b_s,
              conv_w=m_conv_w, conv_b=m_conv_b, conv_gn_g=m_conv_gn_g, conv_gn_b=m_conv_gn_b, w_out_ab=m_w_out_ab,
              w_qkv_c=m_w_qkv_c, w_out_c=m_w_out_c, mem_wq=m_mem_wq, mem_wk=m_mem_wk, mem_wv=m_mem_wv, mem_wo=m_mem_wo,
              ffn_w1=m_ffn_w1, ffn_w3=m_ffn_w3, ffn_w2=m_ffn_w2, ln_g=m_ln_g, ln_b=m_ln_b)
    vo = dict(w_in_ab=v_w_in_ab, gmlp_ln_g=v_gmlp_ln_g, gmlp_ln_b=v_gmlp_ln_b, gmlp_w_s=v_gmlp_w_s, gmlp_b_s=v_gmlp_b_s,
              conv_w=v_conv_w, conv_b=v_conv_b, conv_gn_g=v_conv_gn_g, conv_gn_b=v_conv_gn_b, w_out_ab=v_w_out_ab,
              w_qkv_c=v_w_qkv_c, w_out_c=v_w_out_c, mem_wq=v_mem_wq, mem_wk=v_mem_wk, mem_wv=v_mem_wv, mem_wo=v_mem_wo,
              ffn_w1=v_ffn_w1, ffn_w3=v_ffn_w3, ffn_w2=v_ffn_w2, ln_g=v_ln_g, ln_b=v_ln_b)
    me = (2 * lax.axis_index("x") + lax.axis_index("y")).astype(jnp.int32).reshape(1)

    per_layer = [_layer_weights(l) for l in range(DEPTH)]
    srcs = [w[n] for n in _SMALL_SHARD] + [w[n][i].astype(MM) for lw in per_layer for n, i in lw]
    lands = [lax.dynamic_update_index_in_dim(jnp.zeros((N_CHIPS, *s.shape), s.dtype), s[None], me[0], 0) for s in srcs]
    gathered = _gather_two_level(srcs, lands)
    cw_g, lg_g, lb_g = gathered[:3]
    small = {n: w[n] for n in _SMALL_REPL}
    small["conv_w"] = jnp.moveaxis(cw_g, 0, 2).reshape(cw_g.shape[1], CONV_WIDTH, -1)
    small["ln_g"] = jnp.moveaxis(lg_g, 0, 2).reshape(DEPTH, 3, -1)
    small["ln_b"] = jnp.moveaxis(lb_g, 0, 2).reshape(DEPTH, 3, -1)
    it = iter(gathered[3:])
    layer_w = [{n: next(it) for n, _ in lw} for lw in per_layer]

    loss, grad_x, layer_g, small_g = _local_step(x, mem, loss_target, layer_w, small)
    loss = lax.psum(loss, ("x", "y", "c"))

    core = lax.axis_index("c").astype(jnp.int32).reshape(1)
    grads_flat = [layer_g[l][n] for l in range(DEPTH) for n, _ in per_layer[l]]
    pair = [_sum_pair(g, a, core) for g, a in zip(grads_flat, _swap_halves(grads_flat))]
    recv = _scatter_chips(pair)
    per_name = {n: [None] * (DEPTH if n.startswith(("mem_", "ffn_")) else DEPTH // 2) for n in _BIG}
    flat_names = [ni for lw in per_layer for ni in lw]
    for (n, i), p, rc in zip(flat_names, pair, recv):
        per_name[n][i] = _sum_chips(p, rc, me)
    sums = [jnp.stack(per_name[n]) for n in _BIG]
    sib = _swap_sibling(sums)

    out = {}
    for n, s_own, s_sib in zip(_BIG, sums, sib):
        out[n] = _adamw_halves(f"adamw_{n}", s_own, s_sib, core, w[n], mo[n], vo[n])

    order = _SMALL_REPL + _SMALL_SHARD
    part = _pack([small_g[n] for n in order])
    total = _sum_devices(_gather_all(part))
    full = dict(zip(order, _unpack(total, [small_g[n].shape for n in order])))
    x_i, y_i = lax.axis_index("x"), lax.axis_index("y")
    chip = 2 * x_i + y_i
    loc = {n: full[n] for n in _SMALL_REPL}
    for n in _SMALL_SHARD:
        wd = w[n].shape[-1]
        loc[n] = lax.dynamic_slice_in_dim(full[n], chip * wd, wd, axis=full[n].ndim - 1)
    gp, wp, mp, vp = (_pack([src[n] for n in order]) for src in (loc, w, mo, vo))
    r128 = gp.shape[0]
    res = _adamw("adamw_small", [gp.reshape(1, r128, 128)], wp.reshape(1, r128, 128), mp.reshape(1, r128, 128),
                 vp.reshape(1, r128, 128))
    shapes = [w[n].shape for n in order]
    unp = [_unpack(r.reshape(r128, 128), shapes) for r in res]
    for i, n in enumerate(order):
        out[n] = tuple(u[i] for u in unp)

    grads = [out[n][0] for n in _NAMES]
    deltas = [out[n][1] for n in _NAMES]
    new_m = [out[n][2] for n in _NAMES]
    new_v = [out[n][3] for n in _NAMES]
    return (loss, grad_x, *grads, *deltas, *new_m, *new_v)
```

```python
import functools
import math

import jax
import jax.numpy as jnp
from jax import lax
from jax.experimental import pallas as pl
from jax.experimental.pallas import tpu as pltpu

F32 = jnp.float32
MM = jnp.bfloat16
ACT = jnp.bfloat16
MESH = pl.DeviceIdType.MESH

DEPTH = 4
CHUNK = 128
CONV_WIDTH = 31
HALO = 32
MEM_HEADS = 4
C_HEAD_DIM = 64
ALPHA = (2.0 * DEPTH) ** 0.25
LN_EPS = 1e-5
ADAM_LR, ADAM_B1, ADAM_B2, ADAM_EPS, ADAM_WD, ADAM_STEP = 0.001, 0.9, 0.999, 1e-08, 0.01, 10

VMEM_CAP_V7X = 64 * 1024 * 1024
VMEM_MAX_REQUEST = 56 * 1024 * 1024
N_CHIPS = 4
N_DEV = 8


def _tile(n, pref):
    if n <= pref:
        return n
    for t in range(pref - pref % 8, 7, -8):
        if n % t == 0:
            return t
    return n


def _nbytes(shape, dtype):
    return math.prod(1 if s is None else s for s in shape) * jnp.dtype(dtype).itemsize


def _vmem_limit(block_bytes, scratch_bytes=0, temp_bytes=0):
    est = 2 * block_bytes + scratch_bytes + temp_bytes
    return int(min(VMEM_MAX_REQUEST, max(16 * 1024 * 1024, est * 5 // 4)))


def _params(sem, block_bytes, scratch_bytes=0, temp_bytes=0):
    return pltpu.CompilerParams(dimension_semantics=sem,
                                vmem_limit_bytes=_vmem_limit(block_bytes, scratch_bytes, temp_bytes))


_DIMS = {"nn": (((1,), (0,)), ((), ())), "nt": (((1,), (1,)), ((), ())), "tn": (((0,), (0,)), ((), ()))}


def _mm(name, mode, grid, a, a_spec, bs, b_specs, outs, out_specs, acc_shape,
        extras=(), extra_specs=(), epilogue=None):
    nb, ne, no = len(bs), len(extras), len(outs)
    nk = grid[2]

    def body(*refs):
        a_ref = refs[0]
        b_refs = refs[1:1 + nb]
        e_refs = refs[1 + nb:1 + nb + ne]
        o_refs = refs[1 + nb + ne:1 + nb + ne + no]
        accs = refs[1 + nb + ne + no:]

        def finish(vals):
            if epilogue is not None:
                vals = epilogue(vals, [e[...].astype(F32) for e in e_refs])
            for o, v in zip(o_refs, vals):
                o[...] = v.astype(o.dtype)

        av = a_ref[...].astype(MM)
        if nk == 1:
            finish([lax.dot_general(av, b_ref[...].astype(MM), _DIMS[mode], preferred_element_type=F32) for b_ref in b_refs])
            return
        k = pl.program_id(2)

        @pl.when(k == 0)
        def _():
            for acc in accs:
                acc[...] = jnp.zeros_like(acc)

        for b_ref, acc in zip(b_refs, accs):
            acc[...] += lax.dot_general(av, b_ref[...].astype(MM), _DIMS[mode], preferred_element_type=F32)

        @pl.when(k == nk - 1)
        def _():
            finish([acc[...] for acc in accs])

    blocks = (_nbytes(a_spec.block_shape, a.dtype)
              + sum(_nbytes(s.block_shape, b.dtype) for s, b in zip(b_specs, bs))
              + sum(_nbytes(s.block_shape, e.dtype) for s, e in zip(extra_specs, extras))
              + sum(_nbytes(s.block_shape, o.dtype) for s, o in zip(out_specs, outs)))
    acc_bytes = nb * _nbytes(acc_shape, F32)
    res = pl.pallas_call(
        body, name=name, grid=grid,
        in_specs=[a_spec, *b_specs, *extra_specs], out_specs=list(out_specs), out_shape=list(outs),
        scratch_shapes=[pltpu.VMEM(acc_shape, F32)] * (nb if nk > 1 else 0),
        compiler_params=_params(("parallel", "parallel", "arbitrary"), blocks, acc_bytes if nk > 1 else 0, 4 * acc_bytes),
    )(a, *bs, *extras)
    return res


def _sds(shape, dtype):
    return jax.ShapeDtypeStruct(shape, dtype)


def _dense(name, a, w, out_dtype):
    t, kdim = a.shape
    n = w.shape[1]
    tm, tn, tk = _tile(t, 1024), _tile(n, 1024), _tile(kdim, 1024)
    grid = (t // tm, n // tn, kdim // tk)
    return _mm(name, "nn", grid, a, pl.BlockSpec((tm, tk), lambda i, j, k: (i, k)),
               [w], [pl.BlockSpec((tk, tn), lambda i, j, k: (k, j))],
               [_sds((t, n), out_dtype)], [pl.BlockSpec((tm, tn), lambda i, j, k: (i, j))], (tm, tn))[0]


def _dense_nt(name, a, w, out_dtype, extras=(), epilogue=None):
    t, n = a.shape
    kout = w.shape[0]
    tm, tn, tk = _tile(t, 1024), _tile(kout, 1024), _tile(n, 1024)
    grid = (t // tm, kout // tn, n // tk)
    return _mm(name, "nt", grid, a, pl.BlockSpec((tm, tk), lambda i, j, k: (i, k)),
               [w], [pl.BlockSpec((tn, tk), lambda i, j, k: (j, k))],
               [_sds((t, kout), out_dtype)], [pl.BlockSpec((tm, tn), lambda i, j, k: (i, j))], (tm, tn),
               extras=extras, extra_specs=[pl.BlockSpec((tm, tn), lambda i, j, k: (i, j))] * len(extras),
               epilogue=epilogue)[0]


def _dense_tn(name, a, b, out_dtype=MM):
    t, m = a.shape
    n = b.shape[1]
    tm, tn, tk = _tile(m, 1024), _tile(n, 1024), _tile(t, 1024)
    grid = (m // tm, n // tn, t // tk)
    return _mm(name, "tn", grid, a, pl.BlockSpec((tk, tm), lambda i, j, k: (k, i)),
               [b], [pl.BlockSpec((tk, tn), lambda i, j, k: (k, j))],
               [_sds((m, n), out_dtype)], [pl.BlockSpec((tm, tn), lambda i, j, k: (i, j))], (tm, tn))[0]


_INV_SQRT2 = 0.7071067811865476
_INV_SQRT_2PI = 0.3989422804014327


def _gelu(x):
    return 0.5 * x * (1.0 + lax.erf(x * _INV_SQRT2))


def _gelu_grad(x):
    return 0.5 * (1.0 + lax.erf(x * _INV_SQRT2)) + x * jnp.exp(-0.5 * x * x) * _INV_SQRT_2PI


def _sigmoid(x):
    return 1.0 / (1.0 + jnp.exp(-x))


def _norm_stats(x):
    mu = jnp.mean(x, axis=-1, keepdims=True)
    xc = x - mu
    var = jnp.mean(xc * xc, axis=-1, keepdims=True)
    rstd = lax.rsqrt(var + LN_EPS)
    return xc * rstd, rstd


def _norm_bwd(dy_g, xh, rstd):
    m1 = jnp.mean(dy_g, axis=-1, keepdims=True)
    m2 = jnp.mean(dy_g * xh, axis=-1, keepdims=True)
    return rstd * (dy_g - m1 - xh * m2)


def _rows8(x):
    r, c = x.shape
    return jnp.sum(x.reshape(r // 8, 8, c), axis=0)


def _ln_epilogue(vals, ex):
    xp, gp, bp, g, b = ex
    xh, rstd = _norm_stats(ALPHA * (xp * gp + bp) + vals[0])
    return [xh, xh * g + b, rstd]


def _proj_ln(name, a, a_spec_of, w, w_spec, nk, xh_prev, g_prev, b_prev, g, b):
    t, d = xh_prev.shape
    tm = _tile(t, 1024)
    row = pl.BlockSpec((tm, d), lambda i, j, k: (i, 0))
    vec = pl.BlockSpec((1, d), lambda i, j, k: (0, 0))
    return _mm(name, "nn", (t // tm, 1, nk), a, a_spec_of(tm), [w], [w_spec],
               [_sds((t, d), F32), _sds((t, d), ACT), _sds((t, 1), F32)],
               [row, row, pl.BlockSpec((tm, 1), lambda i, j, k: (i, 0))], (tm, d),
               extras=[xh_prev, g_prev, b_prev, g, b], extra_specs=[row, vec, vec, vec, vec], epilogue=_ln_epilogue)


def _dense_ln(name, a, w, *ln_args):
    kdim, d = w.shape
    return _proj_ln(name, a, lambda tm: pl.BlockSpec((tm, kdim), lambda i, j, k: (i, 0)), w,
                    pl.BlockSpec((kdim, d), lambda i, j, k: (0, 0)), 1, *ln_args)


def _ffn_down_ln(name, gact, w2, *ln_args):
    f, d = w2.shape[-2:]
    return _proj_ln(name, gact, lambda tm: pl.BlockSpec((None, tm, f), lambda i, j, k: (k, i, 0)), w2,
                    pl.BlockSpec((None, f, d), lambda i, j, k: (k, 0, 0)), N_CHIPS, *ln_args)


def _ln_bwd(name, dy, xh, rstd, g):
    t, d = dy.shape
    tm = _tile(t, 512)
    n = t // tm

    def body(dy_ref, xh_ref, rstd_ref, g_ref, dr_ref, dra_ref, dg_ref, db_ref, dg_acc, db_acc):
        i = pl.program_id(0)

        @pl.when(i == 0)
        def _():
            dg_acc[...] = jnp.zeros_like(dg_acc)
            db_acc[...] = jnp.zeros_like(db_acc)

        dyv = dy_ref[...]
        xhv = xh_ref[...]
        dr = _norm_bwd(dyv * g_ref[...], xhv, rstd_ref[...])
        dr_ref[...] = dr
        dra_ref[...] = dr.astype(dra_ref.dtype)
        dg_acc[...] += _rows8(dyv * xhv)
        db_acc[...] += _rows8(dyv)

        @pl.when(i == n - 1)
        def _():
            dg_ref[...] = jnp.sum(dg_acc[...], axis=0, keepdims=True)
            db_ref[...] = jnp.sum(db_acc[...], axis=0, keepdims=True)

    row = pl.BlockSpec((tm, d), lambda i: (i, 0))
    vec = pl.BlockSpec((1, d), lambda i: (0, 0))
    return pl.pallas_call(
        body, name=name, grid=(n,),
        in_specs=[row, row, pl.BlockSpec((tm, 1), lambda i: (i, 0)), vec],
        out_specs=[row, row, vec, vec],
        out_shape=[_sds((t, d), F32), _sds((t, d), ACT), _sds((1, d), F32), _sds((1, d), F32)],
        scratch_shapes=[pltpu.VMEM((8, d), F32), pltpu.VMEM((8, d), F32)],
        compiler_params=_params(("arbitrary",), 4 * tm * d * 4, 0, 4 * tm * d * 4),
    )(dy, xh, rstd, g)


def _loss_head(xh, g, b, target):
    t, d = xh.shape
    tm = _tile(t, 512)
    n = t // tm

    def body(xh_ref, g_ref, b_ref, tg_ref, dy_ref, loss_ref, acc):
        i = pl.program_id(0)

        @pl.when(i == 0)
        def _():
            acc[...] = jnp.zeros_like(acc)

        err = xh_ref[...] * g_ref[...] + b_ref[...] - tg_ref[...]
        dy_ref[...] = err * (1.0 / d)
        acc[...] += _rows8(err * err)

        @pl.when(i == n - 1)
        def _():
            s = jnp.sum(jnp.sum(acc[...], axis=0, keepdims=True), axis=1, keepdims=True)
            loss_ref[...] = jnp.broadcast_to(s * (0.5 / d), loss_ref.shape)

    row = pl.BlockSpec((tm, d), lambda i: (i, 0))
    vec = pl.BlockSpec((1, d), lambda i: (0, 0))
    dy, loss = pl.pallas_call(
        body, name="loss_head", grid=(n,),
        in_specs=[row, vec, vec, row],
        out_specs=[row, pl.BlockSpec((8, 128), lambda i: (0, 0))],
        out_shape=[_sds((t, d), F32), _sds((8, 128), F32)],
        scratch_shapes=[pltpu.VMEM((8, d), F32)],
        compiler_params=_params(("arbitrary",), 3 * tm * d * 4, 0, 2 * tm * d * 4),
    )(xh, g, b, target)
    return dy, loss[0, 0]


def _causal_w(w):
    r = lax.broadcasted_iota(jnp.int32, w.shape, 0)
    c = lax.broadcasted_iota(jnp.int32, w.shape, 1)
    return jnp.where(r >= c, w, 0.0)


def _gmlp_fwd(name, h, ln_g, ln_b, w_s, b_s_col):
    t = h.shape[0]
    tt = _tile(t, 256)
    wd = 4 * CHUNK

    def body(u_ref, v_ref, g_ref, b_ref, w_ref, bs_ref, ya_ref):
        for gi in range(4):
            ln = slice(gi * CHUNK, (gi + 1) * CHUNK)
            u = _gelu(u_ref[:, ln].astype(F32))
            v = _gelu(v_ref[:, ln].astype(F32))
            xh, _ = _norm_stats(v)
            vg = (xh * g_ref[:, ln] + b_ref[:, ln]).astype(MM)
            w = _causal_w(w_ref[gi]).astype(MM)
            for c in range(tt // CHUNK):
                rs = slice(c * CHUNK, (c + 1) * CHUNK)
                mixed = jnp.dot(w, vg[rs], preferred_element_type=F32) + bs_ref[gi]
                ya_ref[rs, ln] = (u[rs] * mixed).astype(ya_ref.dtype)

    vec = pl.BlockSpec((1, wd), lambda i: (0, 0))
    return pl.pallas_call(
        body, name=name, grid=(t // tt,),
        in_specs=[pl.BlockSpec((tt, wd), lambda i: (i, 0)), pl.BlockSpec((tt, wd), lambda i: (i, 1)), vec, vec,
                  pl.BlockSpec((4, CHUNK, CHUNK), lambda i: (0, 0, 0)), pl.BlockSpec((4, CHUNK, 1), lambda i: (0, 0, 0))],
        out_specs=pl.BlockSpec((tt, wd), lambda i: (i, 0)),
        out_shape=_sds((t, wd), ACT),
        compiler_params=_params(("parallel",), 3 * tt * wd * 4, 0, 8 * tt * CHUNK * 4),
    )(h, h, ln_g, ln_b, w_s, b_s_col)


def _gmlp_bwd(name, h, dyab, ln_g, ln_b, w_s, b_s_col):
    t = h.shape[0]
    tt = _tile(t, 256)
    n = t // tt
    wd = 4 * CHUNK

    def body(u_ref, v_ref, dy_ref, g_ref, b_ref, w_ref, bs_ref, duv_ref, dg_ref, db_ref, dw_ref, dbs_ref,
             dg_acc, db_acc):
        i = pl.program_id(0)

        @pl.when(i == 0)
        def _():
            dg_acc[...] = jnp.zeros_like(dg_acc)
            db_acc[...] = jnp.zeros_like(db_acc)
            dw_ref[...] = jnp.zeros_like(dw_ref)
            dbs_ref[...] = jnp.zeros_like(dbs_ref)

        for gi in range(4):
            ln = slice(gi * CHUNK, (gi + 1) * CHUNK)
            upre = u_ref[:, ln].astype(F32)
            vpre = v_ref[:, ln].astype(F32)
            u = _gelu(upre)
            v = _gelu(vpre)
            xh, rstd = _norm_stats(v)
            gv = g_ref[:, ln]
            vg = (xh * gv + b_ref[:, ln]).astype(MM)
            w = _causal_w(w_ref[gi]).astype(MM)
            dya = dy_ref[:, ln].astype(F32)
            dmixed = dya * u
            dmm = dmixed.astype(MM)
            dvg_parts, mixed_parts = [], []
            dw = jnp.zeros((CHUNK, CHUNK), F32)
            dbs = jnp.zeros((CHUNK, 1), F32)
            for c in range(tt // CHUNK):
                rs = slice(c * CHUNK, (c + 1) * CHUNK)
                mixed_parts.append(jnp.dot(w, vg[rs], preferred_element_type=F32) + bs_ref[gi])
                dw = dw + lax.dot_general(dmm[rs], vg[rs], _DIMS["nt"], preferred_element_type=F32)
                dbs = dbs + jnp.sum(dmixed[rs], axis=1, keepdims=True)
                dvg_parts.append(lax.dot_general(w, dmm[rs], _DIMS["tn"], preferred_element_type=F32))
            mixed = jnp.concatenate(mixed_parts, axis=0)
            dvg = jnp.concatenate(dvg_parts, axis=0)
            dw_ref[gi] += _causal_w(dw)
            dbs_ref[gi] += dbs
            dg_acc[:, ln] += _rows8(dvg * xh)
            db_acc[:, ln] += _rows8(dvg)
            dv = _norm_bwd(dvg * gv, xh, rstd) * _gelu_grad(vpre)
            du = dya * mixed * _gelu_grad(upre)
            duv_ref[:, ln] = du.astype(duv_ref.dtype)
            duv_ref[:, wd + gi * CHUNK: wd + (gi + 1) * CHUNK] = dv.astype(duv_ref.dtype)

        @pl.when(i == n - 1)
        def _():
            dg_ref[...] = jnp.sum(dg_acc[...], axis=0, keepdims=True)
            db_ref[...] = jnp.sum(db_acc[...], axis=0, keepdims=True)

    vec = pl.BlockSpec((1, wd), lambda i: (0, 0))
    wspec = pl.BlockSpec((4, CHUNK, CHUNK), lambda i: (0, 0, 0))
    bspec = pl.BlockSpec((4, CHUNK, 1), lambda i: (0, 0, 0))
    return pl.pallas_call(
        body, name=name, grid=(n,),
        in_specs=[pl.BlockSpec((tt, wd), lambda i: (i, 0)), pl.BlockSpec((tt, wd), lambda i: (i, 1)),
                  pl.BlockSpec((tt, wd), lambda i: (i, 0)), vec, vec, wspec, bspec],
        out_specs=[pl.BlockSpec((tt, 2 * wd), lambda i: (i, 0)), vec, vec, wspec, bspec],
        out_shape=[_sds((t, 2 * wd), ACT), _sds((1, wd), F32), _sds((1, wd), F32),
                   _sds((4, CHUNK, CHUNK), F32), _sds((4, CHUNK, 1), F32)],
        scratch_shapes=[pltpu.VMEM((8, wd), F32), pltpu.VMEM((8, wd), F32)],
        compiler_params=_params(("arbitrary",), 5 * tt * wd * 4, 0, 16 * tt * CHUNK * 4),
    )(h, h, dyab, ln_g, ln_b, w_s, b_s_col)


_ROWS = 256


_SUBLANES = 8


def _shifted(win, sh_ref):
    n = win.shape[0]
    for r in range(_SUBLANES):
        sh_ref[r, 0:n - r, :] = win[r:, :]
    return sh_ref


def _rows_at(sh_ref, s):
    r = s % _SUBLANES
    return sh_ref[r, s - r:s - r + _ROWS, :]


def _conv_taps(shifted, cw, lo):
    acc = jnp.zeros((_ROWS, CHUNK), F32)
    for w in range(CONV_WIDTH):
        acc = acc + cw[w:w + 1, :] * _rows_at(shifted, lo(w))
    return acc


def _conv_fwd(name, h, n_ex, cw, cb, gg, gb):
    t = h.shape[0]
    s = t // n_ex
    nt = s // _ROWS

    def body(a_ref, gt_ref, cw_ref, cb_ref, gg_ref, gb_ref, yb_ref, hh, sh):
        hh[0:HALO, :] = jnp.zeros((HALO, CHUNK), F32)
        hh[HALO:HALO + s, :] = a_ref[...].astype(F32) * _sigmoid(gt_ref[...].astype(F32))
        cwv = cw_ref[...]

        def tile(i, carry):
            r0 = pl.multiple_of(i * _ROWS, _ROWS)
            win = _shifted(hh[pl.ds(r0, _ROWS + HALO), :], sh)
            c = _conv_taps(win, cwv, lambda w: w + HALO - (CONV_WIDTH - 1)) + cb_ref[...]
            xh, _ = _norm_stats(c)
            hg = xh * gg_ref[...] + gb_ref[...]
            yb_ref[pl.ds(r0, _ROWS), :] = (hg * _sigmoid(hg)).astype(yb_ref.dtype)
            return carry

        lax.fori_loop(0, nt, tile, 0)

    vec = pl.BlockSpec((1, CHUNK), lambda g, b: (0, g))
    return pl.pallas_call(
        body, name=name, grid=(4, n_ex),
        in_specs=[pl.BlockSpec((s, CHUNK), lambda g, b: (b, 8 + g)), pl.BlockSpec((s, CHUNK), lambda g, b: (b, 12 + g)),
                  pl.BlockSpec((CONV_WIDTH, CHUNK), lambda g, b: (0, g)), vec, vec, vec],
        out_specs=pl.BlockSpec((s, CHUNK), lambda g, b: (b, g)),
        out_shape=_sds((t, 4 * CHUNK), ACT),
        scratch_shapes=[pltpu.VMEM((s + HALO, CHUNK), F32), pltpu.VMEM((_SUBLANES, _ROWS + HALO, CHUNK), F32)],
        compiler_params=_params(("parallel", "parallel"), 3 * s * CHUNK * 4, (s + HALO) * CHUNK * 4, 4 * s * CHUNK * 4),
    )(h, h, cw, cb, gg, gb)


def _conv_bwd(name, h, dyab, n_ex, cw, cb, gg, gb):
    t = h.shape[0]
    s = t // n_ex
    nt = s // _ROWS

    def body(a_ref, gt_ref, dy_ref, cw_ref, cb_ref, gg_ref, gb_ref,
             da_ref, dgt_ref, dcw_ref, dcb_ref, dgg_ref, dgb_ref, hh, dcs, acc, sh):
        b = pl.program_id(1)

        @pl.when(b == 0)
        def _():
            dcw_ref[...] = jnp.zeros_like(dcw_ref)
            dcb_ref[...] = jnp.zeros_like(dcb_ref)
            dgg_ref[...] = jnp.zeros_like(dgg_ref)
            dgb_ref[...] = jnp.zeros_like(dgb_ref)

        hh[0:HALO, :] = jnp.zeros((HALO, CHUNK), F32)
        hh[HALO:HALO + s, :] = a_ref[...].astype(F32) * _sigmoid(gt_ref[...].astype(F32))
        dcs[s:s + HALO, :] = jnp.zeros((HALO, CHUNK), F32)
        acc[...] = jnp.zeros_like(acc)
        cwv = cw_ref[...]
        off = HALO - (CONV_WIDTH - 1)
        taps = 8 * CONV_WIDTH

        def tile1(i, carry):
            r0 = pl.multiple_of(i * _ROWS, _ROWS)
            win = _shifted(hh[pl.ds(r0, _ROWS + HALO), :], sh)
            c = _conv_taps(win, cwv, lambda w: w + off) + cb_ref[...]
            xh, rstd = _norm_stats(c)
            hg = xh * gg_ref[...] + gb_ref[...]
            sg = _sigmoid(hg)
            dhg = dy_ref[pl.ds(r0, _ROWS), :].astype(F32) * (sg * (1.0 + hg * (1.0 - sg)))
            acc[taps:taps + 8, :] += _rows8(dhg * xh)
            acc[taps + 8:taps + 16, :] += _rows8(dhg)
            dc = _norm_bwd(dhg * gg_ref[...], xh, rstd)
            dcs[pl.ds(r0, _ROWS), :] = dc
            acc[taps + 16:taps + 24, :] += _rows8(dc)
            for w in range(CONV_WIDTH):
                acc[8 * w:8 * w + 8, :] += _rows8(dc * _rows_at(win, w + off))
            return carry

        lax.fori_loop(0, nt, tile1, 0)

        def tile2(i, carry):
            r0 = pl.multiple_of(i * _ROWS, _ROWS)
            win = _shifted(dcs[pl.ds(r0, _ROWS + HALO), :], sh)
            dhh = _conv_taps(win, cwv, lambda w: CONV_WIDTH - 1 - w)
            av = a_ref[pl.ds(r0, _ROWS), :].astype(F32)
            sg = _sigmoid(gt_ref[pl.ds(r0, _ROWS), :].astype(F32))
            da_ref[pl.ds(r0, _ROWS), :] = (dhh * sg).astype(da_ref.dtype)
            dgt_ref[pl.ds(r0, _ROWS), :] = (dhh * av * sg * (1.0 - sg)).astype(dgt_ref.dtype)
            return carry

        lax.fori_loop(0, nt, tile2, 0)
        dcw_ref[...] += jnp.sum(acc[0:taps, :].reshape(CONV_WIDTH, 8, CHUNK), axis=1)
        dgg_ref[...] += jnp.sum(acc[taps:taps + 8, :], axis=0, keepdims=True)
        dgb_ref[...] += jnp.sum(acc[taps + 8:taps + 16, :], axis=0, keepdims=True)
        dcb_ref[...] += jnp.sum(acc[taps + 16:taps + 24, :], axis=0, keepdims=True)

    vec = pl.BlockSpec((1, CHUNK), lambda g, b: (0, g))
    tap = pl.BlockSpec((CONV_WIDTH, CHUNK), lambda g, b: (0, g))
    seq = pl.BlockSpec((s, CHUNK), lambda g, b: (b, g))
    return pl.pallas_call(
        body, name=name, grid=(4, n_ex),
        in_specs=[pl.BlockSpec((s, CHUNK), lambda g, b: (b, 8 + g)), pl.BlockSpec((s, CHUNK), lambda g, b: (b, 12 + g)),
                  pl.BlockSpec((s, CHUNK), lambda g, b: (b, 4 + g)), tap, vec, vec, vec],
        out_specs=[seq, seq, tap, vec, vec, vec],
        out_shape=[_sds((t, 4 * CHUNK), ACT), _sds((t, 4 * CHUNK), ACT), _sds((CONV_WIDTH, 4 * CHUNK), F32),
                   _sds((1, 4 * CHUNK), F32), _sds((1, 4 * CHUNK), F32), _sds((1, 4 * CHUNK), F32)],
        scratch_shapes=[pltpu.VMEM((s + HALO, CHUNK), F32), pltpu.VMEM((s + HALO, CHUNK), F32),
                        pltpu.VMEM((8 * CONV_WIDTH + 24, CHUNK), F32), pltpu.VMEM((_SUBLANES, _ROWS + HALO, CHUNK), F32)],
        compiler_params=_params(("parallel", "arbitrary"), 5 * s * CHUNK * 4, 2 * (s + HALO) * CHUNK * 4, 4 * s * CHUNK * 4),
    )(h, h, dyab, cw, cb, gg, gb)


_TQ = 256
_SB_DEAD = -110.0


def _tri(kind):
    r = lax.broadcasted_iota(jnp.int32, (_TQ, _TQ), 0)
    c = lax.broadcasted_iota(jnp.int32, (_TQ, _TQ), 1)
    m = {"gt": r > c, "le": r <= c, "lt": r < c}[kind]
    return jnp.where(m, 1.0, 0.0).astype(jnp.bfloat16)


def _split_dot(x, tri2):
    hi = x.astype(jnp.bfloat16)
    lo = (x - hi.astype(F32)).astype(jnp.bfloat16)
    return jnp.dot(jnp.concatenate([hi, lo], axis=1), tri2, preferred_element_type=F32)


def _neg_abs(x):
    bits = lax.bitcast_convert_type(x, jnp.uint32) | jnp.uint32(0x80000000)
    return lax.bitcast_convert_type(bits, F32)


def _log_not_beta(nz):
    return jnp.minimum(nz, 0.0) - jnp.log(1.0 + jnp.exp(_neg_abs(nz)))


def _sb_fwd(name, qkv, n_ex):
    t = qkv.shape[0]
    d = qkv.shape[1] // 3
    npair = d // CHUNK
    s = t // n_ex
    nq = s // _TQ
    neg_a = -(C_HEAD_DIM ** -0.5)

    def body(q_ref, k_ref, v_ref, o_ref, lt_ref, o_acc, c_acc):
        i = pl.program_id(2)
        first = lax.broadcasted_iota(jnp.int32, (_TQ, CHUNK), 1) < C_HEAD_DIM
        q2 = q_ref[...]
        zero = jnp.zeros_like(q2)
        qh = [jnp.where(first, q2, zero), jnp.where(first, zero, q2)]
        tri2 = jnp.concatenate([_tri("gt")] * 2, axis=0)
        o_acc[...] = jnp.zeros_like(o_acc)
        c_acc[...] = jnp.zeros_like(c_acc)

        def tiles(js, mask):
            work = [(a, hd) for a in range(len(js)) for hd in range(2)]
            rows = [pl.ds(pl.multiple_of(j * _TQ, _TQ), _TQ) for j in js]
            kts = [k_ref[r, :] for r in rows]
            vts = [v_ref[r, :] for r in rows]
            nzs = {w: lax.dot_general(qh[w[1]], kts[w[0]], _DIMS["nt"], preferred_element_type=F32) * neg_a for w in work}
            lns = {w: _log_not_beta(nzs[w]) for w in work}
            if mask is not None:
                lns = {w: jnp.where(mask, lns[w], 0.0) for w in work}
            locs = {w: _split_dot(lns[w], tri2) for w in work}
            laters = {}
            for hd in range(2):
                carry = c_acc[hd]
                for a in range(len(js)):
                    laters[a, hd] = carry + locs[a, hd]
                    carry = laters[a, hd][:, 0:1] + lns[a, hd][:, 0:1]
                c_acc[hd] = carry
            atts = {w: jnp.exp(lns[w] - nzs[w] + laters[w]) for w in work}
            if mask is not None:
                atts = {w: jnp.where(mask, atts[w], 0.0) for w in work}
            for hd in range(2):
                acc = o_acc[hd]
                for a in range(len(js)):
                    acc = acc + jnp.dot(atts[a, hd].astype(MM), vts[a], preferred_element_type=F32)
                o_acc[hd] = acc

        tiles([i], lax.broadcasted_iota(jnp.int32, (_TQ, _TQ), 1) < lax.broadcasted_iota(jnp.int32, (_TQ, _TQ), 0))

        def alive():
            return jnp.max(jnp.maximum(c_acc[0], c_acc[1])) >= _SB_DEAD

        def cond(st):
            return jnp.logical_and(st[0] >= 0, st[1])

        def step(st):
            tiles([st[0]], None)
            return st[0] - 1, alive()

        j_last, _ = lax.while_loop(cond, step, (i - 1, alive()))
        o_ref[...] = jnp.where(first, o_acc[0], o_acc[1])
        lt_ref[:, 0:1] = c_acc[0]
        lt_ref[:, 1:2] = c_acc[1]
        lt_ref[:, 2:3] = jnp.full((_TQ, 1), j_last + 1, jnp.int32).astype(F32)

    return pl.pallas_call(
        body, name=name, grid=(n_ex, npair, nq),
        in_specs=[pl.BlockSpec((_TQ, CHUNK), lambda b, p, i: (b * nq + i, p)),
                  pl.BlockSpec((s, CHUNK), lambda b, p, i: (b, npair + p)),
                  pl.BlockSpec((s, CHUNK), lambda b, p, i: (b, 2 * npair + p))],
        out_specs=[pl.BlockSpec((_TQ, CHUNK), lambda b, p, i: (b * nq + i, p)),
                   pl.BlockSpec((None, _TQ, 3), lambda b, p, i: (p, b * nq + i, 0))],
        out_shape=[_sds((t, d), F32), _sds((npair, t, 3), F32)],
        scratch_shapes=[pltpu.VMEM((2, _TQ, CHUNK), F32), pltpu.VMEM((2, _TQ, 1), F32)],
        compiler_params=_params(("parallel", "parallel", "arbitrary"), 2 * s * CHUNK * 2 + 4 * _TQ * CHUNK * 4,
                                4 * _TQ * CHUNK * 4, 24 * _TQ * _TQ * 4),
    )(qkv, qkv, qkv)


def _sb_bwd(name, qkv, do, ltot, n_ex):
    t = qkv.shape[0]
    d = qkv.shape[1] // 3
    npair = d // CHUNK
    s = t // n_ex
    nq = s // _TQ
    scale = C_HEAD_DIM ** -0.5
    neg_a = -scale

    def body(q_ref, k_ref, v_ref, do_ref, lt_ref, dq_ref, dk_ref, dv_ref, dq_acc, cp_acc, cg_acc, dk_acc, dv_acc):
        i = pl.program_id(2)

        @pl.when(i == 0)
        def _():
            dk_acc[...] = jnp.zeros_like(dk_acc)
            dv_acc[...] = jnp.zeros_like(dv_acc)

        first = lax.broadcasted_iota(jnp.int32, (_TQ, CHUNK), 1) < C_HEAD_DIM
        q2 = q_ref[...]
        do2 = do_ref[...]
        qs = (q2 * scale).astype(q2.dtype)
        zero = jnp.zeros_like(q2)
        qh = [jnp.where(first, q2, zero), jnp.where(first, zero, q2)]
        doh = [jnp.where(first, do2, zero), jnp.where(first, zero, do2)]
        lt = [lt_ref[:, 0:1], lt_ref[:, 1:2]]
        tri2_le = jnp.concatenate([_tri("le")] * 2, axis=0)
        tri_lt = _tri("lt").astype(MM)
        dq_acc[...] = jnp.zeros_like(dq_acc)
        cp_acc[...] = jnp.zeros_like(cp_acc)
        cg_acc[...] = jnp.zeros_like(cg_acc)

        def tiles(js, mask):
            na = len(js)
            work = [(a, hd) for a in range(na) for hd in range(2)]
            last = slice(_TQ - 1, _TQ)
            rows = [pl.ds(pl.multiple_of(j * _TQ, _TQ), _TQ) for j in js]
            kts = [k_ref[r, :] for r in rows]
            vts = [v_ref[r, :] for r in rows]
            ksc = [(kt * scale).astype(kt.dtype) for kt in kts]
            nzs = {w: lax.dot_general(qh[w[1]], kts[w[0]], _DIMS["nt"], preferred_element_type=F32) * neg_a for w in work}
            datts = {w: lax.dot_general(doh[w[1]], vts[w[0]], _DIMS["nt"], preferred_element_type=F32) for w in work}
            lns = {w: _log_not_beta(nzs[w]) for w in work}
            if mask is not None:
                lns = {w: jnp.where(mask, lns[w], 0.0) for w in work}
            pins = {w: _split_dot(lns[w], tri2_le) for w in work}
            lss = {w: lns[w] - nzs[w] for w in work}
            atts = {}
            for hd in range(2):
                cp = cp_acc[hd]
                for a in range(na):
                    atts[a, hd] = jnp.exp(lss[a, hd] + ((lt[hd] - cp) - pins[a, hd]))
                    cp = cp + pins[a, hd][:, last]
                cp_acc[hd] = cp
            if mask is not None:
                atts = {w: jnp.where(mask, atts[w], 0.0) for w in work}
            gs = {w: datts[w] * atts[w] for w in work}
            locg = {w: jnp.dot(gs[w].astype(MM), tri_lt, preferred_element_type=F32) for w in work}
            dzs = {}
            for hd in range(2):
                carry = cg_acc[hd]
                for a in range(na):
                    big = carry + locg[a, hd]
                    dzs[a, hd] = gs[a, hd] - (gs[a, hd] + big) * jnp.exp(lss[a, hd])
                    carry = big[:, last] + gs[a, hd][:, last]
                cg_acc[hd] = carry
            if mask is not None:
                dzs = {w: jnp.where(mask, dzs[w], 0.0) for w in work}
            dzs = {w: dzs[w].astype(MM) for w in work}
            attm = {w: atts[w].astype(MM) for w in work}
            for hd in range(2):
                acc = dq_acc[hd]
                for a in range(na):
                    acc = acc + jnp.dot(dzs[a, hd], ksc[a], preferred_element_type=F32)
                dq_acc[hd] = acc
            for a in range(na):
                dk0, dk1 = [lax.dot_general(dzs[a, hd], qs, _DIMS["tn"], preferred_element_type=F32) for hd in range(2)]
                dv0, dv1 = [lax.dot_general(attm[a, hd], do2, _DIMS["tn"], preferred_element_type=F32) for hd in range(2)]
                dk_acc[rows[a], :] += jnp.where(first, dk0, dk1)
                dv_acc[rows[a], :] += jnp.where(first, dv0, dv1)

        def single(j, carry):
            tiles([j], None)
            return carry

        j_first = jnp.clip(jnp.max(lt_ref[:, 2:3]).astype(jnp.int32), 0, i)
        lax.fori_loop(j_first, i, single, 0)
        tiles([i], lax.broadcasted_iota(jnp.int32, (_TQ, _TQ), 1) < lax.broadcasted_iota(jnp.int32, (_TQ, _TQ), 0))
        dq_ref[...] = jnp.where(first, dq_acc[0], dq_acc[1]).astype(dq_ref.dtype)

        @pl.when(i == nq - 1)
        def _():
            dk_ref[...] = dk_acc[...].astype(dk_ref.dtype)
            dv_ref[...] = dv_acc[...].astype(dv_ref.dtype)

    qspec = pl.BlockSpec((_TQ, CHUNK), lambda b, p, i: (b * nq + i, p))
    kv_out = pl.BlockSpec((s, CHUNK), lambda b, p, i: (b, p))
    return pl.pallas_call(
        body, name=name, grid=(n_ex, npair, nq),
        in_specs=[qspec, pl.BlockSpec((s, CHUNK), lambda b, p, i: (b, npair + p)),
                  pl.BlockSpec((s, CHUNK), lambda b, p, i: (b, 2 * npair + p)), qspec,
                  pl.BlockSpec((None, _TQ, 3), lambda b, p, i: (p, b * nq + i, 0))],
        out_specs=[qspec, kv_out, kv_out],
        out_shape=[_sds((t, d), ACT)] * 3,
        scratch_shapes=[pltpu.VMEM((2, _TQ, CHUNK), F32), pltpu.VMEM((2, _TQ, 1), F32), pltpu.VMEM((2, _TQ, 1), F32),
                        pltpu.VMEM((s, CHUNK), F32), pltpu.VMEM((s, CHUNK), F32)],
        compiler_params=_params(("parallel", "parallel", "arbitrary"), 4 * s * CHUNK * 2,
                                4 * _TQ * CHUNK * 4 + 2 * s * CHUNK * 4, 32 * _TQ * _TQ * 4),
    )(qkv, qkv, qkv, do, ltot)


def _xattn_fwd(name, q, kk, vv, n_ex):
    t, d = q.shape
    m = kk.shape[0] // n_ex
    s = t // n_ex
    tq = _tile(s, 512)
    nq = s // tq
    hd_dim = d // MEM_HEADS
    scale = hd_dim ** -0.5

    def body(q_ref, k_ref, v_ref, o_ref):
        for hd in range(MEM_HEADS):
            ln = slice(hd * hd_dim, (hd + 1) * hd_dim)
            sc = lax.dot_general(q_ref[:, ln], k_ref[:, ln], _DIMS["nt"], preferred_element_type=F32) * scale
            p = jnp.exp(sc - jnp.max(sc, axis=-1, keepdims=True))
            p = p / jnp.sum(p, axis=-1, keepdims=True)
            o_ref[:, ln] = jnp.dot(p.astype(MM), v_ref[:, ln], preferred_element_type=F32).astype(o_ref.dtype)

    qspec = pl.BlockSpec((tq, d), lambda b, i: (b * nq + i, 0))
    kspec = pl.BlockSpec((m, d), lambda b, i: (b, 0))
    return pl.pallas_call(
        body, name=name, grid=(n_ex, nq), in_specs=[qspec, kspec, kspec], out_specs=qspec,
        out_shape=_sds((t, d), ACT),
        compiler_params=_params(("parallel", "parallel"), 2 * tq * d * 2 + 2 * m * d * 2, 0, 6 * tq * m * 4),
    )(q, kk, vv)


def _xattn_bwd(name, q, kk, vv, do, n_ex):
    t, d = q.shape
    m = kk.shape[0] // n_ex
    s = t // n_ex
    tq = _tile(s, 512)
    nq = s // tq
    hd_dim = d // MEM_HEADS
    scale = hd_dim ** -0.5

    def body(q_ref, k_ref, v_ref, do_ref, dq_ref, dk_ref, dv_ref):
        i = pl.program_id(1)

        @pl.when(i == 0)
        def _():
            dk_ref[...] = jnp.zeros_like(dk_ref)
            dv_ref[...] = jnp.zeros_like(dv_ref)

        for hd in range(MEM_HEADS):
            ln = slice(hd * hd_dim, (hd + 1) * hd_dim)
            qv, kv, vv_, dov = q_ref[:, ln], k_ref[:, ln], v_ref[:, ln], do_ref[:, ln]
            sc = lax.dot_general(qv, kv, _DIMS["nt"], preferred_element_type=F32) * scale
            p = jnp.exp(sc - jnp.max(sc, axis=-1, keepdims=True))
            p = p / jnp.sum(p, axis=-1, keepdims=True)
            dp = lax.dot_general(dov, vv_, _DIMS["nt"], preferred_element_type=F32)
            ds = (p * (dp - jnp.sum(p * dp, axis=-1, keepdims=True)) * scale).astype(MM)
            dq_ref[:, ln] = jnp.dot(ds, kv, preferred_element_type=F32).astype(dq_ref.dtype)
            dk_ref[:, ln] += lax.dot_general(ds, qv, _DIMS["tn"], preferred_element_type=F32)
            dv_ref[:, ln] += lax.dot_general(p.astype(MM), dov, _DIMS["tn"], preferred_element_type=F32)

    qspec = pl.BlockSpec((tq, d), lambda b, i: (b * nq + i, 0))
    kspec = pl.BlockSpec((m, d), lambda b, i: (b, 0))
    return pl.pallas_call(
        body, name=name, grid=(n_ex, nq), in_specs=[qspec, kspec, kspec, qspec], out_specs=[qspec, kspec, kspec],
        out_shape=[_sds((t, d), ACT), _sds((n_ex * m, d), F32), _sds((n_ex * m, d), F32)],
        compiler_params=_params(("parallel", "arbitrary"), 3 * tq * d * 2 + 2 * m * d * 2 + 2 * m * d * 4, 0,
                                8 * tq * m * 4),
    )(q, kk, vv, do)


def _ffn_up(name, y, w1, w3):
    t, d = y.shape
    f = w1.shape[-1]
    tm = _tile(t, 1024)
    wspec = pl.BlockSpec((None, d, f), lambda i, j, k: (j, 0, 0))
    hspec = pl.BlockSpec((None, tm, f), lambda i, j, k: (j, i, 0))

    def epi(vals, _):
        h1, h3 = vals
        return [h1, h3, h1 * _sigmoid(h1) * h3]

    return _mm(name, "nn", (t // tm, N_CHIPS, 1), y, pl.BlockSpec((tm, d), lambda i, j, k: (i, 0)),
               [w1, w3], [wspec, wspec], [_sds((N_CHIPS, t, f), ACT)] * 3, [hspec] * 3, (tm, f), epilogue=epi)


def _ffn_down_bwd(name, dr, w2, h1, h3):
    t, d = dr.shape
    f = w2.shape[-2]
    tm = _tile(t, 1024)
    hspec = pl.BlockSpec((None, tm, f), lambda i, j, k: (j, i, 0))

    def epi(vals, ex):
        dg, = vals
        h1v, h3v = ex
        sg = _sigmoid(h1v)
        return [dg * h3v * (sg * (1.0 + h1v * (1.0 - sg))), dg * h1v * sg]

    return _mm(name, "nt", (t // tm, N_CHIPS, 1), dr, pl.BlockSpec((tm, d), lambda i, j, k: (i, 0)),
               [w2], [pl.BlockSpec((None, f, d), lambda i, j, k: (j, 0, 0))],
               [_sds((N_CHIPS, t, f), ACT)] * 2, [hspec] * 2, (tm, f),
               extras=[h1, h3], extra_specs=[hspec, hspec], epilogue=epi)


def _ffn_up_bwd(name, dh, w, extras, epilogue):
    _, t, f = dh.shape
    d = w.shape[-2]
    tm, tn = _tile(t, 1024), _tile(d, 1024)
    ospec = pl.BlockSpec((tm, tn), lambda i, j, k: (i, j))
    return _mm(name, "nt", (t // tm, d // tn, N_CHIPS), dh, pl.BlockSpec((None, tm, f), lambda i, j, k: (k, i, 0)),
               [w], [pl.BlockSpec((None, tn, f), lambda i, j, k: (k, j, 0))],
               [_sds((t, d), F32)], [ospec], (tm, tn),
               extras=extras, extra_specs=[ospec] * len(extras), epilogue=epilogue)[0]


def _ffn_wgrad_up(name, y, dh1, dh3):
    t, d = y.shape
    f = dh1.shape[-1]
    tm, tk = _tile(d, 1024), _tile(t, 1024)
    hspec = pl.BlockSpec((None, tk, f), lambda i, j, k: (j, k, 0))
    ospec = pl.BlockSpec((None, tm, f), lambda i, j, k: (j, i, 0))
    return _mm(name, "tn", (d // tm, N_CHIPS, t // tk), y, pl.BlockSpec((tk, tm), lambda i, j, k: (k, i)),
               [dh1, dh3], [hspec, hspec], [_sds((N_CHIPS, d, f), MM)] * 2, [ospec, ospec], (tm, f))


def _ffn_wgrad_down(name, g, dr):
    _, t, f = g.shape
    d = dr.shape[1]
    tn, tk = _tile(d, 1024), _tile(t, 1024)
    return _mm(name, "tn", (N_CHIPS, d // tn, t // tk), g, pl.BlockSpec((None, tk, f), lambda i, j, k: (i, k, 0)),
               [dr], [pl.BlockSpec((tk, tn), lambda i, j, k: (k, j))],
               [_sds((N_CHIPS, f, d), MM)], [pl.BlockSpec((None, f, tn), lambda i, j, k: (i, 0, j))], (f, tn))[0]


def _proj_cols(name, y, w, out_dtype):
    t, kdim = y.shape
    wd = w.shape[-1]
    tn = _tile(wd, 512)
    per = wd // tn
    tm = _tile(t, 1024)
    return _mm(name, "nn", (t // tm, N_CHIPS * per, 1), y, pl.BlockSpec((tm, kdim), lambda i, j, k: (i, 0)),
               [w], [pl.BlockSpec((None, kdim, tn), lambda i, j, k: (j // per, 0, j % per))],
               [_sds((t, N_CHIPS * wd), out_dtype)], [pl.BlockSpec((tm, tn), lambda i, j, k: (i, j))], (tm, tn))[0]


def _proj_cols_bwd(name, dh, w, extras, epilogue):
    t = dh.shape[0]
    kdim, wd = w.shape[-2], w.shape[-1]
    tm, tn = _tile(t, 1024), _tile(kdim, 1024)
    ospec = pl.BlockSpec((tm, tn), lambda i, j, k: (i, j))
    return _mm(name, "nt", (t // tm, kdim // tn, N_CHIPS), dh, pl.BlockSpec((tm, wd), lambda i, j, k: (i, k)),
               [w], [pl.BlockSpec((None, tn, wd), lambda i, j, k: (k, j, 0))],
               [_sds((t, kdim), F32)], [ospec], (tm, tn),
               extras=extras, extra_specs=[ospec] * len(extras), epilogue=epilogue)[0]


def _proj_cols_wgrad(name, y, dh):
    t, kdim = y.shape
    wd = dh.shape[1] // N_CHIPS
    tm, tk = _tile(kdim, 1024), _tile(t, 1024)
    return _mm(name, "tn", (kdim // tm, N_CHIPS, t // tk), y, pl.BlockSpec((tk, tm), lambda i, j, k: (k, i)),
               [dh], [pl.BlockSpec((tk, wd), lambda i, j, k: (k, j))],
               [_sds((N_CHIPS, kdim, wd), MM)], [pl.BlockSpec((None, tm, wd), lambda i, j, k: (j, i, 0))], (tm, wd))[0]


def _coords():
    return lax.axis_index("x"), lax.axis_index("y"), lax.axis_index("c")


def _chip_peers(x, y):
    return [(1 - x, y), (x, 1 - y), (1 - x, 1 - y)]


_ANY = pl.BlockSpec(memory_space=pl.ANY)


def _half(ref, c):
    h = ref.shape[0] // 2
    return ref.at[pl.ds(c * h, h)]


def _gather_two_level(srcs, lands):
    n = len(srcs)

    def body(*refs):
        ins, lz = refs[:n], refs[2 * n:3 * n]
        ici_send, ici_recv, d2d_send, d2d_recv = refs[3 * n:]
        x, y, c = _coords()
        me = 2 * x + y
        peers = _chip_peers(x, y)
        ici, d2d = [], []
        for t in range(n):
            for j, chip in enumerate(peers):
                k = 3 * t + j
                ici.append(pltpu.make_async_remote_copy(
                    src_ref=_half(ins[t], c), dst_ref=_half(lz[t].at[me], c), send_sem=ici_send.at[k], recv_sem=ici_recv.at[k],
                    device_id=(*chip, c), device_id_type=MESH))
                got = _half(lz[t].at[2 * chip[0] + chip[1]], c)
                d2d.append(pltpu.make_async_remote_copy(
                    src_ref=got, dst_ref=got, send_sem=d2d_send.at[k], recv_sem=d2d_recv.at[k],
                    device_id=(x, y, 1 - c), device_id_type=MESH))
        for cp in ici:
            cp.start()
        for cp, fw in zip(ici, d2d):
            cp.wait_recv()
            fw.start()
        for cp, fw in zip(ici, d2d):
            cp.wait_send()
            fw.wait_send()
            fw.wait_recv()

    res = pl.pallas_call(
        body, name="gather_weights", in_specs=[_ANY] * (2 * n), out_specs=[_ANY] * n,
        out_shape=[_sds(a.shape, a.dtype) for a in lands], input_output_aliases={n + i: i for i in range(n)},
        scratch_shapes=[pltpu.SemaphoreType.DMA((3 * n,))] * 4,
    )(*srcs, *lands)
    return list(res)


def _scatter_chips(parts):
    n = len(parts)

    def body(*refs):
        ins, outs = refs[:n], refs[n:2 * n]
        send_sems, recv_sems = refs[2 * n:]
        x, y, c = _coords()
        copies = []
        for t in range(n):
            for j, chip in enumerate(_chip_peers(x, y)):
                copies.append(pltpu.make_async_remote_copy(
                    src_ref=ins[t].at[2 * chip[0] + chip[1]], dst_ref=outs[t].at[j], send_sem=send_sems.at[3 * t + j],
                    recv_sem=recv_sems.at[3 * t + j], device_id=(*chip, c), device_id_type=MESH))
        for cp in copies:
            cp.start()
        for cp in copies:
            cp.wait()

    return pl.pallas_call(
        body, name="scatter_grads", in_specs=[_ANY] * n, out_specs=[_ANY] * n,
        out_shape=[_sds((3, *a.shape[1:]), a.dtype) for a in parts],
        scratch_shapes=[pltpu.SemaphoreType.DMA((3 * n,))] * 2,
    )(*parts)


def _swap_halves(grads):
    n = len(grads)

    def body(*refs):
        ins, outs = refs[:n], refs[n:2 * n]
        send_sems, recv_sems = refs[2 * n:]
        x, y, c = _coords()
        copies = []
        for t in range(n):
            h = ins[t].shape[1] // 2
            copies.append(pltpu.make_async_remote_copy(
                src_ref=ins[t].at[:, pl.ds((1 - c) * h, h)], dst_ref=outs[t], send_sem=send_sems.at[t], recv_sem=recv_sems.at[t],
                device_id=(x, y, 1 - c), device_id_type=MESH))
        for cp in copies:
            cp.start()
        for cp in copies:
            cp.wait()

    return pl.pallas_call(
        body, name="swap_grad_halves", in_specs=[_ANY] * n, out_specs=[_ANY] * n,
        out_shape=[_sds((a.shape[0], a.shape[1] // 2, a.shape[2]), a.dtype) for a in grads],
        scratch_shapes=[pltpu.SemaphoreType.DMA((n,))] * 2,
    )(*grads)


def _swap_sibling(arrs):
    n = len(arrs)

    def body(*refs):
        ins, outs = refs[:n], refs[n:2 * n]
        send_sems, recv_sems = refs[2 * n:]
        x, y, c = _coords()
        copies = []
        for t in range(n):
            cp = pltpu.make_async_remote_copy(src_ref=ins[t], dst_ref=outs[t], send_sem=send_sems.at[t],
                                              recv_sem=recv_sems.at[t], device_id=(x, y, 1 - c), device_id_type=MESH)
            cp.start()
            copies.append(cp)
        for cp in copies:
            cp.wait()

    return pl.pallas_call(
        body, name="swap_sibling", in_specs=[_ANY] * n, out_specs=[_ANY] * n,
        out_shape=[_sds(a.shape, a.dtype) for a in arrs],
        scratch_shapes=[pltpu.SemaphoreType.DMA((n,)), pltpu.SemaphoreType.DMA((n,))],
    )(*arrs)


def _gather_all(part):
    def body(in_ref, out_ref, send_sems, recv_sems, loc_sem):
        x, y, c = _coords()
        dst = out_ref.at[4 * x + 2 * y + c]
        copies = [pltpu.make_async_copy(in_ref, dst, loc_sem)]
        for r in range(1, N_DEV):
            fx, fy, fc = (r >> 2) & 1, (r >> 1) & 1, r & 1
            peer = (x ^ fx, y ^ fy, c ^ fc)
            copies.append(pltpu.make_async_remote_copy(src_ref=in_ref, dst_ref=dst, send_sem=send_sems.at[r - 1],
                                                       recv_sem=recv_sems.at[r - 1], device_id=peer, device_id_type=MESH))
        for cp in copies:
            cp.start()
        for cp in copies:
            cp.wait()

    return pl.pallas_call(
        body, name="gather_small_grads", in_specs=[_ANY], out_specs=_ANY,
        out_shape=_sds((N_DEV, *part.shape), part.dtype),
        scratch_shapes=[pltpu.SemaphoreType.DMA((N_DEV - 1,)), pltpu.SemaphoreType.DMA((N_DEV - 1,)), pltpu.SemaphoreType.DMA],
    )(part)


def _sum_chips(grad, recv, me):
    _, rr, cc = recv.shape
    tr = _tile(rr, 512)

    def body(me_ref, g_ref, r0_ref, r1_ref, r2_ref, o_ref):
        o_ref[...] = ((g_ref[...].astype(F32) + r0_ref[...].astype(F32)) + r1_ref[...].astype(F32)) + r2_ref[...].astype(F32)

    gspec = pl.BlockSpec((None, tr, cc), lambda r, m: (m[0], r, 0))
    rspecs = [pl.BlockSpec((None, tr, cc), functools.partial(lambda r, m, j: (j, r, 0), j=j)) for j in range(3)]
    return pl.pallas_call(
        body, name="sum_chip_grads",
        grid_spec=pltpu.PrefetchScalarGridSpec(
            num_scalar_prefetch=1, grid=(rr // tr,), in_specs=[gspec, *rspecs],
            out_specs=pl.BlockSpec((tr, cc), lambda r, m: (r, 0))),
        out_shape=_sds((rr, cc), F32),
        compiler_params=_params(("parallel",), 4 * tr * cc * 2 + tr * cc * 4, 0, 2 * tr * cc * 4),
    )(me, grad, recv, recv, recv)


def _sum_pair(grad, sib, core):
    k, h, cc = sib.shape
    tr = _tile(h, 512)
    nb = h // tr

    def body(c_ref, g_ref, a_ref, o_ref):
        o_ref[...] = (g_ref[...].astype(F32) + a_ref[...].astype(F32)).astype(o_ref.dtype)

    spec = pl.BlockSpec((None, tr, cc), lambda s, r, c: (s, r, 0))
    return pl.pallas_call(
        body, name="sum_core_grads",
        grid_spec=pltpu.PrefetchScalarGridSpec(
            num_scalar_prefetch=1, grid=(k, nb),
            in_specs=[pl.BlockSpec((None, tr, cc), lambda s, r, c: (s, c[0] * nb + r, 0)), spec], out_specs=spec),
        out_shape=_sds(sib.shape, sib.dtype),
        compiler_params=_params(("parallel", "parallel"), 3 * tr * cc * 2, 0, 2 * tr * cc * 4),
    )(core, grad, sib)


def _adamw_math(w, g, m, v):
    m = ADAM_B1 * m + (1.0 - ADAM_B1) * g
    v = ADAM_B2 * v + (1.0 - ADAM_B2) * (g * g)
    m_hat = m / (1.0 - ADAM_B1 ** ADAM_STEP)
    v_hat = v / (1.0 - ADAM_B2 ** ADAM_STEP)
    delta = -ADAM_LR * (m_hat / (jnp.sqrt(v_hat) + ADAM_EPS) + ADAM_WD * w)
    return delta, m, v


def _adamw(name, parts, w, m, v):
    ll, rr, cc = w.shape
    tr = _tile(rr, 256)
    npart = len(parts)

    def body(*refs):
        p_refs = refs[:npart]
        w_ref, m_ref, v_ref, g_ref, d_ref, nm_ref, nv_ref = refs[npart:]
        g = p_refs[0][...]
        for p in p_refs[1:]:
            g = g + p[...]
        d, nm, nv = _adamw_math(w_ref[...], g, m_ref[...], v_ref[...])
        g_ref[...] = g
        d_ref[...] = d
        nm_ref[...] = nm
        nv_ref[...] = nv

    spec = pl.BlockSpec((None, tr, cc), lambda l, r: (l, r, 0))
    out = _sds((ll, rr, cc), F32)
    return pl.pallas_call(
        body, name=name, grid=(ll, rr // tr), in_specs=[spec] * (npart + 3), out_specs=[spec] * 4, out_shape=[out] * 4,
        compiler_params=_params(("parallel", "parallel"), (npart + 7) * tr * cc * 4, 0, 4 * tr * cc * 4),
    )(*parts, w, m, v)


def _adamw_halves(name, s_own, s_sib, core, w, m, v):
    ll, rr, cc = w.shape
    h = rr // 2
    tr = _tile(h, 256)
    nb = h // tr

    def body(c_ref, own_ref, sib_ref, w_ref, m_ref, v_ref, g_ref, d_ref, nm_ref, nv_ref):
        g = jnp.where(pl.program_id(1) == c_ref[0], own_ref[...], sib_ref[...])
        d, nm, nv = _adamw_math(w_ref[...], g, m_ref[...], v_ref[...])
        g_ref[...] = g
        d_ref[...] = d
        nm_ref[...] = nm
        nv_ref[...] = nv

    half = pl.BlockSpec((None, tr, cc), lambda l, hf, r, c: (l, r, 0))
    full = pl.BlockSpec((None, tr, cc), lambda l, hf, r, c: (l, hf * nb + r, 0))
    out = _sds((ll, rr, cc), F32)
    return pl.pallas_call(
        body, name=name,
        grid_spec=pltpu.PrefetchScalarGridSpec(num_scalar_prefetch=1, grid=(ll, 2, nb), in_specs=[half, half, full, full, full],
                                               out_specs=[full] * 4),
        out_shape=[out] * 4,
        compiler_params=_params(("parallel", "parallel", "parallel"), 9 * tr * cc * 4, 0, 4 * tr * cc * 4),
    )(core, s_own, s_sib, w, m, v)


def _sum_devices(allparts):
    _, rr, cc = allparts.shape

    def body(p_ref, o_ref):
        s = p_ref[0]
        for k in range(1, N_DEV):
            s = s + p_ref[k]
        o_ref[...] = s

    return pl.pallas_call(
        body, name="sum_small_grads", grid=(1,), in_specs=[pl.BlockSpec((N_DEV, rr, cc), lambda i: (0, 0, 0))],
        out_specs=pl.BlockSpec((rr, cc), lambda i: (0, 0)), out_shape=_sds((rr, cc), F32),
        compiler_params=_params(("arbitrary",), 9 * rr * cc * 4),
    )(allparts)


def _pack(arrs):
    flat = jnp.concatenate([a.reshape(-1).astype(F32) for a in arrs])
    n = flat.shape[0]
    total = -(-n // 1024) * 1024
    return jnp.pad(flat, (0, total - n)).reshape(total // 128, 128)


def _unpack(block, shapes):
    flat = block.reshape(-1)
    out, off = [], 0
    for sh in shapes:
        n = math.prod(sh)
        out.append(flat[off:off + n].reshape(sh))
        off += n
    return out


_BIG = ["w_in_ab", "w_out_ab", "w_qkv_c", "w_out_c", "mem_wq", "mem_wk", "mem_wv", "mem_wo", "ffn_w1", "ffn_w3", "ffn_w2"]
_ROW_SHARDED = ("w_out_ab", "w_out_c", "mem_wq", "mem_wk", "mem_wv", "mem_wo")
_SMALL_REPL = ["gmlp_ln_g", "gmlp_ln_b", "gmlp_w_s", "gmlp_b_s", "conv_b", "conv_gn_g", "conv_gn_b"]
_SMALL_SHARD = ["conv_w", "ln_g", "ln_b"]
_NAMES = ["w_in_ab", "gmlp_ln_g", "gmlp_ln_b", "gmlp_w_s", "gmlp_b_s", "conv_w", "conv_b", "conv_gn_g", "conv_gn_b",
          "w_out_ab", "w_qkv_c", "w_out_c", "mem_wq", "mem_wk", "mem_wv", "mem_wo", "ffn_w1", "ffn_w3", "ffn_w2",
          "ln_g", "ln_b"]


def _layer_weights(l):
    mixer = ["w_in_ab", "w_out_ab"] if l % 2 == 0 else ["w_qkv_c", "w_out_c"]
    return [(n, l // 2) for n in mixer] + [(n, l) for n in _BIG if n.startswith(("mem_", "ffn_"))]


def _natural(w):
    return w.reshape(-1, w.shape[-1])


def _local_step(x, mem, target, layer_w, small):
    n_ex, s, d = x.shape
    t = n_ex * s
    x2 = x.reshape(t, d)
    mem_a = mem.reshape(-1, d).astype(ACT)
    tgt = target.reshape(t, d)
    one = jnp.ones((1, d), F32)
    zero = jnp.zeros((1, d), F32)
    ln_g, ln_b = small["ln_g"], small["ln_b"]

    def vec(a):
        return a.reshape(1, -1)

    saved = []
    xh, gp, bp = x2, one, zero
    y_act = x2.astype(ACT)
    for l in range(DEPTH):
        wts = layer_w[l]
        sv = {"y0": y_act, "w": wts}
        if l % 2 == 0:
            e = l // 2
            h = _proj_cols(f"in_ab_{l}", y_act, wts["w_in_ab"], ACT)
            gl = (vec(small["gmlp_ln_g"][e]), vec(small["gmlp_ln_b"][e]), small["gmlp_w_s"][e],
                  small["gmlp_b_s"][e].reshape(4, CHUNK, 1))
            cl = (small["conv_w"][e], vec(small["conv_b"][e]), vec(small["conv_gn_g"][e]), vec(small["conv_gn_b"][e]))
            ya = _gmlp_fwd(f"gmlp_fwd_{l}", h, *gl)
            yb = _conv_fwd(f"conv_fwd_{l}", h, n_ex, *cl)
            yab = jnp.concatenate([ya, yb], axis=1)
            mixed, w_mix = yab, _natural(wts["w_out_ab"])
            sv.update(h=h, yab=yab, gl=gl, cl=cl)
        else:
            qkv = _proj_cols(f"qkv_{l}", y_act, wts["w_qkv_c"], ACT)
            att, ltot = _sb_fwd(f"sb_fwd_{l}", qkv, n_ex)
            att_a = att.astype(ACT)
            mixed, w_mix = att_a, _natural(wts["w_out_c"])
            sv.update(qkv=qkv, att=att_a, ltot=ltot)
        g1, b1 = vec(ln_g[l, 0]), vec(ln_b[l, 0])
        xh1, y1, rstd1 = _dense_ln(f"mix_out_ln1_{l}", mixed, w_mix, xh, gp, bp, g1, b1)
        q = _dense(f"mem_q_{l}", y1, _natural(wts["mem_wq"]), ACT)
        kk = _dense(f"mem_k_{l}", mem_a, _natural(wts["mem_wk"]), ACT)
        vv = _dense(f"mem_v_{l}", mem_a, _natural(wts["mem_wv"]), ACT)
        oc = _xattn_fwd(f"xattn_fwd_{l}", q, kk, vv, n_ex)
        g2, b2 = vec(ln_g[l, 1]), vec(ln_b[l, 1])
        xh2, y2, rstd2 = _dense_ln(f"mem_o_ln2_{l}", oc, _natural(wts["mem_wo"]), xh1, g1, b1, g2, b2)
        h1, h3, gact = _ffn_up(f"ffn_up_{l}", y2, wts["ffn_w1"], wts["ffn_w3"])
        g3, b3 = vec(ln_g[l, 2]), vec(ln_b[l, 2])
        xh3, y3, rstd3 = _ffn_down_ln(f"ffn_down_ln3_{l}", gact, wts["ffn_w2"], xh2, g2, b2, g3, b3)
        sv.update(xh1=xh1, y1=y1, rstd1=rstd1, g1=g1, q=q, kk=kk, vv=vv, oc=oc, xh2=xh2, y2=y2, rstd2=rstd2, g2=g2,
                  h1=h1, h3=h3, gact=gact, xh3=xh3, rstd3=rstd3, g3=g3)
        saved.append(sv)
        xh, gp, bp, y_act = xh3, g3, b3, y3

    dy, loss = _loss_head(xh, gp, bp, tgt)

    sm = {n: [None] * (DEPTH // 2) for n in _SMALL_REPL + ["conv_w"]}
    d_ln_g = [[None] * 3 for _ in range(DEPTH)]
    d_ln_b = [[None] * 3 for _ in range(DEPTH)]

    def add_res(vals, ex):
        return [vals[0] + ALPHA * ex[0]]

    def add_res2(vals, ex):
        return [vals[0] + ex[0] + ALPHA * ex[1]]

    layer_g = [None] * DEPTH
    for l in reversed(range(DEPTH)):
        sv = saved[l]
        wts = sv["w"]
        big = {}
        dr3, dr3a, d_ln_g[l][2], d_ln_b[l][2] = _ln_bwd(f"ln3_bwd_{l}", dy, sv["xh3"], sv["rstd3"], sv["g3"])
        big["ffn_w2"] = _ffn_wgrad_down(f"ffn_w2_grad_{l}", sv["gact"], dr3a)
        dh1, dh3 = _ffn_down_bwd(f"ffn_down_bwd_{l}", dr3a, wts["ffn_w2"], sv["h1"], sv["h3"])
        big["ffn_w1"], big["ffn_w3"] = _ffn_wgrad_up(f"ffn_w13_grad_{l}", sv["y2"], dh1, dh3)
        part = _ffn_up_bwd(f"ffn_up_bwd1_{l}", dh1, wts["ffn_w1"], [], None)
        dy = _ffn_up_bwd(f"ffn_up_bwd3_{l}", dh3, wts["ffn_w3"], [part, dr3], add_res2)
        dr2, dr2a, d_ln_g[l][1], d_ln_b[l][1] = _ln_bwd(f"ln2_bwd_{l}", dy, sv["xh2"], sv["rstd2"], sv["g2"])
        big["mem_wo"] = _dense_tn(f"mem_wo_grad_{l}", sv["oc"], dr2a)
        doc = _dense_nt(f"mem_o_bwd_{l}", dr2a, _natural(wts["mem_wo"]), ACT)
        dq, dkk, dvv = _xattn_bwd(f"xattn_bwd_{l}", sv["q"], sv["kk"], sv["vv"], doc, n_ex)
        big["mem_wq"] = _dense_tn(f"mem_wq_grad_{l}", sv["y1"], dq)
        big["mem_wk"] = _dense_tn(f"mem_wk_grad_{l}", mem_a, dkk)
        big["mem_wv"] = _dense_tn(f"mem_wv_grad_{l}", mem_a, dvv)
        dy = _dense_nt(f"mem_q_bwd_{l}", dq, _natural(wts["mem_wq"]), F32, extras=[dr2], epilogue=add_res)
        dr1, dr1a, d_ln_g[l][0], d_ln_b[l][0] = _ln_bwd(f"ln1_bwd_{l}", dy, sv["xh1"], sv["rstd1"], sv["g1"])
        if l % 2 == 0:
            e = l // 2
            big["w_out_ab"] = _dense_tn(f"out_ab_grad_{l}", sv["yab"], dr1a)
            dyab = _dense_nt(f"out_ab_bwd_{l}", dr1a, _natural(wts["w_out_ab"]), ACT)
            duv, dgg, dgb, dws, dbs = _gmlp_bwd(f"gmlp_bwd_{l}", sv["h"], dyab, *sv["gl"])
            da, dgt, dcw, dcb, dng, dnb = _conv_bwd(f"conv_bwd_{l}", sv["h"], dyab, n_ex, *sv["cl"])
            sm["gmlp_ln_g"][e], sm["gmlp_ln_b"][e] = dgg.reshape(-1), dgb.reshape(-1)
            sm["gmlp_w_s"][e], sm["gmlp_b_s"][e] = dws, dbs.reshape(4, CHUNK)
            sm["conv_w"][e], sm["conv_b"][e] = dcw, dcb.reshape(-1)
            sm["conv_gn_g"][e], sm["conv_gn_b"][e] = dng.reshape(-1), dnb.reshape(-1)
            dh = jnp.concatenate([duv, da, dgt], axis=1)
            big["w_in_ab"] = _proj_cols_wgrad(f"in_ab_grad_{l}", sv["y0"], dh)
            dy = _proj_cols_bwd(f"in_ab_bwd_{l}", dh, wts["w_in_ab"], [dr1], add_res)
        else:
            big["w_out_c"] = _dense_tn(f"out_c_grad_{l}", sv["att"], dr1a)
            datt = _dense_nt(f"out_c_bwd_{l}", dr1a, _natural(wts["w_out_c"]), ACT)
            dq_, dk_, dv_ = _sb_bwd(f"sb_bwd_{l}", sv["qkv"], datt, sv["ltot"], n_ex)
            dqkv = jnp.concatenate([dq_, dk_, dv_], axis=1)
            big["w_qkv_c"] = _proj_cols_wgrad(f"qkv_grad_{l}", sv["y0"], dqkv)
            dy = _proj_cols_bwd(f"qkv_bwd_{l}", dqkv, wts["w_qkv_c"], [dr1], add_res)
        for n in _ROW_SHARDED:
            if n in big:
                big[n] = big[n].reshape(N_CHIPS, -1, big[n].shape[-1])
        layer_g[l] = big

    grad_x = dy.reshape(n_ex, s, d)
    small_g = {n: jnp.stack(sm[n]) for n in sm}
    small_g["ln_g"] = jnp.stack([jnp.concatenate(r, axis=0) for r in d_ln_g])
    small_g["ln_b"] = jnp.stack([jnp.concatenate(r, axis=0) for r in d_ln_b])
    return loss, grad_x, layer_g, small_g


def kernel(x, mem, w_in_ab, gmlp_ln_g, gmlp_ln_b, gmlp_w_s, gmlp_b_s, conv_w, conv_b, conv_gn_g, conv_gn_b, w_out_ab, w_qkv_c, w_out_c, mem_wq, mem_wk, mem_wv, mem_wo, ffn_w1, ffn_w3, ffn_w2, ln_g, ln_b, loss_target, m_w_in_ab, m_gmlp_ln_g, m_gmlp_ln_b, m_gmlp_w_s, m_gmlp_b_s, m_conv_w, m_conv_b, m_conv_gn_g, m_conv_gn_b, m_w_out_ab, m_w_qkv_c, m_w_out_c, m_mem_wq, m_mem_wk, m_mem_wv, m_mem_wo, m_ffn_w1, m_ffn_w3, m_ffn_w2, m_ln_g, m_ln_b, v_w_in_ab, v_gmlp_ln_g, v_gmlp_ln_b, v_gmlp_w_s, v_gmlp_b_s, v_conv_w, v_conv_b, v_conv_gn_g, v_conv_gn_b, v_w_out_ab, v_w_qkv_c, v_w_out_c, v_mem_wq, v_mem_wk, v_mem_wv, v_mem_wo, v_ffn_w1, v_ffn_w3, v_ffn_w2, v_ln_g, v_ln_b):
    w = dict(w_in_ab=w_in_ab, gmlp_ln_g=gmlp_ln_g, gmlp_ln_b=gmlp_ln_b, gmlp_w_s=gmlp_w_s, gmlp_b_s=gmlp_b_s, conv_w=conv_w,
             conv_b=conv_b, conv_gn_g=conv_gn_g, conv_gn_b=conv_gn_b, w_out_ab=w_out_ab, w_qkv_c=w_qkv_c, w_out_c=w_out_c,
             mem_wq=mem_wq, mem_wk=mem_wk, mem_wv=mem_wv, mem_wo=mem_wo, ffn_w1=ffn_w1, ffn_w3=ffn_w3, ffn_w2=ffn_w2,
             ln_g=ln_g, ln_b=ln_b)
    mo = dict(w_in_ab=m_w_in_ab, gmlp_ln_g=m_gmlp_ln_g, gmlp_ln_b=m_gmlp_ln_b, gmlp_w_s=m_gmlp_w_s, gmlp_b_s=m_gmlp_b_s,
              conv_w=m_conv_w, conv_b=m_conv_b, conv_gn_g=m_conv_gn_g, conv_gn_b=m_conv_gn_b, w_out_ab=m_w_out_ab,
              w_qkv_c=m_w_qkv_c, w_out_c=m_w_out_c, mem_wq=m_mem_wq, mem_wk=m_mem_wk, mem_wv=m_mem_wv, mem_wo=m_mem_wo,
              ffn_w1=m_ffn_w1, ffn_w3=m_ffn_w3, ffn_w2=m_ffn_w2, ln_g=m_ln_g, ln_b=m_ln_b)
    vo = dict(w_in_ab=v_w_in_ab, gmlp_ln_g=v_gmlp_ln_g, gmlp_ln_b=v_gmlp_ln_b, gmlp_w_s=v_gmlp_w_s, gmlp_b_s=v_gmlp_b_s,
              conv_w=v_conv_w, conv_b=v_conv_b, conv_gn_g=v_conv_gn_g, conv_gn_b=v_conv_gn_b, w_out_ab=v_w_out_ab,
              w_qkv_c=v_w_qkv_c, w_out_c=v_w_out_c, mem_wq=v_mem_wq, mem_wk=v_mem_wk, mem_wv=v_mem_wv, mem_wo=v_mem_wo,
              ffn_w1=v_ffn_w1, ffn_w3=v_ffn_w3, ffn_w2=v_ffn_w2, ln_g=v_ln_g, ln_b=v_ln_b)
    me = (2 * lax.axis_index("x") + lax.axis_index("y")).astype(jnp.int32).reshape(1)

    per_layer = [_layer_weights(l) for l in range(DEPTH)]
    srcs = [w[n] for n in _SMALL_SHARD] + [w[n][i].astype(MM) for lw in per_layer for n, i in lw]
    lands = [lax.dynamic_update_index_in_dim(jnp.zeros((N_CHIPS, *s.shape), s.dtype), s[None], me[0], 0) for s in srcs]
    gathered = _gather_two_level(srcs, lands)
    cw_g, lg_g, lb_g = gathered[:3]
    small = {n: w[n] for n in _SMALL_REPL}
    small["conv_w"] = jnp.moveaxis(cw_g, 0, 2).reshape(cw_g.shape[1], CONV_WIDTH, -1)
    small["ln_g"] = jnp.moveaxis(lg_g, 0, 2).reshape(DEPTH, 3, -1)
    small["ln_b"] = jnp.moveaxis(lb_g, 0, 2).reshape(DEPTH, 3, -1)
    it = iter(gathered[3:])
    layer_w = [{n: next(it) for n, _ in lw} for lw in per_layer]

    loss, grad_x, layer_g, small_g = _local_step(x, mem, loss_target, layer_w, small)
    loss = lax.psum(loss, ("x", "y", "c"))

    core = lax.axis_index("c").astype(jnp.int32).reshape(1)
    grads_flat = [layer_g[l][n] for l in range(DEPTH) for n, _ in per_layer[l]]
    pair = [_sum_pair(g, a, core) for g, a in zip(grads_flat, _swap_halves(grads_flat))]
    recv = _scatter_chips(pair)
    per_name = {n: [None] * (DEPTH if n.startswith(("mem_", "ffn_")) else DEPTH // 2) for n in _BIG}
    flat_names = [ni for lw in per_layer for ni in lw]
    for (n, i), p, rc in zip(flat_names, pair, recv):
        per_name[n][i] = _sum_chips(p, rc, me)
    sums = [jnp.stack(per_name[n]) for n in _BIG]
    sib = _swap_sibling(sums)

    out = {}
    for n, s_own, s_sib in zip(_BIG, sums, sib):
        out[n] = _adamw_halves(f"adamw_{n}", s_own, s_sib, core, w[n], mo[n], vo[n])

    order = _SMALL_REPL + _SMALL_SHARD
    part = _pack([small_g[n] for n in order])
    total = _sum_devices(_gather_all(part))
    full = dict(zip(order, _unpack(total, [small_g[n].shape for n in order])))
    x_i, y_i = lax.axis_index("x"), lax.axis_index("y")
    chip = 2 * x_i + y_i
    loc = {n: full[n] for n in _SMALL_REPL}
    for n in _SMALL_SHARD:
        wd = w[n].shape[-1]
        loc[n] = lax.dynamic_slice_in_dim(full[n], chip * wd, wd, axis=full[n].ndim - 1)
    gp, wp, mp, vp = (_pack([src[n] for n in order]) for src in (loc, w, mo, vo))
    r128 = gp.shape[0]
    res = _adamw("adamw_small", [gp.reshape(1, r128, 128)], wp.reshape(1, r128, 128), mp.reshape(1, r128, 128),
                 vp.reshape(1, r128, 128))
    shapes = [w[n].shape for n in order]
    unp = [_unpack(r.reshape(r128, 128), shapes) for r in res]
    for i, n in enumerate(order):
        out[n] = tuple(u[i] for u in unp)

    grads = [out[n][0] for n in _NAMES]
    deltas = [out[n][1] for n in _NAMES]
    new_m = [out[n][2] for n in _NAMES]
    new_v = [out[n][3] for n in _NAMES]
    return (loss, grad_x, *grads, *deltas, *new_m, *new_v)
```

```python
import functools
import math

import jax
import jax.numpy as jnp
from jax import lax
from jax.experimental import pallas as pl
from jax.experimental.pallas import tpu as pltpu

F32 = jnp.float32
MM = jnp.bfloat16
ACT = jnp.bfloat16
MESH = pl.DeviceIdType.MESH

DEPTH = 4
CHUNK = 128
CONV_WIDTH = 31
HALO = 32
MEM_HEADS = 4
C_HEAD_DIM = 64
ALPHA = (2.0 * DEPTH) ** 0.25
LN_EPS = 1e-5
ADAM_LR, ADAM_B1, ADAM_B2, ADAM_EPS, ADAM_WD, ADAM_STEP = 0.001, 0.9, 0.999, 1e-08, 0.01, 10

VMEM_CAP_V7X = 64 * 1024 * 1024
VMEM_MAX_REQUEST = 56 * 1024 * 1024
N_CHIPS = 4
N_DEV = 8


def _tile(n, pref):
    if n <= pref:
        return n
    for t in range(pref - pref % 8, 7, -8):
        if n % t == 0:
            return t
    return n


def _nbytes(shape, dtype):
    return math.prod(1 if s is None else s for s in shape) * jnp.dtype(dtype).itemsize


def _vmem_limit(block_bytes, scratch_bytes=0, temp_bytes=0):
    est = 2 * block_bytes + scratch_bytes + temp_bytes
    return int(min(VMEM_MAX_REQUEST, max(16 * 1024 * 1024, est * 5 // 4)))


def _params(sem, block_bytes, scratch_bytes=0, temp_bytes=0):
    return pltpu.CompilerParams(dimension_semantics=sem,
                                vmem_limit_bytes=_vmem_limit(block_bytes, scratch_bytes, temp_bytes))


_DIMS = {"nn": (((1,), (0,)), ((), ())), "nt": (((1,), (1,)), ((), ())), "tn": (((0,), (0,)), ((), ()))}


def _mm(name, mode, grid, a, a_spec, bs, b_specs, outs, out_specs, acc_shape,
        extras=(), extra_specs=(), epilogue=None):
    nb, ne, no = len(bs), len(extras), len(outs)
    nk = grid[2]

    def body(*refs):
        a_ref = refs[0]
        b_refs = refs[1:1 + nb]
        e_refs = refs[1 + nb:1 + nb + ne]
        o_refs = refs[1 + nb + ne:1 + nb + ne + no]
        accs = refs[1 + nb + ne + no:]

        def finish(vals):
            if epilogue is not None:
                vals = epilogue(vals, [e[...].astype(F32) for e in e_refs])
            for o, v in zip(o_refs, vals):
                o[...] = v.astype(o.dtype)

        av = a_ref[...].astype(MM)

        def product(bv):
            if av.ndim == 2:
                return lax.dot_general(av, bv, _DIMS[mode], preferred_element_type=F32)
            out = lax.dot_general(av[0], bv[0], _DIMS[mode], preferred_element_type=F32)
            for s in range(1, av.shape[0]):
                out = out + lax.dot_general(av[s], bv[s], _DIMS[mode], preferred_element_type=F32)
            return out

        if nk == 1:
            finish([product(b_ref[...].astype(MM)) for b_ref in b_refs])
            return
        k = pl.program_id(2)

        @pl.when(k == 0)
        def _():
            for acc in accs:
                acc[...] = jnp.zeros_like(acc)

        for b_ref, acc in zip(b_refs, accs):
            acc[...] += product(b_ref[...].astype(MM))

        @pl.when(k == nk - 1)
        def _():
            finish([acc[...] for acc in accs])

    blocks = (_nbytes(a_spec.block_shape, a.dtype)
              + sum(_nbytes(s.block_shape, b.dtype) for s, b in zip(b_specs, bs))
              + sum(_nbytes(s.block_shape, e.dtype) for s, e in zip(extra_specs, extras))
              + sum(_nbytes(s.block_shape, o.dtype) for s, o in zip(out_specs, outs)))
    acc_bytes = nb * _nbytes(acc_shape, F32)
    res = pl.pallas_call(
        body, name=name, grid=grid,
        in_specs=[a_spec, *b_specs, *extra_specs], out_specs=list(out_specs), out_shape=list(outs),
        scratch_shapes=[pltpu.VMEM(acc_shape, F32)] * (nb if nk > 1 else 0),
        compiler_params=_params(("parallel", "parallel", "arbitrary"), blocks, acc_bytes if nk > 1 else 0, 4 * acc_bytes),
    )(a, *bs, *extras)
    return res


def _sds(shape, dtype):
    return jax.ShapeDtypeStruct(shape, dtype)


def _dense(name, a, w, out_dtype):
    t, kdim = a.shape
    n = w.shape[1]
    tm, tn, tk = _tile(t, 1024), _tile(n, 1024), _tile(kdim, 1024)
    grid = (t // tm, n // tn, kdim // tk)
    return _mm(name, "nn", grid, a, pl.BlockSpec((tm, tk), lambda i, j, k: (i, k)),
               [w], [pl.BlockSpec((tk, tn), lambda i, j, k: (k, j))],
               [_sds((t, n), out_dtype)], [pl.BlockSpec((tm, tn), lambda i, j, k: (i, j))], (tm, tn))[0]


def _dense_nt(name, a, w, out_dtype, extras=(), epilogue=None):
    t, n = a.shape
    kout = w.shape[0]
    tm, tn, tk = _tile(t, 1024), _tile(kout, 1024), _tile(n, 1024)
    grid = (t // tm, kout // tn, n // tk)
    return _mm(name, "nt", grid, a, pl.BlockSpec((tm, tk), lambda i, j, k: (i, k)),
               [w], [pl.BlockSpec((tn, tk), lambda i, j, k: (j, k))],
               [_sds((t, kout), out_dtype)], [pl.BlockSpec((tm, tn), lambda i, j, k: (i, j))], (tm, tn),
               extras=extras, extra_specs=[pl.BlockSpec((tm, tn), lambda i, j, k: (i, j))] * len(extras),
               epilogue=epilogue)[0]


def _dense_tn(name, a, b, out_dtype=MM):
    t, m = a.shape
    n = b.shape[1]
    tm, tn, tk = _tile(m, 1024), _tile(n, 1024), _tile(t, 1024)
    grid = (m // tm, n // tn, t // tk)
    return _mm(name, "tn", grid, a, pl.BlockSpec((tk, tm), lambda i, j, k: (k, i)),
               [b], [pl.BlockSpec((tk, tn), lambda i, j, k: (k, j))],
               [_sds((m, n), out_dtype)], [pl.BlockSpec((tm, tn), lambda i, j, k: (i, j))], (tm, tn))[0]


_INV_SQRT2 = 0.7071067811865476
_INV_SQRT_2PI = 0.3989422804014327


def _gelu(x):
    return 0.5 * x * (1.0 + lax.erf(x * _INV_SQRT2))


def _gelu_grad(x):
    return 0.5 * (1.0 + lax.erf(x * _INV_SQRT2)) + x * jnp.exp(-0.5 * x * x) * _INV_SQRT_2PI


def _sigmoid(x):
    return 1.0 / (1.0 + jnp.exp(-x))


def _norm_stats(x):
    mu = jnp.mean(x, axis=-1, keepdims=True)
    xc = x - mu
    var = jnp.mean(xc * xc, axis=-1, keepdims=True)
    rstd = lax.rsqrt(var + LN_EPS)
    return xc * rstd, rstd


def _norm_bwd(dy_g, xh, rstd):
    m1 = jnp.mean(dy_g, axis=-1, keepdims=True)
    m2 = jnp.mean(dy_g * xh, axis=-1, keepdims=True)
    return rstd * (dy_g - m1 - xh * m2)


def _rows8(x):
    r, c = x.shape
    return jnp.sum(x.reshape(r // 8, 8, c), axis=0)


def _ln_epilogue(vals, ex):
    xp, gp, bp, g, b = ex
    xh, rstd = _norm_stats(ALPHA * (xp * gp + bp) + vals[0])
    return [xh, xh * g + b, rstd]


def _proj_ln(name, a, a_spec_of, w, w_spec, tm_pref, xh_prev, g_prev, b_prev, g, b):
    t, d = xh_prev.shape
    tm = _tile(t, tm_pref)
    row = pl.BlockSpec((tm, d), lambda i, j, k: (i, 0))
    vec = pl.BlockSpec((1, d), lambda i, j, k: (0, 0))
    return _mm(name, "nn", (t // tm, 1, 1), a, a_spec_of(tm), [w], [w_spec],
               [_sds((t, d), F32), _sds((t, d), ACT), _sds((t, 1), F32)],
               [row, row, pl.BlockSpec((tm, 1), lambda i, j, k: (i, 0))], (tm, d),
               extras=[xh_prev, g_prev, b_prev, g, b], extra_specs=[row, vec, vec, vec, vec], epilogue=_ln_epilogue)


def _dense_ln(name, a, w, *ln_args):
    kdim, d = w.shape
    return _proj_ln(name, a, lambda tm: pl.BlockSpec((tm, kdim), lambda i, j, k: (i, 0)), w,
                    pl.BlockSpec((kdim, d), lambda i, j, k: (0, 0)), 1024, *ln_args)


def _ffn_down_ln(name, gact, w2, *ln_args):
    f, d = w2.shape[-2:]
    return _proj_ln(name, gact, lambda tm: pl.BlockSpec((N_CHIPS, tm, f), lambda i, j, k: (0, i, 0)), w2,
                    pl.BlockSpec((N_CHIPS, f, d), lambda i, j, k: (0, 0, 0)), 512, *ln_args)


def _ln_bwd(name, dy, xh, rstd, g):
    t, d = dy.shape
    tm = _tile(t, 512)
    n = t // tm

    def body(dy_ref, xh_ref, rstd_ref, g_ref, dr_ref, dra_ref, dg_ref, db_ref, dg_acc, db_acc):
        i = pl.program_id(0)

        @pl.when(i == 0)
        def _():
            dg_acc[...] = jnp.zeros_like(dg_acc)
            db_acc[...] = jnp.zeros_like(db_acc)

        dyv = dy_ref[...]
        xhv = xh_ref[...]
        dr = _norm_bwd(dyv * g_ref[...], xhv, rstd_ref[...])
        dr_ref[...] = dr
        dra_ref[...] = dr.astype(dra_ref.dtype)
        dg_acc[...] += _rows8(dyv * xhv)
        db_acc[...] += _rows8(dyv)

        @pl.when(i == n - 1)
        def _():
            dg_ref[...] = jnp.sum(dg_acc[...], axis=0, keepdims=True)
            db_ref[...] = jnp.sum(db_acc[...], axis=0, keepdims=True)

    row = pl.BlockSpec((tm, d), lambda i: (i, 0))
    vec = pl.BlockSpec((1, d), lambda i: (0, 0))
    return pl.pallas_call(
        body, name=name, grid=(n,),
        in_specs=[row, row, pl.BlockSpec((tm, 1), lambda i: (i, 0)), vec],
        out_specs=[row, row, vec, vec],
        out_shape=[_sds((t, d), F32), _sds((t, d), ACT), _sds((1, d), F32), _sds((1, d), F32)],
        scratch_shapes=[pltpu.VMEM((8, d), F32), pltpu.VMEM((8, d), F32)],
        compiler_params=_params(("arbitrary",), 4 * tm * d * 4, 0, 4 * tm * d * 4),
    )(dy, xh, rstd, g)


def _loss_head(xh, g, b, target):
    t, d = xh.shape
    tm = _tile(t, 512)
    n = t // tm

    def body(xh_ref, g_ref, b_ref, tg_ref, dy_ref, loss_ref, acc):
        i = pl.program_id(0)

        @pl.when(i == 0)
        def _():
            acc[...] = jnp.zeros_like(acc)

        err = xh_ref[...] * g_ref[...] + b_ref[...] - tg_ref[...]
        dy_ref[...] = err * (1.0 / d)
        acc[...] += _rows8(err * err)

        @pl.when(i == n - 1)
        def _():
            s = jnp.sum(jnp.sum(acc[...], axis=0, keepdims=True), axis=1, keepdims=True)
            loss_ref[...] = jnp.broadcast_to(s * (0.5 / d), loss_ref.shape)

    row = pl.BlockSpec((tm, d), lambda i: (i, 0))
    vec = pl.BlockSpec((1, d), lambda i: (0, 0))
    dy, loss = pl.pallas_call(
        body, name="loss_head", grid=(n,),
        in_specs=[row, vec, vec, row],
        out_specs=[row, pl.BlockSpec((8, 128), lambda i: (0, 0))],
        out_shape=[_sds((t, d), F32), _sds((8, 128), F32)],
        scratch_shapes=[pltpu.VMEM((8, d), F32)],
        compiler_params=_params(("arbitrary",), 3 * tm * d * 4, 0, 2 * tm * d * 4),
    )(xh, g, b, target)
    return dy, loss[0, 0]


def _causal_w(w):
    r = lax.broadcasted_iota(jnp.int32, w.shape, 0)
    c = lax.broadcasted_iota(jnp.int32, w.shape, 1)
    return jnp.where(r >= c, w, 0.0)


def _gmlp_fwd(name, h, ln_g, ln_b, w_s, b_s_col):
    t = h.shape[0]
    tt = _tile(t, 256)
    wd = 4 * CHUNK

    def body(u_ref, v_ref, g_ref, b_ref, w_ref, bs_ref, ya_ref):
        for gi in range(4):
            ln = slice(gi * CHUNK, (gi + 1) * CHUNK)
            u = _gelu(u_ref[:, ln].astype(F32))
            v = _gelu(v_ref[:, ln].astype(F32))
            xh, _ = _norm_stats(v)
            vg = (xh * g_ref[:, ln] + b_ref[:, ln]).astype(MM)
            w = _causal_w(w_ref[gi]).astype(MM)
            for c in range(tt // CHUNK):
                rs = slice(c * CHUNK, (c + 1) * CHUNK)
                mixed = jnp.dot(w, vg[rs], preferred_element_type=F32) + bs_ref[gi]
                ya_ref[rs, ln] = (u[rs] * mixed).astype(ya_ref.dtype)

    vec = pl.BlockSpec((1, wd), lambda i: (0, 0))
    return pl.pallas_call(
        body, name=name, grid=(t // tt,),
        in_specs=[pl.BlockSpec((tt, wd), lambda i: (i, 0)), pl.BlockSpec((tt, wd), lambda i: (i, 1)), vec, vec,
                  pl.BlockSpec((4, CHUNK, CHUNK), lambda i: (0, 0, 0)), pl.BlockSpec((4, CHUNK, 1), lambda i: (0, 0, 0))],
        out_specs=pl.BlockSpec((tt, wd), lambda i: (i, 0)),
        out_shape=_sds((t, wd), ACT),
        compiler_params=_params(("parallel",), 3 * tt * wd * 4, 0, 8 * tt * CHUNK * 4),
    )(h, h, ln_g, ln_b, w_s, b_s_col)


def _gmlp_bwd(name, h, dyab, ln_g, ln_b, w_s, b_s_col):
    t = h.shape[0]
    tt = _tile(t, 256)
    n = t // tt
    wd = 4 * CHUNK

    def body(u_ref, v_ref, dy_ref, g_ref, b_ref, w_ref, bs_ref, duv_ref, dg_ref, db_ref, dw_ref, dbs_ref,
             dg_acc, db_acc):
        i = pl.program_id(0)

        @pl.when(i == 0)
        def _():
            dg_acc[...] = jnp.zeros_like(dg_acc)
            db_acc[...] = jnp.zeros_like(db_acc)
            dw_ref[...] = jnp.zeros_like(dw_ref)
            dbs_ref[...] = jnp.zeros_like(dbs_ref)

        for gi in range(4):
            ln = slice(gi * CHUNK, (gi + 1) * CHUNK)
            upre = u_ref[:, ln].astype(F32)
            vpre = v_ref[:, ln].astype(F32)
            u = _gelu(upre)
            v = _gelu(vpre)
            xh, rstd = _norm_stats(v)
            gv = g_ref[:, ln]
            vg = (xh * gv + b_ref[:, ln]).astype(MM)
            w = _causal_w(w_ref[gi]).astype(MM)
            dya = dy_ref[:, ln].astype(F32)
            dmixed = dya * u
            dmm = dmixed.astype(MM)
            dvg_parts, mixed_parts = [], []
            dw = jnp.zeros((CHUNK, CHUNK), F32)
            dbs = jnp.zeros((CHUNK, 1), F32)
            for c in range(tt // CHUNK):
                rs = slice(c * CHUNK, (c + 1) * CHUNK)
                mixed_parts.append(jnp.dot(w, vg[rs], preferred_element_type=F32) + bs_ref[gi])
                dw = dw + lax.dot_general(dmm[rs], vg[rs], _DIMS["nt"], preferred_element_type=F32)
                dbs = dbs + jnp.sum(dmixed[rs], axis=1, keepdims=True)
                dvg_parts.append(lax.dot_general(w, dmm[rs], _DIMS["tn"], preferred_element_type=F32))
            mixed = jnp.concatenate(mixed_parts, axis=0)
            dvg = jnp.concatenate(dvg_parts, axis=0)
            dw_ref[gi] += _causal_w(dw)
            dbs_ref[gi] += dbs
            dg_acc[:, ln] += _rows8(dvg * xh)
            db_acc[:, ln] += _rows8(dvg)
            dv = _norm_bwd(dvg * gv, xh, rstd) * _gelu_grad(vpre)
            du = dya * mixed * _gelu_grad(upre)
            duv_ref[:, ln] = du.astype(duv_ref.dtype)
            duv_ref[:, wd + gi * CHUNK: wd + (gi + 1) * CHUNK] = dv.astype(duv_ref.dtype)

        @pl.when(i == n - 1)
        def _():
            dg_ref[...] = jnp.sum(dg_acc[...], axis=0, keepdims=True)
            db_ref[...] = jnp.sum(db_acc[...], axis=0, keepdims=True)

    vec = pl.BlockSpec((1, wd), lambda i: (0, 0))
    wspec = pl.BlockSpec((4, CHUNK, CHUNK), lambda i: (0, 0, 0))
    bspec = pl.BlockSpec((4, CHUNK, 1), lambda i: (0, 0, 0))
    return pl.pallas_call(
        body, name=name, grid=(n,),
        in_specs=[pl.BlockSpec((tt, wd), lambda i: (i, 0)), pl.BlockSpec((tt, wd), lambda i: (i, 1)),
                  pl.BlockSpec((tt, wd), lambda i: (i, 0)), vec, vec, wspec, bspec],
        out_specs=[pl.BlockSpec((tt, 2 * wd), lambda i: (i, 0)), vec, vec, wspec, bspec],
        out_shape=[_sds((t, 2 * wd), ACT), _sds((1, wd), F32), _sds((1, wd), F32),
                   _sds((4, CHUNK, CHUNK), F32), _sds((4, CHUNK, 1), F32)],
        scratch_shapes=[pltpu.VMEM((8, wd), F32), pltpu.VMEM((8, wd), F32)],
        compiler_params=_params(("arbitrary",), 5 * tt * wd * 4, 0, 16 * tt * CHUNK * 4),
    )(h, h, dyab, ln_g, ln_b, w_s, b_s_col)


_ROWS = 256


_SUBLANES = 8


def _shifted(win, sh_ref):
    n = win.shape[0]
    for r in range(_SUBLANES):
        sh_ref[r, 0:n - r, :] = win[r:, :]
    return sh_ref


def _rows_at(sh_ref, s):
    r = s % _SUBLANES
    return sh_ref[r, s - r:s - r + _ROWS, :]


def _conv_taps(shifted, cw, lo):
    acc = jnp.zeros((_ROWS, CHUNK), F32)
    for w in range(CONV_WIDTH):
        acc = acc + cw[w:w + 1, :] * _rows_at(shifted, lo(w))
    return acc


def _conv_fwd(name, h, n_ex, cw, cb, gg, gb):
    t = h.shape[0]
    s = t // n_ex
    nt = s // _ROWS

    def body(a_ref, gt_ref, cw_ref, cb_ref, gg_ref, gb_ref, yb_ref, hh, sh):
        hh[0:HALO, :] = jnp.zeros((HALO, CHUNK), F32)
        hh[HALO:HALO + s, :] = a_ref[...].astype(F32) * _sigmoid(gt_ref[...].astype(F32))
        cwv = cw_ref[...]

        def tile(i, carry):
            r0 = pl.multiple_of(i * _ROWS, _ROWS)
            win = _shifted(hh[pl.ds(r0, _ROWS + HALO), :], sh)
            c = _conv_taps(win, cwv, lambda w: w + HALO - (CONV_WIDTH - 1)) + cb_ref[...]
            xh, _ = _norm_stats(c)
            hg = xh * gg_ref[...] + gb_ref[...]
            yb_ref[pl.ds(r0, _ROWS), :] = (hg * _sigmoid(hg)).astype(yb_ref.dtype)
            return carry

        lax.fori_loop(0, nt, tile, 0)

    vec = pl.BlockSpec((1, CHUNK), lambda g, b: (0, g))
    return pl.pallas_call(
        body, name=name, grid=(4, n_ex),
        in_specs=[pl.BlockSpec((s, CHUNK), lambda g, b: (b, 8 + g)), pl.BlockSpec((s, CHUNK), lambda g, b: (b, 12 + g)),
                  pl.BlockSpec((CONV_WIDTH, CHUNK), lambda g, b: (0, g)), vec, vec, vec],
        out_specs=pl.BlockSpec((s, CHUNK), lambda g, b: (b, g)),
        out_shape=_sds((t, 4 * CHUNK), ACT),
        scratch_shapes=[pltpu.VMEM((s + HALO, CHUNK), F32), pltpu.VMEM((_SUBLANES, _ROWS + HALO, CHUNK), F32)],
        compiler_params=_params(("parallel", "parallel"), 3 * s * CHUNK * 4, (s + HALO) * CHUNK * 4, 4 * s * CHUNK * 4),
    )(h, h, cw, cb, gg, gb)


def _conv_bwd(name, h, dyab, n_ex, cw, cb, gg, gb):
    t = h.shape[0]
    s = t // n_ex
    nt = s // _ROWS

    def body(a_ref, gt_ref, dy_ref, cw_ref, cb_ref, gg_ref, gb_ref,
             da_ref, dgt_ref, dcw_ref, dcb_ref, dgg_ref, dgb_ref, hh, dcs, acc, sh):
        b = pl.program_id(1)

        @pl.when(b == 0)
        def _():
            dcw_ref[...] = jnp.zeros_like(dcw_ref)
            dcb_ref[...] = jnp.zeros_like(dcb_ref)
            dgg_ref[...] = jnp.zeros_like(dgg_ref)
            dgb_ref[...] = jnp.zeros_like(dgb_ref)

        hh[0:HALO, :] = jnp.zeros((HALO, CHUNK), F32)
        hh[HALO:HALO + s, :] = a_ref[...].astype(F32) * _sigmoid(gt_ref[...].astype(F32))
        dcs[s:s + HALO, :] = jnp.zeros((HALO, CHUNK), F32)
        acc[...] = jnp.zeros_like(acc)
        cwv = cw_ref[...]
        off = HALO - (CONV_WIDTH - 1)
        taps = 8 * CONV_WIDTH

        def tile1(i, carry):
            r0 = pl.multiple_of(i * _ROWS, _ROWS)
            win = _shifted(hh[pl.ds(r0, _ROWS + HALO), :], sh)
            c = _conv_taps(win, cwv, lambda w: w + off) + cb_ref[...]
            xh, rstd = _norm_stats(c)
            hg = xh * gg_ref[...] + gb_ref[...]
            sg = _sigmoid(hg)
            dhg = dy_ref[pl.ds(r0, _ROWS), :].astype(F32) * (sg * (1.0 + hg * (1.0 - sg)))
            acc[taps:taps + 8, :] += _rows8(dhg * xh)
            acc[taps + 8:taps + 16, :] += _rows8(dhg)
            dc = _norm_bwd(dhg * gg_ref[...], xh, rstd)
            dcs[pl.ds(r0, _ROWS), :] = dc
            acc[taps + 16:taps + 24, :] += _rows8(dc)
            for w in range(CONV_WIDTH):
                acc[8 * w:8 * w + 8, :] += _rows8(dc * _rows_at(win, w + off))
            return carry

        lax.fori_loop(0, nt, tile1, 0)

        def tile2(i, carry):
            r0 = pl.multiple_of(i * _ROWS, _ROWS)
            win = _shifted(dcs[pl.ds(r0, _ROWS + HALO), :], sh)
            dhh = _conv_taps(win, cwv, lambda w: CONV_WIDTH - 1 - w)
            av = a_ref[pl.ds(r0, _ROWS), :].astype(F32)
            sg = _sigmoid(gt_ref[pl.ds(r0, _ROWS), :].astype(F32))
            da_ref[pl.ds(r0, _ROWS), :] = (dhh * sg).astype(da_ref.dtype)
            dgt_ref[pl.ds(r0, _ROWS), :] = (dhh * av * sg * (1.0 - sg)).astype(dgt_ref.dtype)
            return carry

        lax.fori_loop(0, nt, tile2, 0)
        dcw_ref[...] += jnp.sum(acc[0:taps, :].reshape(CONV_WIDTH, 8, CHUNK), axis=1)
        dgg_ref[...] += jnp.sum(acc[taps:taps + 8, :], axis=0, keepdims=True)
        dgb_ref[...] += jnp.sum(acc[taps + 8:taps + 16, :], axis=0, keepdims=True)
        dcb_ref[...] += jnp.sum(acc[taps + 16:taps + 24, :], axis=0, keepdims=True)

    vec = pl.BlockSpec((1, CHUNK), lambda g, b: (0, g))
    tap = pl.BlockSpec((CONV_WIDTH, CHUNK), lambda g, b: (0, g))
    seq = pl.BlockSpec((s, CHUNK), lambda g, b: (b, g))
    return pl.pallas_call(
        body, name=name, grid=(4, n_ex),
        in_specs=[pl.BlockSpec((s, CHUNK), lambda g, b: (b, 8 + g)), pl.BlockSpec((s, CHUNK), lambda g, b: (b, 12 + g)),
                  pl.BlockSpec((s, CHUNK), lambda g, b: (b, 4 + g)), tap, vec, vec, vec],
        out_specs=[seq, seq, tap, vec, vec, vec],
        out_shape=[_sds((t, 4 * CHUNK), ACT), _sds((t, 4 * CHUNK), ACT), _sds((CONV_WIDTH, 4 * CHUNK), F32),
                   _sds((1, 4 * CHUNK), F32), _sds((1, 4 * CHUNK), F32), _sds((1, 4 * CHUNK), F32)],
        scratch_shapes=[pltpu.VMEM((s + HALO, CHUNK), F32), pltpu.VMEM((s + HALO, CHUNK), F32),
                        pltpu.VMEM((8 * CONV_WIDTH + 24, CHUNK), F32), pltpu.VMEM((_SUBLANES, _ROWS + HALO, CHUNK), F32)],
        compiler_params=_params(("parallel", "arbitrary"), 5 * s * CHUNK * 4, 2 * (s + HALO) * CHUNK * 4, 4 * s * CHUNK * 4),
    )(h, h, dyab, cw, cb, gg, gb)


_TQ = 256
_SB_DEAD = -110.0


def _tri(kind):
    r = lax.broadcasted_iota(jnp.int32, (_TQ, _TQ), 0)
    c = lax.broadcasted_iota(jnp.int32, (_TQ, _TQ), 1)
    m = {"gt": r > c, "le": r <= c, "lt": r < c}[kind]
    return jnp.where(m, 1.0, 0.0).astype(jnp.bfloat16)


def _split_dot(x, tri2):
    hi = x.astype(jnp.bfloat16)
    lo = (x - hi.astype(F32)).astype(jnp.bfloat16)
    return jnp.dot(jnp.concatenate([hi, lo], axis=1), tri2, preferred_element_type=F32)


def _neg_abs(x):
    bits = lax.bitcast_convert_type(x, jnp.uint32) | jnp.uint32(0x80000000)
    return lax.bitcast_convert_type(bits, F32)


def _log_not_beta(nz):
    return jnp.minimum(nz, 0.0) - jnp.log(1.0 + jnp.exp(_neg_abs(nz)))


def _sb_fwd(name, qkv, n_ex):
    t = qkv.shape[0]
    d = qkv.shape[1] // 3
    npair = d // CHUNK
    s = t // n_ex
    nq = s // _TQ
    neg_a = -(C_HEAD_DIM ** -0.5)

    def body(q_ref, k_ref, v_ref, o_ref, lt_ref, o_acc, c_acc):
        i = pl.program_id(2)
        first = lax.broadcasted_iota(jnp.int32, (_TQ, CHUNK), 1) < C_HEAD_DIM
        q2 = q_ref[...]
        zero = jnp.zeros_like(q2)
        qh = [jnp.where(first, q2, zero), jnp.where(first, zero, q2)]
        tri2 = jnp.concatenate([_tri("gt")] * 2, axis=0)
        o_acc[...] = jnp.zeros_like(o_acc)
        c_acc[...] = jnp.zeros_like(c_acc)

        def tiles(js, mask):
            work = [(a, hd) for a in range(len(js)) for hd in range(2)]
            rows = [pl.ds(pl.multiple_of(j * _TQ, _TQ), _TQ) for j in js]
            kts = [k_ref[r, :] for r in rows]
            vts = [v_ref[r, :] for r in rows]
            nzs = {w: lax.dot_general(qh[w[1]], kts[w[0]], _DIMS["nt"], preferred_element_type=F32) * neg_a for w in work}
            lns = {w: _log_not_beta(nzs[w]) for w in work}
            if mask is not None:
                lns = {w: jnp.where(mask, lns[w], 0.0) for w in work}
            locs = {w: _split_dot(lns[w], tri2) for w in work}
            laters = {}
            for hd in range(2):
                carry = c_acc[hd]
                for a in range(len(js)):
                    laters[a, hd] = carry + locs[a, hd]
                    carry = laters[a, hd][:, 0:1] + lns[a, hd][:, 0:1]
                c_acc[hd] = carry
            atts = {w: jnp.exp(lns[w] - nzs[w] + laters[w]) for w in work}
            if mask is not None:
                atts = {w: jnp.where(mask, atts[w], 0.0) for w in work}
            for hd in range(2):
                acc = o_acc[hd]
                for a in range(len(js)):
                    acc = acc + jnp.dot(atts[a, hd].astype(MM), vts[a], preferred_element_type=F32)
                o_acc[hd] = acc

        tiles([i], lax.broadcasted_iota(jnp.int32, (_TQ, _TQ), 1) < lax.broadcasted_iota(jnp.int32, (_TQ, _TQ), 0))

        def alive():
            return jnp.max(jnp.maximum(c_acc[0], c_acc[1])) >= _SB_DEAD

        def cond(st):
            return jnp.logical_and(st[0] >= 0, st[1])

        def step(st):
            tiles([st[0]], None)
            return st[0] - 1, alive()

        j_last, _ = lax.while_loop(cond, step, (i - 1, alive()))
        o_ref[...] = jnp.where(first, o_acc[0], o_acc[1])
        lt_ref[:, 0:1] = c_acc[0]
        lt_ref[:, 1:2] = c_acc[1]
        lt_ref[:, 2:3] = jnp.full((_TQ, 1), j_last + 1, jnp.int32).astype(F32)

    return pl.pallas_call(
        body, name=name, grid=(n_ex, npair, nq),
        in_specs=[pl.BlockSpec((_TQ, CHUNK), lambda b, p, i: (b * nq + i, p)),
                  pl.BlockSpec((s, CHUNK), lambda b, p, i: (b, npair + p)),
                  pl.BlockSpec((s, CHUNK), lambda b, p, i: (b, 2 * npair + p))],
        out_specs=[pl.BlockSpec((_TQ, CHUNK), lambda b, p, i: (b * nq + i, p)),
                   pl.BlockSpec((None, _TQ, 3), lambda b, p, i: (p, b * nq + i, 0))],
        out_shape=[_sds((t, d), F32), _sds((npair, t, 3), F32)],
        scratch_shapes=[pltpu.VMEM((2, _TQ, CHUNK), F32), pltpu.VMEM((2, _TQ, 1), F32)],
        compiler_params=_params(("parallel", "parallel", "arbitrary"), 2 * s * CHUNK * 2 + 4 * _TQ * CHUNK * 4,
                                4 * _TQ * CHUNK * 4, 24 * _TQ * _TQ * 4),
    )(qkv, qkv, qkv)


def _sb_bwd(name, qkv, do, ltot, n_ex):
    t = qkv.shape[0]
    d = qkv.shape[1] // 3
    npair = d // CHUNK
    s = t // n_ex
    nq = s // _TQ
    scale = C_HEAD_DIM ** -0.5
    neg_a = -scale

    def body(q_ref, k_ref, v_ref, do_ref, lt_ref, dq_ref, dk_ref, dv_ref, dq_acc, cp_acc, cg_acc, dk_acc, dv_acc):
        i = pl.program_id(2)

        @pl.when(i == 0)
        def _():
            dk_acc[...] = jnp.zeros_like(dk_acc)
            dv_acc[...] = jnp.zeros_like(dv_acc)

        first = lax.broadcasted_iota(jnp.int32, (_TQ, CHUNK), 1) < C_HEAD_DIM
        q2 = q_ref[...]
        do2 = do_ref[...]
        qs = (q2 * scale).astype(q2.dtype)
        zero = jnp.zeros_like(q2)
        qh = [jnp.where(first, q2, zero), jnp.where(first, zero, q2)]
        doh = [jnp.where(first, do2, zero), jnp.where(first, zero, do2)]
        lt = [lt_ref[:, 0:1], lt_ref[:, 1:2]]
        tri2_le = jnp.concatenate([_tri("le")] * 2, axis=0)
        tri_lt = _tri("lt").astype(MM)
        dq_acc[...] = jnp.zeros_like(dq_acc)
        cp_acc[...] = jnp.zeros_like(cp_acc)
        cg_acc[...] = jnp.zeros_like(cg_acc)

        def tiles(js, mask):
            na = len(js)
            work = [(a, hd) for a in range(na) for hd in range(2)]
            last = slice(_TQ - 1, _TQ)
            rows = [pl.ds(pl.multiple_of(j * _TQ, _TQ), _TQ) for j in js]
            kts = [k_ref[r, :] for r in rows]
            vts = [v_ref[r, :] for r in rows]
            ksc = [(kt * scale).astype(kt.dtype) for kt in kts]
            nzs = {w: lax.dot_general(qh[w[1]], kts[w[0]], _DIMS["nt"], preferred_element_type=F32) * neg_a for w in work}
            datts = {w: lax.dot_general(doh[w[1]], vts[w[0]], _DIMS["nt"], preferred_element_type=F32) for w in work}
            lns = {w: _log_not_beta(nzs[w]) for w in work}
            if mask is not None:
                lns = {w: jnp.where(mask, lns[w], 0.0) for w in work}
            pins = {w: _split_dot(lns[w], tri2_le) for w in work}
            lss = {w: lns[w] - nzs[w] for w in work}
            atts = {}
            for hd in range(2):
                cp = cp_acc[hd]
                for a in range(na):
                    atts[a, hd] = jnp.exp(lss[a, hd] + ((lt[hd] - cp) - pins[a, hd]))
                    cp = cp + pins[a, hd][:, last]
                cp_acc[hd] = cp
            if mask is not None:
                atts = {w: jnp.where(mask, atts[w], 0.0) for w in work}
            gs = {w: datts[w] * atts[w] for w in work}
            locg = {w: jnp.dot(gs[w].astype(MM), tri_lt, preferred_element_type=F32) for w in work}
            dzs = {}
            for hd in range(2):
                carry = cg_acc[hd]
                for a in range(na):
                    big = carry + locg[a, hd]
                    dzs[a, hd] = gs[a, hd] - (gs[a, hd] + big) * jnp.exp(lss[a, hd])
                    carry = big[:, last] + gs[a, hd][:, last]
                cg_acc[hd] = carry
            if mask is not None:
                dzs = {w: jnp.where(mask, dzs[w], 0.0) for w in work}
            dzs = {w: dzs[w].astype(MM) for w in work}
            attm = {w: atts[w].astype(MM) for w in work}
            for hd in range(2):
                acc = dq_acc[hd]
                for a in range(na):
                    acc = acc + jnp.dot(dzs[a, hd], ksc[a], preferred_element_type=F32)
                dq_acc[hd] = acc
            for a in range(na):
                dk0, dk1 = [lax.dot_general(dzs[a, hd], qs, _DIMS["tn"], preferred_element_type=F32) for hd in range(2)]
                dv0, dv1 = [lax.dot_general(attm[a, hd], do2, _DIMS["tn"], preferred_element_type=F32) for hd in range(2)]
                dk_acc[rows[a], :] += jnp.where(first, dk0, dk1)
                dv_acc[rows[a], :] += jnp.where(first, dv0, dv1)

        def single(j, carry):
            tiles([j], None)
            return carry

        j_first = jnp.clip(jnp.max(lt_ref[:, 2:3]).astype(jnp.int32), 0, i)
        lax.fori_loop(j_first, i, single, 0)
        tiles([i], lax.broadcasted_iota(jnp.int32, (_TQ, _TQ), 1) < lax.broadcasted_iota(jnp.int32, (_TQ, _TQ), 0))
        dq_ref[...] = jnp.where(first, dq_acc[0], dq_acc[1]).astype(dq_ref.dtype)

        @pl.when(i == nq - 1)
        def _():
            dk_ref[...] = dk_acc[...].astype(dk_ref.dtype)
            dv_ref[...] = dv_acc[...].astype(dv_ref.dtype)

    qspec = pl.BlockSpec((_TQ, CHUNK), lambda b, p, i: (b * nq + i, p))
    kv_out = pl.BlockSpec((s, CHUNK), lambda b, p, i: (b, p))
    return pl.pallas_call(
        body, name=name, grid=(n_ex, npair, nq),
        in_specs=[qspec, pl.BlockSpec((s, CHUNK), lambda b, p, i: (b, npair + p)),
                  pl.BlockSpec((s, CHUNK), lambda b, p, i: (b, 2 * npair + p)), qspec,
                  pl.BlockSpec((None, _TQ, 3), lambda b, p, i: (p, b * nq + i, 0))],
        out_specs=[qspec, kv_out, kv_out],
        out_shape=[_sds((t, d), ACT)] * 3,
        scratch_shapes=[pltpu.VMEM((2, _TQ, CHUNK), F32), pltpu.VMEM((2, _TQ, 1), F32), pltpu.VMEM((2, _TQ, 1), F32),
                        pltpu.VMEM((s, CHUNK), F32), pltpu.VMEM((s, CHUNK), F32)],
        compiler_params=_params(("parallel", "parallel", "arbitrary"), 4 * s * CHUNK * 2,
                                4 * _TQ * CHUNK * 4 + 2 * s * CHUNK * 4, 32 * _TQ * _TQ * 4),
    )(qkv, qkv, qkv, do, ltot)


def _xattn_fwd(name, q, kk, vv, n_ex):
    t, d = q.shape
    m = kk.shape[0] // n_ex
    s = t // n_ex
    tq = _tile(s, 512)
    nq = s // tq
    hd_dim = d // MEM_HEADS
    scale = hd_dim ** -0.5

    def body(q_ref, k_ref, v_ref, o_ref):
        for hd in range(MEM_HEADS):
            ln = slice(hd * hd_dim, (hd + 1) * hd_dim)
            sc = lax.dot_general(q_ref[:, ln], k_ref[:, ln], _DIMS["nt"], preferred_element_type=F32) * scale
            p = jnp.exp(sc - jnp.max(sc, axis=-1, keepdims=True))
            p = p / jnp.sum(p, axis=-1, keepdims=True)
            o_ref[:, ln] = jnp.dot(p.astype(MM), v_ref[:, ln], preferred_element_type=F32).astype(o_ref.dtype)

    qspec = pl.BlockSpec((tq, d), lambda b, i: (b * nq + i, 0))
    kspec = pl.BlockSpec((m, d), lambda b, i: (b, 0))
    return pl.pallas_call(
        body, name=name, grid=(n_ex, nq), in_specs=[qspec, kspec, kspec], out_specs=qspec,
        out_shape=_sds((t, d), ACT),
        compiler_params=_params(("parallel", "parallel"), 2 * tq * d * 2 + 2 * m * d * 2, 0, 6 * tq * m * 4),
    )(q, kk, vv)


def _xattn_bwd(name, q, kk, vv, do, n_ex):
    t, d = q.shape
    m = kk.shape[0] // n_ex
    s = t // n_ex
    tq = _tile(s, 512)
    nq = s // tq
    hd_dim = d // MEM_HEADS
    scale = hd_dim ** -0.5

    def body(q_ref, k_ref, v_ref, do_ref, dq_ref, dk_ref, dv_ref):
        i = pl.program_id(1)

        @pl.when(i == 0)
        def _():
            dk_ref[...] = jnp.zeros_like(dk_ref)
            dv_ref[...] = jnp.zeros_like(dv_ref)

        for hd in range(MEM_HEADS):
            ln = slice(hd * hd_dim, (hd + 1) * hd_dim)
            qv, kv, vv_, dov = q_ref[:, ln], k_ref[:, ln], v_ref[:, ln], do_ref[:, ln]
            sc = lax.dot_general(qv, kv, _DIMS["nt"], preferred_element_type=F32) * scale
            p = jnp.exp(sc - jnp.max(sc, axis=-1, keepdims=True))
            p = p / jnp.sum(p, axis=-1, keepdims=True)
            dp = lax.dot_general(dov, vv_, _DIMS["nt"], preferred_element_type=F32)
            ds = (p * (dp - jnp.sum(p * dp, axis=-1, keepdims=True)) * scale).astype(MM)
            dq_ref[:, ln] = jnp.dot(ds, kv, preferred_element_type=F32).astype(dq_ref.dtype)
            dk_ref[:, ln] += lax.dot_general(ds, qv, _DIMS["tn"], preferred_element_type=F32)
            dv_ref[:, ln] += lax.dot_general(p.astype(MM), dov, _DIMS["tn"], preferred_element_type=F32)

    qspec = pl.BlockSpec((tq, d), lambda b, i: (b * nq + i, 0))
    kspec = pl.BlockSpec((m, d), lambda b, i: (b, 0))
    return pl.pallas_call(
        body, name=name, grid=(n_ex, nq), in_specs=[qspec, kspec, kspec, qspec], out_specs=[qspec, kspec, kspec],
        out_shape=[_sds((t, d), ACT), _sds((n_ex * m, d), F32), _sds((n_ex * m, d), F32)],
        compiler_params=_params(("parallel", "arbitrary"), 3 * tq * d * 2 + 2 * m * d * 2 + 2 * m * d * 4, 0,
                                8 * tq * m * 4),
    )(q, kk, vv, do)


def _ffn_up(name, y, w1, w3):
    t, d = y.shape
    f = w1.shape[-1]
    tm = _tile(t, 1024)
    wspec = pl.BlockSpec((None, d, f), lambda i, j, k: (j, 0, 0))
    hspec = pl.BlockSpec((None, tm, f), lambda i, j, k: (j, i, 0))

    def epi(vals, _):
        h1, h3 = vals
        return [h1, h3, h1 * _sigmoid(h1) * h3]

    return _mm(name, "nn", (t // tm, N_CHIPS, 1), y, pl.BlockSpec((tm, d), lambda i, j, k: (i, 0)),
               [w1, w3], [wspec, wspec], [_sds((N_CHIPS, t, f), ACT)] * 3, [hspec] * 3, (tm, f), epilogue=epi)


def _ffn_down_bwd(name, dr, w2, h1, h3):
    t, d = dr.shape
    f = w2.shape[-2]
    tm = _tile(t, 1024)
    hspec = pl.BlockSpec((None, tm, f), lambda i, j, k: (j, i, 0))

    def epi(vals, ex):
        dg, = vals
        h1v, h3v = ex
        sg = _sigmoid(h1v)
        return [dg * h3v * (sg * (1.0 + h1v * (1.0 - sg))), dg * h1v * sg]

    return _mm(name, "nt", (t // tm, N_CHIPS, 1), dr, pl.BlockSpec((tm, d), lambda i, j, k: (i, 0)),
               [w2], [pl.BlockSpec((None, f, d), lambda i, j, k: (j, 0, 0))],
               [_sds((N_CHIPS, t, f), ACT)] * 2, [hspec] * 2, (tm, f),
               extras=[h1, h3], extra_specs=[hspec, hspec], epilogue=epi)


def _ffn_up_bwd(name, dh, w, extras, epilogue):
    _, t, f = dh.shape
    d = w.shape[-2]
    tm, tn = _tile(t, 512), _tile(d, 1024)
    ospec = pl.BlockSpec((tm, tn), lambda i, j, k: (i, j))
    return _mm(name, "nt", (t // tm, d // tn, 1), dh, pl.BlockSpec((N_CHIPS, tm, f), lambda i, j, k: (0, i, 0)),
               [w], [pl.BlockSpec((N_CHIPS, tn, f), lambda i, j, k: (0, j, 0))],
               [_sds((t, d), F32)], [ospec], (tm, tn),
               extras=extras, extra_specs=[ospec] * len(extras), epilogue=epilogue)[0]


def _ffn_wgrad_up(name, y, dh1, dh3):
    t, d = y.shape
    f = dh1.shape[-1]
    tm, tk = _tile(d, 1024), _tile(t, 1024)
    hspec = pl.BlockSpec((None, tk, f), lambda i, j, k: (j, k, 0))
    ospec = pl.BlockSpec((None, tm, f), lambda i, j, k: (j, i, 0))
    return _mm(name, "tn", (d // tm, N_CHIPS, t // tk), y, pl.BlockSpec((tk, tm), lambda i, j, k: (k, i)),
               [dh1, dh3], [hspec, hspec], [_sds((N_CHIPS, d, f), MM)] * 2, [ospec, ospec], (tm, f))


def _ffn_wgrad_down(name, g, dr):
    _, t, f = g.shape
    d = dr.shape[1]
    tn, tk = _tile(d, 1024), _tile(t, 1024)
    return _mm(name, "tn", (N_CHIPS, d // tn, t // tk), g, pl.BlockSpec((None, tk, f), lambda i, j, k: (i, k, 0)),
               [dr], [pl.BlockSpec((tk, tn), lambda i, j, k: (k, j))],
               [_sds((N_CHIPS, f, d), MM)], [pl.BlockSpec((None, f, tn), lambda i, j, k: (i, 0, j))], (f, tn))[0]


def _proj_cols(name, y, w, out_dtype):
    t, kdim = y.shape
    wd = w.shape[-1]
    tn = _tile(wd, 512)
    per = wd // tn
    tm = _tile(t, 1024)
    return _mm(name, "nn", (t // tm, N_CHIPS * per, 1), y, pl.BlockSpec((tm, kdim), lambda i, j, k: (i, 0)),
               [w], [pl.BlockSpec((None, kdim, tn), lambda i, j, k: (j // per, 0, j % per))],
               [_sds((t, N_CHIPS * wd), out_dtype)], [pl.BlockSpec((tm, tn), lambda i, j, k: (i, j))], (tm, tn))[0]


def _proj_cols_bwd(name, dh, w, extras, epilogue):
    t = dh.shape[0]
    kdim, wd = w.shape[-2], w.shape[-1]
    tm, tn = _tile(t, 1024), _tile(kdim, 1024)
    ospec = pl.BlockSpec((tm, tn), lambda i, j, k: (i, j))
    return _mm(name, "nt", (t // tm, kdim // tn, N_CHIPS), dh, pl.BlockSpec((tm, wd), lambda i, j, k: (i, k)),
               [w], [pl.BlockSpec((None, tn, wd), lambda i, j, k: (k, j, 0))],
               [_sds((t, kdim), F32)], [ospec], (tm, tn),
               extras=extras, extra_specs=[ospec] * len(extras), epilogue=epilogue)[0]


def _proj_cols_wgrad(name, y, dh):
    t, kdim = y.shape
    wd = dh.shape[1] // N_CHIPS
    tm, tk = _tile(kdim, 1024), _tile(t, 1024)
    return _mm(name, "tn", (kdim // tm, N_CHIPS, t // tk), y, pl.BlockSpec((tk, tm), lambda i, j, k: (k, i)),
               [dh], [pl.BlockSpec((tk, wd), lambda i, j, k: (k, j))],
               [_sds((N_CHIPS, kdim, wd), MM)], [pl.BlockSpec((None, tm, wd), lambda i, j, k: (j, i, 0))], (tm, wd))[0]


def _coords():
    return lax.axis_index("x"), lax.axis_index("y"), lax.axis_index("c")


def _chip_peers(x, y):
    return [(1 - x, y), (x, 1 - y), (1 - x, 1 - y)]


_ANY = pl.BlockSpec(memory_space=pl.ANY)


def _half(ref, c):
    h = ref.shape[0] // 2
    return ref.at[pl.ds(c * h, h)]


def _gather_two_level(srcs, lands):
    n = len(srcs)

    def body(*refs):
        ins, lz = refs[:n], refs[2 * n:3 * n]
        ici_send, ici_recv, d2d_send, d2d_recv = refs[3 * n:]
        x, y, c = _coords()
        me = 2 * x + y
        peers = _chip_peers(x, y)
        ici, d2d = [], []
        for t in range(n):
            for j, chip in enumerate(peers):
                k = 3 * t + j
                ici.append(pltpu.make_async_remote_copy(
                    src_ref=_half(ins[t], c), dst_ref=_half(lz[t].at[me], c), send_sem=ici_send.at[k], recv_sem=ici_recv.at[k],
                    device_id=(*chip, c), device_id_type=MESH))
                got = _half(lz[t].at[2 * chip[0] + chip[1]], c)
                d2d.append(pltpu.make_async_remote_copy(
                    src_ref=got, dst_ref=got, send_sem=d2d_send.at[k], recv_sem=d2d_recv.at[k],
                    device_id=(x, y, 1 - c), device_id_type=MESH))
        for cp in ici:
            cp.start()
        for cp, fw in zip(ici, d2d):
            cp.wait_recv()
            fw.start()
        for cp, fw in zip(ici, d2d):
            cp.wait_send()
            fw.wait_send()
            fw.wait_recv()

    res = pl.pallas_call(
        body, name="gather_weights", in_specs=[_ANY] * (2 * n), out_specs=[_ANY] * n,
        out_shape=[_sds(a.shape, a.dtype) for a in lands], input_output_aliases={n + i: i for i in range(n)},
        scratch_shapes=[pltpu.SemaphoreType.DMA((3 * n,))] * 4,
    )(*srcs, *lands)
    return list(res)


def _scatter_chips(parts):
    n = len(parts)

    def body(*refs):
        ins, outs = refs[:n], refs[n:2 * n]
        send_sems, recv_sems = refs[2 * n:]
        x, y, c = _coords()
        copies = []
        for t in range(n):
            for j, chip in enumerate(_chip_peers(x, y)):
                copies.append(pltpu.make_async_remote_copy(
                    src_ref=ins[t].at[2 * chip[0] + chip[1]], dst_ref=outs[t].at[j], send_sem=send_sems.at[3 * t + j],
                    recv_sem=recv_sems.at[3 * t + j], device_id=(*chip, c), device_id_type=MESH))
        for cp in copies:
            cp.start()
        for cp in copies:
            cp.wait()

    return pl.pallas_call(
        body, name="scatter_grads", in_specs=[_ANY] * n, out_specs=[_ANY] * n,
        out_shape=[_sds((3, *a.shape[1:]), a.dtype) for a in parts],
        scratch_shapes=[pltpu.SemaphoreType.DMA((3 * n,))] * 2,
    )(*parts)


def _swap_halves(grads):
    n = len(grads)

    def body(*refs):
        ins, outs = refs[:n], refs[n:2 * n]
        send_sems, recv_sems = refs[2 * n:]
        x, y, c = _coords()
        copies = []
        for t in range(n):
            h = ins[t].shape[1] // 2
            copies.append(pltpu.make_async_remote_copy(
                src_ref=ins[t].at[:, pl.ds((1 - c) * h, h)], dst_ref=outs[t], send_sem=send_sems.at[t], recv_sem=recv_sems.at[t],
                device_id=(x, y, 1 - c), device_id_type=MESH))
        for cp in copies:
            cp.start()
        for cp in copies:
            cp.wait()

    return pl.pallas_call(
        body, name="swap_grad_halves", in_specs=[_ANY] * n, out_specs=[_ANY] * n,
        out_shape=[_sds((a.shape[0], a.shape[1] // 2, a.shape[2]), a.dtype) for a in grads],
        scratch_shapes=[pltpu.SemaphoreType.DMA((n,))] * 2,
    )(*grads)


def _swap_sibling(arrs):
    n = len(arrs)

    def body(*refs):
        ins, outs = refs[:n], refs[n:2 * n]
        send_sems, recv_sems = refs[2 * n:]
        x, y, c = _coords()
        copies = []
        for t in range(n):
            cp = pltpu.make_async_remote_copy(src_ref=ins[t], dst_ref=outs[t], send_sem=send_sems.at[t],
                                              recv_sem=recv_sems.at[t], device_id=(x, y, 1 - c), device_id_type=MESH)
            cp.start()
            copies.append(cp)
        for cp in copies:
            cp.wait()

    return pl.pallas_call(
        body, name="swap_sibling", in_specs=[_ANY] * n, out_specs=[_ANY] * n,
        out_shape=[_sds(a.shape, a.dtype) for a in arrs],
        scratch_shapes=[pltpu.SemaphoreType.DMA((n,)), pltpu.SemaphoreType.DMA((n,))],
    )(*arrs)


def _gather_all(part):
    def body(in_ref, out_ref, send_sems, recv_sems, loc_sem):
        x, y, c = _coords()
        dst = out_ref.at[4 * x + 2 * y + c]
        copies = [pltpu.make_async_copy(in_ref, dst, loc_sem)]
        for r in range(1, N_DEV):
            fx, fy, fc = (r >> 2) & 1, (r >> 1) & 1, r & 1
            peer = (x ^ fx, y ^ fy, c ^ fc)
            copies.append(pltpu.make_async_remote_copy(src_ref=in_ref, dst_ref=dst, send_sem=send_sems.at[r - 1],
                                                       recv_sem=recv_sems.at[r - 1], device_id=peer, device_id_type=MESH))
        for cp in copies:
            cp.start()
        for cp in copies:
            cp.wait()

    return pl.pallas_call(
        body, name="gather_small_grads", in_specs=[_ANY], out_specs=_ANY,
        out_shape=_sds((N_DEV, *part.shape), part.dtype),
        scratch_shapes=[pltpu.SemaphoreType.DMA((N_DEV - 1,)), pltpu.SemaphoreType.DMA((N_DEV - 1,)), pltpu.SemaphoreType.DMA],
    )(part)


def _sum_chips(grad, recv, me):
    _, rr, cc = recv.shape
    tr = _tile(rr, 512)

    def body(me_ref, g_ref, r0_ref, r1_ref, r2_ref, o_ref):
        o_ref[...] = ((g_ref[...].astype(F32) + r0_ref[...].astype(F32)) + r1_ref[...].astype(F32)) + r2_ref[...].astype(F32)

    gspec = pl.BlockSpec((None, tr, cc), lambda r, m: (m[0], r, 0))
    rspecs = [pl.BlockSpec((None, tr, cc), functools.partial(lambda r, m, j: (j, r, 0), j=j)) for j in range(3)]
    return pl.pallas_call(
        body, name="sum_chip_grads",
        grid_spec=pltpu.PrefetchScalarGridSpec(
            num_scalar_prefetch=1, grid=(rr // tr,), in_specs=[gspec, *rspecs],
            out_specs=pl.BlockSpec((tr, cc), lambda r, m: (r, 0))),
        out_shape=_sds((rr, cc), F32),
        compiler_params=_params(("parallel",), 4 * tr * cc * 2 + tr * cc * 4, 0, 2 * tr * cc * 4),
    )(me, grad, recv, recv, recv)


def _sum_pair(grad, sib, core):
    k, h, cc = sib.shape
    tr = _tile(h, 512)
    nb = h // tr

    def body(c_ref, g_ref, a_ref, o_ref):
        o_ref[...] = (g_ref[...].astype(F32) + a_ref[...].astype(F32)).astype(o_ref.dtype)

    spec = pl.BlockSpec((None, tr, cc), lambda s, r, c: (s, r, 0))
    return pl.pallas_call(
        body, name="sum_core_grads",
        grid_spec=pltpu.PrefetchScalarGridSpec(
            num_scalar_prefetch=1, grid=(k, nb),
            in_specs=[pl.BlockSpec((None, tr, cc), lambda s, r, c: (s, c[0] * nb + r, 0)), spec], out_specs=spec),
        out_shape=_sds(sib.shape, sib.dtype),
        compiler_params=_params(("parallel", "parallel"), 3 * tr * cc * 2, 0, 2 * tr * cc * 4),
    )(core, grad, sib)


def _adamw_math(w, g, m, v):
    m = ADAM_B1 * m + (1.0 - ADAM_B1) * g
    v = ADAM_B2 * v + (1.0 - ADAM_B2) * (g * g)
    m_hat = m / (1.0 - ADAM_B1 ** ADAM_STEP)
    v_hat = v / (1.0 - ADAM_B2 ** ADAM_STEP)
    delta = -ADAM_LR * (m_hat / (jnp.sqrt(v_hat) + ADAM_EPS) + ADAM_WD * w)
    return delta, m, v


def _adamw(name, parts, w, m, v):
    ll, rr, cc = w.shape
    tr = _tile(rr, 256)
    npart = len(parts)

    def body(*refs):
        p_refs = refs[:npart]
        w_ref, m_ref, v_ref, g_ref, d_ref, nm_ref, nv_ref = refs[npart:]
        g = p_refs[0][...]
        for p in p_refs[1:]:
            g = g + p[...]
        d, nm, nv = _adamw_math(w_ref[...], g, m_ref[...], v_ref[...])
        g_ref[...] = g
        d_ref[...] = d
        nm_ref[...] = nm
        nv_ref[...] = nv

    spec = pl.BlockSpec((None, tr, cc), lambda l, r: (l, r, 0))
    out = _sds((ll, rr, cc), F32)
    return pl.pallas_call(
        body, name=name, grid=(ll, rr // tr), in_specs=[spec] * (npart + 3), out_specs=[spec] * 4, out_shape=[out] * 4,
        compiler_params=_params(("parallel", "parallel"), (npart + 7) * tr * cc * 4, 0, 4 * tr * cc * 4),
    )(*parts, w, m, v)


def _adamw_halves(name, s_own, s_sib, core, w, m, v):
    ll, rr, cc = w.shape
    h = rr // 2
    tr = _tile(h, 256)
    nb = h // tr

    def body(c_ref, own_ref, sib_ref, w_ref, m_ref, v_ref, g_ref, d_ref, nm_ref, nv_ref):
        g = jnp.where(pl.program_id(1) == c_ref[0], own_ref[...], sib_ref[...])
        d, nm, nv = _adamw_math(w_ref[...], g, m_ref[...], v_ref[...])
        g_ref[...] = g
        d_ref[...] = d
        nm_ref[...] = nm
        nv_ref[...] = nv

    half = pl.BlockSpec((None, tr, cc), lambda l, hf, r, c: (l, r, 0))
    full = pl.BlockSpec((None, tr, cc), lambda l, hf, r, c: (l, hf * nb + r, 0))
    out = _sds((ll, rr, cc), F32)
    return pl.pallas_call(
        body, name=name,
        grid_spec=pltpu.PrefetchScalarGridSpec(num_scalar_prefetch=1, grid=(ll, 2, nb), in_specs=[half, half, full, full, full],
                                               out_specs=[full] * 4),
        out_shape=[out] * 4,
        compiler_params=_params(("parallel", "parallel", "parallel"), 9 * tr * cc * 4, 0, 4 * tr * cc * 4),
    )(core, s_own, s_sib, w, m, v)


def _sum_devices(allparts):
    _, rr, cc = allparts.shape

    def body(p_ref, o_ref):
        s = p_ref[0]
        for k in range(1, N_DEV):
            s = s + p_ref[k]
        o_ref[...] = s

    return pl.pallas_call(
        body, name="sum_small_grads", grid=(1,), in_specs=[pl.BlockSpec((N_DEV, rr, cc), lambda i: (0, 0, 0))],
        out_specs=pl.BlockSpec((rr, cc), lambda i: (0, 0)), out_shape=_sds((rr, cc), F32),
        compiler_params=_params(("arbitrary",), 9 * rr * cc * 4),
    )(allparts)


def _pack(arrs):
    flat = jnp.concatenate([a.reshape(-1).astype(F32) for a in arrs])
    n = flat.shape[0]
    total = -(-n // 1024) * 1024
    return jnp.pad(flat, (0, total - n)).reshape(total // 128, 128)


def _unpack(block, shapes):
    flat = block.reshape(-1)
    out, off = [], 0
    for sh in shapes:
        n = math.prod(sh)
        out.append(flat[off:off + n].reshape(sh))
        off += n
    return out


_BIG = ["w_in_ab", "w_out_ab", "w_qkv_c", "w_out_c", "mem_wq", "mem_wk", "mem_wv", "mem_wo", "ffn_w1", "ffn_w3", "ffn_w2"]
_ROW_SHARDED = ("w_out_ab", "w_out_c", "mem_wq", "mem_wk", "mem_wv", "mem_wo")
_SMALL_REPL = ["gmlp_ln_g", "gmlp_ln_b", "gmlp_w_s", "gmlp_b_s", "conv_b", "conv_gn_g", "conv_gn_b"]
_SMALL_SHARD = ["conv_w", "ln_g", "ln_b"]
_NAMES = ["w_in_ab", "gmlp_ln_g", "gmlp_ln_b", "gmlp_w_s", "gmlp_b_s", "conv_w", "conv_b", "conv_gn_g", "conv_gn_b",
          "w_out_ab", "w_qkv_c", "w_out_c", "mem_wq", "mem_wk", "mem_wv", "mem_wo", "ffn_w1", "ffn_w3", "ffn_w2",
          "ln_g", "ln_b"]


def _layer_weights(l):
    mixer = ["w_in_ab", "w_out_ab"] if l % 2 == 0 else ["w_qkv_c", "w_out_c"]
    return [(n, l // 2) for n in mixer] + [(n, l) for n in _BIG if n.startswith(("mem_", "ffn_"))]


def _natural(w):
    return w.reshape(-1, w.shape[-1])


def _local_step(x, mem, target, layer_w, small):
    n_ex, s, d = x.shape
    t = n_ex * s
    x2 = x.reshape(t, d)
    mem_a = mem.reshape(-1, d).astype(ACT)
    tgt = target.reshape(t, d)
    one = jnp.ones((1, d), F32)
    zero = jnp.zeros((1, d), F32)
    ln_g, ln_b = small["ln_g"], small["ln_b"]

    def vec(a):
        return a.reshape(1, -1)

    saved = []
    xh, gp, bp = x2, one, zero
    y_act = x2.astype(ACT)
    for l in range(DEPTH):
        wts = layer_w[l]
        sv = {"y0": y_act, "w": wts}
        if l % 2 == 0:
            e = l // 2
            h = _proj_cols(f"in_ab_{l}", y_act, wts["w_in_ab"], ACT)
            gl = (vec(small["gmlp_ln_g"][e]), vec(small["gmlp_ln_b"][e]), small["gmlp_w_s"][e],
                  small["gmlp_b_s"][e].reshape(4, CHUNK, 1))
            cl = (small["conv_w"][e], vec(small["conv_b"][e]), vec(small["conv_gn_g"][e]), vec(small["conv_gn_b"][e]))
            ya = _gmlp_fwd(f"gmlp_fwd_{l}", h, *gl)
            yb = _conv_fwd(f"conv_fwd_{l}", h, n_ex, *cl)
            yab = jnp.concatenate([ya, yb], axis=1)
            mixed, w_mix = yab, _natural(wts["w_out_ab"])
            sv.update(h=h, yab=yab, gl=gl, cl=cl)
        else:
            qkv = _proj_cols(f"qkv_{l}", y_act, wts["w_qkv_c"], ACT)
            att, ltot = _sb_fwd(f"sb_fwd_{l}", qkv, n_ex)
            att_a = att.astype(ACT)
            mixed, w_mix = att_a, _natural(wts["w_out_c"])
            sv.update(qkv=qkv, att=att_a, ltot=ltot)
        g1, b1 = vec(ln_g[l, 0]), vec(ln_b[l, 0])
        xh1, y1, rstd1 = _dense_ln(f"mix_out_ln1_{l}", mixed, w_mix, xh, gp, bp, g1, b1)
        q = _dense(f"mem_q_{l}", y1, _natural(wts["mem_wq"]), ACT)
        kk = _dense(f"mem_k_{l}", mem_a, _natural(wts["mem_wk"]), ACT)
        vv = _dense(f"mem_v_{l}", mem_a, _natural(wts["mem_wv"]), ACT)
        oc = _xattn_fwd(f"xattn_fwd_{l}", q, kk, vv, n_ex)
        g2, b2 = vec(ln_g[l, 1]), vec(ln_b[l, 1])
        xh2, y2, rstd2 = _dense_ln(f"mem_o_ln2_{l}", oc, _natural(wts["mem_wo"]), xh1, g1, b1, g2, b2)
        h1, h3, gact = _ffn_up(f"ffn_up_{l}", y2, wts["ffn_w1"], wts["ffn_w3"])
        g3, b3 = vec(ln_g[l, 2]), vec(ln_b[l, 2])
        xh3, y3, rstd3 = _ffn_down_ln(f"ffn_down_ln3_{l}", gact, wts["ffn_w2"], xh2, g2, b2, g3, b3)
        sv.update(xh1=xh1, y1=y1, rstd1=rstd1, g1=g1, q=q, kk=kk, vv=vv, oc=oc, xh2=xh2, y2=y2, rstd2=rstd2, g2=g2,
                  h1=h1, h3=h3, gact=gact, xh3=xh3, rstd3=rstd3, g3=g3)
        saved.append(sv)
        xh, gp, bp, y_act = xh3, g3, b3, y3

    dy, loss = _loss_head(xh, gp, bp, tgt)

    sm = {n: [None] * (DEPTH // 2) for n in _SMALL_REPL + ["conv_w"]}
    d_ln_g = [[None] * 3 for _ in range(DEPTH)]
    d_ln_b = [[None] * 3 for _ in range(DEPTH)]

    def add_res(vals, ex):
        return [vals[0] + ALPHA * ex[0]]

    def add_res2(vals, ex):
        return [vals[0] + ex[0] + ALPHA * ex[1]]

    layer_g = [None] * DEPTH
    for l in reversed(range(DEPTH)):
        sv = saved[l]
        wts = sv["w"]
        big = {}
        dr3, dr3a, d_ln_g[l][2], d_ln_b[l][2] = _ln_bwd(f"ln3_bwd_{l}", dy, sv["xh3"], sv["rstd3"], sv["g3"])
        big["ffn_w2"] = _ffn_wgrad_down(f"ffn_w2_grad_{l}", sv["gact"], dr3a)
        dh1, dh3 = _ffn_down_bwd(f"ffn_down_bwd_{l}", dr3a, wts["ffn_w2"], sv["h1"], sv["h3"])
        big["ffn_w1"], big["ffn_w3"] = _ffn_wgrad_up(f"ffn_w13_grad_{l}", sv["y2"], dh1, dh3)
        part = _ffn_up_bwd(f"ffn_up_bwd1_{l}", dh1, wts["ffn_w1"], [], None)
        dy = _ffn_up_bwd(f"ffn_up_bwd3_{l}", dh3, wts["ffn_w3"], [part, dr3], add_res2)
        dr2, dr2a, d_ln_g[l][1], d_ln_b[l][1] = _ln_bwd(f"ln2_bwd_{l}", dy, sv["xh2"], sv["rstd2"], sv["g2"])
        big["mem_wo"] = _dense_tn(f"mem_wo_grad_{l}", sv["oc"], dr2a)
        doc = _dense_nt(f"mem_o_bwd_{l}", dr2a, _natural(wts["mem_wo"]), ACT)
        dq, dkk, dvv = _xattn_bwd(f"xattn_bwd_{l}", sv["q"], sv["kk"], sv["vv"], doc, n_ex)
        big["mem_wq"] = _dense_tn(f"mem_wq_grad_{l}", sv["y1"], dq)
        big["mem_wk"] = _dense_tn(f"mem_wk_grad_{l}", mem_a, dkk)
        big["mem_wv"] = _dense_tn(f"mem_wv_grad_{l}", mem_a, dvv)
        dy = _dense_nt(f"mem_q_bwd_{l}", dq, _natural(wts["mem_wq"]), F32, extras=[dr2], epilogue=add_res)
        dr1, dr1a, d_ln_g[l][0], d_ln_b[l][0] = _ln_bwd(f"ln1_bwd_{l}", dy, sv["xh1"], sv["rstd1"], sv["g1"])
        if l % 2 == 0:
            e = l // 2
            big["w_out_ab"] = _dense_tn(f"out_ab_grad_{l}", sv["yab"], dr1a)
            dyab = _dense_nt(f"out_ab_bwd_{l}", dr1a, _natural(wts["w_out_ab"]), ACT)
            duv, dgg, dgb, dws, dbs = _gmlp_bwd(f"gmlp_bwd_{l}", sv["h"], dyab, *sv["gl"])
            da, dgt, dcw, dcb, dng, dnb = _conv_bwd(f"conv_bwd_{l}", sv["h"], dyab, n_ex, *sv["cl"])
            sm["gmlp_ln_g"][e], sm["gmlp_ln_b"][e] = dgg.reshape(-1), dgb.reshape(-1)
            sm["gmlp_w_s"][e], sm["gmlp_b_s"][e] = dws, dbs.reshape(4, CHUNK)
            sm["conv_w"][e], sm["conv_b"][e] = dcw, dcb.reshape(-1)
            sm["conv_gn_g"][e], sm["conv_gn_b"][e] = dng.reshape(-1), dnb.reshape(-1)
            dh = jnp.concatenate([duv, da, dgt], axis=1)
            big["w_in_ab"] = _proj_cols_wgrad(f"in_ab_grad_{l}", sv["y0"], dh)
            dy = _proj_cols_bwd(f"in_ab_bwd_{l}", dh, wts["w_in_ab"], [dr1], add_res)
        else:
            big["w_out_c"] = _dense_tn(f"out_c_grad_{l}", sv["att"], dr1a)
            datt = _dense_nt(f"out_c_bwd_{l}", dr1a, _natural(wts["w_out_c"]), ACT)
            dq_, dk_, dv_ = _sb_bwd(f"sb_bwd_{l}", sv["qkv"], datt, sv["ltot"], n_ex)
            dqkv = jnp.concatenate([dq_, dk_, dv_], axis=1)
            big["w_qkv_c"] = _proj_cols_wgrad(f"qkv_grad_{l}", sv["y0"], dqkv)
            dy = _proj_cols_bwd(f"qkv_bwd_{l}", dqkv, wts["w_qkv_c"], [dr1], add_res)
        for n in _ROW_SHARDED:
            if n in big:
                big[n] = big[n].reshape(N_CHIPS, -1, big[n].shape[-1])
        layer_g[l] = big

    grad_x = dy.reshape(n_ex, s, d)
    small_g = {n: jnp.stack(sm[n]) for n in sm}
    small_g["ln_g"] = jnp.stack([jnp.concatenate(r, axis=0) for r in d_ln_g])
    small_g["ln_b"] = jnp.stack([jnp.concatenate(r, axis=0) for r in d_ln_b])
    return loss, grad_x, layer_g, small_g


def kernel(x, mem, w_in_ab, gmlp_ln_g, gmlp_ln_b, gmlp_w_s, gmlp_b_s, conv_w, conv_b, conv_gn_g, conv_gn_b, w_out_ab, w_qkv_c, w_out_c, mem_wq, mem_wk, mem_wv, mem_wo, ffn_w1, ffn_w3, ffn_w2, ln_g, ln_b, loss_target, m_w_in_ab, m_gmlp_ln_g, m_gmlp_ln_b, m_gmlp_w_s, m_gmlp_b_s, m_conv_w, m_conv_b, m_conv_gn_g, m_conv_gn_b, m_w_out_ab, m_w_qkv_c, m_w_out_c, m_mem_wq, m_mem_wk, m_mem_wv, m_mem_wo, m_ffn_w1, m_ffn_w3, m_ffn_w2, m_ln_g, m_ln_b, v_w_in_ab, v_gmlp_ln_g, v_gmlp_ln_b, v_gmlp_w_s, v_gmlp_b_s, v_conv_w, v_conv_b, v_conv_gn_g, v_conv_gn_b, v_w_out_ab, v_w_qkv_c, v_w_out_c, v_mem_wq, v_mem_wk, v_mem_wv, v_mem_wo, v_ffn_w1, v_ffn_w3, v_ffn_w2, v_ln_g, v_ln_b):
    w = dict(w_in_ab=w_in_ab, gmlp_ln_g=gmlp_ln_g, gmlp_ln_b=gmlp_ln_b, gmlp_w_s=gmlp_w_s, gmlp_b_s=gmlp_b_s, conv_w=conv_w,
             conv_b=conv_b, conv_gn_g=conv_gn_g, conv_gn_b=conv_gn_b, w_out_ab=w_out_ab, w_qkv_c=w_qkv_c, w_out_c=w_out_c,
             mem_wq=mem_wq, mem_wk=mem_wk, mem_wv=mem_wv, mem_wo=mem_wo, ffn_w1=ffn_w1, ffn_w3=ffn_w3, ffn_w2=ffn_w2,
             ln_g=ln_g, ln_b=ln_b)
    mo = dict(w_in_ab=m_w_in_ab, gmlp_ln_g=m_gmlp_ln_g, gmlp_ln_b=m_gmlp_ln_b, gmlp_w_s=m_gmlp_w_s, gmlp_b_s=m_gmlp_b_s,
              conv_w=m_conv_w, conv_b=m_conv_b, conv_gn_g=m_conv_gn_g, conv_gn_b=m_conv_gn_b, w_out_ab=m_w_out_ab,
              w_qkv_c=m_w_qkv_c, w_out_c=m_w_out_c, mem_wq=m_mem_wq, mem_wk=m_mem_wk, mem_wv=m_mem_wv, mem_wo=m_mem_wo,
              ffn_w1=m_ffn_w1, ffn_w3=m_ffn_w3, ffn_w2=m_ffn_w2, ln_g=m_ln_g, ln_b=m_ln_b)
    vo = dict(w_in_ab=v_w_in_ab, gmlp_ln_g=v_gmlp_ln_g, gmlp_ln_b=v_gmlp_ln_b, gmlp_w_s=v_gmlp_w_s, gmlp_b_s=v_gmlp_b_s,
              conv_w=v_conv_w, conv_b=v_conv_b, conv_gn_g=v_conv_gn_g, conv_gn_b=v_conv_gn_b, w_out_ab=v_w_out_ab,
              w_qkv_c=v_w_qkv_c, w_out_c=v_w_out_c, mem_wq=v_mem_wq, mem_wk=v_mem_wk, mem_wv=v_mem_wv, mem_wo=v_mem_wo,
              ffn_w1=v_ffn_w1, ffn_w3=v_ffn_w3, ffn_w2=v_ffn_w2, ln_g=v_ln_g, ln_b=v_ln_b)
    me = (2 * lax.axis_index("x") + lax.axis_index("y")).astype(jnp.int32).reshape(1)

    per_layer = [_layer_weights(l) for l in range(DEPTH)]
    srcs = [w[n] for n in _SMALL_SHARD] + [w[n][i].astype(MM) for lw in per_layer for n, i in lw]
    lands = [lax.dynamic_update_index_in_dim(jnp.zeros((N_CHIPS, *s.shape), s.dtype), s[None], me[0], 0) for s in srcs]
    gathered = _gather_two_level(srcs, lands)
    cw_g, lg_g, lb_g = gathered[:3]
    small = {n: w[n] for n in _SMALL_REPL}
    small["conv_w"] = jnp.moveaxis(cw_g, 0, 2).reshape(cw_g.shape[1], CONV_WIDTH, -1)
    small["ln_g"] = jnp.moveaxis(lg_g, 0, 2).reshape(DEPTH, 3, -1)
    small["ln_b"] = jnp.moveaxis(lb_g, 0, 2).reshape(DEPTH, 3, -1)
    it = iter(gathered[3:])
    layer_w = [{n: next(it) for n, _ in lw} for lw in per_layer]

    loss, grad_x, layer_g, small_g = _local_step(x, mem, loss_target, layer_w, small)
    loss = lax.psum(loss, ("x", "y", "c"))

    core = lax.axis_index("c").astype(jnp.int32).reshape(1)
    grads_flat = [layer_g[l][n] for l in range(DEPTH) for n, _ in per_layer[l]]
    pair = [_sum_pair(g, a, core) for g, a in zip(grads_flat, _swap_halves(grads_flat))]
    recv = _scatter_chips(pair)
    per_name = {n: [None] * (DEPTH if n.startswith(("mem_", "ffn_")) else DEPTH // 2) for n in _BIG}
    flat_names = [ni for lw in per_layer for ni in lw]
    for (n, i), p, rc in zip(flat_names, pair, recv):
        per_name[n][i] = _sum_chips(p, rc, me)
    sums = [jnp.stack(per_name[n]) for n in _BIG]
    sib = _swap_sibling(sums)

    out = {}
    for n, s_own, s_sib in zip(_BIG, sums, sib):
        out[n] = _adamw_halves(f"adamw_{n}", s_own, s_sib, core, w[n], mo[n], vo[n])

    order = _SMALL_REPL + _SMALL_SHARD
    part = _pack([small_g[n] for n in order])
    total = _sum_devices(_gather_all(part))
    full = dict(zip(order, _unpack(total, [small_g[n].shape for n in order])))
    x_i, y_i = lax.axis_index("x"), lax.axis_index("y")
    chip = 2 * x_i + y_i
    loc = {n: full[n] for n in _SMALL_REPL}
    for n in _SMALL_SHARD:
        wd = w[n].shape[-1]
        loc[n] = lax.dynamic_slice_in_dim(full[n], chip * wd, wd, axis=full[n].ndim - 1)
    gp, wp, mp, vp = (_pack([src[n] for n in order]) for src in (loc, w, mo, vo))
    r128 = gp.shape[0]
    res = _adamw("adamw_small", [gp.reshape(1, r128, 128)], wp.reshape(1, r128, 128), mp.reshape(1, r128, 128),
                 vp.reshape(1, r128, 128))
    shapes = [w[n].shape for n in order]
    unp = [_unpack(r.reshape(r128, 128), shapes) for r in res]
    for i, n in enumerate(order):
        out[n] = tuple(u[i] for u in unp)

    grads = [out[n][0] for n in _NAMES]
    deltas = [out[n][1] for n in _NAMES]
    new_m = [out[n][2] for n in _NAMES]
    new_v = [out[n][3] for n in _NAMES]
    return (loss, grad_x, *grads, *deltas, *new_m, *new_v)
```

```python
import functools
import math

import jax
import jax.numpy as jnp
from jax import lax
from jax.experimental import pallas as pl
from jax.experimental.pallas import tpu as pltpu

F32 = jnp.float32
MM = jnp.bfloat16
ACT = jnp.bfloat16
MESH = pl.DeviceIdType.MESH

DEPTH = 4
CHUNK = 128
CONV_WIDTH = 31
HALO = 32
MEM_HEADS = 4
C_HEAD_DIM = 64
ALPHA = (2.0 * DEPTH) ** 0.25
LN_EPS = 1e-5
ADAM_LR, ADAM_B1, ADAM_B2, ADAM_EPS, ADAM_WD, ADAM_STEP = 0.001, 0.9, 0.999, 1e-08, 0.01, 10

VMEM_CAP_V7X = 64 * 1024 * 1024
VMEM_MAX_REQUEST = 56 * 1024 * 1024
N_CHIPS = 4
N_DEV = 8


def _tile(n, pref):
    if n <= pref:
        return n
    for t in range(pref - pref % 8, 7, -8):
        if n % t == 0:
            return t
    return n


def _nbytes(shape, dtype):
    return math.prod(1 if s is None else s for s in shape) * jnp.dtype(dtype).itemsize


def _vmem_limit(block_bytes, scratch_bytes=0, temp_bytes=0):
    est = 2 * block_bytes + scratch_bytes + temp_bytes
    return int(min(VMEM_MAX_REQUEST, max(16 * 1024 * 1024, est * 5 // 4)))


def _params(sem, block_bytes, scratch_bytes=0, temp_bytes=0):
    return pltpu.CompilerParams(dimension_semantics=sem,
                                vmem_limit_bytes=_vmem_limit(block_bytes, scratch_bytes, temp_bytes))


_DIMS = {"nn": (((1,), (0,)), ((), ())), "nt": (((1,), (1,)), ((), ())), "tn": (((0,), (0,)), ((), ()))}


def _mm(name, mode, grid, a, a_spec, bs, b_specs, outs, out_specs, acc_shape,
        extras=(), extra_specs=(), epilogue=None):
    nb, ne, no = len(bs), len(extras), len(outs)
    nk = grid[2]

    def body(*refs):
        a_ref = refs[0]
        b_refs = refs[1:1 + nb]
        e_refs = refs[1 + nb:1 + nb + ne]
        o_refs = refs[1 + nb + ne:1 + nb + ne + no]
        accs = refs[1 + nb + ne + no:]

        def finish(vals):
            if epilogue is not None:
                vals = epilogue(vals, [e[...].astype(F32) for e in e_refs])
            for o, v in zip(o_refs, vals):
                o[...] = v.astype(o.dtype)

        av = a_ref[...].astype(MM)

        def product(bv):
            if bv.ndim == 2:
                return lax.dot_general(av, bv, _DIMS[mode], preferred_element_type=F32)
            kw = bv.shape[2] if mode == "nt" else bv.shape[1]
            parts = [av[s] if av.ndim == 3 else av[:, s * kw:(s + 1) * kw] for s in range(bv.shape[0])]
            out = lax.dot_general(parts[0], bv[0], _DIMS[mode], preferred_element_type=F32)
            for s in range(1, bv.shape[0]):
                out = out + lax.dot_general(parts[s], bv[s], _DIMS[mode], preferred_element_type=F32)
            return out

        if nk == 1:
            finish([product(b_ref[...].astype(MM)) for b_ref in b_refs])
            return
        k = pl.program_id(2)

        @pl.when(k == 0)
        def _():
            for acc in accs:
                acc[...] = jnp.zeros_like(acc)

        for b_ref, acc in zip(b_refs, accs):
            acc[...] += product(b_ref[...].astype(MM))

        @pl.when(k == nk - 1)
        def _():
            finish([acc[...] for acc in accs])

    blocks = (_nbytes(a_spec.block_shape, a.dtype)
              + sum(_nbytes(s.block_shape, b.dtype) for s, b in zip(b_specs, bs))
              + sum(_nbytes(s.block_shape, e.dtype) for s, e in zip(extra_specs, extras))
              + sum(_nbytes(s.block_shape, o.dtype) for s, o in zip(out_specs, outs)))
    acc_bytes = nb * _nbytes(acc_shape, F32)
    res = pl.pallas_call(
        body, name=name, grid=grid,
        in_specs=[a_spec, *b_specs, *extra_specs], out_specs=list(out_specs), out_shape=list(outs),
        scratch_shapes=[pltpu.VMEM(acc_shape, F32)] * (nb if nk > 1 else 0),
        compiler_params=_params(("parallel", "parallel", "arbitrary"), blocks, acc_bytes if nk > 1 else 0, 4 * acc_bytes),
    )(a, *bs, *extras)
    return res


def _sds(shape, dtype):
    return jax.ShapeDtypeStruct(shape, dtype)


def _dense(name, a, w, out_dtype):
    t, kdim = a.shape
    n = w.shape[1]
    tm, tn, tk = _tile(t, 1024), _tile(n, 1024), _tile(kdim, 1024)
    grid = (t // tm, n // tn, kdim // tk)
    return _mm(name, "nn", grid, a, pl.BlockSpec((tm, tk), lambda i, j, k: (i, k)),
               [w], [pl.BlockSpec((tk, tn), lambda i, j, k: (k, j))],
               [_sds((t, n), out_dtype)], [pl.BlockSpec((tm, tn), lambda i, j, k: (i, j))], (tm, tn))[0]


def _dense_nt(name, a, w, out_dtype, extras=(), epilogue=None):
    t, n = a.shape
    kout = w.shape[0]
    tm, tn, tk = _tile(t, 1024), _tile(kout, 1024), _tile(n, 1024)
    grid = (t // tm, kout // tn, n // tk)
    return _mm(name, "nt", grid, a, pl.BlockSpec((tm, tk), lambda i, j, k: (i, k)),
               [w], [pl.BlockSpec((tn, tk), lambda i, j, k: (j, k))],
               [_sds((t, kout), out_dtype)], [pl.BlockSpec((tm, tn), lambda i, j, k: (i, j))], (tm, tn),
               extras=extras, extra_specs=[pl.BlockSpec((tm, tn), lambda i, j, k: (i, j))] * len(extras),
               epilogue=epilogue)[0]


def _dense_tn(name, a, b, out_dtype=MM):
    t, m = a.shape
    n = b.shape[1]
    tm, tn, tk = _tile(m, 1024), _tile(n, 1024), _tile(t, 1024)
    grid = (m // tm, n // tn, t // tk)
    return _mm(name, "tn", grid, a, pl.BlockSpec((tk, tm), lambda i, j, k: (k, i)),
               [b], [pl.BlockSpec((tk, tn), lambda i, j, k: (k, j))],
               [_sds((m, n), out_dtype)], [pl.BlockSpec((tm, tn), lambda i, j, k: (i, j))], (tm, tn))[0]


_INV_SQRT2 = 0.7071067811865476
_INV_SQRT_2PI = 0.3989422804014327


def _gelu(x):
    return 0.5 * x * (1.0 + lax.erf(x * _INV_SQRT2))


def _gelu_grad(x):
    return 0.5 * (1.0 + lax.erf(x * _INV_SQRT2)) + x * jnp.exp(-0.5 * x * x) * _INV_SQRT_2PI


def _sigmoid(x):
    return 1.0 / (1.0 + jnp.exp(-x))


def _norm_stats(x):
    mu = jnp.mean(x, axis=-1, keepdims=True)
    xc = x - mu
    var = jnp.mean(xc * xc, axis=-1, keepdims=True)
    rstd = lax.rsqrt(var + LN_EPS)
    return xc * rstd, rstd


def _norm_bwd(dy_g, xh, rstd):
    m1 = jnp.mean(dy_g, axis=-1, keepdims=True)
    m2 = jnp.mean(dy_g * xh, axis=-1, keepdims=True)
    return rstd * (dy_g - m1 - xh * m2)


def _rows8(x):
    r, c = x.shape
    return jnp.sum(x.reshape(r // 8, 8, c), axis=0)


def _ln_epilogue(vals, ex):
    xp, gp, bp, g, b = ex
    xh, rstd = _norm_stats(ALPHA * (xp * gp + bp) + vals[0])
    return [xh, xh * g + b, rstd]


def _proj_ln(name, a, a_spec_of, w, w_spec, tm_pref, xh_prev, g_prev, b_prev, g, b):
    t, d = xh_prev.shape
    tm = _tile(t, tm_pref)
    row = pl.BlockSpec((tm, d), lambda i, j, k: (i, 0))
    vec = pl.BlockSpec((1, d), lambda i, j, k: (0, 0))
    return _mm(name, "nn", (t // tm, 1, 1), a, a_spec_of(tm), [w], [w_spec],
               [_sds((t, d), F32), _sds((t, d), ACT), _sds((t, 1), F32)],
               [row, row, pl.BlockSpec((tm, 1), lambda i, j, k: (i, 0))], (tm, d),
               extras=[xh_prev, g_prev, b_prev, g, b], extra_specs=[row, vec, vec, vec, vec], epilogue=_ln_epilogue)


def _dense_ln(name, a, w, *ln_args):
    kdim, d = w.shape
    return _proj_ln(name, a, lambda tm: pl.BlockSpec((tm, kdim), lambda i, j, k: (i, 0)), w,
                    pl.BlockSpec((kdim, d), lambda i, j, k: (0, 0)), 1024, *ln_args)


def _ffn_down_ln(name, gact, w2, *ln_args):
    f, d = w2.shape[-2:]
    return _proj_ln(name, gact, lambda tm: pl.BlockSpec((N_CHIPS, tm, f), lambda i, j, k: (0, i, 0)), w2,
                    pl.BlockSpec((N_CHIPS, f, d), lambda i, j, k: (0, 0, 0)), 512, *ln_args)


def _ln_bwd(name, dy, xh, rstd, g):
    t, d = dy.shape
    tm = _tile(t, 512)
    n = t // tm

    def body(dy_ref, xh_ref, rstd_ref, g_ref, dr_ref, dra_ref, dg_ref, db_ref, dg_acc, db_acc):
        i = pl.program_id(0)

        @pl.when(i == 0)
        def _():
            dg_acc[...] = jnp.zeros_like(dg_acc)
            db_acc[...] = jnp.zeros_like(db_acc)

        dyv = dy_ref[...]
        xhv = xh_ref[...]
        dr = _norm_bwd(dyv * g_ref[...], xhv, rstd_ref[...])
        dr_ref[...] = dr
        dra_ref[...] = dr.astype(dra_ref.dtype)
        dg_acc[...] += _rows8(dyv * xhv)
        db_acc[...] += _rows8(dyv)

        @pl.when(i == n - 1)
        def _():
            dg_ref[...] = jnp.sum(dg_acc[...], axis=0, keepdims=True)
            db_ref[...] = jnp.sum(db_acc[...], axis=0, keepdims=True)

    row = pl.BlockSpec((tm, d), lambda i: (i, 0))
    vec = pl.BlockSpec((1, d), lambda i: (0, 0))
    return pl.pallas_call(
        body, name=name, grid=(n,),
        in_specs=[row, row, pl.BlockSpec((tm, 1), lambda i: (i, 0)), vec],
        out_specs=[row, row, vec, vec],
        out_shape=[_sds((t, d), F32), _sds((t, d), ACT), _sds((1, d), F32), _sds((1, d), F32)],
        scratch_shapes=[pltpu.VMEM((8, d), F32), pltpu.VMEM((8, d), F32)],
        compiler_params=_params(("arbitrary",), 4 * tm * d * 4, 0, 4 * tm * d * 4),
    )(dy, xh, rstd, g)


def _loss_head(xh, g, b, target):
    t, d = xh.shape
    tm = _tile(t, 512)
    n = t // tm

    def body(xh_ref, g_ref, b_ref, tg_ref, dy_ref, loss_ref, acc):
        i = pl.program_id(0)

        @pl.when(i == 0)
        def _():
            acc[...] = jnp.zeros_like(acc)

        err = xh_ref[...] * g_ref[...] + b_ref[...] - tg_ref[...]
        dy_ref[...] = err * (1.0 / d)
        acc[...] += _rows8(err * err)

        @pl.when(i == n - 1)
        def _():
            s = jnp.sum(jnp.sum(acc[...], axis=0, keepdims=True), axis=1, keepdims=True)
            loss_ref[...] = jnp.broadcast_to(s * (0.5 / d), loss_ref.shape)

    row = pl.BlockSpec((tm, d), lambda i: (i, 0))
    vec = pl.BlockSpec((1, d), lambda i: (0, 0))
    dy, loss = pl.pallas_call(
        body, name="loss_head", grid=(n,),
        in_specs=[row, vec, vec, row],
        out_specs=[row, pl.BlockSpec((8, 128), lambda i: (0, 0))],
        out_shape=[_sds((t, d), F32), _sds((8, 128), F32)],
        scratch_shapes=[pltpu.VMEM((8, d), F32)],
        compiler_params=_params(("arbitrary",), 3 * tm * d * 4, 0, 2 * tm * d * 4),
    )(xh, g, b, target)
    return dy, loss[0, 0]


def _causal_w(w):
    r = lax.broadcasted_iota(jnp.int32, w.shape, 0)
    c = lax.broadcasted_iota(jnp.int32, w.shape, 1)
    return jnp.where(r >= c, w, 0.0)


def _gmlp_fwd(name, h, ln_g, ln_b, w_s, b_s_col):
    t = h.shape[0]
    tt = _tile(t, 256)
    wd = 4 * CHUNK

    def body(u_ref, v_ref, g_ref, b_ref, w_ref, bs_ref, ya_ref):
        for gi in range(4):
            ln = slice(gi * CHUNK, (gi + 1) * CHUNK)
            u = _gelu(u_ref[:, ln].astype(F32))
            v = _gelu(v_ref[:, ln].astype(F32))
            xh, _ = _norm_stats(v)
            vg = (xh * g_ref[:, ln] + b_ref[:, ln]).astype(MM)
            w = _causal_w(w_ref[gi]).astype(MM)
            for c in range(tt // CHUNK):
                rs = slice(c * CHUNK, (c + 1) * CHUNK)
                mixed = jnp.dot(w, vg[rs], preferred_element_type=F32) + bs_ref[gi]
                ya_ref[rs, ln] = (u[rs] * mixed).astype(ya_ref.dtype)

    vec = pl.BlockSpec((1, wd), lambda i: (0, 0))
    return pl.pallas_call(
        body, name=name, grid=(t // tt,),
        in_specs=[pl.BlockSpec((tt, wd), lambda i: (i, 0)), pl.BlockSpec((tt, wd), lambda i: (i, 1)), vec, vec,
                  pl.BlockSpec((4, CHUNK, CHUNK), lambda i: (0, 0, 0)), pl.BlockSpec((4, CHUNK, 1), lambda i: (0, 0, 0))],
        out_specs=pl.BlockSpec((tt, wd), lambda i: (i, 0)),
        out_shape=_sds((t, wd), ACT),
        compiler_params=_params(("parallel",), 3 * tt * wd * 4, 0, 8 * tt * CHUNK * 4),
    )(h, h, ln_g, ln_b, w_s, b_s_col)


def _gmlp_bwd(name, h, dyab, ln_g, ln_b, w_s, b_s_col):
    t = h.shape[0]
    tt = _tile(t, 256)
    n = t // tt
    wd = 4 * CHUNK

    def body(u_ref, v_ref, dy_ref, g_ref, b_ref, w_ref, bs_ref, duv_ref, dg_ref, db_ref, dw_ref, dbs_ref,
             dg_acc, db_acc):
        i = pl.program_id(0)

        @pl.when(i == 0)
        def _():
            dg_acc[...] = jnp.zeros_like(dg_acc)
            db_acc[...] = jnp.zeros_like(db_acc)
            dw_ref[...] = jnp.zeros_like(dw_ref)
            dbs_ref[...] = jnp.zeros_like(dbs_ref)

        for gi in range(4):
            ln = slice(gi * CHUNK, (gi + 1) * CHUNK)
            upre = u_ref[:, ln].astype(F32)
            vpre = v_ref[:, ln].astype(F32)
            u = _gelu(upre)
            v = _gelu(vpre)
            xh, rstd = _norm_stats(v)
            gv = g_ref[:, ln]
            vg = (xh * gv + b_ref[:, ln]).astype(MM)
            w = _causal_w(w_ref[gi]).astype(MM)
            dya = dy_ref[:, ln].astype(F32)
            dmixed = dya * u
            dmm = dmixed.astype(MM)
            dvg_parts, mixed_parts = [], []
            dw = jnp.zeros((CHUNK, CHUNK), F32)
            dbs = jnp.zeros((CHUNK, 1), F32)
            for c in range(tt // CHUNK):
                rs = slice(c * CHUNK, (c + 1) * CHUNK)
                mixed_parts.append(jnp.dot(w, vg[rs], preferred_element_type=F32) + bs_ref[gi])
                dw = dw + lax.dot_general(dmm[rs], vg[rs], _DIMS["nt"], preferred_element_type=F32)
                dbs = dbs + jnp.sum(dmixed[rs], axis=1, keepdims=True)
                dvg_parts.append(lax.dot_general(w, dmm[rs], _DIMS["tn"], preferred_element_type=F32))
            mixed = jnp.concatenate(mixed_parts, axis=0)
            dvg = jnp.concatenate(dvg_parts, axis=0)
            dw_ref[gi] += _causal_w(dw)
            dbs_ref[gi] += dbs
            dg_acc[:, ln] += _rows8(dvg * xh)
            db_acc[:, ln] += _rows8(dvg)
            dv = _norm_bwd(dvg * gv, xh, rstd) * _gelu_grad(vpre)
            du = dya * mixed * _gelu_grad(upre)
            duv_ref[:, ln] = du.astype(duv_ref.dtype)
            duv_ref[:, wd + gi * CHUNK: wd + (gi + 1) * CHUNK] = dv.astype(duv_ref.dtype)

        @pl.when(i == n - 1)
        def _():
            dg_ref[...] = jnp.sum(dg_acc[...], axis=0, keepdims=True)
            db_ref[...] = jnp.sum(db_acc[...], axis=0, keepdims=True)

    vec = pl.BlockSpec((1, wd), lambda i: (0, 0))
    wspec = pl.BlockSpec((4, CHUNK, CHUNK), lambda i: (0, 0, 0))
    bspec = pl.BlockSpec((4, CHUNK, 1), lambda i: (0, 0, 0))
    return pl.pallas_call(
        body, name=name, grid=(n,),
        in_specs=[pl.BlockSpec((tt, wd), lambda i: (i, 0)), pl.BlockSpec((tt, wd), lambda i: (i, 1)),
                  pl.BlockSpec((tt, wd), lambda i: (i, 0)), vec, vec, wspec, bspec],
        out_specs=[pl.BlockSpec((tt, 2 * wd), lambda i: (i, 0)), vec, vec, wspec, bspec],
        out_shape=[_sds((t, 2 * wd), ACT), _sds((1, wd), F32), _sds((1, wd), F32),
                   _sds((4, CHUNK, CHUNK), F32), _sds((4, CHUNK, 1), F32)],
        scratch_shapes=[pltpu.VMEM((8, wd), F32), pltpu.VMEM((8, wd), F32)],
        compiler_params=_params(("arbitrary",), 5 * tt * wd * 4, 0, 16 * tt * CHUNK * 4),
    )(h, h, dyab, ln_g, ln_b, w_s, b_s_col)


_ROWS = 256


_SUBLANES = 8


def _shifted(win, sh_ref):
    n = win.shape[0]
    for r in range(_SUBLANES):
        sh_ref[r, 0:n - r, :] = win[r:, :]
    return sh_ref


def _rows_at(sh_ref, s):
    r = s % _SUBLANES
    return sh_ref[r, s - r:s - r + _ROWS, :]


def _conv_taps(shifted, cw, lo):
    acc = jnp.zeros((_ROWS, CHUNK), F32)
    for w in range(CONV_WIDTH):
        acc = acc + cw[w:w + 1, :] * _rows_at(shifted, lo(w))
    return acc


def _conv_fwd(name, h, n_ex, cw, cb, gg, gb):
    t = h.shape[0]
    s = t // n_ex
    nt = s // _ROWS

    def body(a_ref, gt_ref, cw_ref, cb_ref, gg_ref, gb_ref, yb_ref, hh, sh):
        hh[0:HALO, :] = jnp.zeros((HALO, CHUNK), F32)
        hh[HALO:HALO + s, :] = a_ref[...].astype(F32) * _sigmoid(gt_ref[...].astype(F32))
        cwv = cw_ref[...]

        def tile(i, carry):
            r0 = pl.multiple_of(i * _ROWS, _ROWS)
            win = _shifted(hh[pl.ds(r0, _ROWS + HALO), :], sh)
            c = _conv_taps(win, cwv, lambda w: w + HALO - (CONV_WIDTH - 1)) + cb_ref[...]
            xh, _ = _norm_stats(c)
            hg = xh * gg_ref[...] + gb_ref[...]
            yb_ref[pl.ds(r0, _ROWS), :] = (hg * _sigmoid(hg)).astype(yb_ref.dtype)
            return carry

        lax.fori_loop(0, nt, tile, 0)

    vec = pl.BlockSpec((1, CHUNK), lambda g, b: (0, g))
    return pl.pallas_call(
        body, name=name, grid=(4, n_ex),
        in_specs=[pl.BlockSpec((s, CHUNK), lambda g, b: (b, 8 + g)), pl.BlockSpec((s, CHUNK), lambda g, b: (b, 12 + g)),
                  pl.BlockSpec((CONV_WIDTH, CHUNK), lambda g, b: (0, g)), vec, vec, vec],
        out_specs=pl.BlockSpec((s, CHUNK), lambda g, b: (b, g)),
        out_shape=_sds((t, 4 * CHUNK), ACT),
        scratch_shapes=[pltpu.VMEM((s + HALO, CHUNK), F32), pltpu.VMEM((_SUBLANES, _ROWS + HALO, CHUNK), F32)],
        compiler_params=_params(("parallel", "parallel"), 3 * s * CHUNK * 4, (s + HALO) * CHUNK * 4, 4 * s * CHUNK * 4),
    )(h, h, cw, cb, gg, gb)


def _conv_bwd(name, h, dyab, n_ex, cw, cb, gg, gb):
    t = h.shape[0]
    s = t // n_ex
    nt = s // _ROWS

    def body(a_ref, gt_ref, dy_ref, cw_ref, cb_ref, gg_ref, gb_ref,
             da_ref, dgt_ref, dcw_ref, dcb_ref, dgg_ref, dgb_ref, hh, dcs, acc, sh):
        b = pl.program_id(1)

        @pl.when(b == 0)
        def _():
            dcw_ref[...] = jnp.zeros_like(dcw_ref)
            dcb_ref[...] = jnp.zeros_like(dcb_ref)
            dgg_ref[...] = jnp.zeros_like(dgg_ref)
            dgb_ref[...] = jnp.zeros_like(dgb_ref)

        hh[0:HALO, :] = jnp.zeros((HALO, CHUNK), F32)
        hh[HALO:HALO + s, :] = a_ref[...].astype(F32) * _sigmoid(gt_ref[...].astype(F32))
        dcs[s:s + HALO, :] = jnp.zeros((HALO, CHUNK), F32)
        acc[...] = jnp.zeros_like(acc)
        cwv = cw_ref[...]
        off = HALO - (CONV_WIDTH - 1)
        taps = 8 * CONV_WIDTH

        def tile1(i, carry):
            r0 = pl.multiple_of(i * _ROWS, _ROWS)
            win = _shifted(hh[pl.ds(r0, _ROWS + HALO), :], sh)
            c = _conv_taps(win, cwv, lambda w: w + off) + cb_ref[...]
            xh, rstd = _norm_stats(c)
            hg = xh * gg_ref[...] + gb_ref[...]
            sg = _sigmoid(hg)
            dhg = dy_ref[pl.ds(r0, _ROWS), :].astype(F32) * (sg * (1.0 + hg * (1.0 - sg)))
            acc[taps:taps + 8, :] += _rows8(dhg * xh)
            acc[taps + 8:taps + 16, :] += _rows8(dhg)
            dc = _norm_bwd(dhg * gg_ref[...], xh, rstd)
            dcs[pl.ds(r0, _ROWS), :] = dc
            acc[taps + 16:taps + 24, :] += _rows8(dc)
            for w in range(CONV_WIDTH):
                acc[8 * w:8 * w + 8, :] += _rows8(dc * _rows_at(win, w + off))
            return carry

        lax.fori_loop(0, nt, tile1, 0)

        def tile2(i, carry):
            r0 = pl.multiple_of(i * _ROWS, _ROWS)
            win = _shifted(dcs[pl.ds(r0, _ROWS + HALO), :], sh)
            dhh = _conv_taps(win, cwv, lambda w: CONV_WIDTH - 1 - w)
            av = a_ref[pl.ds(r0, _ROWS), :].astype(F32)
            sg = _sigmoid(gt_ref[pl.ds(r0, _ROWS), :].astype(F32))
            da_ref[pl.ds(r0, _ROWS), :] = (dhh * sg).astype(da_ref.dtype)
            dgt_ref[pl.ds(r0, _ROWS), :] = (dhh * av * sg * (1.0 - sg)).astype(dgt_ref.dtype)
            return carry

        lax.fori_loop(0, nt, tile2, 0)
        dcw_ref[...] += jnp.sum(acc[0:taps, :].reshape(CONV_WIDTH, 8, CHUNK), axis=1)
        dgg_ref[...] += jnp.sum(acc[taps:taps + 8, :], axis=0, keepdims=True)
        dgb_ref[...] += jnp.sum(acc[taps + 8:taps + 16, :], axis=0, keepdims=True)
        dcb_ref[...] += jnp.sum(acc[taps + 16:taps + 24, :], axis=0, keepdims=True)

    vec = pl.BlockSpec((1, CHUNK), lambda g, b: (0, g))
    tap = pl.BlockSpec((CONV_WIDTH, CHUNK), lambda g, b: (0, g))
    seq = pl.BlockSpec((s, CHUNK), lambda g, b: (b, g))
    return pl.pallas_call(
        body, name=name, grid=(4, n_ex),
        in_specs=[pl.BlockSpec((s, CHUNK), lambda g, b: (b, 8 + g)), pl.BlockSpec((s, CHUNK), lambda g, b: (b, 12 + g)),
                  pl.BlockSpec((s, CHUNK), lambda g, b: (b, 4 + g)), tap, vec, vec, vec],
        out_specs=[seq, seq, tap, vec, vec, vec],
        out_shape=[_sds((t, 4 * CHUNK), ACT), _sds((t, 4 * CHUNK), ACT), _sds((CONV_WIDTH, 4 * CHUNK), F32),
                   _sds((1, 4 * CHUNK), F32), _sds((1, 4 * CHUNK), F32), _sds((1, 4 * CHUNK), F32)],
        scratch_shapes=[pltpu.VMEM((s + HALO, CHUNK), F32), pltpu.VMEM((s + HALO, CHUNK), F32),
                        pltpu.VMEM((8 * CONV_WIDTH + 24, CHUNK), F32), pltpu.VMEM((_SUBLANES, _ROWS + HALO, CHUNK), F32)],
        compiler_params=_params(("parallel", "arbitrary"), 5 * s * CHUNK * 4, 2 * (s + HALO) * CHUNK * 4, 4 * s * CHUNK * 4),
    )(h, h, dyab, cw, cb, gg, gb)


_TQ = 256
_SB_DEAD = -110.0


def _tri(kind):
    r = lax.broadcasted_iota(jnp.int32, (_TQ, _TQ), 0)
    c = lax.broadcasted_iota(jnp.int32, (_TQ, _TQ), 1)
    m = {"gt": r > c, "le": r <= c, "lt": r < c}[kind]
    return jnp.where(m, 1.0, 0.0).astype(jnp.bfloat16)


def _split_dot(x, tri2):
    hi = x.astype(jnp.bfloat16)
    lo = (x - hi.astype(F32)).astype(jnp.bfloat16)
    return jnp.dot(jnp.concatenate([hi, lo], axis=1), tri2, preferred_element_type=F32)


def _neg_abs(x):
    bits = lax.bitcast_convert_type(x, jnp.uint32) | jnp.uint32(0x80000000)
    return lax.bitcast_convert_type(bits, F32)


def _log_not_beta(nz):
    return jnp.minimum(nz, 0.0) - jnp.log(1.0 + jnp.exp(_neg_abs(nz)))


def _sb_fwd(name, qkv, n_ex):
    t = qkv.shape[0]
    d = qkv.shape[1] // 3
    npair = d // CHUNK
    s = t // n_ex
    nq = s // _TQ
    neg_a = -(C_HEAD_DIM ** -0.5)

    def body(q_ref, k_ref, v_ref, o_ref, lt_ref, o_acc, c_acc):
        i = pl.program_id(2)
        first = lax.broadcasted_iota(jnp.int32, (_TQ, CHUNK), 1) < C_HEAD_DIM
        q2 = q_ref[...]
        zero = jnp.zeros_like(q2)
        qh = [jnp.where(first, q2, zero), jnp.where(first, zero, q2)]
        tri2 = jnp.concatenate([_tri("gt")] * 2, axis=0)
        o_acc[...] = jnp.zeros_like(o_acc)
        c_acc[...] = jnp.zeros_like(c_acc)

        def tiles(js, mask):
            work = [(a, hd) for a in range(len(js)) for hd in range(2)]
            rows = [pl.ds(pl.multiple_of(j * _TQ, _TQ), _TQ) for j in js]
            kts = [k_ref[r, :] for r in rows]
            vts = [v_ref[r, :] for r in rows]
            nzs = {w: lax.dot_general(qh[w[1]], kts[w[0]], _DIMS["nt"], preferred_element_type=F32) * neg_a for w in work}
            lns = {w: _log_not_beta(nzs[w]) for w in work}
            if mask is not None:
                lns = {w: jnp.where(mask, lns[w], 0.0) for w in work}
            locs = {w: _split_dot(lns[w], tri2) for w in work}
            laters = {}
            for hd in range(2):
                carry = c_acc[hd]
                for a in range(len(js)):
                    laters[a, hd] = carry + locs[a, hd]
                    carry = laters[a, hd][:, 0:1] + lns[a, hd][:, 0:1]
                c_acc[hd] = carry
            atts = {w: jnp.exp(lns[w] - nzs[w] + laters[w]) for w in work}
            if mask is not None:
                atts = {w: jnp.where(mask, atts[w], 0.0) for w in work}
            for hd in range(2):
                acc = o_acc[hd]
                for a in range(len(js)):
                    acc = acc + jnp.dot(atts[a, hd].astype(MM), vts[a], preferred_element_type=F32)
                o_acc[hd] = acc

        tiles([i], lax.broadcasted_iota(jnp.int32, (_TQ, _TQ), 1) < lax.broadcasted_iota(jnp.int32, (_TQ, _TQ), 0))

        def alive():
            return jnp.max(jnp.maximum(c_acc[0], c_acc[1])) >= _SB_DEAD

        def cond(st):
            return jnp.logical_and(st[0] >= 0, st[1])

        def step(st):
            tiles([st[0]], None)
            return st[0] - 1, alive()

        j_last, _ = lax.while_loop(cond, step, (i - 1, alive()))
        o_ref[...] = jnp.where(first, o_acc[0], o_acc[1])
        lt_ref[:, 0:1] = c_acc[0]
        lt_ref[:, 1:2] = c_acc[1]
        lt_ref[:, 2:3] = jnp.full((_TQ, 1), j_last + 1, jnp.int32).astype(F32)

    return pl.pallas_call(
        body, name=name, grid=(n_ex, npair, nq),
        in_specs=[pl.BlockSpec((_TQ, CHUNK), lambda b, p, i: (b * nq + i, p)),
                  pl.BlockSpec((s, CHUNK), lambda b, p, i: (b, npair + p)),
                  pl.BlockSpec((s, CHUNK), lambda b, p, i: (b, 2 * npair + p))],
        out_specs=[pl.BlockSpec((_TQ, CHUNK), lambda b, p, i: (b * nq + i, p)),
                   pl.BlockSpec((None, _TQ, 3), lambda b, p, i: (p, b * nq + i, 0))],
        out_shape=[_sds((t, d), F32), _sds((npair, t, 3), F32)],
        scratch_shapes=[pltpu.VMEM((2, _TQ, CHUNK), F32), pltpu.VMEM((2, _TQ, 1), F32)],
        compiler_params=_params(("parallel", "parallel", "arbitrary"), 2 * s * CHUNK * 2 + 4 * _TQ * CHUNK * 4,
                                4 * _TQ * CHUNK * 4, 24 * _TQ * _TQ * 4),
    )(qkv, qkv, qkv)


def _sb_bwd(name, qkv, do, ltot, n_ex):
    t = qkv.shape[0]
    d = qkv.shape[1] // 3
    npair = d // CHUNK
    s = t // n_ex
    nq = s // _TQ
    scale = C_HEAD_DIM ** -0.5
    neg_a = -scale

    def body(q_ref, k_ref, v_ref, do_ref, lt_ref, dq_ref, dk_ref, dv_ref, dq_acc, cp_acc, cg_acc, dk_acc, dv_acc):
        i = pl.program_id(2)

        @pl.when(i == 0)
        def _():
            dk_acc[...] = jnp.zeros_like(dk_acc)
            dv_acc[...] = jnp.zeros_like(dv_acc)

        first = lax.broadcasted_iota(jnp.int32, (_TQ, CHUNK), 1) < C_HEAD_DIM
        q2 = q_ref[...]
        do2 = do_ref[...]
        qs = (q2 * scale).astype(q2.dtype)
        zero = jnp.zeros_like(q2)
        qh = [jnp.where(first, q2, zero), jnp.where(first, zero, q2)]
        doh = [jnp.where(first, do2, zero), jnp.where(first, zero, do2)]
        lt = [lt_ref[:, 0:1], lt_ref[:, 1:2]]
        tri2_le = jnp.concatenate([_tri("le")] * 2, axis=0)
        tri_lt = _tri("lt").astype(MM)
        dq_acc[...] = jnp.zeros_like(dq_acc)
        cp_acc[...] = jnp.zeros_like(cp_acc)
        cg_acc[...] = jnp.zeros_like(cg_acc)

        def tiles(js, mask):
            na = len(js)
            work = [(a, hd) for a in range(na) for hd in range(2)]
            last = slice(_TQ - 1, _TQ)
            rows = [pl.ds(pl.multiple_of(j * _TQ, _TQ), _TQ) for j in js]
            kts = [k_ref[r, :] for r in rows]
            vts = [v_ref[r, :] for r in rows]
            ksc = [(kt * scale).astype(kt.dtype) for kt in kts]
            nzs = {w: lax.dot_general(qh[w[1]], kts[w[0]], _DIMS["nt"], preferred_element_type=F32) * neg_a for w in work}
            datts = {w: lax.dot_general(doh[w[1]], vts[w[0]], _DIMS["nt"], preferred_element_type=F32) for w in work}
            lns = {w: _log_not_beta(nzs[w]) for w in work}
            if mask is not None:
                lns = {w: jnp.where(mask, lns[w], 0.0) for w in work}
            pins = {w: _split_dot(lns[w], tri2_le) for w in work}
            lss = {w: lns[w] - nzs[w] for w in work}
            atts = {}
            for hd in range(2):
                cp = cp_acc[hd]
                for a in range(na):
                    atts[a, hd] = jnp.exp(lss[a, hd] + ((lt[hd] - cp) - pins[a, hd]))
                    cp = cp + pins[a, hd][:, last]
                cp_acc[hd] = cp
            if mask is not None:
                atts = {w: jnp.where(mask, atts[w], 0.0) for w in work}
            gs = {w: datts[w] * atts[w] for w in work}
            locg = {w: jnp.dot(gs[w].astype(MM), tri_lt, preferred_element_type=F32) for w in work}
            dzs = {}
            for hd in range(2):
                carry = cg_acc[hd]
                for a in range(na):
                    big = carry + locg[a, hd]
                    dzs[a, hd] = gs[a, hd] - (gs[a, hd] + big) * jnp.exp(lss[a, hd])
                    carry = big[:, last] + gs[a, hd][:, last]
                cg_acc[hd] = carry
            if mask is not None:
                dzs = {w: jnp.where(mask, dzs[w], 0.0) for w in work}
            dzs = {w: dzs[w].astype(MM) for w in work}
            attm = {w: atts[w].astype(MM) for w in work}
            for hd in range(2):
                acc = dq_acc[hd]
                for a in range(na):
                    acc = acc + jnp.dot(dzs[a, hd], ksc[a], preferred_element_type=F32)
                dq_acc[hd] = acc
            for a in range(na):
                dk0, dk1 = [lax.dot_general(dzs[a, hd], qs, _DIMS["tn"], preferred_element_type=F32) for hd in range(2)]
                dv0, dv1 = [lax.dot_general(attm[a, hd], do2, _DIMS["tn"], preferred_element_type=F32) for hd in range(2)]
                dk_acc[rows[a], :] += jnp.where(first, dk0, dk1)
                dv_acc[rows[a], :] += jnp.where(first, dv0, dv1)

        def single(j, carry):
            tiles([j], None)
            return carry

        j_first = jnp.clip(jnp.max(lt_ref[:, 2:3]).astype(jnp.int32), 0, i)
        lax.fori_loop(j_first, i, single, 0)
        tiles([i], lax.broadcasted_iota(jnp.int32, (_TQ, _TQ), 1) < lax.broadcasted_iota(jnp.int32, (_TQ, _TQ), 0))
        dq_ref[...] = jnp.where(first, dq_acc[0], dq_acc[1]).astype(dq_ref.dtype)

        @pl.when(i == nq - 1)
        def _():
            dk_ref[...] = dk_acc[...].astype(dk_ref.dtype)
            dv_ref[...] = dv_acc[...].astype(dv_ref.dtype)

    qspec = pl.BlockSpec((_TQ, CHUNK), lambda b, p, i: (b * nq + i, p))
    kv_out = pl.BlockSpec((s, CHUNK), lambda b, p, i: (b, p))
    return pl.pallas_call(
        body, name=name, grid=(n_ex, npair, nq),
        in_specs=[qspec, pl.BlockSpec((s, CHUNK), lambda b, p, i: (b, npair + p)),
                  pl.BlockSpec((s, CHUNK), lambda b, p, i: (b, 2 * npair + p)), qspec,
                  pl.BlockSpec((None, _TQ, 3), lambda b, p, i: (p, b * nq + i, 0))],
        out_specs=[qspec, kv_out, kv_out],
        out_shape=[_sds((t, d), ACT)] * 3,
        scratch_shapes=[pltpu.VMEM((2, _TQ, CHUNK), F32), pltpu.VMEM((2, _TQ, 1), F32), pltpu.VMEM((2, _TQ, 1), F32),
                        pltpu.VMEM((s, CHUNK), F32), pltpu.VMEM((s, CHUNK), F32)],
        compiler_params=_params(("parallel", "parallel", "arbitrary"), 4 * s * CHUNK * 2,
                                4 * _TQ * CHUNK * 4 + 2 * s * CHUNK * 4, 32 * _TQ * _TQ * 4),
    )(qkv, qkv, qkv, do, ltot)


def _xattn_fwd(name, q, kk, vv, n_ex):
    t, d = q.shape
    m = kk.shape[0] // n_ex
    s = t // n_ex
    tq = _tile(s, 512)
    nq = s // tq
    hd_dim = d // MEM_HEADS
    scale = hd_dim ** -0.5

    def body(q_ref, k_ref, v_ref, o_ref):
        for hd in range(MEM_HEADS):
            ln = slice(hd * hd_dim, (hd + 1) * hd_dim)
            sc = lax.dot_general(q_ref[:, ln], k_ref[:, ln], _DIMS["nt"], preferred_element_type=F32) * scale
            p = jnp.exp(sc - jnp.max(sc, axis=-1, keepdims=True))
            p = p / jnp.sum(p, axis=-1, keepdims=True)
            o_ref[:, ln] = jnp.dot(p.astype(MM), v_ref[:, ln], preferred_element_type=F32).astype(o_ref.dtype)

    qspec = pl.BlockSpec((tq, d), lambda b, i: (b * nq + i, 0))
    kspec = pl.BlockSpec((m, d), lambda b, i: (b, 0))
    return pl.pallas_call(
        body, name=name, grid=(n_ex, nq), in_specs=[qspec, kspec, kspec], out_specs=qspec,
        out_shape=_sds((t, d), ACT),
        compiler_params=_params(("parallel", "parallel"), 2 * tq * d * 2 + 2 * m * d * 2, 0, 6 * tq * m * 4),
    )(q, kk, vv)


def _xattn_bwd(name, q, kk, vv, do, n_ex):
    t, d = q.shape
    m = kk.shape[0] // n_ex
    s = t // n_ex
    tq = _tile(s, 512)
    nq = s // tq
    hd_dim = d // MEM_HEADS
    scale = hd_dim ** -0.5

    def body(q_ref, k_ref, v_ref, do_ref, dq_ref, dk_ref, dv_ref):
        i = pl.program_id(1)

        @pl.when(i == 0)
        def _():
            dk_ref[...] = jnp.zeros_like(dk_ref)
            dv_ref[...] = jnp.zeros_like(dv_ref)

        for hd in range(MEM_HEADS):
            ln = slice(hd * hd_dim, (hd + 1) * hd_dim)
            qv, kv, vv_, dov = q_ref[:, ln], k_ref[:, ln], v_ref[:, ln], do_ref[:, ln]
            sc = lax.dot_general(qv, kv, _DIMS["nt"], preferred_element_type=F32) * scale
            p = jnp.exp(sc - jnp.max(sc, axis=-1, keepdims=True))
            p = p / jnp.sum(p, axis=-1, keepdims=True)
            dp = lax.dot_general(dov, vv_, _DIMS["nt"], preferred_element_type=F32)
            ds = (p * (dp - jnp.sum(p * dp, axis=-1, keepdims=True)) * scale).astype(MM)
            dq_ref[:, ln] = jnp.dot(ds, kv, preferred_element_type=F32).astype(dq_ref.dtype)
            dk_ref[:, ln] += lax.dot_general(ds, qv, _DIMS["tn"], preferred_element_type=F32)
            dv_ref[:, ln] += lax.dot_general(p.astype(MM), dov, _DIMS["tn"], preferred_element_type=F32)

    qspec = pl.BlockSpec((tq, d), lambda b, i: (b * nq + i, 0))
    kspec = pl.BlockSpec((m, d), lambda b, i: (b, 0))
    return pl.pallas_call(
        body, name=name, grid=(n_ex, nq), in_specs=[qspec, kspec, kspec, qspec], out_specs=[qspec, kspec, kspec],
        out_shape=[_sds((t, d), ACT), _sds((n_ex * m, d), F32), _sds((n_ex * m, d), F32)],
        compiler_params=_params(("parallel", "arbitrary"), 3 * tq * d * 2 + 2 * m * d * 2 + 2 * m * d * 4, 0,
                                8 * tq * m * 4),
    )(q, kk, vv, do)


def _ffn_up(name, y, w1, w3):
    t, d = y.shape
    f = w1.shape[-1]
    tm = _tile(t, 1024)
    wspec = pl.BlockSpec((None, d, f), lambda i, j, k: (j, 0, 0))
    hspec = pl.BlockSpec((None, tm, f), lambda i, j, k: (j, i, 0))

    def epi(vals, _):
        h1, h3 = vals
        return [h1, h3, h1 * _sigmoid(h1) * h3]

    return _mm(name, "nn", (t // tm, N_CHIPS, 1), y, pl.BlockSpec((tm, d), lambda i, j, k: (i, 0)),
               [w1, w3], [wspec, wspec], [_sds((N_CHIPS, t, f), ACT)] * 3, [hspec] * 3, (tm, f), epilogue=epi)


def _ffn_down_bwd(name, dr, w2, h1, h3):
    t, d = dr.shape
    f = w2.shape[-2]
    tm = _tile(t, 1024)
    hspec = pl.BlockSpec((None, tm, f), lambda i, j, k: (j, i, 0))

    def epi(vals, ex):
        dg, = vals
        h1v, h3v = ex
        sg = _sigmoid(h1v)
        return [dg * h3v * (sg * (1.0 + h1v * (1.0 - sg))), dg * h1v * sg]

    return _mm(name, "nt", (t // tm, N_CHIPS, 1), dr, pl.BlockSpec((tm, d), lambda i, j, k: (i, 0)),
               [w2], [pl.BlockSpec((None, f, d), lambda i, j, k: (j, 0, 0))],
               [_sds((N_CHIPS, t, f), ACT)] * 2, [hspec] * 2, (tm, f),
               extras=[h1, h3], extra_specs=[hspec, hspec], epilogue=epi)


def _ffn_up_bwd(name, dh, w, extras, epilogue):
    _, t, f = dh.shape
    d = w.shape[-2]
    tm, tn = _tile(t, 512), _tile(d, 1024)
    ospec = pl.BlockSpec((tm, tn), lambda i, j, k: (i, j))
    return _mm(name, "nt", (t // tm, d // tn, 1), dh, pl.BlockSpec((N_CHIPS, tm, f), lambda i, j, k: (0, i, 0)),
               [w], [pl.BlockSpec((N_CHIPS, tn, f), lambda i, j, k: (0, j, 0))],
               [_sds((t, d), F32)], [ospec], (tm, tn),
               extras=extras, extra_specs=[ospec] * len(extras), epilogue=epilogue)[0]


def _ffn_wgrad_up(name, y, dh1, dh3):
    t, d = y.shape
    f = dh1.shape[-1]
    tm, tk = _tile(d, 1024), _tile(t, 1024)
    hspec = pl.BlockSpec((None, tk, f), lambda i, j, k: (j, k, 0))
    ospec = pl.BlockSpec((None, tm, f), lambda i, j, k: (j, i, 0))
    return _mm(name, "tn", (d // tm, N_CHIPS, t // tk), y, pl.BlockSpec((tk, tm), lambda i, j, k: (k, i)),
               [dh1, dh3], [hspec, hspec], [_sds((N_CHIPS, d, f), MM)] * 2, [ospec, ospec], (tm, f))


def _ffn_wgrad_down(name, g, dr):
    _, t, f = g.shape
    d = dr.shape[1]
    tn, tk = _tile(d, 1024), _tile(t, 1024)
    return _mm(name, "tn", (N_CHIPS, d // tn, t // tk), g, pl.BlockSpec((None, tk, f), lambda i, j, k: (i, k, 0)),
               [dr], [pl.BlockSpec((tk, tn), lambda i, j, k: (k, j))],
               [_sds((N_CHIPS, f, d), MM)], [pl.BlockSpec((None, f, tn), lambda i, j, k: (i, 0, j))], (f, tn))[0]


def _proj_cols(name, y, w, out_dtype):
    t, kdim = y.shape
    wd = w.shape[-1]
    tn = _tile(wd, 512)
    per = wd // tn
    tm = _tile(t, 1024)
    return _mm(name, "nn", (t // tm, N_CHIPS * per, 1), y, pl.BlockSpec((tm, kdim), lambda i, j, k: (i, 0)),
               [w], [pl.BlockSpec((None, kdim, tn), lambda i, j, k: (j // per, 0, j % per))],
               [_sds((t, N_CHIPS * wd), out_dtype)], [pl.BlockSpec((tm, tn), lambda i, j, k: (i, j))], (tm, tn))[0]


def _proj_cols_bwd(name, dh, w, extras, epilogue):
    t = dh.shape[0]
    kdim, wd = w.shape[-2], w.shape[-1]
    tm, tn = _tile(t, 512), _tile(kdim, 1024)
    ospec = pl.BlockSpec((tm, tn), lambda i, j, k: (i, j))
    return _mm(name, "nt", (t // tm, kdim // tn, 1), dh, pl.BlockSpec((tm, N_CHIPS * wd), lambda i, j, k: (i, 0)),
               [w], [pl.BlockSpec((N_CHIPS, tn, wd), lambda i, j, k: (0, j, 0))],
               [_sds((t, kdim), F32)], [ospec], (tm, tn),
               extras=extras, extra_specs=[ospec] * len(extras), epilogue=epilogue)[0]


def _proj_cols_wgrad(name, y, dh):
    t, kdim = y.shape
    wd = dh.shape[1] // N_CHIPS
    tm, tk = _tile(kdim, 1024), _tile(t, 1024)
    return _mm(name, "tn", (kdim // tm, N_CHIPS, t // tk), y, pl.BlockSpec((tk, tm), lambda i, j, k: (k, i)),
               [dh], [pl.BlockSpec((tk, wd), lambda i, j, k: (k, j))],
               [_sds((N_CHIPS, kdim, wd), MM)], [pl.BlockSpec((None, tm, wd), lambda i, j, k: (j, i, 0))], (tm, wd))[0]


def _coords():
    return lax.axis_index("x"), lax.axis_index("y"), lax.axis_index("c")


def _chip_peers(x, y):
    return [(1 - x, y), (x, 1 - y), (1 - x, 1 - y)]


_ANY = pl.BlockSpec(memory_space=pl.ANY)


def _half(ref, c):
    h = ref.shape[0] // 2
    return ref.at[pl.ds(c * h, h)]


def _gather_two_level(srcs, lands):
    n = len(srcs)

    def body(*refs):
        ins, lz = refs[:n], refs[2 * n:3 * n]
        ici_send, ici_recv, d2d_send, d2d_recv = refs[3 * n:]
        x, y, c = _coords()
        me = 2 * x + y
        peers = _chip_peers(x, y)
        ici, d2d = [], []
        for t in range(n):
            for j, chip in enumerate(peers):
                k = 3 * t + j
                ici.append(pltpu.make_async_remote_copy(
                    src_ref=_half(ins[t], c), dst_ref=_half(lz[t].at[me], c), send_sem=ici_send.at[k], recv_sem=ici_recv.at[k],
                    device_id=(*chip, c), device_id_type=MESH))
                got = _half(lz[t].at[2 * chip[0] + chip[1]], c)
                d2d.append(pltpu.make_async_remote_copy(
                    src_ref=got, dst_ref=got, send_sem=d2d_send.at[k], recv_sem=d2d_recv.at[k],
                    device_id=(x, y, 1 - c), device_id_type=MESH))
        for cp in ici:
            cp.start()
        for cp, fw in zip(ici, d2d):
            cp.wait_recv()
            fw.start()
        for cp, fw in zip(ici, d2d):
            cp.wait_send()
            fw.wait_send()
            fw.wait_recv()

    res = pl.pallas_call(
        body, name="gather_weights", in_specs=[_ANY] * (2 * n), out_specs=[_ANY] * n,
        out_shape=[_sds(a.shape, a.dtype) for a in lands], input_output_aliases={n + i: i for i in range(n)},
        scratch_shapes=[pltpu.SemaphoreType.DMA((3 * n,))] * 4,
    )(*srcs, *lands)
    return list(res)


def _scatter_chips(parts):
    n = len(parts)

    def body(*refs):
        ins, outs = refs[:n], refs[n:2 * n]
        send_sems, recv_sems = refs[2 * n:]
        x, y, c = _coords()
        copies = []
        for t in range(n):
            for j, chip in enumerate(_chip_peers(x, y)):
                copies.append(pltpu.make_async_remote_copy(
                    src_ref=ins[t].at[2 * chip[0] + chip[1]], dst_ref=outs[t].at[j], send_sem=send_sems.at[3 * t + j],
                    recv_sem=recv_sems.at[3 * t + j], device_id=(*chip, c), device_id_type=MESH))
        for cp in copies:
            cp.start()
        for cp in copies:
            cp.wait()

    return pl.pallas_call(
        body, name="scatter_grads", in_specs=[_ANY] * n, out_specs=[_ANY] * n,
        out_shape=[_sds((3, *a.shape[1:]), a.dtype) for a in parts],
        scratch_shapes=[pltpu.SemaphoreType.DMA((3 * n,))] * 2,
    )(*parts)


def _swap_halves(grads):
    n = len(grads)

    def body(*refs):
        ins, outs = refs[:n], refs[n:2 * n]
        send_sems, recv_sems = refs[2 * n:]
        x, y, c = _coords()
        copies = []
        for t in range(n):
            h = ins[t].shape[1] // 2
            copies.append(pltpu.make_async_remote_copy(
                src_ref=ins[t].at[:, pl.ds((1 - c) * h, h)], dst_ref=outs[t], send_sem=send_sems.at[t], recv_sem=recv_sems.at[t],
                device_id=(x, y, 1 - c), device_id_type=MESH))
        for cp in copies:
            cp.start()
        for cp in copies:
            cp.wait()

    return pl.pallas_call(
        body, name="swap_grad_halves", in_specs=[_ANY] * n, out_specs=[_ANY] * n,
        out_shape=[_sds((a.shape[0], a.shape[1] // 2, a.shape[2]), a.dtype) for a in grads],
        scratch_shapes=[pltpu.SemaphoreType.DMA((n,))] * 2,
    )(*grads)


def _swap_sibling(arrs):
    n = len(arrs)

    def body(*refs):
        ins, outs = refs[:n], refs[n:2 * n]
        send_sems, recv_sems = refs[2 * n:]
        x, y, c = _coords()
        copies = []
        for t in range(n):
            cp = pltpu.make_async_remote_copy(src_ref=ins[t], dst_ref=outs[t], send_sem=send_sems.at[t],
                                              recv_sem=recv_sems.at[t], device_id=(x, y, 1 - c), device_id_type=MESH)
            cp.start()
            copies.append(cp)
        for cp in copies:
            cp.wait()

    return pl.pallas_call(
        body, name="swap_sibling", in_specs=[_ANY] * n, out_specs=[_ANY] * n,
        out_shape=[_sds(a.shape, a.dtype) for a in arrs],
        scratch_shapes=[pltpu.SemaphoreType.DMA((n,)), pltpu.SemaphoreType.DMA((n,))],
    )(*arrs)


def _gather_all(part):
    def body(in_ref, out_ref, send_sems, recv_sems, loc_sem):
        x, y, c = _coords()
        dst = out_ref.at[4 * x + 2 * y + c]
        copies = [pltpu.make_async_copy(in_ref, dst, loc_sem)]
        for r in range(1, N_DEV):
            fx, fy, fc = (r >> 2) & 1, (r >> 1) & 1, r & 1
            peer = (x ^ fx, y ^ fy, c ^ fc)
            copies.append(pltpu.make_async_remote_copy(src_ref=in_ref, dst_ref=dst, send_sem=send_sems.at[r - 1],
                                                       recv_sem=recv_sems.at[r - 1], device_id=peer, device_id_type=MESH))
        for cp in copies:
            cp.start()
        for cp in copies:
            cp.wait()

    return pl.pallas_call(
        body, name="gather_small_grads", in_specs=[_ANY], out_specs=_ANY,
        out_shape=_sds((N_DEV, *part.shape), part.dtype),
        scratch_shapes=[pltpu.SemaphoreType.DMA((N_DEV - 1,)), pltpu.SemaphoreType.DMA((N_DEV - 1,)), pltpu.SemaphoreType.DMA],
    )(part)


def _sum_chips(grad, recv, me):
    _, rr, cc = recv.shape
    tr = _tile(rr, 512)

    def body(me_ref, g_ref, r0_ref, r1_ref, r2_ref, o_ref):
        o_ref[...] = ((g_ref[...].astype(F32) + r0_ref[...].astype(F32)) + r1_ref[...].astype(F32)) + r2_ref[...].astype(F32)

    gspec = pl.BlockSpec((None, tr, cc), lambda r, m: (m[0], r, 0))
    rspecs = [pl.BlockSpec((None, tr, cc), functools.partial(lambda r, m, j: (j, r, 0), j=j)) for j in range(3)]
    return pl.pallas_call(
        body, name="sum_chip_grads",
        grid_spec=pltpu.PrefetchScalarGridSpec(
            num_scalar_prefetch=1, grid=(rr // tr,), in_specs=[gspec, *rspecs],
            out_specs=pl.BlockSpec((tr, cc), lambda r, m: (r, 0))),
        out_shape=_sds((rr, cc), F32),
        compiler_params=_params(("parallel",), 4 * tr * cc * 2 + tr * cc * 4, 0, 2 * tr * cc * 4),
    )(me, grad, recv, recv, recv)


def _sum_pair(grad, sib, core):
    k, h, cc = sib.shape
    tr = _tile(h, 512)
    nb = h // tr

    def body(c_ref, g_ref, a_ref, o_ref):
        o_ref[...] = (g_ref[...].astype(F32) + a_ref[...].astype(F32)).astype(o_ref.dtype)

    spec = pl.BlockSpec((None, tr, cc), lambda s, r, c: (s, r, 0))
    return pl.pallas_call(
        body, name="sum_core_grads",
        grid_spec=pltpu.PrefetchScalarGridSpec(
            num_scalar_prefetch=1, grid=(k, nb),
            in_specs=[pl.BlockSpec((None, tr, cc), lambda s, r, c: (s, c[0] * nb + r, 0)), spec], out_specs=spec),
        out_shape=_sds(sib.shape, sib.dtype),
        compiler_params=_params(("parallel", "parallel"), 3 * tr * cc * 2, 0, 2 * tr * cc * 4),
    )(core, grad, sib)


def _adamw_math(w, g, m, v):
    m = ADAM_B1 * m + (1.0 - ADAM_B1) * g
    v = ADAM_B2 * v + (1.0 - ADAM_B2) * (g * g)
    m_hat = m / (1.0 - ADAM_B1 ** ADAM_STEP)
    v_hat = v / (1.0 - ADAM_B2 ** ADAM_STEP)
    delta = -ADAM_LR * (m_hat / (jnp.sqrt(v_hat) + ADAM_EPS) + ADAM_WD * w)
    return delta, m, v


def _adamw(name, parts, w, m, v):
    ll, rr, cc = w.shape
    tr = _tile(rr, 256)
    npart = len(parts)

    def body(*refs):
        p_refs = refs[:npart]
        w_ref, m_ref, v_ref, g_ref, d_ref, nm_ref, nv_ref = refs[npart:]
        g = p_refs[0][...]
        for p in p_refs[1:]:
            g = g + p[...]
        d, nm, nv = _adamw_math(w_ref[...], g, m_ref[...], v_ref[...])
        g_ref[...] = g
        d_ref[...] = d
        nm_ref[...] = nm
        nv_ref[...] = nv

    spec = pl.BlockSpec((None, tr, cc), lambda l, r: (l, r, 0))
    out = _sds((ll, rr, cc), F32)
    return pl.pallas_call(
        body, name=name, grid=(ll, rr // tr), in_specs=[spec] * (npart + 3), out_specs=[spec] * 4, out_shape=[out] * 4,
        compiler_params=_params(("parallel", "parallel"), (npart + 7) * tr * cc * 4, 0, 4 * tr * cc * 4),
    )(*parts, w, m, v)


def _adamw_halves(name, s_own, s_sib, core, w, m, v):
    ll, rr, cc = w.shape
    h = rr // 2
    tr = _tile(h, 256)
    nb = h // tr

    def body(c_ref, own_ref, sib_ref, w_ref, m_ref, v_ref, g_ref, d_ref, nm_ref, nv_ref):
        g = jnp.where(pl.program_id(1) == c_ref[0], own_ref[...], sib_ref[...])
        d, nm, nv = _adamw_math(w_ref[...], g, m_ref[...], v_ref[...])
        g_ref[...] = g
        d_ref[...] = d
        nm_ref[...] = nm
        nv_ref[...] = nv

    half = pl.BlockSpec((None, tr, cc), lambda l, hf, r, c: (l, r, 0))
    full = pl.BlockSpec((None, tr, cc), lambda l, hf, r, c: (l, hf * nb + r, 0))
    out = _sds((ll, rr, cc), F32)
    return pl.pallas_call(
        body, name=name,
        grid_spec=pltpu.PrefetchScalarGridSpec(num_scalar_prefetch=1, grid=(ll, 2, nb), in_specs=[half, half, full, full, full],
                                               out_specs=[full] * 4),
        out_shape=[out] * 4,
        compiler_params=_params(("parallel", "parallel", "parallel"), 9 * tr * cc * 4, 0, 4 * tr * cc * 4),
    )(core, s_own, s_sib, w, m, v)


def _sum_devices(allparts):
    _, rr, cc = allparts.shape

    def body(p_ref, o_ref):
        s = p_ref[0]
        for k in range(1, N_DEV):
            s = s + p_ref[k]
        o_ref[...] = s

    return pl.pallas_call(
        body, name="sum_small_grads", grid=(1,), in_specs=[pl.BlockSpec((N_DEV, rr, cc), lambda i: (0, 0, 0))],
        out_specs=pl.BlockSpec((rr, cc), lambda i: (0, 0)), out_shape=_sds((rr, cc), F32),
        compiler_params=_params(("arbitrary",), 9 * rr * cc * 4),
    )(allparts)


def _pack(arrs):
    flat = jnp.concatenate([a.reshape(-1).astype(F32) for a in arrs])
    n = flat.shape[0]
    total = -(-n // 1024) * 1024
    return jnp.pad(flat, (0, total - n)).reshape(total // 128, 128)


def _unpack(block, shapes):
    flat = block.reshape(-1)
    out, off = [], 0
    for sh in shapes:
        n = math.prod(sh)
        out.append(flat[off:off + n].reshape(sh))
        off += n
    return out


_BIG = ["w_in_ab", "w_out_ab", "w_qkv_c", "w_out_c", "mem_wq", "mem_wk", "mem_wv", "mem_wo", "ffn_w1", "ffn_w3", "ffn_w2"]
_ROW_SHARDED = ("w_out_ab", "w_out_c", "mem_wq", "mem_wk", "mem_wv", "mem_wo")
_SMALL_REPL = ["gmlp_ln_g", "gmlp_ln_b", "gmlp_w_s", "gmlp_b_s", "conv_b", "conv_gn_g", "conv_gn_b"]
_SMALL_SHARD = ["conv_w", "ln_g", "ln_b"]
_NAMES = ["w_in_ab", "gmlp_ln_g", "gmlp_ln_b", "gmlp_w_s", "gmlp_b_s", "conv_w", "conv_b", "conv_gn_g", "conv_gn_b",
          "w_out_ab", "w_qkv_c", "w_out_c", "mem_wq", "mem_wk", "mem_wv", "mem_wo", "ffn_w1", "ffn_w3", "ffn_w2",
          "ln_g", "ln_b"]


def _layer_weights(l):
    mixer = ["w_in_ab", "w_out_ab"] if l % 2 == 0 else ["w_qkv_c", "w_out_c"]
    return [(n, l // 2) for n in mixer] + [(n, l) for n in _BIG if n.startswith(("mem_", "ffn_"))]


def _natural(w):
    return w.reshape(-1, w.shape[-1])


def _local_step(x, mem, target, layer_w, small):
    n_ex, s, d = x.shape
    t = n_ex * s
    x2 = x.reshape(t, d)
    mem_a = mem.reshape(-1, d).astype(ACT)
    tgt = target.reshape(t, d)
    one = jnp.ones((1, d), F32)
    zero = jnp.zeros((1, d), F32)
    ln_g, ln_b = small["ln_g"], small["ln_b"]

    def vec(a):
        return a.reshape(1, -1)

    saved = []
    xh, gp, bp = x2, one, zero
    y_act = x2.astype(ACT)
    for l in range(DEPTH):
        wts = layer_w[l]
        sv = {"y0": y_act, "w": wts}
        if l % 2 == 0:
            e = l // 2
            h = _proj_cols(f"in_ab_{l}", y_act, wts["w_in_ab"], ACT)
            gl = (vec(small["gmlp_ln_g"][e]), vec(small["gmlp_ln_b"][e]), small["gmlp_w_s"][e],
                  small["gmlp_b_s"][e].reshape(4, CHUNK, 1))
            cl = (small["conv_w"][e], vec(small["conv_b"][e]), vec(small["conv_gn_g"][e]), vec(small["conv_gn_b"][e]))
            ya = _gmlp_fwd(f"gmlp_fwd_{l}", h, *gl)
            yb = _conv_fwd(f"conv_fwd_{l}", h, n_ex, *cl)
            yab = jnp.concatenate([ya, yb], axis=1)
            mixed, w_mix = yab, _natural(wts["w_out_ab"])
            sv.update(h=h, yab=yab, gl=gl, cl=cl)
        else:
            qkv = _proj_cols(f"qkv_{l}", y_act, wts["w_qkv_c"], ACT)
            att, ltot = _sb_fwd(f"sb_fwd_{l}", qkv, n_ex)
            att_a = att.astype(ACT)
            mixed, w_mix = att_a, _natural(wts["w_out_c"])
            sv.update(qkv=qkv, att=att_a, ltot=ltot)
        g1, b1 = vec(ln_g[l, 0]), vec(ln_b[l, 0])
        xh1, y1, rstd1 = _dense_ln(f"mix_out_ln1_{l}", mixed, w_mix, xh, gp, bp, g1, b1)
        q = _dense(f"mem_q_{l}", y1, _natural(wts["mem_wq"]), ACT)
        kk = _dense(f"mem_k_{l}", mem_a, _natural(wts["mem_wk"]), ACT)
        vv = _dense(f"mem_v_{l}", mem_a, _natural(wts["mem_wv"]), ACT)
        oc = _xattn_fwd(f"xattn_fwd_{l}", q, kk, vv, n_ex)
        g2, b2 = vec(ln_g[l, 1]), vec(ln_b[l, 1])
        xh2, y2, rstd2 = _dense_ln(f"mem_o_ln2_{l}", oc, _natural(wts["mem_wo"]), xh1, g1, b1, g2, b2)
        h1, h3, gact = _ffn_up(f"ffn_up_{l}", y2, wts["ffn_w1"], wts["ffn_w3"])
        g3, b3 = vec(ln_g[l, 2]), vec(ln_b[l, 2])
        xh3, y3, rstd3 = _ffn_down_ln(f"ffn_down_ln3_{l}", gact, wts["ffn_w2"], xh2, g2, b2, g3, b3)
        sv.update(xh1=xh1, y1=y1, rstd1=rstd1, g1=g1, q=q, kk=kk, vv=vv, oc=oc, xh2=xh2, y2=y2, rstd2=rstd2, g2=g2,
                  h1=h1, h3=h3, gact=gact, xh3=xh3, rstd3=rstd3, g3=g3)
        saved.append(sv)
        xh, gp, bp, y_act = xh3, g3, b3, y3

    dy, loss = _loss_head(xh, gp, bp, tgt)

    sm = {n: [None] * (DEPTH // 2) for n in _SMALL_REPL + ["conv_w"]}
    d_ln_g = [[None] * 3 for _ in range(DEPTH)]
    d_ln_b = [[None] * 3 for _ in range(DEPTH)]

    def add_res(vals, ex):
        return [vals[0] + ALPHA * ex[0]]

    def add_res2(vals, ex):
        return [vals[0] + ex[0] + ALPHA * ex[1]]

    layer_g = [None] * DEPTH
    for l in reversed(range(DEPTH)):
        sv = saved[l]
        wts = sv["w"]
        big = {}
        dr3, dr3a, d_ln_g[l][2], d_ln_b[l][2] = _ln_bwd(f"ln3_bwd_{l}", dy, sv["xh3"], sv["rstd3"], sv["g3"])
        big["ffn_w2"] = _ffn_wgrad_down(f"ffn_w2_grad_{l}", sv["gact"], dr3a)
        dh1, dh3 = _ffn_down_bwd(f"ffn_down_bwd_{l}", dr3a, wts["ffn_w2"], sv["h1"], sv["h3"])
        big["ffn_w1"], big["ffn_w3"] = _ffn_wgrad_up(f"ffn_w13_grad_{l}", sv["y2"], dh1, dh3)
        part = _ffn_up_bwd(f"ffn_up_bwd1_{l}", dh1, wts["ffn_w1"], [], None)
        dy = _ffn_up_bwd(f"ffn_up_bwd3_{l}", dh3, wts["ffn_w3"], [part, dr3], add_res2)
        dr2, dr2a, d_ln_g[l][1], d_ln_b[l][1] = _ln_bwd(f"ln2_bwd_{l}", dy, sv["xh2"], sv["rstd2"], sv["g2"])
        big["mem_wo"] = _dense_tn(f"mem_wo_grad_{l}", sv["oc"], dr2a)
        doc = _dense_nt(f"mem_o_bwd_{l}", dr2a, _natural(wts["mem_wo"]), ACT)
        dq, dkk, dvv = _xattn_bwd(f"xattn_bwd_{l}", sv["q"], sv["kk"], sv["vv"], doc, n_ex)
        big["mem_wq"] = _dense_tn(f"mem_wq_grad_{l}", sv["y1"], dq)
        big["mem_wk"] = _dense_tn(f"mem_wk_grad_{l}", mem_a, dkk)
        big["mem_wv"] = _dense_tn(f"mem_wv_grad_{l}", mem_a, dvv)
        dy = _dense_nt(f"mem_q_bwd_{l}", dq, _natural(wts["mem_wq"]), F32, extras=[dr2], epilogue=add_res)
        dr1, dr1a, d_ln_g[l][0], d_ln_b[l][0] = _ln_bwd(f"ln1_bwd_{l}", dy, sv["xh1"], sv["rstd1"], sv["g1"])
        if l % 2 == 0:
            e = l // 2
            big["w_out_ab"] = _dense_tn(f"out_ab_grad_{l}", sv["yab"], dr1a)
            dyab = _dense_nt(f"out_ab_bwd_{l}", dr1a, _natural(wts["w_out_ab"]), ACT)
            duv, dgg, dgb, dws, dbs = _gmlp_bwd(f"gmlp_bwd_{l}", sv["h"], dyab, *sv["gl"])
            da, dgt, dcw, dcb, dng, dnb = _conv_bwd(f"conv_bwd_{l}", sv["h"], dyab, n_ex, *sv["cl"])
            sm["gmlp_ln_g"][e], sm["gmlp_ln_b"][e] = dgg.reshape(-1), dgb.reshape(-1)
            sm["gmlp_w_s"][e], sm["gmlp_b_s"][e] = dws, dbs.reshape(4, CHUNK)
            sm["conv_w"][e], sm["conv_b"][e] = dcw, dcb.reshape(-1)
            sm["conv_gn_g"][e], sm["conv_gn_b"][e] = dng.reshape(-1), dnb.reshape(-1)
            dh = jnp.concatenate([duv, da, dgt], axis=1)
            big["w_in_ab"] = _proj_cols_wgrad(f"in_ab_grad_{l}", sv["y0"], dh)
            dy = _proj_cols_bwd(f"in_ab_bwd_{l}", dh, wts["w_in_ab"], [dr1], add_res)
        else:
            big["w_out_c"] = _dense_tn(f"out_c_grad_{l}", sv["att"], dr1a)
            datt = _dense_nt(f"out_c_bwd_{l}", dr1a, _natural(wts["w_out_c"]), ACT)
            dq_, dk_, dv_ = _sb_bwd(f"sb_bwd_{l}", sv["qkv"], datt, sv["ltot"], n_ex)
            dqkv = jnp.concatenate([dq_, dk_, dv_], axis=1)
            big["w_qkv_c"] = _proj_cols_wgrad(f"qkv_grad_{l}", sv["y0"], dqkv)
            dy = _proj_cols_bwd(f"qkv_bwd_{l}", dqkv, wts["w_qkv_c"], [dr1], add_res)
        for n in _ROW_SHARDED:
            if n in big:
                big[n] = big[n].reshape(N_CHIPS, -1, big[n].shape[-1])
        layer_g[l] = big

    grad_x = dy.reshape(n_ex, s, d)
    small_g = {n: jnp.stack(sm[n]) for n in sm}
    small_g["ln_g"] = jnp.stack([jnp.concatenate(r, axis=0) for r in d_ln_g])
    small_g["ln_b"] = jnp.stack([jnp.concatenate(r, axis=0) for r in d_ln_b])
    return loss, grad_x, layer_g, small_g


def kernel(x, mem, w_in_ab, gmlp_ln_g, gmlp_ln_b, gmlp_w_s, gmlp_b_s, conv_w, conv_b, conv_gn_g, conv_gn_b, w_out_ab, w_qkv_c, w_out_c, mem_wq, mem_wk, mem_wv, mem_wo, ffn_w1, ffn_w3, ffn_w2, ln_g, ln_b, loss_target, m_w_in_ab, m_gmlp_ln_g, m_gmlp_ln_b, m_gmlp_w_s, m_gmlp_b_s, m_conv_w, m_conv_b, m_conv_gn_g, m_conv_gn_b, m_w_out_ab, m_w_qkv_c, m_w_out_c, m_mem_wq, m_mem_wk, m_mem_wv, m_mem_wo, m_ffn_w1, m_ffn_w3, m_ffn_w2, m_ln_g, m_ln_b, v_w_in_ab, v_gmlp_ln_g, v_gmlp_ln_b, v_gmlp_w_s, v_gmlp_b_s, v_conv_w, v_conv_b, v_conv_gn_g, v_conv_gn_b, v_w_out_ab, v_w_qkv_c, v_w_out_c, v_mem_wq, v_mem_wk, v_mem_wv, v_mem_wo, v_ffn_w1, v_ffn_w3, v_ffn_w2, v_ln_g, v_ln_b):
    w = dict(w_in_ab=w_in_ab, gmlp_ln_g=gmlp_ln_g, gmlp_ln_b=gmlp_ln_b, gmlp_w_s=gmlp_w_s, gmlp_b_s=gmlp_b_s, conv_w=conv_w,
             conv_b=conv_b, conv_gn_g=conv_gn_g, conv_gn_b=conv_gn_b, w_out_ab=w_out_ab, w_qkv_c=w_qkv_c, w_out_c=w_out_c,
             mem_wq=mem_wq, mem_wk=mem_wk, mem_wv=mem_wv, mem_wo=mem_wo, ffn_w1=ffn_w1, ffn_w3=ffn_w3, ffn_w2=ffn_w2,
             ln_g=ln_g, ln_b=ln_b)
    mo = dict(w_in_ab=m_w_in_ab, gmlp_ln_g=m_gmlp_ln_g, gmlp_ln_b=m_gmlp_ln_b, gmlp_w_s=m_gmlp_w_s, gmlp_b_s=m_gmlp_b_s,
              conv_w=m_conv_w, conv_b=m_conv_b, conv_gn_g=m_conv_gn_g, conv_gn_b=m_conv_gn_b, w_out_ab=m_w_out_ab,
              w_qkv_c=m_w_qkv_c, w_out_c=m_w_out_c, mem_wq=m_mem_wq, mem_wk=m_mem_wk, mem_wv=m_mem_wv, mem_wo=m_mem_wo,
              ffn_w1=m_ffn_w1, ffn_w3=m_ffn_w3, ffn_w2=m_ffn_w2, ln_g=m_ln_g, ln_b=m_ln_b)
    vo = dict(w_in_ab=v_w_in_ab, gmlp_ln_g=v_gmlp_ln_g, gmlp_ln_b=v_gmlp_ln_b, gmlp_w_s=v_gmlp_w_s, gmlp_b_s=v_gmlp_b_s,
              conv_w=v_conv_w, conv_b=v_conv_b, conv_gn_g=v_conv_gn_g, conv_gn_b=v_conv_gn_b, w_out_ab=v_w_out_ab,
              w_qkv_c=v_w_qkv_c, w_out_c=v_w_out_c, mem_wq=v_mem_wq, mem_wk=v_mem_wk, mem_wv=v_mem_wv, mem_wo=v_mem_wo,
              ffn_w1=v_ffn_w1, ffn_w3=v_ffn_w3, ffn_w2=v_ffn_w2, ln_g=v_ln_g, ln_b=v_ln_b)
    me = (2 * lax.axis_index("x") + lax.axis_index("y")).astype(jnp.int32).reshape(1)

    per_layer = [_layer_weights(l) for l in range(DEPTH)]
    srcs = [w[n] for n in _SMALL_SHARD] + [w[n][i].astype(MM) for lw in per_layer for n, i in lw]
    lands = [lax.dynamic_update_index_in_dim(jnp.zeros((N_CHIPS, *s.shape), s.dtype), s[None], me[0], 0) for s in srcs]
    gathered = _gather_two_level(srcs, lands)
    cw_g, lg_g, lb_g = gathered[:3]
    small = {n: w[n] for n in _SMALL_REPL}
    small["conv_w"] = jnp.moveaxis(cw_g, 0, 2).reshape(cw_g.shape[1], CONV_WIDTH, -1)
    small["ln_g"] = jnp.moveaxis(lg_g, 0, 2).reshape(DEPTH, 3, -1)
    small["ln_b"] = jnp.moveaxis(lb_g, 0, 2).reshape(DEPTH, 3, -1)
    it = iter(gathered[3:])
    layer_w = [{n: next(it) for n, _ in lw} for lw in per_layer]

    loss, grad_x, layer_g, small_g = _local_step(x, mem, loss_target, layer_w, small)
    loss = lax.psum(loss, ("x", "y", "c"))

    core = lax.axis_index("c").astype(jnp.int32).reshape(1)
    grads_flat = [layer_g[l][n] for l in range(DEPTH) for n, _ in per_layer[l]]
    pair = [_sum_pair(g, a, core) for g, a in zip(grads_flat, _swap_halves(grads_flat))]
    recv = _scatter_chips(pair)
    per_name = {n: [None] * (DEPTH if n.startswith(("mem_", "ffn_")) else DEPTH // 2) for n in _BIG}
    flat_names = [ni for lw in per_layer for ni in lw]
    for (n, i), p, rc in zip(flat_names, pair, recv):
        per_name[n][i] = _sum_chips(p, rc, me)
    sums = [jnp.stack(per_name[n]) for n in _BIG]
    sib = _swap_sibling(sums)

    out = {}
    for n, s_own, s_sib in zip(_BIG, sums, sib):
        out[n] = _adamw_halves(f"adamw_{n}", s_own, s_sib, core, w[n], mo[n], vo[n])

    order = _SMALL_REPL + _SMALL_SHARD
    part = _pack([small_g[n] for n in order])
    total = _sum_devices(_gather_all(part))
    full = dict(zip(order, _unpack(total, [small_g[n].shape for n in order])))
    x_i, y_i = lax.axis_index("x"), lax.axis_index("y")
    chip = 2 * x_i + y_i
    loc = {n: full[n] for n in _SMALL_REPL}
    for n in _SMALL_SHARD:
        wd = w[n].shape[-1]
        loc[n] = lax.dynamic_slice_in_dim(full[n], chip * wd, wd, axis=full[n].ndim - 1)
    gp, wp, mp, vp = (_pack([src[n] for n in order]) for src in (loc, w, mo, vo))
    r128 = gp.shape[0]
    res = _adamw("adamw_small", [gp.reshape(1, r128, 128)], wp.reshape(1, r128, 128), mp.reshape(1, r128, 128),
                 vp.reshape(1, r128, 128))
    shapes = [w[n].shape for n in order]
    unp = [_unpack(r.reshape(r128, 128), shapes) for r in res]
    for i, n in enumerate(order):
        out[n] = tuple(u[i] for u in unp)

    grads = [out[n][0] for n in _NAMES]
    deltas = [out[n][1] for n in _NAMES]
    new_m = [out[n][2] for n in _NAMES]
    new_v = [out[n][3] for n in _NAMES]
    return (loss, grad_x, *grads, *deltas, *new_m, *new_v)
```
